```python
import math
import jax
import jax.numpy as jnp
from jax import lax
import numpy as np

D_MODEL = 2048
BATCH = 4
SEQ = 4096
DEPTH = 4

GRID_W = 64
CTX_LEN = 256
EPS = 1e-6
NEG = -1e30
HEAD_DIM = 64
GROUP_W = D_MODEL // 4
MIX_W = 4 * GROUP_W

A_HEADS = GROUP_W // HEAD_DIM
A_KV_HEADS = max(1, A_HEADS // 4)
WINDOW = 128
BLOCK = 128
ROPE_BASE = 10000.0
HY_CH = GROUP_W
HY_ORDER = 2
HY_EMB = 33
HY_FFN = 64
HY_SHORT = 3
HY_DECAY_TARGET = 1e-2
HY_FAST_PCT = 0.3
HY_SLOW_PCT = 1.5
NA_HEADS = GROUP_W // HEAD_DIM
NA_KR = 8
NA_KC = 16
SSM_INNER = GROUP_W
SSM_HEAD_DIM = 64
SSM_HEADS = SSM_INNER // SSM_HEAD_DIM
SSM_STATE = 128
SSM_GROUPS = 2
SSM_CONV = 3
SSM_CHUNK = 128
SSM_XBC = SSM_INNER + 2 * SSM_GROUPS * SSM_STATE
D_FF = 4 * D_MODEL

A_COLS = (A_HEADS + 2 * A_KV_HEADS) * HEAD_DIM
HY_COLS = (HY_ORDER + 1) * HY_CH
NA_COLS = 3 * NA_HEADS * HEAD_DIM
SSM_COLS = SSM_INNER + SSM_XBC + 2 * SSM_HEADS
OFF_HY = A_COLS
OFF_NA = OFF_HY + HY_COLS
OFF_SSM = OFF_NA + NA_COLS
N_IN = OFF_SSM + SSM_COLS

kernel_name = "hybrid_parallel_heads_diffusion_trunk"

F32 = jnp.float32


def rms_norm(x, g):
    xf = x.astype(F32)
    y = xf * lax.rsqrt(jnp.mean(xf * xf, axis=-1, keepdims=True) + EPS)
    return (y * g.astype(F32)).astype(x.dtype)


def centred_conv(u, w, b):
    K = w.shape[0]
    L = u.shape[1]
    pad = K // 2
    up = jnp.pad(u, ((0, 0), (pad, pad), (0, 0)))
    out = up[:, 0:L] * w[0]
    for k in range(1, K):
        out = out + up[:, k:k + L] * w[k]
    return out + b


def axial_rope(x, row, col):
    half = x.shape[-1] // 2
    quarter = half // 2
    inv = ROPE_BASE ** (-jnp.arange(quarter, dtype=F32) / quarter)

    def rot(xp, pos):
        ang = pos.astype(F32)[:, None] * inv[None]
        cos = jnp.cos(ang)[None, :, None, :].astype(x.dtype)
        sin = jnp.sin(ang)[None, :, None, :].astype(x.dtype)
        x1, x2 = xp[..., :quarter], xp[..., quarter:]
        return jnp.concatenate([x1 * cos - x2 * sin, x2 * cos + x1 * sin], axis=-1)

    return jnp.concatenate([rot(x[..., :half], row), rot(x[..., half:], col)], axis=-1)


def split_heads(p, n_q, n_kv):
    Bn, L, _ = p.shape
    q = p[..., :n_q * HEAD_DIM].reshape(Bn, L, n_q, HEAD_DIM)
    k = p[..., n_q * HEAD_DIM:(n_q + n_kv) * HEAD_DIM].reshape(Bn, L, n_kv, HEAD_DIM)
    v = p[..., (n_q + n_kv) * HEAD_DIM:(n_q + 2 * n_kv) * HEAD_DIM].reshape(Bn, L, n_kv, HEAD_DIM)
    return q, k, v


def context_attention(qc, kc, vc, sink=None):
    Bn, Lc, Hq, Dh = qc.shape
    Hkv = kc.shape[2]
    G = Hq // Hkv
    qg = qc.reshape(Bn, Lc, Hkv, G, Dh)
    s = jnp.einsum('bqhgd,bkhd->bhgqk', qg, kc).astype(F32) * (Dh ** -0.5)
    if sink is not None:
        sk = jnp.broadcast_to(sink.astype(F32).reshape(1, Hkv, G, 1, 1), s.shape[:-1] + (1,))
        s = jnp.concatenate([s, sk], axis=-1)
    p = jax.nn.softmax(s, axis=-1)[..., :Lc].astype(vc.dtype)
    return jnp.einsum('bhgqk,bkhd->bqhgd', p, vc).reshape(Bn, Lc, Hq * Dh)


def window_attention(q, k, v, kc, vc, sink):
    Bn, S, Hq, Dh = q.shape
    Hkv = k.shape[2]
    G = Hq // Hkv
    nb = S // BLOCK
    Lc = kc.shape[1]
    scale = Dh ** -0.5
    qb = q.reshape(Bn, nb, BLOCK, Hkv, G, Dh)

    def band(t):
        tp = jnp.pad(t, ((0, 0), (BLOCK, BLOCK), (0, 0), (0, 0))).reshape(Bn, nb + 2, BLOCK, Hkv, Dh)
        return jnp.concatenate([tp[:, :-2], tp[:, 1:-1], tp[:, 2:]], axis=2)

    kb, vb = band(k), band(v)
    s_loc = jnp.einsum('bnqhgd,bnkhd->bnhgqk', qb, kb).astype(F32) * scale
    s_ctx = jnp.einsum('bnqhgd,bchd->bnhgqc', qb, kc).astype(F32) * scale
    qi = jnp.arange(BLOCK)[:, None]
    kj = jnp.arange(3 * BLOCK)[None, :]
    rel = kj - BLOCK - qi
    kpos = jnp.arange(nb)[:, None, None] * BLOCK + (kj - BLOCK)[None]
    valid = (jnp.abs(rel) <= WINDOW)[None] & (kpos >= 0) & (kpos < S)
    s_loc = jnp.where(valid[None, :, None, None], s_loc, NEG)
    sk = jnp.broadcast_to(sink.astype(F32).reshape(1, 1, Hkv, G, 1, 1), s_loc.shape[:-1] + (1,))
    p = jax.nn.softmax(jnp.concatenate([s_loc, s_ctx, sk], axis=-1), axis=-1)
    p_loc = p[..., :3 * BLOCK].astype(v.dtype)
    p_ctx = p[..., 3 * BLOCK:3 * BLOCK + Lc].astype(v.dtype)
    out = (jnp.einsum('bnhgqk,bnkhd->bnqhgd', p_loc, vb)
           + jnp.einsum('bnhgqc,bchd->bnqhgd', p_ctx, vc))
    return out.reshape(Bn, S, Hq * Dh)


def neighbourhood_attention(q, k, v, kc, vc, rpb):
    Bn, S, H, Dh = q.shape
    rows = S // GRID_W
    kr = min(NA_KR, rows)
    Lc = kc.shape[1]
    scale = Dh ** -0.5
    qg = q.reshape(Bn, rows, GRID_W, H, Dh)
    r = jnp.arange(rows)
    rstart = jnp.clip(r - kr // 2, 0, rows - kr)
    ridx = rstart[:, None] + jnp.arange(kr)[None]
    kg = jnp.take(k.reshape(Bn, rows, GRID_W, H, Dh), ridx, axis=1)
    vg = jnp.take(v.reshape(Bn, rows, GRID_W, H, Dh), ridx, axis=1)
    s_loc = jnp.einsum('brqhd,brawhd->brhqaw', qg, kg).astype(F32) * scale
    cq = jnp.arange(GRID_W)
    ck = jnp.arange(GRID_W)
    cstart = jnp.clip(cq - NA_KC // 2, 0, GRID_W - NA_KC)
    col_valid = (ck[None] >= cstart[:, None]) & (ck[None] < cstart[:, None] + NA_KC)
    roff = ridx - r[:, None] + NA_KR - 1
    coff = jnp.clip(ck[None] - cq[:, None], -(NA_KC - 1), NA_KC - 1) + NA_KC - 1
    bias = rpb[:, roff][:, :, :, coff]
    bias = bias.transpose(1, 0, 3, 2, 4).astype(F32)
    s_loc = jnp.where(col_valid[:, None, :], s_loc + bias[None], NEG)
    s_loc = s_loc.reshape(Bn, rows, H, GRID_W, kr * GRID_W)
    s_ctx = jnp.einsum('brqhd,bchd->brhqc', qg, kc).astype(F32) * scale
    p = jax.nn.softmax(jnp.concatenate([s_loc, s_ctx], axis=-1), axis=-1)
    p_loc = p[..., :kr * GRID_W].reshape(Bn, rows, H, GRID_W, kr, GRID_W).astype(v.dtype)
    p_ctx = p[..., kr * GRID_W:].astype(v.dtype)
    out = (jnp.einsum('brhqaw,brawhd->brqhd', p_loc, vg)
           + jnp.einsum('brhqc,bchd->brqhd', p_ctx, vc))
    return out.reshape(Bn, S, H * Dh)


def hyena_filters(L, w1, b1, w2, b2, w3, b3, w4, freq):
    t = jnp.linspace(0.0, 1.0, L, dtype=F32)[:, None]
    bands = (HY_EMB - 1) // 2
    f = jnp.linspace(1e-4, bands - 1, bands, dtype=F32)[None]
    wpos = 2.0 * math.pi * jnp.arange(L, dtype=F32)[:, None] / L
    z = jnp.concatenate([t, jnp.cos(f * wpos), -jnp.sin(f * wpos)], axis=-1)
    fr = freq.astype(F32)
    h = jnp.sin(fr * (z @ w1.astype(F32) + b1.astype(F32)))
    h = jnp.sin(fr * (h @ w2.astype(F32) + b2.astype(F32)))
    h = jnp.sin(fr * (h @ w3.astype(F32) + b3.astype(F32)))
    h = (h @ w4.astype(F32)).reshape(L, HY_ORDER, 2, HY_CH)
    max_decay = math.log(HY_DECAY_TARGET) / HY_FAST_PCT
    min_decay = math.log(HY_DECAY_TARGET) / HY_SLOW_PCT
    deltas = jnp.linspace(min_decay, max_decay, HY_CH, dtype=F32)
    decay = jnp.exp(-t * jnp.abs(deltas)[None])
    return h * decay[:, None, None, :]


def bidir_long_conv(u, h_fwd, h_bwd, skip):
    Bn, L, C = u.shape
    kern = jnp.concatenate([h_fwd, jnp.zeros((1, C), F32), h_bwd[1:][::-1]], axis=0)
    U = jnp.fft.rfft(u, n=2 * L, axis=1)
    Kf = jnp.fft.rfft(kern, n=2 * L, axis=0)
    y = jnp.fft.irfft(U * Kf[None], n=2 * L, axis=1)[:, :L]
    return y + u * skip.astype(F32)


def hyena_mixer(p, short_w, short_b, filt, skip):
    u = centred_conv(p, short_w, short_b).astype(F32)
    x1, x2, v = u[..., :HY_CH], u[..., HY_CH:2 * HY_CH], u[..., 2 * HY_CH:]
    z = x1 * bidir_long_conv(v, filt[:, 0, 0], filt[:, 0, 1], skip[0])
    z = x2 * bidir_long_conv(z, filt[:, 1, 0], filt[:, 1, 1], skip[1])
    return z.astype(p.dtype)


def segsum(a):
    T = a.shape[-1]
    cs = jnp.cumsum(a, axis=-1)
    diff = cs[..., :, None] - cs[..., None, :]
    mask = jnp.tril(jnp.ones((T, T), dtype=bool))
    return jnp.where(mask, diff, -jnp.inf)


def ssd_chunked(xs, dt, A, Bm, Cm, h0, return_y):
    b, l, nh, p = xs.shape
    g, n = Bm.shape[2], Bm.shape[3]
    nc = l // SSM_CHUNK
    Bh = jnp.repeat(Bm, nh // g, axis=2).reshape(b, nc, SSM_CHUNK, nh, n)
    Ch = jnp.repeat(Cm, nh // g, axis=2).reshape(b, nc, SSM_CHUNK, nh, n)
    X = (xs * dt[..., None]).reshape(b, nc, SSM_CHUNK, nh, p)
    Adt = (dt * A).reshape(b, nc, SSM_CHUNK, nh).transpose(0, 3, 1, 2)
    A_cs = jnp.cumsum(Adt, axis=-1)
    decay_states = jnp.exp(A_cs[..., -1:] - A_cs).transpose(0, 2, 3, 1)
    states = jnp.einsum('bclhn,bclhp->bchpn', Bh, X * decay_states[..., None])
    states = jnp.concatenate([h0[:, None], states], axis=1)
    chunk_decay = jnp.exp(segsum(jnp.pad(A_cs[..., -1], ((0, 0), (0, 0), (1, 0)))))
    new_states = jnp.einsum('bhzc,bchpn->bzhpn', chunk_decay, states)
    final = new_states[:, -1]
    if not return_y:
        return None, final
    prev = new_states[:, :-1]
    Lmat = jnp.exp(segsum(Adt))
    cb = jnp.einsum('bclhn,bcshn->bhcls', Ch, Bh) * Lmat
    y_diag = jnp.einsum('bhcls,bcshp->bclhp', cb, X)
    out_decay = jnp.exp(A_cs).transpose(0, 2, 3, 1)
    y_off = jnp.einsum('bclhn,bchpn->bclhp', Ch, prev) * out_decay[..., None]
    return (y_diag + y_off).reshape(b, l, nh, p), final


def ssm_inputs(p, conv_w, conv_b):
    Bn, L, _ = p.shape
    z = p[..., :SSM_INNER]
    xbc = jax.nn.silu(centred_conv(p[..., SSM_INNER:SSM_INNER + SSM_XBC], conv_w, conv_b)).astype(F32)
    dt_raw = p[..., SSM_INNER + SSM_XBC:].astype(F32).reshape(Bn, L, 2, SSM_HEADS)
    gn = SSM_GROUPS * SSM_STATE
    xs = xbc[..., :SSM_INNER].reshape(Bn, L, SSM_HEADS, SSM_HEAD_DIM)
    Bm = xbc[..., SSM_INNER:SSM_INNER + gn].reshape(Bn, L, SSM_GROUPS, SSM_STATE)
    Cm = xbc[..., SSM_INNER + gn:].reshape(Bn, L, SSM_GROUPS, SSM_STATE)
    return z, xs, Bm, Cm, dt_raw


def ssm_direction(xs, Bm, Cm, dt_raw_d, dt_bias_d, a_log_d, h0, reverse, return_y):
    dt = jax.nn.softplus(dt_raw_d + dt_bias_d.astype(F32))
    A = -jnp.exp(a_log_d.astype(F32))
    if reverse:
        xs, Bm, Cm, dt = (jnp.flip(t, axis=1) for t in (xs, Bm, Cm, dt))
    y, hT = ssd_chunked(xs, dt, A, Bm, Cm, h0, return_y)
    if reverse and return_y:
        y = jnp.flip(y, axis=1)
    return y, hT


def ssm_output(y, xs, z, d_skip, norm_w):
    Bn, L = z.shape[0], z.shape[1]
    y = y + xs * d_skip.astype(F32)[:, None]
    y = y.reshape(Bn, L, SSM_INNER) * jax.nn.silu(z.astype(F32))
    yg = y.reshape(Bn, L, SSM_GROUPS, SSM_INNER // SSM_GROUPS)
    yg = yg * lax.rsqrt(jnp.mean(yg * yg, axis=-1, keepdims=True) + EPS)
    return (yg.reshape(Bn, L, SSM_INNER) * norm_w.astype(F32)).astype(z.dtype)


def sq_relu_mlp(h, w1, w2):
    return jnp.square(jax.nn.relu(h @ w1)) @ w2


def setup_inputs(seed: int = 0) -> dict:
    key = jax.random.key(seed)
    ks = iter(jax.random.split(key, 40))

    def nrm(shape, scale):
        return jax.random.normal(next(ks), shape, F32) * scale

    x = nrm((BATCH, SEQ, D_MODEL), 1.0)
    c = nrm((BATCH, D_MODEL), 1.0)
    ctx = nrm((BATCH, CTX_LEN, D_MODEL), 1.0)
    c_ctx = nrm((D_MODEL,), 1.0)
    ada_w = nrm((DEPTH, D_MODEL, 6 * D_MODEL), D_MODEL ** -0.5)
    ada_b = nrm((DEPTH, 6 * D_MODEL), 0.02)
    norm_mix = 1.0 + nrm((DEPTH, D_MODEL), 0.02)
    norm_mlp = 1.0 + nrm((DEPTH, D_MODEL), 0.02)
    w_in = nrm((DEPTH, D_MODEL, N_IN), D_MODEL ** -0.5)
    w_out = nrm((DEPTH, MIX_W, D_MODEL), MIX_W ** -0.5)
    attn_sink = nrm((DEPTH, A_HEADS), 0.5)
    hy_short_w = nrm((DEPTH, HY_SHORT, HY_COLS), HY_SHORT ** -0.5)
    hy_short_b = nrm((DEPTH, HY_COLS), 0.02)
    hy_w1 = nrm((DEPTH, HY_EMB, HY_FFN), HY_EMB ** -0.5)
    hy_b1 = nrm((DEPTH, HY_FFN), 0.02)
    hy_w2 = nrm((DEPTH, HY_FFN, HY_FFN), HY_FFN ** -0.5)
    hy_b2 = nrm((DEPTH, HY_FFN), 0.02)
    hy_w3 = nrm((DEPTH, HY_FFN, HY_FFN), HY_FFN ** -0.5)
    hy_b3 = nrm((DEPTH, HY_FFN), 0.02)
    hy_w4 = nrm((DEPTH, HY_FFN, HY_ORDER * 2 * HY_CH), 0.02 * HY_FFN ** -0.5)
    hy_freq = 1.0 + nrm((DEPTH, HY_FFN), 0.02)
    hy_skip = 1.0 + nrm((DEPTH, HY_ORDER, HY_CH), 0.1)
    na_rpb = nrm((DEPTH, NA_HEADS, 2 * NA_KR - 1, 2 * NA_KC - 1), 0.02)
    ssm_conv_w = nrm((DEPTH, SSM_CONV, SSM_XBC), SSM_CONV ** -0.5)
    ssm_conv_b = nrm((DEPTH, SSM_XBC), 0.02)
    u = jax.random.uniform(next(ks), (DEPTH, 2, SSM_HEADS), F32)
    dt0 = jnp.exp(u * (math.log(0.1) - math.log(0.001)) + math.log(0.001))
    ssm_dt_bias = dt0 + jnp.log(-jnp.expm1(-dt0))
    ssm_a_log = jnp.log(jax.random.uniform(next(ks), (DEPTH, 2, SSM_HEADS), F32, 1.0, 16.0))
    ssm_d = 1.0 + nrm((DEPTH, SSM_HEADS), 0.02)
    ssm_norm = 1.0 + nrm((DEPTH, SSM_INNER), 0.02)
    mlp_w1 = nrm((DEPTH, D_MODEL, D_FF), D_MODEL ** -0.5)
    mlp_w2 = nrm((DEPTH, D_FF, D_MODEL), D_FF ** -0.5)
    final_norm = 1.0 + nrm((D_MODEL,), 0.02)
    return {"x": x, "c": c, "ctx": ctx, "c_ctx": c_ctx, "ada_w": ada_w, "ada_b": ada_b,
            "norm_mix": norm_mix, "norm_mlp": norm_mlp, "w_in": w_in, "w_out": w_out,
            "attn_sink": attn_sink, "hy_short_w": hy_short_w, "hy_short_b": hy_short_b,
            "hy_w1": hy_w1, "hy_b1": hy_b1, "hy_w2": hy_w2, "hy_b2": hy_b2, "hy_w3": hy_w3,
            "hy_b3": hy_b3, "hy_w4": hy_w4, "hy_freq": hy_freq, "hy_skip": hy_skip,
            "na_rpb": na_rpb, "ssm_conv_w": ssm_conv_w, "ssm_conv_b": ssm_conv_b,
            "ssm_dt_bias": ssm_dt_bias, "ssm_a_log": ssm_a_log, "ssm_d": ssm_d, "ssm_norm": ssm_norm,
            "mlp_w1": mlp_w1, "mlp_w2": mlp_w2, "final_norm": final_norm}


def reference(x, c, ctx, c_ctx, ada_w, ada_b, norm_mix, norm_mlp, w_in, w_out, attn_sink,
              hy_short_w, hy_short_b, hy_w1, hy_b1, hy_w2, hy_b2, hy_w3, hy_b3, hy_w4, hy_freq, hy_skip,
              na_rpb, ssm_conv_w, ssm_conv_b, ssm_dt_bias, ssm_a_log, ssm_d, ssm_norm,
              mlp_w1, mlp_w2, final_norm):
    Bn, S, _ = x.shape
    Lc = ctx.shape[1]
    t = jnp.arange(S)
    row, col = t // GRID_W, t % GRID_W
    xc = ctx
    for i in range(DEPTH):
        last = i == DEPTH - 1
        need_ctx = not last
        mod = (jax.nn.silu(c) @ ada_w[i] + ada_b[i])[:, None, :]
        sh_a, sc_a, g_a, sh_m, sc_m, g_m = jnp.split(mod, 6, axis=-1)
        modc = (jax.nn.silu(c_ctx) @ ada_w[i] + ada_b[i])[None, None, :]
        shc_a, scc_a, gc_a, shc_m, scc_m, gc_m = jnp.split(modc, 6, axis=-1)

        h = rms_norm(x, norm_mix[i]) * (1 + sc_a) + sh_a
        hc = rms_norm(xc, norm_mix[i]) * (1 + scc_a) + shc_a
        p = h @ w_in[i]
        pc = hc @ w_in[i]

        qa, ka, va = split_heads(p[..., :OFF_HY], A_HEADS, A_KV_HEADS)
        qac, kac, vac = split_heads(pc[..., :OFF_HY], A_HEADS, A_KV_HEADS)
        qa = axial_rope(qa, row, col)
        ka = axial_rope(ka, row, col)
        ya = window_attention(qa, ka, va, kac, vac, attn_sink[i])

        filt = hyena_filters(S, hy_w1[i], hy_b1[i], hy_w2[i], hy_b2[i], hy_w3[i], hy_b3[i], hy_w4[i], hy_freq[i])
        yb = hyena_mixer(p[..., OFF_HY:OFF_NA], hy_short_w[i], hy_short_b[i], filt, hy_skip[i])

        qn, kn, vn = split_heads(p[..., OFF_NA:OFF_SSM], NA_HEADS, NA_HEADS)
        qnc, knc, vnc = split_heads(pc[..., OFF_NA:OFF_SSM], NA_HEADS, NA_HEADS)
        yn = neighbourhood_attention(qn, kn, vn, knc, vnc, na_rpb[i])

        zc, xsc, Bc, Cc, dtc = ssm_inputs(pc[..., OFF_SSM:], ssm_conv_w[i], ssm_conv_b[i])
        h0 = jnp.zeros((Bn, SSM_HEADS, SSM_HEAD_DIM, SSM_STATE), F32)
        yc_f, hf = ssm_direction(xsc, Bc, Cc, dtc[:, :, 0], ssm_dt_bias[i, 0], ssm_a_log[i, 0], h0, False, need_ctx)
        yc_b, hb = ssm_direction(xsc, Bc, Cc, dtc[:, :, 1], ssm_dt_bias[i, 1], ssm_a_log[i, 1], h0, True, need_ctx)
        zl, xsl, Bl, Cl, dtl = ssm_inputs(p[..., OFF_SSM:], ssm_conv_w[i], ssm_conv_b[i])
        yl_f, _ = ssm_direction(xsl, Bl, Cl, dtl[:, :, 0], ssm_dt_bias[i, 0], ssm_a_log[i, 0], hf, False, True)
        yl_b, _ = ssm_direction(xsl, Bl, Cl, dtl[:, :, 1], ssm_dt_bias[i, 1], ssm_a_log[i, 1], hb, True, True)
        yd = ssm_output(yl_f + yl_b, xsl, zl, ssm_d[i], ssm_norm[i])

        y = jnp.concatenate([ya, yb, yn, yd], axis=-1)
        x = x + g_a * (y @ w_out[i])

        if need_ctx:
            filt_c = hyena_filters(Lc, hy_w1[i], hy_b1[i], hy_w2[i], hy_b2[i], hy_w3[i], hy_b3[i], hy_w4[i], hy_freq[i])
            yc = jnp.concatenate([
                context_attention(qac, kac, vac, attn_sink[i]),
                hyena_mixer(pc[..., OFF_HY:OFF_NA], hy_short_w[i], hy_short_b[i], filt_c, hy_skip[i]),
                context_attention(qnc, knc, vnc),
                ssm_output(yc_f + yc_b, xsc, zc, ssm_d[i], ssm_norm[i]),
            ], axis=-1)
            xc = xc + gc_a * (yc @ w_out[i])

        hm = rms_norm(x, norm_mlp[i]) * (1 + sc_m) + sh_m
        x = x + g_m * sq_relu_mlp(hm, mlp_w1[i], mlp_w2[i])
        if need_ctx:
            hmc = rms_norm(xc, norm_mlp[i]) * (1 + scc_m) + shc_m
            xc = xc + gc_m * sq_relu_mlp(hmc, mlp_w1[i], mlp_w2[i])
    return rms_norm(x, final_norm)
```

```python
import functools
import math

import numpy as np
import jax
import jax.numpy as jnp
from jax import lax
from jax.experimental import pallas as pl
from jax.experimental.pallas import tpu as pltpu

F32 = jnp.float32
BF16 = jnp.bfloat16

D_MODEL = 2048
GRID_W = 64
EPS = 1e-6
NEG = -1e30
HEAD_DIM = 64
GROUP_W = D_MODEL // 4
N_HEADS = GROUP_W // HEAD_DIM
WINDOW = 128
BLOCK = 128
ROPE_BASE = 10000.0
HY_EMB = 33
HY_FFN = 64
HY_DECAY_TARGET = 1e-2
HY_FAST_PCT = 0.3
HY_SLOW_PCT = 1.5
NA_KR = 8
NA_KC = 16
SSM_STATE = 128
SSM_CHUNK = 128
D_FF = 4 * D_MODEL
SCALE = HEAD_DIM ** -0.5

COL_QA, COL_KA, COL_VA, COL_DT = 0, 512, 640, 768
COL_HY = 1024
COL_NQ, COL_NK, COL_NV = 2560, 3072, 3584
COL_SZ, COL_SX, COL_SBC = 4096, 4608, 5120
N_PROJ = 5632
HEAD_ORDER = (0, 4, 1, 5, 2, 6, 3, 7)

P_DTYPE = F32
HALO = 16

V7X_VMEM_BYTES = 64 * 1024 * 1024
VMEM_LIMIT = 56 * 1024 * 1024


def _cparams(sem):
    return pltpu.CompilerParams(dimension_semantics=sem, vmem_limit_bytes=VMEM_LIMIT)


def _silu(x):
    return x * jax.nn.sigmoid(x)


def _ada_kernel(cs_ref, w_ref, b_ref, o_ref):
    a = _silu(cs_ref[...]).astype(BF16)
    o_ref[...] = jnp.dot(a, w_ref[...].astype(BF16), preferred_element_type=F32) + b_ref[...]


def _ada_mod(cs, ada_w, ada_b, tn=1024):
    depth, d, n = ada_w.shape
    return pl.pallas_call(
        _ada_kernel,
        grid=(depth, n // tn),
        in_specs=[
            pl.BlockSpec((8, d), lambda i, j: (0, 0)),
            pl.BlockSpec((None, d, tn), lambda i, j: (i, 0, j)),
            pl.BlockSpec((None, 1, tn), lambda i, j: (i, 0, j)),
        ],
        out_specs=pl.BlockSpec((None, 8, tn), lambda i, j: (i, 0, j)),
        out_shape=jax.ShapeDtypeStruct((depth, 8, n), F32),
        compiler_params=_cparams(("parallel", "parallel")),
        name="ada_mod",
    )(cs, ada_w, ada_b.reshape(depth, 1, n))


def _norm_matmul_kernel(x_ref, g_ref, sh_ref, sc_ref, w_ref, o_ref, h_ref, *, act):
    @pl.when(pl.program_id(2) == 0)
    def _():
        xf = x_ref[...]
        ms = jnp.mean(xf * xf, axis=-1, keepdims=True)
        y = xf * lax.rsqrt(ms + EPS) * g_ref[...]
        h_ref[...] = (y * (1.0 + sc_ref[...]) + sh_ref[...]).astype(BF16)

    r = jnp.dot(h_ref[...], w_ref[...], preferred_element_type=F32)
    if act:
        r = jnp.square(jnp.maximum(r, 0.0))
    o_ref[...] = r.astype(o_ref.dtype)


def _norm_matmul(x, g, mod3, row_of_b, sh_idx, sc_idx, w, out_dtype, act, tm, tn):
    bsz, t, d = x.shape
    n = w.shape[1]
    tm = min(tm, t)
    return pl.pallas_call(
        functools.partial(_norm_matmul_kernel, act=act),
        grid=(bsz, t // tm, n // tn),
        in_specs=[
            pl.BlockSpec((None, tm, d), lambda b, m, j: (b, m, 0)),
            pl.BlockSpec((1, d), lambda b, m, j: (0, 0)),
            pl.BlockSpec((None, 1, d), lambda b, m, j: (row_of_b(b), 0, sh_idx)),
            pl.BlockSpec((None, 1, d), lambda b, m, j: (row_of_b(b), 0, sc_idx)),
            pl.BlockSpec((d, tn), lambda b, m, j: (0, j)),
        ],
        out_specs=pl.BlockSpec((None, tm, tn), lambda b, m, j: (b, m, j)),
        out_shape=jax.ShapeDtypeStruct((bsz, t, n), out_dtype),
        scratch_shapes=[pltpu.VMEM((tm, d), BF16)],
        compiler_params=_cparams(("parallel", "parallel", "arbitrary")),
        name="norm_matmul",
    )(x, g.reshape(1, d), mod3, mod3, w)


def _mm_res_kernel(a_ref, w_ref, x_ref, gate_ref, o_ref, acc_ref, *, nk):
    k = pl.program_id(3)

    @pl.when(k == 0)
    def _():
        acc_ref[...] = jnp.zeros_like(acc_ref)

    acc_ref[...] += jnp.dot(a_ref[...], w_ref[...], preferred_element_type=F32)

    @pl.when(k == nk - 1)
    def _():
        o_ref[...] = x_ref[...] + gate_ref[...] * acc_ref[...]


def _matmul_residual(a, w, x, mod3, row_of_b, gate_idx, tm, tn, tk):
    bsz, t, kdim = a.shape
    n = w.shape[1]
    tm = min(tm, t)
    nk = kdim // tk
    return pl.pallas_call(
        functools.partial(_mm_res_kernel, nk=nk),
        grid=(bsz, t // tm, n // tn, nk),
        in_specs=[
            pl.BlockSpec((None, tm, tk), lambda b, m, j, k: (b, m, k)),
            pl.BlockSpec((tk, tn), lambda b, m, j, k: (k, j)),
            pl.BlockSpec((None, tm, tn), lambda b, m, j, k: (b, m, j)),
            pl.BlockSpec((None, 1, tn), lambda b, m, j, k: (row_of_b(b), 0, gate_idx * (D_MODEL // tn) + j)),
        ],
        out_specs=pl.BlockSpec((None, tm, tn), lambda b, m, j, k: (b, m, j)),
        out_shape=jax.ShapeDtypeStruct(x.shape, F32),
        scratch_shapes=[pltpu.VMEM((tm, tn), F32)],
        compiler_params=_cparams(("parallel", "parallel", "parallel", "arbitrary")),
        name="matmul_residual",
    )(a, w, x, mod3)


def _final_norm_kernel(x_ref, g_ref, o_ref):
    xf = x_ref[...]
    ms = jnp.mean(xf * xf, axis=-1, keepdims=True)
    o_ref[...] = xf * lax.rsqrt(ms + EPS) * g_ref[...]


def _final_norm(x, g, tm=1024):
    bsz, t, d = x.shape
    tm = min(tm, t)
    return pl.pallas_call(
        _final_norm_kernel,
        grid=(bsz, t // tm),
        in_specs=[pl.BlockSpec((None, tm, d), lambda b, m: (b, m, 0)),
                  pl.BlockSpec((1, d), lambda b, m: (0, 0))],
        out_specs=pl.BlockSpec((None, tm, d), lambda b, m: (b, m, 0)),
        out_shape=jax.ShapeDtypeStruct(x.shape, F32),
        compiler_params=_cparams(("parallel", "parallel")),
        name="final_norm",
    )(x, g.reshape(1, d))


def _rope(x, cos, sin_signed, lane_lo):
    w = x.shape[-1]
    partner = jnp.where(lane_lo, pltpu.roll(x, w - 16, 1), pltpu.roll(x, 16, 1))
    return x * cos + partner * sin_signed


def _win_attn_kernel(sink_ref, q_ref, kc_ref, vc_ref, *rest, seq, local):
    if local:
        k_ref, v_ref, cos_ref, sin_ref, o_ref = rest
    else:
        (o_ref,) = rest
    n = pl.program_id(1)
    lane = lax.broadcasted_iota(jnp.int32, (BLOCK, 128), 1)
    lo = lane < HEAD_DIM
    q = q_ref[...].astype(F32) * SCALE
    if local:
        lane_lo = (lane % 32) < 16
        r0 = pl.multiple_of(n * BLOCK, BLOCK)
        cos_q = cos_ref[pl.ds(r0, BLOCK), :]
        sin_q = sin_ref[pl.ds(r0, BLOCK), :]
    rows = []
    for m in range(4):
        qm = q[:, 128 * m:128 * (m + 1)]
        if local:
            qm = _rope(qm, cos_q, sin_q, lane_lo)
        rows.append(jnp.where(lo, qm, 0.0))
        rows.append(jnp.where(lo, 0.0, qm))
    qbd = jnp.concatenate(rows, axis=0).astype(BF16)
    nt = (((1,), (1,)), ((), ()))
    kc = kc_ref[...].astype(BF16)
    vc = vc_ref[...].astype(BF16)
    s_ctx = lax.dot_general(qbd, kc, nt, preferred_element_type=F32)
    sink = jnp.concatenate(
        [jnp.full((BLOCK, 1), sink_ref[HEAD_ORDER[i]], F32) for i in range(N_HEADS)], axis=0)
    mx = jnp.maximum(jnp.max(s_ctx, axis=-1, keepdims=True), sink)
    if local:
        start = pl.multiple_of(jnp.clip((n - 1) * BLOCK, 0, seq - 3 * BLOCK), BLOCK)
        lane3 = lax.broadcasted_iota(jnp.int32, (3 * BLOCK, 128), 1)
        kb = _rope(k_ref[pl.ds(start, 3 * BLOCK), :].astype(F32), cos_ref[pl.ds(start, 3 * BLOCK), :],
                   sin_ref[pl.ds(start, 3 * BLOCK), :], (lane3 % 32) < 16).astype(BF16)
        vb = v_ref[pl.ds(start, 3 * BLOCK), :].astype(BF16)
        s_loc = lax.dot_general(qbd, kb, nt, preferred_element_type=F32)
        qi = lax.broadcasted_iota(jnp.int32, s_loc.shape, 0) & (BLOCK - 1)
        kj = lax.broadcasted_iota(jnp.int32, s_loc.shape, 1)
        rel = (start + kj) - (n * BLOCK + qi)
        s_loc = jnp.where(jnp.abs(rel) <= WINDOW, s_loc, NEG)
        mx = jnp.maximum(mx, jnp.max(s_loc, axis=-1, keepdims=True))
    p_ctx = jnp.exp(s_ctx - mx)
    den = jnp.sum(p_ctx, axis=-1, keepdims=True) + jnp.exp(sink - mx)
    acc = jnp.dot(p_ctx.astype(BF16), vc, preferred_element_type=F32)
    if local:
        p_loc = jnp.exp(s_loc - mx)
        den = den + jnp.sum(p_loc, axis=-1, keepdims=True)
        acc = acc + jnp.dot(p_loc.astype(BF16), vb, preferred_element_type=F32)
    o = acc / den
    outs = [jnp.where(lo, o[(2 * m) * BLOCK:(2 * m + 1) * BLOCK], o[(2 * m + 1) * BLOCK:(2 * m + 2) * BLOCK])
            for m in range(4)]
    o_ref[...] = jnp.concatenate(outs, axis=1).astype(o_ref.dtype)


def _win_attn(sink, pq, pc, cos_t, sin_t, local):
    bsz, t, _ = pq.shape
    lc = pc.shape[1]
    in_specs = [
        pl.BlockSpec(memory_space=pltpu.SMEM),
        pl.BlockSpec((None, BLOCK, 512), lambda b, n: (b, n, COL_QA // 512)),
        pl.BlockSpec((None, lc, 128), lambda b, n: (b, 0, COL_KA // 128)),
        pl.BlockSpec((None, lc, 128), lambda b, n: (b, 0, COL_VA // 128)),
    ]
    args = [sink, pq, pc, pc]
    if local:
        in_specs += [
            pl.BlockSpec((None, t, 128), lambda b, n: (b, 0, COL_KA // 128)),
            pl.BlockSpec((None, t, 128), lambda b, n: (b, 0, COL_VA // 128)),
            pl.BlockSpec((t, 128), lambda b, n: (0, 0)),
            pl.BlockSpec((t, 128), lambda b, n: (0, 0)),
        ]
        args += [pq, pq, cos_t, sin_t]
    return pl.pallas_call(
        functools.partial(_win_attn_kernel, seq=t, local=local),
        grid=(bsz, t // BLOCK),
        in_specs=in_specs,
        out_specs=pl.BlockSpec((None, BLOCK, GROUP_W), lambda b, n: (b, n, 0)),
        out_shape=jax.ShapeDtypeStruct((bsz, t, GROUP_W), BF16),
        compiler_params=_cparams(("parallel", "arbitrary")),
        name="win_attn" if local else "ctx_attn_a",
    )(*args)


def _rope_tables(seq):
    t = np.arange(seq)
    row, col = t // GRID_W, t % GRID_W
    quarter = HEAD_DIM // 4
    inv = ROPE_BASE ** (-np.arange(quarter, dtype=np.float64) / quarter)
    inv = inv.astype(np.float32).astype(np.float64)
    lane = np.arange(128)
    j = lane % HEAD_DIM
    pos = np.where((j < HEAD_DIM // 2)[None, :], row[:, None], col[:, None]).astype(np.float64)
    ang = (pos * inv[j % quarter][None, :]).astype(np.float32)
    cos = np.cos(ang.astype(np.float64))
    sin = np.sin(ang.astype(np.float64))
    sign = np.where((j % 32) < 16, -1.0, 1.0)[None, :]
    return jnp.asarray(cos, F32), jnp.asarray(sin * sign, F32)


NA_ROWS_PER_STEP = 2


def _na_kernel(var_ref, ws_ref, q_ref, kc_ref, vc_ref, *rest, local, win_rows):
    del var_ref
    if local:
        k_ref, v_ref, bias_ref, o_ref = rest
    else:
        (o_ref,) = rest
    g = pl.program_id(1)
    tq = q_ref.shape[0]
    q = q_ref[...].astype(F32) * SCALE
    head = lax.broadcasted_iota(jnp.int32, (tq, 256), 1) // HEAD_DIM
    nt = (((1,), (1,)), ((), ()))
    outs = []
    for half in range(2):
        cols = slice(256 * half, 256 * (half + 1))
        q4 = q[:, cols]
        qbd = jnp.concatenate([jnp.where(head == h, q4, 0.0) for h in range(4)], axis=0).astype(BF16)
        kc4 = kc_ref[:, cols].astype(BF16)
        vc4 = vc_ref[:, cols].astype(BF16)
        s_ctx = lax.dot_general(qbd, kc4, nt, preferred_element_type=F32)
        mx = jnp.max(s_ctx, axis=-1, keepdims=True)
        if local:
            nkey = win_rows * GRID_W
            start = pl.multiple_of(ws_ref[g] * GRID_W, GRID_W)
            k4 = k_ref[pl.ds(start, nkey), cols].astype(BF16)
            v4 = v_ref[pl.ds(start, nkey), cols].astype(BF16)
            bias = bias_ref[4 * half:4 * half + 4].astype(F32).reshape(4 * tq, nkey)
            s_loc = lax.dot_general(qbd, k4, nt, preferred_element_type=F32) + bias
            mx = jnp.maximum(mx, jnp.max(s_loc, axis=-1, keepdims=True))
        p_ctx = jnp.exp(s_ctx - mx)
        den = jnp.sum(p_ctx, axis=-1, keepdims=True)
        acc = jnp.dot(p_ctx.astype(BF16), vc4, preferred_element_type=F32)
        if local:
            p_loc = jnp.exp(s_loc - mx)
            den = den + jnp.sum(p_loc, axis=-1, keepdims=True)
            acc = acc + jnp.dot(p_loc.astype(BF16), v4, preferred_element_type=F32)
        o = acc / den
        o4 = jnp.where(head == 0, o[0:tq], 0.0)
        for h in range(1, 4):
            o4 = o4 + jnp.where(head == h, o[h * tq:(h + 1) * tq], 0.0)
        outs.append(o4)
    o_ref[...] = jnp.concatenate(outs, axis=1).astype(o_ref.dtype)


def _na_plan(seq):
    rows = seq // GRID_W
    kr = min(NA_KR, rows)
    r_step = NA_ROWS_PER_STEP
    win_rows = min(r_step + kr, rows)
    n_groups = rows // r_step
    wstart = np.zeros(n_groups, np.int32)
    pats = []
    keys = {}
    var = np.zeros(n_groups, np.int32)
    for g in range(n_groups):
        r0 = g * r_step
        ws = int(np.clip(r0 - kr // 2, 0, rows - win_rows))
        wstart[g] = ws
        r = r0 + np.arange(r_step)
        rstart = np.clip(r - kr // 2, 0, rows - kr)
        krow = ws + np.arange(win_rows)
        valid = (krow[None, :] >= rstart[:, None]) & (krow[None, :] < rstart[:, None] + kr)
        roff = krow[None, :] - r[:, None] + NA_KR - 1
        key = (valid.tobytes(), np.where(valid, roff, 0).tobytes())
        if key not in keys:
            keys[key] = len(pats)
            pats.append((valid, np.where(valid, roff, 0)))
        var[g] = keys[key]
    return rows, win_rows, n_groups, wstart, var, pats


def _na_bias_tables(rpb, pats):
    cq = np.arange(GRID_W)
    ck = np.arange(GRID_W)
    cstart = np.clip(cq - NA_KC // 2, 0, GRID_W - NA_KC)
    col_valid = (ck[None] >= cstart[:, None]) & (ck[None] < cstart[:, None] + NA_KC)
    coff = np.clip(ck[None] - cq[:, None], -(NA_KC - 1), NA_KC - 1) + NA_KC - 1
    tabs = []
    for valid, roff in pats:
        r_step, win_rows = valid.shape
        b = rpb[:, roff][:, :, :, coff]
        ok = valid[:, :, None, None] & col_valid[None, None]
        b = jnp.where(ok[None], b, NEG)
        b = b.transpose(0, 1, 3, 2, 4).reshape(rpb.shape[0], r_step * GRID_W, win_rows * GRID_W)
        tabs.append(b)
    return jnp.stack(tabs).astype(BF16)


def _na_attn(pq, pc, bias_tabs, plan, local):
    bsz, t, _ = pq.shape
    lc = pc.shape[1]
    if local:
        rows, win_rows, n_groups, wstart, var, _ = plan
        tq = NA_ROWS_PER_STEP * GRID_W
    else:
        win_rows, n_groups, tq = 0, 1, t
        wstart = np.zeros(1, np.int32)
        var = np.zeros(1, np.int32)
    in_specs = [
        pl.BlockSpec((None, tq, 512), lambda b, g, vr, ws: (b, g, COL_NQ // 512)),
        pl.BlockSpec((None, lc, 512), lambda b, g, vr, ws: (b, 0, COL_NK // 512)),
        pl.BlockSpec((None, lc, 512), lambda b, g, vr, ws: (b, 0, COL_NV // 512)),
    ]
    args = [pq, pc, pc]
    if local:
        in_specs += [
            pl.BlockSpec((None, t, 512), lambda b, g, vr, ws: (b, 0, COL_NK // 512)),
            pl.BlockSpec((None, t, 512), lambda b, g, vr, ws: (b, 0, COL_NV // 512)),
            pl.BlockSpec((None, N_HEADS, tq, win_rows * GRID_W), lambda b, g, vr, ws: (vr[g], 0, 0, 0)),
        ]
        args += [pq, pq, bias_tabs]
    grid_spec = pltpu.PrefetchScalarGridSpec(
        num_scalar_prefetch=2,
        grid=(bsz, n_groups),
        in_specs=in_specs,
        out_specs=pl.BlockSpec((None, tq, GROUP_W), lambda b, g, vr, ws: (b, g, 0)),
    )
    return pl.pallas_call(
        functools.partial(_na_kernel, local=local, win_rows=win_rows),
        grid_spec=grid_spec,
        out_shape=jax.ShapeDtypeStruct((bsz, t, GROUP_W), BF16),
        compiler_params=_cparams(("parallel", "arbitrary")),
        name="na_attn" if local else "ctx_attn_c",
    )(jnp.asarray(var), jnp.asarray(wstart), *args)


def _softplus(x):
    return jnp.maximum(x, 0.0) + jnp.log(1.0 + jnp.exp(-jnp.abs(x)))


def _conv3_silu(cur, prev_blk, next_blk, w_ref, b_ref, has_prev, has_next):
    x = cur.astype(F32)
    rows = x.shape[0]
    prev_row = prev_blk[HALO - 1:HALO, :].astype(F32) * has_prev
    next_row = next_blk[0:1, :].astype(F32) * has_next
    ri = lax.broadcasted_iota(jnp.int32, x.shape, 0)
    up = jnp.where(ri == 0, prev_row, pltpu.roll(x, 1, 0))
    dn = jnp.where(ri == rows - 1, next_row, pltpu.roll(x, rows - 1, 0))
    u = up * w_ref[0:1, :] + x * w_ref[1:2, :] + dn * w_ref[2:3, :] + b_ref[...]
    return _silu(u)


def _ssd_kernel(*refs, reverse, finalize, nc):
    (xs_ref, bc_ref, xsp_ref, xsn_ref, bcp_ref, bcn_ref, dt_ref, cwx_ref, cbx_ref, cwb_ref, cbb_ref,
     dtb_ref, alog_ref, h0_ref) = refs[:14]
    if finalize:
        z_ref, yf_ref, dskip_ref, nw_ref, y_ref, ht_ref, st_ref = refs[14:]
    else:
        y_ref, ht_ref, st_ref = refs[14:]
    c = pl.program_id(1)
    cid = (nc - 1 - c) if reverse else c
    d_off = 8 if reverse else 0
    hi = lax.Precision.HIGHEST
    L = SSM_CHUNK

    @pl.when(c == 0)
    def _():
        st_ref[...] = h0_ref[...]

    has_prev = (cid > 0).astype(F32)
    has_next = (cid < nc - 1).astype(F32)
    xs = _conv3_silu(xs_ref[...], xsp_ref[...], xsn_ref[...], cwx_ref, cbx_ref, has_prev, has_next)
    bc = _conv3_silu(bc_ref[...], bcp_ref[...], bcn_ref[...], cwb_ref, cbb_ref, has_prev, has_next)

    dt = _softplus(dt_ref[...].astype(F32) + dtb_ref[...])
    a = dt * (-jnp.exp(alog_ref[...]))
    ri = lax.broadcasted_iota(jnp.int32, (L, L), 0)
    ci = lax.broadcasted_iota(jnp.int32, (L, L), 1)
    keep = (ci >= ri) if reverse else (ci <= ri)
    tri = keep.astype(F32)
    c_col = jnp.dot(tri, a, preferred_element_type=F32, precision=hi)
    c_row = lax.dot_general(a.T, tri, (((1,), (1,)), ((), ())), preferred_element_type=F32, precision=hi)
    ej = lax.broadcasted_iota(jnp.int32, (128, GROUP_W), 0)
    eh = lax.broadcasted_iota(jnp.int32, (128, GROUP_W), 1) // HEAD_DIM
    expand = (ej == eh + d_off).astype(F32)
    c_exp = jnp.dot(c_col, expand, preferred_element_type=F32, precision=hi)
    dt_exp = jnp.dot(dt, expand, preferred_element_type=F32, precision=hi)
    end = 0 if reverse else L - 1
    cend = c_exp[end:end + 1, :]
    x_dt = xs * dt_exp
    out_decay = jnp.exp(c_exp)
    x_dec = x_dt * jnp.exp(cend - c_exp)
    chunk_decay = jnp.exp(cend)

    head4 = lax.broadcasted_iota(jnp.int32, (L, 256), 1) // HEAD_DIM
    nt = (((1,), (1,)), ((), ()))
    ys = []
    for g in range(2):
        gl = slice(256 * g, 256 * (g + 1))
        b_g = bc[:, 128 * g:128 * (g + 1)]
        c_g = bc[:, 256 + 128 * g:256 + 128 * (g + 1)].astype(BF16)
        cb = lax.dot_general(c_g, b_g.astype(BF16), nt, preferred_element_type=F32)
        ms = []
        for hh in range(4):
            j = d_off + 4 * g + hh
            diff = c_col[:, j:j + 1] - c_row[j:j + 1, :]
            ms.append(cb * jnp.exp(jnp.where(keep, diff, NEG)))
        m_g = jnp.concatenate(ms, axis=0).astype(BF16)
        o = jnp.dot(m_g, x_dt[:, gl].astype(BF16), preferred_element_type=F32)
        y_diag = jnp.where(head4 == 0, o[0:L], 0.0)
        for hh in range(1, 4):
            y_diag = y_diag + jnp.where(head4 == hh, o[hh * L:(hh + 1) * L], 0.0)
        st = st_ref[g]
        y_off = jnp.dot(c_g, st.astype(BF16), preferred_element_type=F32) * out_decay[:, gl]
        ys.append(y_diag + y_off)
        st_ref[g] = chunk_decay[:, gl] * st + jnp.dot(
            b_g.T.astype(BF16), x_dec[:, gl].astype(BF16), preferred_element_type=F32)
    y = jnp.concatenate(ys, axis=1)

    if finalize:
        y = y + yf_ref[...] + xs * dskip_ref[...]
        y = y * _silu(z_ref[...].astype(F32))
        halves = []
        for g in range(2):
            yg = y[:, 256 * g:256 * (g + 1)]
            halves.append(yg * lax.rsqrt(jnp.mean(yg * yg, axis=-1, keepdims=True) + EPS))
        y = jnp.concatenate(halves, axis=1) * nw_ref[...]
    y_ref[...] = y.astype(y_ref.dtype)

    @pl.when(c == nc - 1)
    def _():
        ht_ref[...] = st_ref[...]


def _ssd_direction(p, consts, h0, reverse, yf=None):
    cwx, cbx, cwb, cbb, dtb, alog, dskip, nw = consts
    bsz, t, _ = p.shape
    nc = t // SSM_CHUNK
    finalize = yf is not None
    hb = SSM_CHUNK // HALO
    nhalo = t // HALO

    def cid(c):
        return (nc - 1 - c) if reverse else c

    def cur(col, width):
        return pl.BlockSpec((None, SSM_CHUNK, width), lambda b, c: (b, cid(c), col // width))

    def prev(col, width):
        return pl.BlockSpec((None, HALO, width), lambda b, c: (b, jnp.maximum(cid(c) * hb - 1, 0), col // width))

    def nxt(col, width):
        return pl.BlockSpec((None, HALO, width), lambda b, c: (b, jnp.minimum((cid(c) + 1) * hb, nhalo - 1), col // width))

    def const(arr):
        return pl.BlockSpec(arr.shape, lambda b, c: (0,) * arr.ndim)

    state_spec = pl.BlockSpec((None, 2, SSM_STATE, 256), lambda b, c: (b, 0, 0, 0))
    in_specs = [cur(COL_SX, 512), cur(COL_SBC, 512), prev(COL_SX, 512), nxt(COL_SX, 512),
                prev(COL_SBC, 512), nxt(COL_SBC, 512), cur(COL_DT, 128),
                const(cwx), const(cbx), const(cwb), const(cbb), const(dtb), const(alog), state_spec]
    args = [p, p, p, p, p, p, p, cwx, cbx, cwb, cbb, dtb, alog, h0]
    if finalize:
        in_specs += [cur(COL_SZ, 512),
                     pl.BlockSpec((None, SSM_CHUNK, GROUP_W), lambda b, c: (b, cid(c), 0)),
                     const(dskip), const(nw)]
        args += [p, yf, dskip, nw]
    y, ht = pl.pallas_call(
        functools.partial(_ssd_kernel, reverse=reverse, finalize=finalize, nc=nc),
        grid=(bsz, nc),
        in_specs=in_specs,
        out_specs=[pl.BlockSpec((None, SSM_CHUNK, GROUP_W), lambda b, c: (b, cid(c), 0)), state_spec],
        out_shape=[jax.ShapeDtypeStruct((bsz, t, GROUP_W), BF16 if finalize else F32),
                   jax.ShapeDtypeStruct((bsz, 2, SSM_STATE, 256), F32)],
        scratch_shapes=[pltpu.VMEM((2, SSM_STATE, 256), F32)],
        compiler_params=_cparams(("parallel", "arbitrary")),
        name="ssd_rev" if reverse else "ssd_fwd",
    )(*args)
    return y, ht


def _ssd_consts(conv_w, conv_b, dt_bias, a_log, d_skip, norm_w):
    cwx, cwb = conv_w[:, :GROUP_W], conv_w[:, GROUP_W:]
    cbx, cbb = conv_b[:GROUP_W].reshape(1, -1), conv_b[GROUP_W:].reshape(1, -1)
    pad = lambda v: jnp.pad(v.reshape(1, -1), ((0, 0), (0, 128 - v.size)))
    return (cwx, cbx, cwb, cbb, pad(dt_bias), pad(a_log),
            jnp.repeat(d_skip, HEAD_DIM).reshape(1, -1), norm_w.reshape(1, -1))


def _ssd_mixer(p, pc, consts):
    bsz = p.shape[0]
    zero = jnp.zeros((bsz, 2, SSM_STATE, 256), F32)
    ycf, hf = _ssd_direction(pc, consts, zero, False)
    yc, hb = _ssd_direction(pc, consts, zero, True, yf=ycf)
    ylf, _ = _ssd_direction(p, consts, hf, False)
    yl, _ = _ssd_direction(p, consts, hb, True, yf=ylf)
    return yl, yc


HY_CB = 128


def _hy_dims(seq):
    n = 2 * seq
    n1 = {4096: 128, 1024: 64, 512: 32, 256: 32, 128: 16}[seq]
    n2 = n // n1
    h = n1 // 2
    return dict(n=n, n1=n1, n2=n2, h=h, pa=h + 8, pb=n2 + 8, pc=n1 + 8)


def _hy_matrices(seq):
    d = _hy_dims(seq)
    n, n1, n2, h = d["n"], d["n1"], d["n2"], d["h"]

    def cis(num, den, sign):
        ang = (2.0 * math.pi / den) * (num % den).astype(F32)
        return jnp.cos(ang), sign * jnp.sin(ang)

    k1 = jnp.arange(n1, dtype=jnp.int32)
    nn = n2 * jnp.arange(n1, dtype=jnp.int32)[None, None, :] + jnp.arange(n2, dtype=jnp.int32)[:, None, None]
    e1r, e1i = cis(k1[None, :, None] * nn, n, -1.0)
    m1f = jnp.concatenate([e1r, e1i], axis=1)
    m1 = jnp.concatenate([jnp.concatenate([e1r[..., :h], -e1i[..., :h]], axis=2),
                          jnp.concatenate([e1i[..., :h], e1r[..., :h]], axis=2)], axis=1)
    a2 = jnp.arange(n2, dtype=jnp.int32)
    g2r, g2i = cis(a2[:, None] * a2[None, :], n2, -1.0)
    m2 = jnp.concatenate([jnp.concatenate([g2r, -g2i], axis=1),
                          jnp.concatenate([g2i, g2r], axis=1)], axis=0)
    num3 = (a2[None, :, None] * a2[None, None, :] * n1 + k1[:, None, None] * a2[None, :, None])
    e3r, e3i = cis(num3, n, 1.0)
    m3 = jnp.concatenate([jnp.concatenate([e3r, -e3i], axis=2),
                          jnp.concatenate([e3i, e3r], axis=2)], axis=1)
    hh = jnp.arange(h, dtype=jnp.int32)
    d4r, d4i = cis(hh[:, None] * k1[None, :], n1, 1.0)
    m4 = jnp.concatenate([jnp.concatenate([d4r, -d4i], axis=1),
                          jnp.concatenate([d4i, d4r], axis=1)], axis=0) / n
    return tuple(m.astype(BF16) for m in (m1f, m1, m2, m3, m4))


def _hy_prep_kernel(p_ref, w_ref, b_ref, o_ref, *, dims):
    n2, h, pa = dims["n2"], dims["h"], dims["pa"]
    seq = h * n2
    o_ref[...] = jnp.zeros_like(o_ref)
    w0, w1, w2, bias = w_ref[0:1, :], w_ref[1:2, :], w_ref[2:3, :], b_ref[...]
    ri = lax.broadcasted_iota(jnp.int32, (n2, o_ref.shape[-1]), 0)

    def body(i, carry):
        r0 = pl.multiple_of(i * n2, n2)
        x = p_ref[pl.ds(r0, n2), :].astype(F32)
        pstart = pl.multiple_of(jnp.maximum(r0 - HALO, 0), HALO)
        nstart = pl.multiple_of(jnp.minimum(r0 + n2, seq - HALO), HALO)
        prev_row = p_ref[pl.ds(pstart, HALO), :][HALO - 1:HALO, :].astype(F32) * (i > 0).astype(F32)
        next_row = p_ref[pl.ds(nstart, HALO), :][0:1, :].astype(F32) * (i < h - 1).astype(F32)
        up = jnp.where(ri == 0, prev_row, pltpu.roll(x, 1, 0))
        dn = jnp.where(ri == n2 - 1, next_row, pltpu.roll(x, n2 - 1, 0))
        o_ref[pl.ds(i, n2, stride=pa), :] = up * w0 + x * w1 + dn * w2 + bias
        return carry

    lax.fori_loop(0, h, body, 0)


def _hy_prep(p, short_w, short_b):
    bsz, t, _ = p.shape
    dims = _hy_dims(t)
    rows = dims["n2"] * dims["pa"]
    ncb = 3 * GROUP_W // HY_CB
    return pl.pallas_call(
        functools.partial(_hy_prep_kernel, dims=dims),
        grid=(bsz, ncb),
        in_specs=[pl.BlockSpec((None, t, HY_CB), lambda b, j: (b, 0, COL_HY // HY_CB + j)),
                  pl.BlockSpec((3, HY_CB), lambda b, j: (0, j)),
                  pl.BlockSpec((1, HY_CB), lambda b, j: (0, j))],
        out_specs=pl.BlockSpec((None, rows, HY_CB), lambda b, j: (b, 0, j)),
        out_shape=jax.ShapeDtypeStruct((bsz, rows, 3 * GROUP_W), F32),
        compiler_params=_cparams(("parallel", "parallel")),
        name="hy_prep",
    )(p, short_w, short_b.reshape(1, -1))


def _hy_filter_kernel(z_ref, w1_ref, b1_ref, w2_ref, b2_ref, w3_ref, b3_ref, w4_ref, fr_ref, dl_ref, o_ref):
    hi = lax.Precision.HIGHEST
    z = z_ref[...]
    fr = fr_ref[...]
    h = jnp.sin(fr * (jnp.dot(z, w1_ref[...], preferred_element_type=F32, precision=hi) + b1_ref[...]))
    h = jnp.sin(fr * (jnp.dot(h, w2_ref[...], preferred_element_type=F32, precision=hi) + b2_ref[...]))
    h = jnp.sin(fr * (jnp.dot(h, w3_ref[...], preferred_element_type=F32, precision=hi) + b3_ref[...]))
    full = jnp.dot(h, w4_ref[...], preferred_element_type=F32, precision=hi)
    t = z[:, 0:1]
    is_bwd = z[:, HY_EMB:HY_EMB + 1] > 0.5
    live = z[:, HY_EMB + 1:HY_EMB + 2]
    decay = jnp.exp(-t * jnp.abs(dl_ref[...])) * live
    for o in range(2):
        fwd = full[:, o * 2 * GROUP_W:o * 2 * GROUP_W + GROUP_W]
        bwd = full[:, o * 2 * GROUP_W + GROUP_W:(o + 1) * 2 * GROUP_W]
        o_ref[o] = jnp.where(is_bwd, bwd, fwd) * decay


def _hy_filter_features(seq):
    d = _hy_dims(seq)
    n, n1, n2 = d["n"], d["n1"], d["n2"]
    row = np.arange(n)
    time = n2 * (row % n1) + row // n1
    is_bwd = time > seq
    pos = np.where(is_bwd, n - time, time)
    live = (time != seq).astype(np.float64)
    pos = np.where(time == seq, 0, pos)
    t = np.linspace(0.0, 1.0, seq)[pos]
    bands = (HY_EMB - 1) // 2
    f = np.linspace(1e-4, bands - 1, bands)[None]
    wpos = (2.0 * math.pi * pos / seq)[:, None]
    feat = np.zeros((n, 128), np.float64)
    feat[:, 0] = t
    feat[:, 1:1 + bands] = np.cos(f * wpos)
    feat[:, 1 + bands:HY_EMB] = -np.sin(f * wpos)
    feat[:, HY_EMB] = is_bwd
    feat[:, HY_EMB + 1] = live
    return jnp.asarray(feat, F32)


def _hy_filter(seq, w1, b1, w2, b2, w3, b3, w4, freq, tr=512):
    d = _hy_dims(seq)
    n = d["n"]
    tr = min(tr, n)
    feat = _hy_filter_features(seq)
    w1p = jnp.pad(w1, ((0, 128 - HY_EMB), (0, 0)))
    max_decay = math.log(HY_DECAY_TARGET) / HY_FAST_PCT
    min_decay = math.log(HY_DECAY_TARGET) / HY_SLOW_PCT
    deltas = jnp.linspace(min_decay, max_decay, GROUP_W, dtype=F32).reshape(1, -1)
    row = lambda v: v.reshape(1, -1)
    const = lambda a: pl.BlockSpec(a.shape, lambda i: (0,) * a.ndim)
    args = [w1p, row(b1), w2, row(b2), w3, row(b3), w4, row(freq), deltas]
    return pl.pallas_call(
        _hy_filter_kernel,
        grid=(n // tr,),
        in_specs=[pl.BlockSpec((tr, 128), lambda i: (i, 0))] + [const(a) for a in args],
        out_specs=pl.BlockSpec((2, tr, GROUP_W), lambda i: (0, i, 0)),
        out_shape=jax.ShapeDtypeStruct((2, n, GROUP_W), F32),
        compiler_params=_cparams(("parallel",)),
        name="hy_filter",
    )(feat, *args)


def _hy_spectrum_kernel(k_ref, m1f_ref, m2_ref, re_ref, im_ref, tr_ref, ti_ref, *, dims):
    n1, n2, pb = dims["n1"], dims["n2"], dims["pb"]

    def stage1(j, carry):
        r0 = pl.multiple_of(j * n1, n1)
        a = jnp.dot(m1f_ref[j], k_ref[pl.ds(r0, n1), :].astype(BF16), preferred_element_type=F32)
        tr_ref[pl.ds(j, n1, stride=pb), :] = a[:n1]
        ti_ref[pl.ds(j, n1, stride=pb), :] = a[n1:]
        return carry

    lax.fori_loop(0, n2, stage1, 0)

    def stage2(k, carry):
        r0 = pl.multiple_of(k * pb, 8)
        rhs = jnp.concatenate([tr_ref[pl.ds(r0, n2), :], ti_ref[pl.ds(r0, n2), :]], axis=0).astype(BF16)
        x = jnp.dot(m2_ref[...], rhs, preferred_element_type=F32)
        o0 = pl.multiple_of(k * n2, n2)
        re_ref[pl.ds(o0, n2), :] = x[:n2]
        im_ref[pl.ds(o0, n2), :] = x[n2:]
        return carry

    lax.fori_loop(0, n1, stage2, 0)


def _hy_spectrum(kern, mats, seq):
    d = _hy_dims(seq)
    n, n1, pb = d["n"], d["n1"], d["pb"]
    m1f, _, m2, _, _ = mats
    ncb = GROUP_W // HY_CB
    blk = pl.BlockSpec((None, n, HY_CB), lambda o, j: (o, 0, j))
    const = lambda a: pl.BlockSpec(a.shape, lambda o, j: (0,) * a.ndim)
    return pl.pallas_call(
        functools.partial(_hy_spectrum_kernel, dims=d),
        grid=(2, ncb),
        in_specs=[blk, const(m1f), const(m2)],
        out_specs=[blk, blk],
        out_shape=[jax.ShapeDtypeStruct((2, n, GROUP_W), F32)] * 2,
        scratch_shapes=[pltpu.VMEM((n1 * pb, HY_CB), F32)] * 2,
        compiler_params=_cparams(("parallel", "parallel")),
        name="hy_spectrum",
    )(kern, m1f, m2)


def _hy_conv_kernel(u_ref, g_ref, kr_ref, ki_ref, skip_ref, m1_ref, m2_ref, m3_ref, m4_ref, o_ref,
                    t1r, t1i, t2r, t2i, *, dims, natural_out):
    n1, n2, h, pa, pb, pc = (dims[k] for k in ("n1", "n2", "h", "pa", "pb", "pc"))

    def fwd1(j, carry):
        r0 = pl.multiple_of(j * pa, 8)
        rhs = jnp.concatenate([u_ref[0, pl.ds(r0, h), :], u_ref[1, pl.ds(r0, h), :]], axis=0).astype(BF16)
        a = jnp.dot(m1_ref[j], rhs, preferred_element_type=F32)
        t1r[pl.ds(j, n1, stride=pb), :] = a[:n1]
        t1i[pl.ds(j, n1, stride=pb), :] = a[n1:]
        return carry

    lax.fori_loop(0, n2, fwd1, 0)

    def mid(k, carry):
        r0 = pl.multiple_of(k * pb, 8)
        rhs = jnp.concatenate([t1r[pl.ds(r0, n2), :], t1i[pl.ds(r0, n2), :]], axis=0).astype(BF16)
        x = jnp.dot(m2_ref[...], rhs, preferred_element_type=F32)
        f0 = pl.multiple_of(k * n2, n2)
        fr, fi = kr_ref[pl.ds(f0, n2), :], ki_ref[pl.ds(f0, n2), :]
        xr, xi = x[:n2], x[n2:]
        y = jnp.concatenate([xr * fr - xi * fi, xr * fi + xi * fr], axis=0).astype(BF16)
        c = jnp.dot(m3_ref[k], y, preferred_element_type=F32)
        t2r[pl.ds(k, n2, stride=pc), :] = c[:n2]
        t2i[pl.ds(k, n2, stride=pc), :] = c[n2:]
        return carry

    lax.fori_loop(0, n1, mid, 0)

    if not natural_out:
        o_ref[...] = jnp.zeros_like(o_ref)
    skip = skip_ref[...]

    def inv2(j, carry):
        r0 = pl.multiple_of(j * pc, 8)
        rhs = jnp.concatenate([t2r[pl.ds(r0, n1), :], t2i[pl.ds(r0, n1), :]], axis=0).astype(BF16)
        y = jnp.dot(m4_ref[...], rhs, preferred_element_type=F32)
        a0 = pl.multiple_of(j * pa, 8)
        for e in range(2):
            val = g_ref[e, pl.ds(a0, h), :] * (y[e * h:(e + 1) * h] + skip * u_ref[e, pl.ds(a0, h), :])
            if natural_out:
                o_ref[e, pl.ds(j, h, stride=n2), :] = val.astype(o_ref.dtype)
            else:
                o_ref[e, pl.ds(a0, h), :] = val
        return carry

    lax.fori_loop(0, n2, inv2, 0)


def _hy_conv(u_arr, u_col, g_arr, g_col, kf_re, kf_im, order, skip, mats, seq, natural_out):
    d = _hy_dims(seq)
    bsz = u_arr.shape[0]
    rows = d["n2"] * d["pa"]
    _, m1, m2, m3, m4 = mats
    ncb = GROUP_W // HY_CB
    single = pl.Buffered(1)
    const = lambda a: pl.BlockSpec(a.shape, lambda p, j: (0,) * a.ndim, pipeline_mode=single)
    in_specs = [
        pl.BlockSpec((2, rows, HY_CB), lambda p, j: (p, 0, u_col // HY_CB + j), pipeline_mode=single),
        pl.BlockSpec((2, rows, HY_CB), lambda p, j: (p, 0, g_col // HY_CB + j), pipeline_mode=single),
        pl.BlockSpec((None, d["n"], HY_CB), lambda p, j: (order, 0, j), pipeline_mode=single),
        pl.BlockSpec((None, d["n"], HY_CB), lambda p, j: (order, 0, j), pipeline_mode=single),
        pl.BlockSpec((None, 1, HY_CB), lambda p, j: (order, 0, j)),
        const(m1), const(m2), const(m3), const(m4),
    ]
    if natural_out:
        out_spec = pl.BlockSpec((2, seq, HY_CB), lambda p, j: (p, 0, j), pipeline_mode=single)
        out_shape = jax.ShapeDtypeStruct((bsz, seq, GROUP_W), F32)
    else:
        out_spec = pl.BlockSpec((2, rows, HY_CB), lambda p, j: (p, 0, j), pipeline_mode=single)
        out_shape = jax.ShapeDtypeStruct((bsz, rows, GROUP_W), F32)
    return pl.pallas_call(
        functools.partial(_hy_conv_kernel, dims=d, natural_out=natural_out),
        grid=(bsz // 2, ncb),
        in_specs=in_specs,
        out_specs=out_spec,
        out_shape=out_shape,
        scratch_shapes=[pltpu.VMEM((d["n1"] * d["pb"], HY_CB), F32)] * 2
        + [pltpu.VMEM((d["n2"] * d["pc"], HY_CB), F32)] * 2,
        compiler_params=_cparams(("parallel", "parallel")),
        name="hy_conv",
    )(u_arr, g_arr, kf_re, kf_im, skip.reshape(2, 1, GROUP_W), m1, m2, m3, m4)


def _hyena_mixer(p, short_w, short_b, filt_params, skip, mats):
    seq = p.shape[1]
    kern = _hy_filter(seq, *filt_params)
    kf_re, kf_im = _hy_spectrum(kern, mats, seq)
    ut = _hy_prep(p, short_w, short_b)
    zt = _hy_conv(ut, 2 * GROUP_W, ut, 0, kf_re, kf_im, 0, skip, mats, seq, natural_out=False)
    return _hy_conv(zt, 0, ut, GROUP_W, kf_re, kf_im, 1, skip, mats, seq, natural_out=True)


def _prep_weights(w_in, w_out):
    d = w_in.shape[1]
    perm = list(HEAD_ORDER)
    q = w_in[..., :512].reshape(-1, d, N_HEADS, HEAD_DIM)[:, :, perm].reshape(-1, d, 512)
    o_a, o_hy, o_na, o_ssm = 768, 768 + 1536, 768 + 3072, 768 + 3072 + 1536 + 16
    pieces = [q, w_in[..., 512:o_a], w_in[..., o_ssm - 16:o_ssm],
              jnp.zeros(w_in.shape[:2] + (COL_HY - COL_DT - 16,), w_in.dtype),
              w_in[..., o_a:o_ssm - 16]]
    w_in_p = jnp.concatenate(pieces, axis=-1).astype(BF16)
    assert w_in_p.shape[-1] == N_PROJ
    wo_a = w_out[:, :512].reshape(-1, N_HEADS, HEAD_DIM, w_out.shape[-1])[:, perm].reshape(-1, 512, w_out.shape[-1])
    w_out_p = jnp.concatenate([wo_a, w_out[:, 512:]], axis=1).astype(BF16)
    return w_in_p, w_out_p


def kernel(x, c, ctx, c_ctx, ada_w, ada_b, norm_mix, norm_mlp, w_in, w_out, attn_sink,
           hy_short_w, hy_short_b, hy_w1, hy_b1, hy_w2, hy_b2, hy_w3, hy_b3, hy_w4, hy_freq, hy_skip,
           na_rpb, ssm_conv_w, ssm_conv_b, ssm_dt_bias, ssm_a_log, ssm_d, ssm_norm,
           mlp_w1, mlp_w2, final_norm):
    bsz, seq, d = x.shape
    lc = ctx.shape[1]
    depth = ada_w.shape[0]
    assert bsz % 2 == 0 and bsz <= 7 and d == D_MODEL

    cs = jnp.zeros((8, d), F32).at[:bsz].set(c).at[bsz].set(c_ctx)
    mod = _ada_mod(cs, ada_w, ada_b)
    w_in_p, w_out_p = _prep_weights(w_in, w_out)
    w1_b = mlp_w1.astype(BF16)
    w2_b = mlp_w2.astype(BF16)
    cos_t, sin_t = _rope_tables(seq)
    na_plan = _na_plan(seq)
    mats_l = _hy_matrices(seq)
    mats_c = _hy_matrices(lc)
    lat_row = lambda b: b
    ctx_row = lambda b: bsz

    xc = ctx
    for i in range(depth):
        last = i == depth - 1
        mod3 = mod[i].reshape(8, 1, 6 * d)
        p = _norm_matmul(x, norm_mix[i], mod3, lat_row, 0, 1, w_in_p[i], P_DTYPE, False, 1024, 512)
        pc = _norm_matmul(xc, norm_mix[i], mod3, ctx_row, 0, 1, w_in_p[i], P_DTYPE, False, 256, 512)

        filt_params = (hy_w1[i], hy_b1[i], hy_w2[i], hy_b2[i], hy_w3[i], hy_b3[i], hy_w4[i], hy_freq[i])
        ssd_consts = _ssd_consts(ssm_conv_w[i], ssm_conv_b[i], ssm_dt_bias[i], ssm_a_log[i], ssm_d[i], ssm_norm[i])
        bias_tabs = _na_bias_tables(na_rpb[i], na_plan[-1])

        ya = _win_attn(attn_sink[i], p, pc, cos_t, sin_t, local=True)
        yb = _hyena_mixer(p, hy_short_w[i], hy_short_b[i], filt_params, hy_skip[i], mats_l)
        yn = _na_attn(p, pc, bias_tabs, na_plan, local=True)
        yd, ydc = _ssd_mixer(p, pc, ssd_consts)
        y = jnp.concatenate([ya, yb.astype(BF16), yn, yd], axis=-1)
        x = _matmul_residual(y, w_out_p[i], x, mod3, lat_row, 2, 1024, 1024, 1024)
        hid = _norm_matmul(x, norm_mlp[i], mod3, lat_row, 3, 4, w1_b[i], BF16, True, 1024, 512)
        x = _matmul_residual(hid, w2_b[i], x, mod3, lat_row, 5, 1024, 1024, 1024)

        if not last:
            yac = _win_attn(attn_sink[i], pc, pc, None, None, local=False)
            ybc = _hyena_mixer(pc, hy_short_w[i], hy_short_b[i], filt_params, hy_skip[i], mats_c)
            ync = _na_attn(pc, pc, None, None, local=False)
            yc = jnp.concatenate([yac, ybc.astype(BF16), ync, ydc], axis=-1)
            xc = _matmul_residual(yc, w_out_p[i], xc, mod3, ctx_row, 2, 256, 1024, 1024)
            hidc = _norm_matmul(xc, norm_mlp[i], mod3, ctx_row, 3, 4, w1_b[i], BF16, True, 256, 512)
            xc = _matmul_residual(hidc, w2_b[i], xc, mod3, ctx_row, 5, 256, 1024, 1024)
    return _final_norm(x, final_norm)
```

```python
import functools
import math

import numpy as np
import jax
import jax.numpy as jnp
from jax import lax
from jax.experimental import pallas as pl
from jax.experimental.pallas import tpu as pltpu

F32 = jnp.float32
BF16 = jnp.bfloat16

D_MODEL = 2048
GRID_W = 64
EPS = 1e-6
NEG = -1e30
HEAD_DIM = 64
GROUP_W = D_MODEL // 4
N_HEADS = GROUP_W // HEAD_DIM
WINDOW = 128
BLOCK = 128
ROPE_BASE = 10000.0
HY_EMB = 33
HY_FFN = 64
HY_DECAY_TARGET = 1e-2
HY_FAST_PCT = 0.3
HY_SLOW_PCT = 1.5
NA_KR = 8
NA_KC = 16
SSM_STATE = 128
SSM_CHUNK = 128
D_FF = 4 * D_MODEL
SCALE = HEAD_DIM ** -0.5

COL_QA, COL_KA, COL_VA, COL_DT = 0, 512, 640, 768
COL_HY = 1024
COL_NQ, COL_NK, COL_NV = 2560, 3072, 3584
COL_SZ, COL_SX, COL_SBC = 4096, 4608, 5120
N_PROJ = 5632
HEAD_ORDER = (0, 4, 1, 5, 2, 6, 3, 7)

P_DTYPE = BF16
HALO = 16

V7X_VMEM_BYTES = 64 * 1024 * 1024
VMEM_LIMIT = 56 * 1024 * 1024


def _cparams(sem):
    return pltpu.CompilerParams(dimension_semantics=sem, vmem_limit_bytes=VMEM_LIMIT)


def _silu(x):
    return x * jax.nn.sigmoid(x)


def _ada_kernel(cs_ref, w_ref, b_ref, o_ref):
    a = _silu(cs_ref[...]).astype(BF16)
    o_ref[...] = jnp.dot(a, w_ref[...].astype(BF16), preferred_element_type=F32) + b_ref[...]


def _ada_mod(cs, ada_w, ada_b, tn=1024):
    depth, d, n = ada_w.shape
    return pl.pallas_call(
        _ada_kernel,
        grid=(depth, n // tn),
        in_specs=[
            pl.BlockSpec((8, d), lambda i, j: (0, 0)),
            pl.BlockSpec((None, d, tn), lambda i, j: (i, 0, j)),
            pl.BlockSpec((None, 1, tn), lambda i, j: (i, 0, j)),
        ],
        out_specs=pl.BlockSpec((None, 8, tn), lambda i, j: (i, 0, j)),
        out_shape=jax.ShapeDtypeStruct((depth, 8, n), F32),
        compiler_params=_cparams(("parallel", "parallel")),
        name="ada_mod",
    )(cs, ada_w, ada_b.reshape(depth, 1, n))


def _norm_matmul_kernel(x_ref, g_ref, sh_ref, sc_ref, w_ref, o_ref, h_ref, *, act):
    @pl.when(pl.program_id(2) == 0)
    def _():
        xf = x_ref[...]
        ms = jnp.mean(xf * xf, axis=-1, keepdims=True)
        y = xf * lax.rsqrt(ms + EPS) * g_ref[...]
        h_ref[...] = (y * (1.0 + sc_ref[...]) + sh_ref[...]).astype(BF16)

    r = jnp.dot(h_ref[...], w_ref[...], preferred_element_type=F32)
    if act:
        r = jnp.square(jnp.maximum(r, 0.0))
    o_ref[...] = r.astype(o_ref.dtype)


def _norm_matmul(x, g, mod3, row_of_b, sh_idx, sc_idx, w, out_dtype, act, tm, tn):
    bsz, t, d = x.shape
    n = w.shape[1]
    tm = min(tm, t)
    return pl.pallas_call(
        functools.partial(_norm_matmul_kernel, act=act),
        grid=(bsz, t // tm, n // tn),
        in_specs=[
            pl.BlockSpec((None, tm, d), lambda b, m, j: (b, m, 0)),
            pl.BlockSpec((1, d), lambda b, m, j: (0, 0)),
            pl.BlockSpec((None, 1, d), lambda b, m, j: (row_of_b(b), 0, sh_idx)),
            pl.BlockSpec((None, 1, d), lambda b, m, j: (row_of_b(b), 0, sc_idx)),
            pl.BlockSpec((d, tn), lambda b, m, j: (0, j)),
        ],
        out_specs=pl.BlockSpec((None, tm, tn), lambda b, m, j: (b, m, j)),
        out_shape=jax.ShapeDtypeStruct((bsz, t, n), out_dtype),
        scratch_shapes=[pltpu.VMEM((tm, d), BF16)],
        compiler_params=_cparams(("parallel", "parallel", "arbitrary")),
        name="norm_matmul",
    )(x, g.reshape(1, d), mod3, mod3, w)


def _mm_res_kernel(a_ref, w_ref, x_ref, gate_ref, o_ref, acc_ref, *, nk):
    k = pl.program_id(3)

    @pl.when(k == 0)
    def _():
        acc_ref[...] = jnp.zeros_like(acc_ref)

    acc_ref[...] += jnp.dot(a_ref[...], w_ref[...], preferred_element_type=F32)

    @pl.when(k == nk - 1)
    def _():
        o_ref[...] = x_ref[...] + gate_ref[...] * acc_ref[...]


def _matmul_residual(a, w, x, mod3, row_of_b, gate_idx, tm, tn, tk):
    bsz, t, kdim = a.shape
    n = w.shape[1]
    tm = min(tm, t)
    nk = kdim // tk
    return pl.pallas_call(
        functools.partial(_mm_res_kernel, nk=nk),
        grid=(bsz, t // tm, n // tn, nk),
        in_specs=[
            pl.BlockSpec((None, tm, tk), lambda b, m, j, k: (b, m, k)),
            pl.BlockSpec((tk, tn), lambda b, m, j, k: (k, j)),
            pl.BlockSpec((None, tm, tn), lambda b, m, j, k: (b, m, j)),
            pl.BlockSpec((None, 1, tn), lambda b, m, j, k: (row_of_b(b), 0, gate_idx * (D_MODEL // tn) + j)),
        ],
        out_specs=pl.BlockSpec((None, tm, tn), lambda b, m, j, k: (b, m, j)),
        out_shape=jax.ShapeDtypeStruct(x.shape, F32),
        scratch_shapes=[pltpu.VMEM((tm, tn), F32)],
        compiler_params=_cparams(("parallel", "parallel", "parallel", "arbitrary")),
        name="matmul_residual",
    )(a, w, x, mod3)


def _final_norm_kernel(x_ref, g_ref, o_ref):
    xf = x_ref[...]
    ms = jnp.mean(xf * xf, axis=-1, keepdims=True)
    o_ref[...] = xf * lax.rsqrt(ms + EPS) * g_ref[...]


def _final_norm(x, g, tm=1024):
    bsz, t, d = x.shape
    tm = min(tm, t)
    return pl.pallas_call(
        _final_norm_kernel,
        grid=(bsz, t // tm),
        in_specs=[pl.BlockSpec((None, tm, d), lambda b, m: (b, m, 0)),
                  pl.BlockSpec((1, d), lambda b, m: (0, 0))],
        out_specs=pl.BlockSpec((None, tm, d), lambda b, m: (b, m, 0)),
        out_shape=jax.ShapeDtypeStruct(x.shape, F32),
        compiler_params=_cparams(("parallel", "parallel")),
        name="final_norm",
    )(x, g.reshape(1, d))


def _rope(x, cos, sin_signed, lane_lo):
    w = x.shape[-1]
    partner = jnp.where(lane_lo, pltpu.roll(x, w - 16, 1), pltpu.roll(x, 16, 1))
    return x * cos + partner * sin_signed


def _win_attn_kernel(sink_ref, q_ref, kc_ref, vc_ref, *rest, seq, local):
    if local:
        k_ref, v_ref, cos_ref, sin_ref, o_ref = rest
    else:
        (o_ref,) = rest
    n = pl.program_id(1)
    lane = lax.broadcasted_iota(jnp.int32, (BLOCK, 128), 1)
    lo = lane < HEAD_DIM
    q = q_ref[...].astype(F32) * SCALE
    if local:
        lane_lo = (lane % 32) < 16
        r0 = pl.multiple_of(n * BLOCK, BLOCK)
        cos_q = cos_ref[pl.ds(r0, BLOCK), :]
        sin_q = sin_ref[pl.ds(r0, BLOCK), :]
    rows = []
    for m in range(4):
        qm = q[:, 128 * m:128 * (m + 1)]
        if local:
            qm = _rope(qm, cos_q, sin_q, lane_lo)
        rows.append(jnp.where(lo, qm, 0.0))
        rows.append(jnp.where(lo, 0.0, qm))
    qbd = jnp.concatenate(rows, axis=0).astype(BF16)
    nt = (((1,), (1,)), ((), ()))
    kc = kc_ref[...].astype(BF16)
    vc = vc_ref[...].astype(BF16)
    s_ctx = lax.dot_general(qbd, kc, nt, preferred_element_type=F32)
    sink = jnp.concatenate(
        [jnp.full((BLOCK, 1), sink_ref[HEAD_ORDER[i]], F32) for i in range(N_HEADS)], axis=0)
    mx = jnp.maximum(jnp.max(s_ctx, axis=-1, keepdims=True), sink)
    if local:
        start = pl.multiple_of(jnp.clip((n - 1) * BLOCK, 0, seq - 3 * BLOCK), BLOCK)
        lane3 = lax.broadcasted_iota(jnp.int32, (3 * BLOCK, 128), 1)
        kb = _rope(k_ref[pl.ds(start, 3 * BLOCK), :].astype(F32), cos_ref[pl.ds(start, 3 * BLOCK), :],
                   sin_ref[pl.ds(start, 3 * BLOCK), :], (lane3 % 32) < 16).astype(BF16)
        vb = v_ref[pl.ds(start, 3 * BLOCK), :].astype(BF16)
        s_loc = lax.dot_general(qbd, kb, nt, preferred_element_type=F32)
        qi = lax.broadcasted_iota(jnp.int32, s_loc.shape, 0) & (BLOCK - 1)
        kj = lax.broadcasted_iota(jnp.int32, s_loc.shape, 1)
        rel = (start + kj) - (n * BLOCK + qi)
        s_loc = jnp.where(jnp.abs(rel) <= WINDOW, s_loc, NEG)
        mx = jnp.maximum(mx, jnp.max(s_loc, axis=-1, keepdims=True))
    p_ctx = jnp.exp(s_ctx - mx)
    den = jnp.sum(p_ctx, axis=-1, keepdims=True) + jnp.exp(sink - mx)
    acc = jnp.dot(p_ctx.astype(BF16), vc, preferred_element_type=F32)
    if local:
        p_loc = jnp.exp(s_loc - mx)
        den = den + jnp.sum(p_loc, axis=-1, keepdims=True)
        acc = acc + jnp.dot(p_loc.astype(BF16), vb, preferred_element_type=F32)
    o = acc / den
    outs = [jnp.where(lo, o[(2 * m) * BLOCK:(2 * m + 1) * BLOCK], o[(2 * m + 1) * BLOCK:(2 * m + 2) * BLOCK])
            for m in range(4)]
    o_ref[...] = jnp.concatenate(outs, axis=1).astype(o_ref.dtype)


def _win_attn(sink, pq, pc, cos_t, sin_t, local):
    bsz, t, _ = pq.shape
    lc = pc.shape[1]
    in_specs = [
        pl.BlockSpec(memory_space=pltpu.SMEM),
        pl.BlockSpec((None, BLOCK, 512), lambda b, n: (b, n, COL_QA // 512)),
        pl.BlockSpec((None, lc, 128), lambda b, n: (b, 0, COL_KA // 128)),
        pl.BlockSpec((None, lc, 128), lambda b, n: (b, 0, COL_VA // 128)),
    ]
    args = [sink, pq, pc, pc]
    if local:
        in_specs += [
            pl.BlockSpec((None, t, 128), lambda b, n: (b, 0, COL_KA // 128)),
            pl.BlockSpec((None, t, 128), lambda b, n: (b, 0, COL_VA // 128)),
            pl.BlockSpec((t, 128), lambda b, n: (0, 0)),
            pl.BlockSpec((t, 128), lambda b, n: (0, 0)),
        ]
        args += [pq, pq, cos_t, sin_t]
    return pl.pallas_call(
        functools.partial(_win_attn_kernel, seq=t, local=local),
        grid=(bsz, t // BLOCK),
        in_specs=in_specs,
        out_specs=pl.BlockSpec((None, BLOCK, GROUP_W), lambda b, n: (b, n, 0)),
        out_shape=jax.ShapeDtypeStruct((bsz, t, GROUP_W), BF16),
        compiler_params=_cparams(("parallel", "arbitrary")),
        name="win_attn" if local else "ctx_attn_a",
    )(*args)


def _rope_tables(seq):
    t = np.arange(seq)
    row, col = t // GRID_W, t % GRID_W
    quarter = HEAD_DIM // 4
    inv = ROPE_BASE ** (-np.arange(quarter, dtype=np.float64) / quarter)
    inv = inv.astype(np.float32).astype(np.float64)
    lane = np.arange(128)
    j = lane % HEAD_DIM
    pos = np.where((j < HEAD_DIM // 2)[None, :], row[:, None], col[:, None]).astype(np.float64)
    ang = (pos * inv[j % quarter][None, :]).astype(np.float32)
    cos = np.cos(ang.astype(np.float64))
    sin = np.sin(ang.astype(np.float64))
    sign = np.where((j % 32) < 16, -1.0, 1.0)[None, :]
    return jnp.asarray(cos, F32), jnp.asarray(sin * sign, F32)


NA_ROWS_PER_STEP = 2


def _na_kernel(var_ref, ws_ref, q_ref, kc_ref, vc_ref, *rest, local, win_rows):
    del var_ref
    if local:
        k_ref, v_ref, bias_ref, o_ref = rest
    else:
        (o_ref,) = rest
    g = pl.program_id(1)
    tq = q_ref.shape[0]
    q = q_ref[...].astype(F32) * SCALE
    head = lax.broadcasted_iota(jnp.int32, (tq, 256), 1) // HEAD_DIM
    nt = (((1,), (1,)), ((), ()))
    outs = []
    for half in range(2):
        cols = slice(256 * half, 256 * (half + 1))
        q4 = q[:, cols]
        qbd = jnp.concatenate([jnp.where(head == h, q4, 0.0) for h in range(4)], axis=0).astype(BF16)
        kc4 = kc_ref[:, cols].astype(BF16)
        vc4 = vc_ref[:, cols].astype(BF16)
        s_ctx = lax.dot_general(qbd, kc4, nt, preferred_element_type=F32)
        mx = jnp.max(s_ctx, axis=-1, keepdims=True)
        if local:
            nkey = win_rows * GRID_W
            start = pl.multiple_of(ws_ref[g] * GRID_W, GRID_W)
            k4 = k_ref[pl.ds(start, nkey), cols].astype(BF16)
            v4 = v_ref[pl.ds(start, nkey), cols].astype(BF16)
            bias = bias_ref[4 * half:4 * half + 4].astype(F32).reshape(4 * tq, nkey)
            s_loc = lax.dot_general(qbd, k4, nt, preferred_element_type=F32) + bias
            mx = jnp.maximum(mx, jnp.max(s_loc, axis=-1, keepdims=True))
        p_ctx = jnp.exp(s_ctx - mx)
        den = jnp.sum(p_ctx, axis=-1, keepdims=True)
        acc = jnp.dot(p_ctx.astype(BF16), vc4, preferred_element_type=F32)
        if local:
            p_loc = jnp.exp(s_loc - mx)
            den = den + jnp.sum(p_loc, axis=-1, keepdims=True)
            acc = acc + jnp.dot(p_loc.astype(BF16), v4, preferred_element_type=F32)
        o = acc / den
        o4 = jnp.where(head == 0, o[0:tq], 0.0)
        for h in range(1, 4):
            o4 = o4 + jnp.where(head == h, o[h * tq:(h + 1) * tq], 0.0)
        outs.append(o4)
    o_ref[...] = jnp.concatenate(outs, axis=1).astype(o_ref.dtype)


def _na_plan(seq):
    rows = seq // GRID_W
    kr = min(NA_KR, rows)
    r_step = NA_ROWS_PER_STEP
    win_rows = min(r_step + kr, rows)
    n_groups = rows // r_step
    wstart = np.zeros(n_groups, np.int32)
    pats = []
    keys = {}
    var = np.zeros(n_groups, np.int32)
    for g in range(n_groups):
        r0 = g * r_step
        ws = int(np.clip(r0 - kr // 2, 0, rows - win_rows))
        wstart[g] = ws
        r = r0 + np.arange(r_step)
        rstart = np.clip(r - kr // 2, 0, rows - kr)
        krow = ws + np.arange(win_rows)
        valid = (krow[None, :] >= rstart[:, None]) & (krow[None, :] < rstart[:, None] + kr)
        roff = krow[None, :] - r[:, None] + NA_KR - 1
        key = (valid.tobytes(), np.where(valid, roff, 0).tobytes())
        if key not in keys:
            keys[key] = len(pats)
            pats.append((valid, np.where(valid, roff, 0)))
        var[g] = keys[key]
    return rows, win_rows, n_groups, wstart, var, pats


def _na_bias_tables(rpb, pats):
    cq = np.arange(GRID_W)
    ck = np.arange(GRID_W)
    cstart = np.clip(cq - NA_KC // 2, 0, GRID_W - NA_KC)
    col_valid = (ck[None] >= cstart[:, None]) & (ck[None] < cstart[:, None] + NA_KC)
    coff = np.clip(ck[None] - cq[:, None], -(NA_KC - 1), NA_KC - 1) + NA_KC - 1
    by_col = jnp.where(col_valid, rpb[..., coff], NEG).astype(BF16)
    masked = jnp.full(by_col.shape[:2] + (GRID_W, GRID_W), NEG, BF16)
    tabs = []
    for valid, roff in pats:
        r_step, win_rows = valid.shape
        q_rows = []
        for i in range(r_step):
            blocks = [by_col[:, :, int(roff[i, a])] if valid[i, a] else masked for a in range(win_rows)]
            q_rows.append(jnp.concatenate(blocks, axis=-1))
        tabs.append(jnp.concatenate(q_rows, axis=-2))
    return jnp.stack(tabs, axis=1)


def _na_attn(pq, pc, bias_tabs, layer, plan, local):
    bsz, t, _ = pq.shape
    lc = pc.shape[1]
    if local:
        rows, win_rows, n_groups, wstart, var, _ = plan
        tq = NA_ROWS_PER_STEP * GRID_W
    else:
        win_rows, n_groups, tq = 0, 1, t
        wstart = np.zeros(1, np.int32)
        var = np.zeros(1, np.int32)
    in_specs = [
        pl.BlockSpec((None, tq, 512), lambda b, g, vr, ws: (b, g, COL_NQ // 512)),
        pl.BlockSpec((None, lc, 512), lambda b, g, vr, ws: (b, 0, COL_NK // 512)),
        pl.BlockSpec((None, lc, 512), lambda b, g, vr, ws: (b, 0, COL_NV // 512)),
    ]
    args = [pq, pc, pc]
    if local:
        in_specs += [
            pl.BlockSpec((None, t, 512), lambda b, g, vr, ws: (b, 0, COL_NK // 512)),
            pl.BlockSpec((None, t, 512), lambda b, g, vr, ws: (b, 0, COL_NV // 512)),
            pl.BlockSpec((None, None, N_HEADS, tq, win_rows * GRID_W),
                         lambda b, g, vr, ws: (layer, vr[g], 0, 0, 0)),
        ]
        args += [pq, pq, bias_tabs]
    grid_spec = pltpu.PrefetchScalarGridSpec(
        num_scalar_prefetch=2,
        grid=(bsz, n_groups),
        in_specs=in_specs,
        out_specs=pl.BlockSpec((None, tq, GROUP_W), lambda b, g, vr, ws: (b, g, 0)),
    )
    return pl.pallas_call(
        functools.partial(_na_kernel, local=local, win_rows=win_rows),
        grid_spec=grid_spec,
        out_shape=jax.ShapeDtypeStruct((bsz, t, GROUP_W), BF16),
        compiler_params=_cparams(("parallel", "arbitrary")),
        name="na_attn" if local else "ctx_attn_c",
    )(jnp.asarray(var), jnp.asarray(wstart), *args)


def _softplus(x):
    return jnp.maximum(x, 0.0) + jnp.log(1.0 + jnp.exp(-jnp.abs(x)))


def _conv3_silu(cur, prev_blk, next_blk, w_ref, b_ref, has_prev, has_next):
    x = cur.astype(F32)
    rows = x.shape[0]
    prev_row = prev_blk.astype(F32)[HALO - 1:HALO, :] * has_prev
    next_row = next_blk.astype(F32)[0:1, :] * has_next
    ri = lax.broadcasted_iota(jnp.int32, x.shape, 0)
    up = jnp.where(ri == 0, prev_row, pltpu.roll(x, 1, 0))
    dn = jnp.where(ri == rows - 1, next_row, pltpu.roll(x, rows - 1, 0))
    u = up * w_ref[0:1, :] + x * w_ref[1:2, :] + dn * w_ref[2:3, :] + b_ref[...]
    return _silu(u)


def _ssd_kernel(*refs, reverse, finalize, nc):
    (xs_ref, bc_ref, xsp_ref, xsn_ref, bcp_ref, bcn_ref, dt_ref, cwx_ref, cbx_ref, cwb_ref, cbb_ref,
     dtb_ref, alog_ref, h0_ref) = refs[:14]
    if finalize:
        z_ref, yf_ref, dskip_ref, nw_ref, y_ref, ht_ref, st_ref = refs[14:]
    else:
        y_ref, ht_ref, st_ref = refs[14:]
    c = pl.program_id(1)
    cid = (nc - 1 - c) if reverse else c
    d_off = 8 if reverse else 0
    hi = lax.Precision.HIGHEST
    L = SSM_CHUNK

    @pl.when(c == 0)
    def _():
        st_ref[...] = h0_ref[...]

    has_prev = (cid > 0).astype(F32)
    has_next = (cid < nc - 1).astype(F32)
    xs = _conv3_silu(xs_ref[...], xsp_ref[...], xsn_ref[...], cwx_ref, cbx_ref, has_prev, has_next)
    bc = _conv3_silu(bc_ref[...], bcp_ref[...], bcn_ref[...], cwb_ref, cbb_ref, has_prev, has_next)

    dt = _softplus(dt_ref[...].astype(F32) + dtb_ref[...])
    a = dt * (-jnp.exp(alog_ref[...]))
    ri = lax.broadcasted_iota(jnp.int32, (L, L), 0)
    ci = lax.broadcasted_iota(jnp.int32, (L, L), 1)
    keep = (ci >= ri) if reverse else (ci <= ri)
    tri = keep.astype(F32)
    c_col = jnp.dot(tri, a, preferred_element_type=F32, precision=hi)
    c_row = lax.dot_general(a.T, tri, (((1,), (1,)), ((), ())), preferred_element_type=F32, precision=hi)
    ej = lax.broadcasted_iota(jnp.int32, (128, GROUP_W), 0)
    eh = lax.broadcasted_iota(jnp.int32, (128, GROUP_W), 1) // HEAD_DIM
    expand = (ej == eh + d_off).astype(F32)
    c_exp = jnp.dot(c_col, expand, preferred_element_type=F32, precision=hi)
    dt_exp = jnp.dot(dt, expand, preferred_element_type=F32, precision=hi)
    end = 0 if reverse else L - 1
    cend = c_exp[end:end + 1, :]
    x_dt = xs * dt_exp
    out_decay = jnp.exp(c_exp)
    x_dec = x_dt * jnp.exp(cend - c_exp)
    chunk_decay = jnp.exp(cend)

    head4 = lax.broadcasted_iota(jnp.int32, (L, 256), 1) // HEAD_DIM
    nt = (((1,), (1,)), ((), ()))
    ys = []
    for g in range(2):
        gl = slice(256 * g, 256 * (g + 1))
        b_g = bc[:, 128 * g:128 * (g + 1)]
        c_g = bc[:, 256 + 128 * g:256 + 128 * (g + 1)].astype(BF16)
        cb = lax.dot_general(c_g, b_g.astype(BF16), nt, preferred_element_type=F32)
        ms = []
        for hh in range(4):
            j = d_off + 4 * g + hh
            diff = c_col[:, j:j + 1] - c_row[j:j + 1, :]
            ms.append(cb * jnp.exp(jnp.where(keep, diff, NEG)))
        m_g = jnp.concatenate(ms, axis=0).astype(BF16)
        o = jnp.dot(m_g, x_dt[:, gl].astype(BF16), preferred_element_type=F32)
        y_diag = jnp.where(head4 == 0, o[0:L], 0.0)
        for hh in range(1, 4):
            y_diag = y_diag + jnp.where(head4 == hh, o[hh * L:(hh + 1) * L], 0.0)
        st = st_ref[g]
        y_off = jnp.dot(c_g, st.astype(BF16), preferred_element_type=F32) * out_decay[:, gl]
        ys.append(y_diag + y_off)
        st_ref[g] = chunk_decay[:, gl] * st + jnp.dot(
            b_g.T.astype(BF16), x_dec[:, gl].astype(BF16), preferred_element_type=F32)
    y = jnp.concatenate(ys, axis=1)

    if finalize:
        y = y + yf_ref[...] + xs * dskip_ref[...]
        y = y * _silu(z_ref[...].astype(F32))
        halves = []
        for g in range(2):
            yg = y[:, 256 * g:256 * (g + 1)]
            halves.append(yg * lax.rsqrt(jnp.mean(yg * yg, axis=-1, keepdims=True) + EPS))
        y = jnp.concatenate(halves, axis=1) * nw_ref[...]
    y_ref[...] = y.astype(y_ref.dtype)

    @pl.when(c == nc - 1)
    def _():
        ht_ref[...] = st_ref[...]


def _ssd_direction(p, consts, h0, reverse, yf=None):
    cwx, cbx, cwb, cbb, dtb, alog, dskip, nw = consts
    bsz, t, _ = p.shape
    nc = t // SSM_CHUNK
    finalize = yf is not None
    hb = SSM_CHUNK // HALO
    nhalo = t // HALO

    def cid(c):
        return (nc - 1 - c) if reverse else c

    def cur(col, width):
        return pl.BlockSpec((None, SSM_CHUNK, width), lambda b, c: (b, cid(c), col // width))

    def prev(col, width):
        return pl.BlockSpec((None, HALO, width), lambda b, c: (b, jnp.maximum(cid(c) * hb - 1, 0), col // width))

    def nxt(col, width):
        return pl.BlockSpec((None, HALO, width), lambda b, c: (b, jnp.minimum((cid(c) + 1) * hb, nhalo - 1), col // width))

    def const(arr):
        return pl.BlockSpec(arr.shape, lambda b, c: (0,) * arr.ndim)

    state_spec = pl.BlockSpec((None, 2, SSM_STATE, 256), lambda b, c: (b, 0, 0, 0))
    in_specs = [cur(COL_SX, 512), cur(COL_SBC, 512), prev(COL_SX, 512), nxt(COL_SX, 512),
                prev(COL_SBC, 512), nxt(COL_SBC, 512), cur(COL_DT, 128),
                const(cwx), const(cbx), const(cwb), const(cbb), const(dtb), const(alog), state_spec]
    args = [p, p, p, p, p, p, p, cwx, cbx, cwb, cbb, dtb, alog, h0]
    if finalize:
        in_specs += [cur(COL_SZ, 512),
                     pl.BlockSpec((None, SSM_CHUNK, GROUP_W), lambda b, c: (b, cid(c), 0)),
                     const(dskip), const(nw)]
        args += [p, yf, dskip, nw]
    y, ht = pl.pallas_call(
        functools.partial(_ssd_kernel, reverse=reverse, finalize=finalize, nc=nc),
        grid=(bsz, nc),
        in_specs=in_specs,
        out_specs=[pl.BlockSpec((None, SSM_CHUNK, GROUP_W), lambda b, c: (b, cid(c), 0)), state_spec],
        out_shape=[jax.ShapeDtypeStruct((bsz, t, GROUP_W), BF16 if finalize else F32),
                   jax.ShapeDtypeStruct((bsz, 2, SSM_STATE, 256), F32)],
        scratch_shapes=[pltpu.VMEM((2, SSM_STATE, 256), F32)],
        compiler_params=_cparams(("parallel", "arbitrary")),
        name="ssd_rev" if reverse else "ssd_fwd",
    )(*args)
    return y, ht


def _ssd_consts(conv_w, conv_b, dt_bias, a_log, d_skip, norm_w):
    cwx, cwb = conv_w[:, :GROUP_W], conv_w[:, GROUP_W:]
    cbx, cbb = conv_b[:GROUP_W].reshape(1, -1), conv_b[GROUP_W:].reshape(1, -1)
    pad = lambda v: jnp.pad(v.reshape(1, -1), ((0, 0), (0, 128 - v.size)))
    return (cwx, cbx, cwb, cbb, pad(dt_bias), pad(a_log),
            jnp.repeat(d_skip, HEAD_DIM).reshape(1, -1), norm_w.reshape(1, -1))


def _ssd_mixer(p, pc, consts):
    bsz = p.shape[0]
    zero = jnp.zeros((bsz, 2, SSM_STATE, 256), F32)
    ycf, hf = _ssd_direction(pc, consts, zero, False)
    yc, hb = _ssd_direction(pc, consts, zero, True, yf=ycf)
    ylf, _ = _ssd_direction(p, consts, hf, False)
    yl, _ = _ssd_direction(p, consts, hb, True, yf=ylf)
    return yl, yc


HY_CB = 128
HY_UNROLL = 16


def _hy_dims(seq):
    n = 2 * seq
    n1 = {4096: 128, 1024: 64, 512: 32, 256: 32, 128: 16}[seq]
    n2 = n // n1
    h = n1 // 2
    return dict(n=n, n1=n1, n2=n2, h=h, pa=h + 8, pb=n2 + 8, pc=n1 + 8)


def _hy_matrices(seq):
    d = _hy_dims(seq)
    n, n1, n2, h = d["n"], d["n1"], d["n2"], d["h"]

    def cis(num, den, sign):
        ang = (2.0 * math.pi / den) * (num % den).astype(F32)
        return jnp.cos(ang), sign * jnp.sin(ang)

    k1 = jnp.arange(n1, dtype=jnp.int32)
    nn = n2 * jnp.arange(n1, dtype=jnp.int32)[None, None, :] + jnp.arange(n2, dtype=jnp.int32)[:, None, None]
    e1r, e1i = cis(k1[None, :, None] * nn, n, -1.0)
    m1f = jnp.concatenate([e1r, e1i], axis=1)
    m1 = jnp.concatenate([jnp.concatenate([e1r[..., :h], -e1i[..., :h]], axis=2),
                          jnp.concatenate([e1i[..., :h], e1r[..., :h]], axis=2)], axis=1)
    a2 = jnp.arange(n2, dtype=jnp.int32)
    g2r, g2i = cis(a2[:, None] * a2[None, :], n2, -1.0)
    m2 = jnp.concatenate([jnp.concatenate([g2r, -g2i], axis=1),
                          jnp.concatenate([g2i, g2r], axis=1)], axis=0)
    num3 = (a2[None, :, None] * a2[None, None, :] * n1 + k1[:, None, None] * a2[None, :, None])
    e3r, e3i = cis(num3, n, 1.0)
    m3 = jnp.concatenate([jnp.concatenate([e3r, -e3i], axis=2),
                          jnp.concatenate([e3i, e3r], axis=2)], axis=1)
    hh = jnp.arange(h, dtype=jnp.int32)
    d4r, d4i = cis(hh[:, None] * k1[None, :], n1, 1.0)
    m4 = jnp.concatenate([jnp.concatenate([d4r, -d4i], axis=1),
                          jnp.concatenate([d4i, d4r], axis=1)], axis=0) / n
    return tuple(m.astype(BF16) for m in (m1f, m1, m2, m3, m4))


def _hy_prep_kernel(p_ref, w_ref, b_ref, o_ref, *, dims):
    n2, h, pa = dims["n2"], dims["h"], dims["pa"]
    seq = h * n2
    o_ref[...] = jnp.zeros_like(o_ref)
    w0, w1, w2, bias = w_ref[0:1, :], w_ref[1:2, :], w_ref[2:3, :], b_ref[...]
    ri = lax.broadcasted_iota(jnp.int32, (n2, o_ref.shape[-1]), 0)

    def body(i, carry):
        r0 = pl.multiple_of(i * n2, n2)
        x = p_ref[pl.ds(r0, n2), :].astype(F32)
        pstart = pl.multiple_of(jnp.maximum(r0 - HALO, 0), HALO)
        nstart = pl.multiple_of(jnp.minimum(r0 + n2, seq - HALO), HALO)
        prev_row = p_ref[pl.ds(pstart, HALO), :].astype(F32)[HALO - 1:HALO, :] * (i > 0).astype(F32)
        next_row = p_ref[pl.ds(nstart, HALO), :].astype(F32)[0:1, :] * (i < h - 1).astype(F32)
        up = jnp.where(ri == 0, prev_row, pltpu.roll(x, 1, 0))
        dn = jnp.where(ri == n2 - 1, next_row, pltpu.roll(x, n2 - 1, 0))
        o_ref[pl.ds(i, n2, stride=pa), :] = up * w0 + x * w1 + dn * w2 + bias
        return carry

    lax.fori_loop(0, h, body, 0)


def _hy_prep(p, short_w, short_b):
    bsz, t, _ = p.shape
    dims = _hy_dims(t)
    rows = dims["n2"] * dims["pa"]
    ncb = 3 * GROUP_W // HY_CB
    return pl.pallas_call(
        functools.partial(_hy_prep_kernel, dims=dims),
        grid=(bsz, ncb),
        in_specs=[pl.BlockSpec((None, t, HY_CB), lambda b, j: (b, 0, COL_HY // HY_CB + j)),
                  pl.BlockSpec((3, HY_CB), lambda b, j: (0, j)),
                  pl.BlockSpec((1, HY_CB), lambda b, j: (0, j))],
        out_specs=pl.BlockSpec((None, rows, HY_CB), lambda b, j: (b, 0, j)),
        out_shape=jax.ShapeDtypeStruct((bsz, rows, 3 * GROUP_W), F32),
        compiler_params=_cparams(("parallel", "parallel")),
        name="hy_prep",
    )(p, short_w, short_b.reshape(1, -1))


def _hy_filter_kernel(z_ref, w1_ref, b1_ref, w2_ref, b2_ref, w3_ref, b3_ref, w4_ref, fr_ref, dl_ref, o_ref):
    hi = lax.Precision.HIGHEST
    z = z_ref[...]
    fr = fr_ref[...]
    h = jnp.sin(fr * (jnp.dot(z, w1_ref[...], preferred_element_type=F32, precision=hi) + b1_ref[...]))
    h = jnp.sin(fr * (jnp.dot(h, w2_ref[...], preferred_element_type=F32, precision=hi) + b2_ref[...]))
    h = jnp.sin(fr * (jnp.dot(h, w3_ref[...], preferred_element_type=F32, precision=hi) + b3_ref[...]))
    full = jnp.dot(h, w4_ref[...], preferred_element_type=F32, precision=hi)
    t = z[:, 0:1]
    is_bwd = z[:, HY_EMB:HY_EMB + 1] > 0.5
    live = z[:, HY_EMB + 1:HY_EMB + 2]
    decay = jnp.exp(-t * jnp.abs(dl_ref[...])) * live
    for o in range(2):
        fwd = full[:, o * 2 * GROUP_W:o * 2 * GROUP_W + GROUP_W]
        bwd = full[:, o * 2 * GROUP_W + GROUP_W:(o + 1) * 2 * GROUP_W]
        o_ref[o] = jnp.where(is_bwd, bwd, fwd) * decay


def _hy_filter_features(seq):
    d = _hy_dims(seq)
    n, n1, n2 = d["n"], d["n1"], d["n2"]
    row = np.arange(n)
    time = n2 * (row % n1) + row // n1
    is_bwd = time > seq
    pos = np.where(is_bwd, n - time, time)
    live = (time != seq).astype(np.float64)
    pos = np.where(time == seq, 0, pos)
    t = np.linspace(0.0, 1.0, seq)[pos]
    bands = (HY_EMB - 1) // 2
    f = np.linspace(1e-4, bands - 1, bands)[None]
    wpos = (2.0 * math.pi * pos / seq)[:, None]
    feat = np.zeros((n, 128), np.float64)
    feat[:, 0] = t
    feat[:, 1:1 + bands] = np.cos(f * wpos)
    feat[:, 1 + bands:HY_EMB] = -np.sin(f * wpos)
    feat[:, HY_EMB] = is_bwd
    feat[:, HY_EMB + 1] = live
    return jnp.asarray(feat, F32)


def _hy_filter(seq, w1, b1, w2, b2, w3, b3, w4, freq, tr=512):
    d = _hy_dims(seq)
    n = d["n"]
    tr = min(tr, n)
    feat = _hy_filter_features(seq)
    w1p = jnp.pad(w1, ((0, 128 - HY_EMB), (0, 0)))
    max_decay = math.log(HY_DECAY_TARGET) / HY_FAST_PCT
    min_decay = math.log(HY_DECAY_TARGET) / HY_SLOW_PCT
    deltas = jnp.linspace(min_decay, max_decay, GROUP_W, dtype=F32).reshape(1, -1)
    row = lambda v: v.reshape(1, -1)
    const = lambda a: pl.BlockSpec(a.shape, lambda i: (0,) * a.ndim)
    args = [w1p, row(b1), w2, row(b2), w3, row(b3), w4, row(freq), deltas]
    return pl.pallas_call(
        _hy_filter_kernel,
        grid=(n // tr,),
        in_specs=[pl.BlockSpec((tr, 128), lambda i: (i, 0))] + [const(a) for a in args],
        out_specs=pl.BlockSpec((2, tr, GROUP_W), lambda i: (0, i, 0)),
        out_shape=jax.ShapeDtypeStruct((2, n, GROUP_W), F32),
        compiler_params=_cparams(("parallel",)),
        name="hy_filter",
    )(feat, *args)


def _hy_spectrum_kernel(k_ref, m1f_ref, m2_ref, re_ref, im_ref, tr_ref, ti_ref, *, dims):
    n1, n2, pb = dims["n1"], dims["n2"], dims["pb"]

    def stage1(j, carry):
        r0 = pl.multiple_of(j * n1, n1)
        a = jnp.dot(m1f_ref[j], k_ref[pl.ds(r0, n1), :].astype(BF16), preferred_element_type=F32)
        tr_ref[pl.ds(j, n1, stride=pb), :] = a[:n1]
        ti_ref[pl.ds(j, n1, stride=pb), :] = a[n1:]
        return carry

    lax.fori_loop(0, n2, stage1, 0, unroll=HY_UNROLL)

    def stage2(k, carry):
        r0 = pl.multiple_of(k * pb, 8)
        rhs = jnp.concatenate([tr_ref[pl.ds(r0, n2), :], ti_ref[pl.ds(r0, n2), :]], axis=0).astype(BF16)
        x = jnp.dot(m2_ref[...], rhs, preferred_element_type=F32)
        o0 = pl.multiple_of(k * n2, n2)
        re_ref[pl.ds(o0, n2), :] = x[:n2]
        im_ref[pl.ds(o0, n2), :] = x[n2:]
        return carry

    lax.fori_loop(0, n1, stage2, 0, unroll=HY_UNROLL)


def _hy_spectrum(kern, mats, seq):
    d = _hy_dims(seq)
    n, n1, pb = d["n"], d["n1"], d["pb"]
    m1f, _, m2, _, _ = mats
    ncb = GROUP_W // HY_CB
    blk = pl.BlockSpec((None, n, HY_CB), lambda o, j: (o, 0, j))
    const = lambda a: pl.BlockSpec(a.shape, lambda o, j: (0,) * a.ndim)
    return pl.pallas_call(
        functools.partial(_hy_spectrum_kernel, dims=d),
        grid=(2, ncb),
        in_specs=[blk, const(m1f), const(m2)],
        out_specs=[blk, blk],
        out_shape=[jax.ShapeDtypeStruct((2, n, GROUP_W), F32)] * 2,
        scratch_shapes=[pltpu.VMEM((n1 * pb, HY_CB), F32)] * 2,
        compiler_params=_cparams(("parallel", "parallel")),
        name="hy_spectrum",
    )(kern, m1f, m2)


def _hy_conv_kernel(u_ref, g_ref, kr_ref, ki_ref, skip_ref, m1_ref, m2_ref, m3_ref, m4_ref, o_ref,
                    t1r, t1i, t2r, t2i, *, dims, natural_out):
    n1, n2, h, pa, pb, pc = (dims[k] for k in ("n1", "n2", "h", "pa", "pb", "pc"))

    def fwd1(j, carry):
        r0 = pl.multiple_of(j * pa, 8)
        rhs = jnp.concatenate([u_ref[0, pl.ds(r0, h), :], u_ref[1, pl.ds(r0, h), :]], axis=0).astype(BF16)
        a = jnp.dot(m1_ref[j], rhs, preferred_element_type=F32)
        t1r[pl.ds(j, n1, stride=pb), :] = a[:n1]
        t1i[pl.ds(j, n1, stride=pb), :] = a[n1:]
        return carry

    lax.fori_loop(0, n2, fwd1, 0, unroll=HY_UNROLL)

    def mid(k, carry):
        r0 = pl.multiple_of(k * pb, 8)
        rhs = jnp.concatenate([t1r[pl.ds(r0, n2), :], t1i[pl.ds(r0, n2), :]], axis=0).astype(BF16)
        x = jnp.dot(m2_ref[...], rhs, preferred_element_type=F32)
        f0 = pl.multiple_of(k * n2, n2)
        fr, fi = kr_ref[pl.ds(f0, n2), :], ki_ref[pl.ds(f0, n2), :]
        xr, xi = x[:n2], x[n2:]
        y = jnp.concatenate([xr * fr - xi * fi, xr * fi + xi * fr], axis=0).astype(BF16)
        c = jnp.dot(m3_ref[k], y, preferred_element_type=F32)
        t2r[pl.ds(k, n2, stride=pc), :] = c[:n2]
        t2i[pl.ds(k, n2, stride=pc), :] = c[n2:]
        return carry

    lax.fori_loop(0, n1, mid, 0, unroll=HY_UNROLL)

    if not natural_out:
        o_ref[...] = jnp.zeros_like(o_ref)
    skip = skip_ref[...]

    def inv2(j, carry):
        r0 = pl.multiple_of(j * pc, 8)
        rhs = jnp.concatenate([t2r[pl.ds(r0, n1), :], t2i[pl.ds(r0, n1), :]], axis=0).astype(BF16)
        y = jnp.dot(m4_ref[...], rhs, preferred_element_type=F32)
        a0 = pl.multiple_of(j * pa, 8)
        for e in range(2):
            val = g_ref[e, pl.ds(a0, h), :] * (y[e * h:(e + 1) * h] + skip * u_ref[e, pl.ds(a0, h), :])
            if natural_out:
                o_ref[e, pl.ds(j, h, stride=n2), :] = val.astype(o_ref.dtype)
            else:
                o_ref[e, pl.ds(a0, h), :] = val
        return carry

    lax.fori_loop(0, n2, inv2, 0, unroll=HY_UNROLL)


def _hy_conv(u_arr, u_col, g_arr, g_col, kf_re, kf_im, order, skip, mats, seq, natural_out):
    d = _hy_dims(seq)
    bsz = u_arr.shape[0]
    rows = d["n2"] * d["pa"]
    _, m1, m2, m3, m4 = mats
    ncb = GROUP_W // HY_CB
    single = pl.Buffered(1)
    const = lambda a: pl.BlockSpec(a.shape, lambda j, p: (0,) * a.ndim, pipeline_mode=single)
    in_specs = [
        pl.BlockSpec((2, rows, HY_CB), lambda j, p: (p, 0, u_col // HY_CB + j), pipeline_mode=single),
        pl.BlockSpec((2, rows, HY_CB), lambda j, p: (p, 0, g_col // HY_CB + j), pipeline_mode=single),
        pl.BlockSpec((None, d["n"], HY_CB), lambda j, p: (order, 0, j), pipeline_mode=single),
        pl.BlockSpec((None, d["n"], HY_CB), lambda j, p: (order, 0, j), pipeline_mode=single),
        pl.BlockSpec((None, 1, HY_CB), lambda j, p: (order, 0, j)),
        const(m1), const(m2), const(m3), const(m4),
    ]
    if natural_out:
        out_spec = pl.BlockSpec((2, seq, HY_CB), lambda j, p: (p, 0, j), pipeline_mode=single)
        out_shape = jax.ShapeDtypeStruct((bsz, seq, GROUP_W), F32)
    else:
        out_spec = pl.BlockSpec((2, rows, HY_CB), lambda j, p: (p, 0, j), pipeline_mode=single)
        out_shape = jax.ShapeDtypeStruct((bsz, rows, GROUP_W), F32)
    return pl.pallas_call(
        functools.partial(_hy_conv_kernel, dims=d, natural_out=natural_out),
        grid=(ncb, bsz // 2),
        in_specs=in_specs,
        out_specs=out_spec,
        out_shape=out_shape,
        scratch_shapes=[pltpu.VMEM((d["n1"] * d["pb"], HY_CB), F32)] * 2
        + [pltpu.VMEM((d["n2"] * d["pc"], HY_CB), F32)] * 2,
        compiler_params=_cparams(("parallel", "parallel")),
        name="hy_conv",
    )(u_arr, g_arr, kf_re, kf_im, skip.reshape(2, 1, GROUP_W), m1, m2, m3, m4)


def _hyena_mixer(p, short_w, short_b, filt_params, skip, mats):
    seq = p.shape[1]
    kern = _hy_filter(seq, *filt_params)
    kf_re, kf_im = _hy_spectrum(kern, mats, seq)
    ut = _hy_prep(p, short_w, short_b)
    zt = _hy_conv(ut, 2 * GROUP_W, ut, 0, kf_re, kf_im, 0, skip, mats, seq, natural_out=False)
    return _hy_conv(zt, 0, ut, GROUP_W, kf_re, kf_im, 1, skip, mats, seq, natural_out=True)


def _prep_layer_weights(w_in, w_out):
    d = w_in.shape[0]
    perm = np.asarray(HEAD_ORDER)
    q = w_in[:, :512].reshape(d, N_HEADS, HEAD_DIM)[:, perm].reshape(d, 512)
    o_a, o_ssm = 768, 768 + 3072 + 1536 + 16
    pieces = [q, w_in[:, 512:o_a], w_in[:, o_ssm - 16:o_ssm],
              jnp.zeros((d, COL_HY - COL_DT - 16), w_in.dtype), w_in[:, o_a:o_ssm - 16]]
    w_in_p = jnp.concatenate(pieces, axis=-1).astype(BF16)
    assert w_in_p.shape[-1] == N_PROJ
    wo_a = w_out[:512].reshape(N_HEADS, HEAD_DIM, w_out.shape[-1])[perm].reshape(512, w_out.shape[-1])
    w_out_p = jnp.concatenate([wo_a, w_out[512:]], axis=0).astype(BF16)
    return w_in_p, w_out_p


TILE_W_IN = (1024, 1408)
TILE_MLP_UP = (1024, 2048)
TILE_DOWN = (1024, 1024, 2048)


def kernel(x, c, ctx, c_ctx, ada_w, ada_b, norm_mix, norm_mlp, w_in, w_out, attn_sink,
           hy_short_w, hy_short_b, hy_w1, hy_b1, hy_w2, hy_b2, hy_w3, hy_b3, hy_w4, hy_freq, hy_skip,
           na_rpb, ssm_conv_w, ssm_conv_b, ssm_dt_bias, ssm_a_log, ssm_d, ssm_norm,
           mlp_w1, mlp_w2, final_norm):
    bsz, seq, d = x.shape
    lc = ctx.shape[1]
    depth = ada_w.shape[0]
    assert bsz % 2 == 0 and bsz <= 7 and d == D_MODEL

    cs = jnp.zeros((8, d), F32).at[:bsz].set(c).at[bsz].set(c_ctx)
    mod = _ada_mod(cs, ada_w, ada_b)
    cos_t, sin_t = _rope_tables(seq)
    na_plan = _na_plan(seq)
    bias_tabs = _na_bias_tables(na_rpb, na_plan[-1])
    mats_l = _hy_matrices(seq)
    mats_c = _hy_matrices(lc)
    lat_row = lambda b: b
    ctx_row = lambda b: bsz
    tm_c = lc

    xc = ctx
    for i in range(depth):
        last = i == depth - 1
        mod3 = mod[i].reshape(8, 1, 6 * d)
        w_in_p, w_out_p = _prep_layer_weights(w_in[i], w_out[i])
        w1_b = mlp_w1[i].astype(BF16)
        w2_b = mlp_w2[i].astype(BF16)
        p = _norm_matmul(x, norm_mix[i], mod3, lat_row, 0, 1, w_in_p, P_DTYPE, False, *TILE_W_IN)
        pc = _norm_matmul(xc, norm_mix[i], mod3, ctx_row, 0, 1, w_in_p, P_DTYPE, False, tm_c, TILE_W_IN[1])

        filt_params = (hy_w1[i], hy_b1[i], hy_w2[i], hy_b2[i], hy_w3[i], hy_b3[i], hy_w4[i], hy_freq[i])
        ssd_consts = _ssd_consts(ssm_conv_w[i], ssm_conv_b[i], ssm_dt_bias[i], ssm_a_log[i], ssm_d[i], ssm_norm[i])

        ya = _win_attn(attn_sink[i], p, pc, cos_t, sin_t, local=True)
        yb = _hyena_mixer(p, hy_short_w[i], hy_short_b[i], filt_params, hy_skip[i], mats_l)
        yn = _na_attn(p, pc, bias_tabs, i, na_plan, local=True)
        yd, ydc = _ssd_mixer(p, pc, ssd_consts)
        y = jnp.concatenate([ya, yb.astype(BF16), yn, yd], axis=-1)
        x = _matmul_residual(y, w_out_p, x, mod3, lat_row, 2, *TILE_DOWN)
        hid = _norm_matmul(x, norm_mlp[i], mod3, lat_row, 3, 4, w1_b, BF16, True, *TILE_MLP_UP)
        x = _matmul_residual(hid, w2_b, x, mod3, lat_row, 5, *TILE_DOWN)

        if not last:
            yac = _win_attn(attn_sink[i], pc, pc, None, None, local=False)
            ybc = _hyena_mixer(pc, hy_short_w[i], hy_short_b[i], filt_params, hy_skip[i], mats_c)
            ync = _na_attn(pc, pc, None, i, None, local=False)
            yc = jnp.concatenate([yac, ybc.astype(BF16), ync, ydc], axis=-1)
            xc = _matmul_residual(yc, w_out_p, xc, mod3, ctx_row, 2, tm_c, *TILE_DOWN[1:])
            hidc = _norm_matmul(xc, norm_mlp[i], mod3, ctx_row, 3, 4, w1_b, BF16, True, tm_c, TILE_MLP_UP[1])
            xc = _matmul_residual(hidc, w2_b, xc, mod3, ctx_row, 5, tm_c, *TILE_DOWN[1:])
    return _final_norm(x, final_norm)
```

```python
import functools
import math

import numpy as np
import jax
import jax.numpy as jnp
from jax import lax
from jax.experimental import pallas as pl
from jax.experimental.pallas import tpu as pltpu

F32 = jnp.float32
BF16 = jnp.bfloat16

D_MODEL = 2048
GRID_W = 64
EPS = 1e-6
NEG = -1e30
HEAD_DIM = 64
GROUP_W = D_MODEL // 4
N_HEADS = GROUP_W // HEAD_DIM
WINDOW = 128
BLOCK = 128
ROPE_BASE = 10000.0
HY_EMB = 33
HY_FFN = 64
HY_DECAY_TARGET = 1e-2
HY_FAST_PCT = 0.3
HY_SLOW_PCT = 1.5
NA_KR = 8
NA_KC = 16
SSM_STATE = 128
SSM_CHUNK = 128
D_FF = 4 * D_MODEL
SCALE = HEAD_DIM ** -0.5
LOG2E = math.log2(math.e)

COL_QA, COL_KA, COL_VA, COL_DT = 0, 512, 640, 768
COL_HY = 1024
COL_NQ, COL_NK, COL_NV = 2560, 3072, 3584
COL_SZ, COL_SX, COL_SBC = 4096, 4608, 5120
N_PROJ = 5632
HEAD_ORDER = (0, 4, 1, 5, 2, 6, 3, 7)

P_DTYPE = BF16
HALO = 16

V7X_VMEM_BYTES = 64 * 1024 * 1024
VMEM_LIMIT = 56 * 1024 * 1024


def _cparams(sem):
    return pltpu.CompilerParams(dimension_semantics=sem, vmem_limit_bytes=VMEM_LIMIT)


def _silu(x):
    return x * jax.nn.sigmoid(x)


def _ada_kernel(cs_ref, w_ref, b_ref, o_ref):
    a = _silu(cs_ref[...]).astype(BF16)
    o_ref[...] = jnp.dot(a, w_ref[...].astype(BF16), preferred_element_type=F32) + b_ref[...]


def _ada_mod(cs, ada_w, ada_b, tn=1024):
    depth, d, n = ada_w.shape
    return pl.pallas_call(
        _ada_kernel,
        grid=(depth, n // tn),
        in_specs=[
            pl.BlockSpec((8, d), lambda i, j: (0, 0)),
            pl.BlockSpec((None, d, tn), lambda i, j: (i, 0, j)),
            pl.BlockSpec((None, 1, tn), lambda i, j: (i, 0, j)),
        ],
        out_specs=pl.BlockSpec((None, 8, tn), lambda i, j: (i, 0, j)),
        out_shape=jax.ShapeDtypeStruct((depth, 8, n), F32),
        compiler_params=_cparams(("parallel", "parallel")),
        name="ada_mod",
    )(cs, ada_w, ada_b.reshape(depth, 1, n))


def _norm_matmul_kernel(x_ref, g_ref, sh_ref, sc_ref, w_ref, o_ref, h_ref, *, act):
    @pl.when(pl.program_id(2) == 0)
    def _():
        xf = x_ref[...]
        ms = jnp.mean(xf * xf, axis=-1, keepdims=True)
        y = xf * lax.rsqrt(ms + EPS) * g_ref[...]
        h_ref[...] = (y * (1.0 + sc_ref[...]) + sh_ref[...]).astype(BF16)

    r = jnp.dot(h_ref[...], w_ref[...], preferred_element_type=F32)
    if act:
        r = jnp.square(jnp.maximum(r, 0.0))
    o_ref[...] = r.astype(o_ref.dtype)


def _norm_matmul(x, g, mod3, row_of_b, sh_idx, sc_idx, w, layer, out_dtype, act, tm, tn):
    bsz, t, d = x.shape
    n = w.shape[2]
    tm = min(tm, t)
    return pl.pallas_call(
        functools.partial(_norm_matmul_kernel, act=act),
        grid=(bsz, t // tm, n // tn),
        in_specs=[
            pl.BlockSpec((None, tm, d), lambda b, m, j: (b, m, 0)),
            pl.BlockSpec((1, d), lambda b, m, j: (0, 0)),
            pl.BlockSpec((None, 1, d), lambda b, m, j: (row_of_b(b), 0, sh_idx)),
            pl.BlockSpec((None, 1, d), lambda b, m, j: (row_of_b(b), 0, sc_idx)),
            pl.BlockSpec((None, d, tn), lambda b, m, j: (layer, 0, j)),
        ],
        out_specs=pl.BlockSpec((None, tm, tn), lambda b, m, j: (b, m, j)),
        out_shape=jax.ShapeDtypeStruct((bsz, t, n), out_dtype),
        scratch_shapes=[pltpu.VMEM((tm, d), BF16)],
        compiler_params=_cparams(("parallel", "parallel", "arbitrary")),
        name="norm_matmul",
    )(x, g.reshape(1, d), mod3, mod3, w)


def _mm_res_kernel(a_ref, w_ref, x_ref, gate_ref, o_ref, acc_ref, *, nk):
    k = pl.program_id(3)

    @pl.when(k == 0)
    def _():
        acc_ref[...] = jnp.zeros_like(acc_ref)

    acc_ref[...] += jnp.dot(a_ref[...], w_ref[...], preferred_element_type=F32)

    @pl.when(k == nk - 1)
    def _():
        o_ref[...] = x_ref[...] + gate_ref[...] * acc_ref[...]


def _matmul_residual(a, w, layer, x, mod3, row_of_b, gate_idx, tm, tn, tk):
    bsz, t, kdim = a.shape
    n = w.shape[2]
    tm = min(tm, t)
    nk = kdim // tk
    return pl.pallas_call(
        functools.partial(_mm_res_kernel, nk=nk),
        grid=(bsz, t // tm, n // tn, nk),
        in_specs=[
            pl.BlockSpec((None, tm, tk), lambda b, m, j, k: (b, m, k)),
            pl.BlockSpec((None, tk, tn), lambda b, m, j, k: (layer, k, j)),
            pl.BlockSpec((None, tm, tn), lambda b, m, j, k: (b, m, j)),
            pl.BlockSpec((None, 1, tn), lambda b, m, j, k: (row_of_b(b), 0, gate_idx * (D_MODEL // tn) + j)),
        ],
        out_specs=pl.BlockSpec((None, tm, tn), lambda b, m, j, k: (b, m, j)),
        out_shape=jax.ShapeDtypeStruct(x.shape, F32),
        scratch_shapes=[pltpu.VMEM((tm, tn), F32)],
        compiler_params=_cparams(("parallel", "parallel", "parallel", "arbitrary")),
        name="matmul_residual",
    )(a, w, x, mod3)


def _out_proj_kernel(ya_ref, yb_ref, yn_ref, yd_ref, w_ref, x_ref, gate_ref, o_ref):
    acc = None
    for g, y_ref in enumerate((ya_ref, yb_ref, yn_ref, yd_ref)):
        part = jnp.dot(y_ref[...].astype(BF16), w_ref[GROUP_W * g:GROUP_W * (g + 1), :],
                       preferred_element_type=F32)
        acc = part if acc is None else acc + part
    o_ref[...] = x_ref[...] + gate_ref[...] * acc


def _out_proj(ys, w, layer, x, mod3, row_of_b, gate_idx, tm, tn):
    bsz, t, d = x.shape
    tm = min(tm, t)
    y_spec = pl.BlockSpec((None, tm, GROUP_W), lambda b, m, j: (b, m, 0))
    return pl.pallas_call(
        _out_proj_kernel,
        grid=(bsz, t // tm, d // tn),
        in_specs=[y_spec, y_spec, y_spec, y_spec,
                  pl.BlockSpec((None, 4 * GROUP_W, tn), lambda b, m, j: (layer, 0, j)),
                  pl.BlockSpec((None, tm, tn), lambda b, m, j: (b, m, j)),
                  pl.BlockSpec((None, 1, tn), lambda b, m, j: (row_of_b(b), 0, gate_idx * (d // tn) + j))],
        out_specs=pl.BlockSpec((None, tm, tn), lambda b, m, j: (b, m, j)),
        out_shape=jax.ShapeDtypeStruct(x.shape, F32),
        compiler_params=_cparams(("parallel", "parallel", "parallel")),
        name="out_proj",
    )(*ys, w, x, mod3)


def _final_norm_kernel(x_ref, g_ref, o_ref):
    xf = x_ref[...]
    ms = jnp.mean(xf * xf, axis=-1, keepdims=True)
    o_ref[...] = xf * lax.rsqrt(ms + EPS) * g_ref[...]


def _final_norm(x, g, tm=1024):
    bsz, t, d = x.shape
    tm = min(tm, t)
    return pl.pallas_call(
        _final_norm_kernel,
        grid=(bsz, t // tm),
        in_specs=[pl.BlockSpec((None, tm, d), lambda b, m: (b, m, 0)),
                  pl.BlockSpec((1, d), lambda b, m: (0, 0))],
        out_specs=pl.BlockSpec((None, tm, d), lambda b, m: (b, m, 0)),
        out_shape=jax.ShapeDtypeStruct(x.shape, F32),
        compiler_params=_cparams(("parallel", "parallel")),
        name="final_norm",
    )(x, g.reshape(1, d))


def _rope(x, cos, sin_signed, lane_lo):
    w = x.shape[-1]
    partner = jnp.where(lane_lo, pltpu.roll(x, w - 16, 1), pltpu.roll(x, 16, 1))
    return x * cos + partner * sin_signed


def _win_attn_kernel(sink_ref, q_ref, kc_ref, vc_ref, *rest, seq, local):
    if local:
        k_ref, v_ref, cos_ref, sin_ref, mask_ref, o_ref = rest
    else:
        (o_ref,) = rest
    n = pl.program_id(1)
    lane = lax.broadcasted_iota(jnp.int32, (BLOCK, 128), 1)
    lo = lane < HEAD_DIM
    q = q_ref[...].astype(F32) * (SCALE * LOG2E)
    if local:
        lane_lo = (lane % 32) < 16
        r0 = pl.multiple_of(n * BLOCK, BLOCK)
        cos_q = cos_ref[pl.ds(r0, BLOCK), :]
        sin_q = sin_ref[pl.ds(r0, BLOCK), :]
    rows = []
    for m in range(4):
        qm = q[:, 128 * m:128 * (m + 1)]
        if local:
            qm = _rope(qm, cos_q, sin_q, lane_lo)
        rows.append(jnp.where(lo, qm, 0.0))
        rows.append(jnp.where(lo, 0.0, qm))
    qbd = jnp.concatenate(rows, axis=0).astype(BF16)
    nt = (((1,), (1,)), ((), ()))
    kc = kc_ref[...].astype(BF16)
    vc = vc_ref[...].astype(BF16)
    s_ctx = lax.dot_general(qbd, kc, nt, preferred_element_type=F32)
    sink = jnp.concatenate(
        [jnp.full((BLOCK, 1), sink_ref[HEAD_ORDER[i]] * LOG2E, F32) for i in range(N_HEADS)], axis=0)
    mx = jnp.maximum(jnp.max(s_ctx, axis=-1, keepdims=True), sink)
    if local:
        start = pl.multiple_of(jnp.clip((n - 1) * BLOCK, 0, seq - 3 * BLOCK), BLOCK)
        lane3 = lax.broadcasted_iota(jnp.int32, (3 * BLOCK, 128), 1)
        kb = _rope(k_ref[pl.ds(start, 3 * BLOCK), :].astype(F32), cos_ref[pl.ds(start, 3 * BLOCK), :],
                   sin_ref[pl.ds(start, 3 * BLOCK), :], (lane3 % 32) < 16).astype(BF16)
        vb = v_ref[pl.ds(start, 3 * BLOCK), :].astype(BF16)
        s_loc = lax.dot_general(qbd, kb, nt, preferred_element_type=F32)
        s_loc = (s_loc.reshape(N_HEADS, BLOCK, 3 * BLOCK) + mask_ref[...][None]).reshape(s_loc.shape)
        mx = jnp.maximum(mx, jnp.max(s_loc, axis=-1, keepdims=True))
    p_ctx = jnp.exp2(s_ctx - mx)
    den = jnp.sum(p_ctx, axis=-1, keepdims=True) + jnp.exp2(sink - mx)
    acc = jnp.dot(p_ctx.astype(BF16), vc, preferred_element_type=F32)
    if local:
        p_loc = jnp.exp2(s_loc - mx)
        den = den + jnp.sum(p_loc, axis=-1, keepdims=True)
        acc = acc + jnp.dot(p_loc.astype(BF16), vb, preferred_element_type=F32)
    o = acc / den
    outs = [jnp.where(lo, o[(2 * m) * BLOCK:(2 * m + 1) * BLOCK], o[(2 * m + 1) * BLOCK:(2 * m + 2) * BLOCK])
            for m in range(4)]
    o_ref[...] = jnp.concatenate(outs, axis=1).astype(o_ref.dtype)


def _win_attn(sink, pq, pc, cos_t, sin_t, local):
    bsz, t, _ = pq.shape
    lc = pc.shape[1]
    nb = t // BLOCK
    in_specs = [
        pl.BlockSpec(memory_space=pltpu.SMEM),
        pl.BlockSpec((None, BLOCK, 512), lambda b, n: (b, n, COL_QA // 512)),
        pl.BlockSpec((None, lc, 128), lambda b, n: (b, 0, COL_KA // 128)),
        pl.BlockSpec((None, lc, 128), lambda b, n: (b, 0, COL_VA // 128)),
    ]
    args = [sink, pq, pc, pc]
    if local:
        in_specs += [
            pl.BlockSpec((None, t, 128), lambda b, n: (b, 0, COL_KA // 128)),
            pl.BlockSpec((None, t, 128), lambda b, n: (b, 0, COL_VA // 128)),
            pl.BlockSpec((t, 128), lambda b, n: (0, 0)),
            pl.BlockSpec((t, 128), lambda b, n: (0, 0)),
            pl.BlockSpec((None, BLOCK, 3 * BLOCK),
                         lambda b, n: (jnp.where(n == 0, 0, jnp.where(n == nb - 1, 2, 1)), 0, 0)),
        ]
        args += [pq, pq, cos_t, sin_t, _window_masks(t)]
    return pl.pallas_call(
        functools.partial(_win_attn_kernel, seq=t, local=local),
        grid=(bsz, t // BLOCK),
        in_specs=in_specs,
        out_specs=pl.BlockSpec((None, BLOCK, GROUP_W), lambda b, n: (b, n, 0)),
        out_shape=jax.ShapeDtypeStruct((bsz, t, GROUP_W), BF16),
        compiler_params=_cparams(("parallel", "arbitrary")),
        name="win_attn" if local else "ctx_attn_a",
    )(*args)


def _window_masks(seq):
    nb = seq // BLOCK
    qi = np.arange(BLOCK)[:, None]
    kj = np.arange(3 * BLOCK)[None, :]
    tabs = []
    for n in (0, 1, nb - 1):
        start = int(np.clip((n - 1) * BLOCK, 0, seq - 3 * BLOCK))
        rel = (start + kj) - (n * BLOCK + qi)
        tabs.append(np.where(np.abs(rel) <= WINDOW, 0.0, NEG))
    return jnp.asarray(np.stack(tabs), F32)


def _rope_tables(seq):
    t = np.arange(seq)
    row, col = t // GRID_W, t % GRID_W
    quarter = HEAD_DIM // 4
    inv = ROPE_BASE ** (-np.arange(quarter, dtype=np.float64) / quarter)
    inv = inv.astype(np.float32).astype(np.float64)
    lane = np.arange(128)
    j = lane % HEAD_DIM
    pos = np.where((j < HEAD_DIM // 2)[None, :], row[:, None], col[:, None]).astype(np.float64)
    ang = (pos * inv[j % quarter][None, :]).astype(np.float32)
    cos = np.cos(ang.astype(np.float64))
    sin = np.sin(ang.astype(np.float64))
    sign = np.where((j % 32) < 16, -1.0, 1.0)[None, :]
    return jnp.asarray(cos, F32), jnp.asarray(sin * sign, F32)


NA_ROWS_PER_STEP = 4


def _na_kernel(var_ref, ws_ref, q_ref, kc_ref, vc_ref, *rest, local, win_rows):
    del var_ref
    if local:
        k_ref, v_ref, bias_ref, o_ref = rest
    else:
        (o_ref,) = rest
    g = pl.program_id(1)
    tq = q_ref.shape[0]
    q = q_ref[...].astype(F32) * (SCALE * LOG2E)
    head = lax.broadcasted_iota(jnp.int32, (tq, 256), 1) // HEAD_DIM
    nt = (((1,), (1,)), ((), ()))
    outs = []
    for half in range(2):
        cols = slice(256 * half, 256 * (half + 1))
        q4 = q[:, cols]
        qbd = jnp.concatenate([jnp.where(head == h, q4, 0.0) for h in range(4)], axis=0).astype(BF16)
        kc4 = kc_ref[:, cols].astype(BF16)
        vc4 = vc_ref[:, cols].astype(BF16)
        s_ctx = lax.dot_general(qbd, kc4, nt, preferred_element_type=F32)
        mx = jnp.max(s_ctx, axis=-1, keepdims=True)
        if local:
            nkey = win_rows * GRID_W
            start = pl.multiple_of(ws_ref[g] * GRID_W, GRID_W)
            k4 = k_ref[pl.ds(start, nkey), cols].astype(BF16)
            v4 = v_ref[pl.ds(start, nkey), cols].astype(BF16)
            bias = bias_ref[4 * half:4 * half + 4].astype(F32).reshape(4 * tq, nkey)
            s_loc = lax.dot_general(qbd, k4, nt, preferred_element_type=F32) + bias
            mx = jnp.maximum(mx, jnp.max(s_loc, axis=-1, keepdims=True))
        p_ctx = jnp.exp2(s_ctx - mx)
        den = jnp.sum(p_ctx, axis=-1, keepdims=True)
        acc = jnp.dot(p_ctx.astype(BF16), vc4, preferred_element_type=F32)
        if local:
            p_loc = jnp.exp2(s_loc - mx)
            den = den + jnp.sum(p_loc, axis=-1, keepdims=True)
            acc = acc + jnp.dot(p_loc.astype(BF16), v4, preferred_element_type=F32)
        o = acc / den
        o4 = jnp.where(head == 0, o[0:tq], 0.0)
        for h in range(1, 4):
            o4 = o4 + jnp.where(head == h, o[h * tq:(h + 1) * tq], 0.0)
        outs.append(o4)
    o_ref[...] = jnp.concatenate(outs, axis=1).astype(o_ref.dtype)


def _na_plan(seq):
    rows = seq // GRID_W
    kr = min(NA_KR, rows)
    r_step = NA_ROWS_PER_STEP
    win_rows = min(r_step + kr, rows)
    n_groups = rows // r_step
    wstart = np.zeros(n_groups, np.int32)
    pats = []
    keys = {}
    var = np.zeros(n_groups, np.int32)
    for g in range(n_groups):
        r0 = g * r_step
        ws = int(np.clip(r0 - kr // 2, 0, rows - win_rows))
        wstart[g] = ws
        r = r0 + np.arange(r_step)
        rstart = np.clip(r - kr // 2, 0, rows - kr)
        krow = ws + np.arange(win_rows)
        valid = (krow[None, :] >= rstart[:, None]) & (krow[None, :] < rstart[:, None] + kr)
        roff = krow[None, :] - r[:, None] + NA_KR - 1
        key = (valid.tobytes(), np.where(valid, roff, 0).tobytes())
        if key not in keys:
            keys[key] = len(pats)
            pats.append((valid, np.where(valid, roff, 0)))
        var[g] = keys[key]
    return rows, win_rows, n_groups, wstart, var, pats


def _na_bias_tables(rpb, pats):
    cq = np.arange(GRID_W)
    ck = np.arange(GRID_W)
    cstart = np.clip(cq - NA_KC // 2, 0, GRID_W - NA_KC)
    col_valid = (ck[None] >= cstart[:, None]) & (ck[None] < cstart[:, None] + NA_KC)
    coff = np.clip(ck[None] - cq[:, None], -(NA_KC - 1), NA_KC - 1) + NA_KC - 1
    by_col = jnp.where(col_valid, rpb[..., coff] * LOG2E, NEG).astype(BF16)
    masked = jnp.full(by_col.shape[:2] + (GRID_W, GRID_W), NEG, BF16)
    tabs = []
    for valid, roff in pats:
        r_step, win_rows = valid.shape
        q_rows = []
        for i in range(r_step):
            blocks = [by_col[:, :, int(roff[i, a])] if valid[i, a] else masked for a in range(win_rows)]
            q_rows.append(jnp.concatenate(blocks, axis=-1))
        tabs.append(jnp.concatenate(q_rows, axis=-2))
    return jnp.stack(tabs, axis=1)


def _na_attn(pq, pc, bias_tabs, layer, plan, local):
    bsz, t, _ = pq.shape
    lc = pc.shape[1]
    if local:
        rows, win_rows, n_groups, wstart, var, _ = plan
        tq = NA_ROWS_PER_STEP * GRID_W
    else:
        win_rows, n_groups, tq = 0, 1, t
        wstart = np.zeros(1, np.int32)
        var = np.zeros(1, np.int32)
    in_specs = [
        pl.BlockSpec((None, tq, 512), lambda b, g, vr, ws: (b, g, COL_NQ // 512)),
        pl.BlockSpec((None, lc, 512), lambda b, g, vr, ws: (b, 0, COL_NK // 512)),
        pl.BlockSpec((None, lc, 512), lambda b, g, vr, ws: (b, 0, COL_NV // 512)),
    ]
    args = [pq, pc, pc]
    if local:
        in_specs += [
            pl.BlockSpec((None, t, 512), lambda b, g, vr, ws: (b, 0, COL_NK // 512)),
            pl.BlockSpec((None, t, 512), lambda b, g, vr, ws: (b, 0, COL_NV // 512)),
            pl.BlockSpec((None, None, N_HEADS, tq, win_rows * GRID_W),
                         lambda b, g, vr, ws: (layer, vr[g], 0, 0, 0)),
        ]
        args += [pq, pq, bias_tabs]
    grid_spec = pltpu.PrefetchScalarGridSpec(
        num_scalar_prefetch=2,
        grid=(bsz, n_groups),
        in_specs=in_specs,
        out_specs=pl.BlockSpec((None, tq, GROUP_W), lambda b, g, vr, ws: (b, g, 0)),
    )
    return pl.pallas_call(
        functools.partial(_na_kernel, local=local, win_rows=win_rows),
        grid_spec=grid_spec,
        out_shape=jax.ShapeDtypeStruct((bsz, t, GROUP_W), BF16),
        compiler_params=_cparams(("parallel", "arbitrary")),
        name="na_attn" if local else "ctx_attn_c",
    )(jnp.asarray(var), jnp.asarray(wstart), *args)


def _softplus(x):
    return jnp.maximum(x, 0.0) + jnp.log(1.0 + jnp.exp(-jnp.abs(x)))


def _bf16_parts(x, n):
    parts = []
    for _ in range(n):
        part = x.astype(BF16)
        parts.append(part)
        x = x - part.astype(F32)
    return parts


def _conv3_silu(cur, prev_blk, next_blk, w_ref, b_ref, has_prev, has_next):
    x = cur.astype(F32)
    rows = x.shape[0]
    prev_row = prev_blk.astype(F32)[HALO - 1:HALO, :] * has_prev
    next_row = next_blk.astype(F32)[0:1, :] * has_next
    ri = lax.broadcasted_iota(jnp.int32, x.shape, 0)
    up = jnp.where(ri == 0, prev_row, pltpu.roll(x, 1, 0))
    dn = jnp.where(ri == rows - 1, next_row, pltpu.roll(x, rows - 1, 0))
    u = up * w_ref[0:1, :] + x * w_ref[1:2, :] + dn * w_ref[2:3, :] + b_ref[...]
    return _silu(u)


def _ssd_kernel(*refs, reverse, finalize, nc):
    (xs_ref, bc_ref, xsp_ref, xsn_ref, bcp_ref, bcn_ref, dt_ref, cwx_ref, cbx_ref, cwb_ref, cbb_ref,
     dtb_ref, alog_ref, h0_ref) = refs[:14]
    if finalize:
        z_ref, yf_ref, dskip_ref, nw_ref, y_ref, ht_ref, st_ref = refs[14:]
    else:
        y_ref, ht_ref, st_ref = refs[14:]
    c = pl.program_id(1)
    cid = (nc - 1 - c) if reverse else c
    d_off = 8 if reverse else 0
    hi = lax.Precision.HIGHEST
    L = SSM_CHUNK

    @pl.when(c == 0)
    def _():
        st_ref[...] = h0_ref[...]

    has_prev = jnp.where(cid > 0, 1.0, 0.0).astype(F32)
    has_next = jnp.where(cid < nc - 1, 1.0, 0.0).astype(F32)
    xs = _conv3_silu(xs_ref[...], xsp_ref[...], xsn_ref[...], cwx_ref, cbx_ref, has_prev, has_next)
    bc = _conv3_silu(bc_ref[...], bcp_ref[...], bcn_ref[...], cwb_ref, cbb_ref, has_prev, has_next)

    dt = _softplus(dt_ref[...].astype(F32) + dtb_ref[...])
    a = dt * (-jnp.exp(alog_ref[...]))
    ri = lax.broadcasted_iota(jnp.int32, (L, L), 0)
    ci = lax.broadcasted_iota(jnp.int32, (L, L), 1)
    keep = (ci >= ri) if reverse else (ci <= ri)
    tri = keep.astype(BF16)
    nt = (((1,), (1,)), ((), ()))
    a_t = a.T
    c_col = sum(jnp.dot(tri, part, preferred_element_type=F32) for part in _bf16_parts(a, 3))
    c_row = sum(lax.dot_general(part, tri, nt, preferred_element_type=F32) for part in _bf16_parts(a_t, 3))
    ej = lax.broadcasted_iota(jnp.int32, (128, GROUP_W), 0)
    eh = lax.broadcasted_iota(jnp.int32, (128, GROUP_W), 1) // HEAD_DIM
    expand = (ej == eh + d_off).astype(BF16)
    c_exp = sum(jnp.dot(part, expand, preferred_element_type=F32) for part in _bf16_parts(c_col, 2))
    dt_exp = sum(jnp.dot(part, expand, preferred_element_type=F32) for part in _bf16_parts(dt, 2))
    end = 0 if reverse else L - 1
    cend = c_exp[end:end + 1, :]
    x_dt = xs * dt_exp
    out_decay = jnp.exp(c_exp)
    x_dec = x_dt * jnp.exp(cend - c_exp)
    chunk_decay = jnp.exp(cend)

    head4 = lax.broadcasted_iota(jnp.int32, (L, 256), 1) // HEAD_DIM
    ys = []
    for g in range(2):
        gl = slice(256 * g, 256 * (g + 1))
        b_g = bc[:, 128 * g:128 * (g + 1)]
        c_g = bc[:, 256 + 128 * g:256 + 128 * (g + 1)].astype(BF16)
        cb = lax.dot_general(c_g, b_g.astype(BF16), nt, preferred_element_type=F32)
        ms = []
        for hh in range(4):
            j = d_off + 4 * g + hh
            diff = c_col[:, j:j + 1] - c_row[j:j + 1, :]
            ms.append(cb * jnp.exp(jnp.where(keep, diff, NEG)))
        m_g = jnp.concatenate(ms, axis=0).astype(BF16)
        o = jnp.dot(m_g, x_dt[:, gl].astype(BF16), preferred_element_type=F32)
        y_diag = jnp.where(head4 == 0, o[0:L], 0.0)
        for hh in range(1, 4):
            y_diag = y_diag + jnp.where(head4 == hh, o[hh * L:(hh + 1) * L], 0.0)
        st = st_ref[g]
        y_off = jnp.dot(c_g, st.astype(BF16), preferred_element_type=F32) * out_decay[:, gl]
        ys.append(y_diag + y_off)
        st_ref[g] = chunk_decay[:, gl] * st + jnp.dot(
            b_g.T.astype(BF16), x_dec[:, gl].astype(BF16), preferred_element_type=F32)
    y = jnp.concatenate(ys, axis=1)

    if finalize:
        y = y + yf_ref[...] + xs * dskip_ref[...]
        y = y * _silu(z_ref[...].astype(F32))
        halves = []
        for g in range(2):
            yg = y[:, 256 * g:256 * (g + 1)]
            halves.append(yg * lax.rsqrt(jnp.mean(yg * yg, axis=-1, keepdims=True) + EPS))
        y = jnp.concatenate(halves, axis=1) * nw_ref[...]
    y_ref[...] = y.astype(y_ref.dtype)

    @pl.when(c == nc - 1)
    def _():
        ht_ref[...] = st_ref[...]


def _ssd_direction(p, consts, h0, reverse, yf=None):
    cwx, cbx, cwb, cbb, dtb, alog, dskip, nw = consts
    bsz, t, _ = p.shape
    nc = t // SSM_CHUNK
    finalize = yf is not None
    hb = SSM_CHUNK // HALO
    nhalo = t // HALO

    def cid(c):
        return (nc - 1 - c) if reverse else c

    def cur(col, width):
        return pl.BlockSpec((None, SSM_CHUNK, width), lambda b, c: (b, cid(c), col // width))

    def prev(col, width):
        return pl.BlockSpec((None, HALO, width), lambda b, c: (b, jnp.maximum(cid(c) * hb - 1, 0), col // width))

    def nxt(col, width):
        return pl.BlockSpec((None, HALO, width), lambda b, c: (b, jnp.minimum((cid(c) + 1) * hb, nhalo - 1), col // width))

    def const(arr):
        return pl.BlockSpec(arr.shape, lambda b, c: (0,) * arr.ndim)

    state_spec = pl.BlockSpec((None, 2, SSM_STATE, 256), lambda b, c: (b, 0, 0, 0))
    in_specs = [cur(COL_SX, 512), cur(COL_SBC, 512), prev(COL_SX, 512), nxt(COL_SX, 512),
                prev(COL_SBC, 512), nxt(COL_SBC, 512), cur(COL_DT, 128),
                const(cwx), const(cbx), const(cwb), const(cbb), const(dtb), const(alog), state_spec]
    args = [p, p, p, p, p, p, p, cwx, cbx, cwb, cbb, dtb, alog, h0]
    if finalize:
        in_specs += [cur(COL_SZ, 512),
                     pl.BlockSpec((None, SSM_CHUNK, GROUP_W), lambda b, c: (b, cid(c), 0)),
                     const(dskip), const(nw)]
        args += [p, yf, dskip, nw]
    y, ht = pl.pallas_call(
        functools.partial(_ssd_kernel, reverse=reverse, finalize=finalize, nc=nc),
        grid=(bsz, nc),
        in_specs=in_specs,
        out_specs=[pl.BlockSpec((None, SSM_CHUNK, GROUP_W), lambda b, c: (b, cid(c), 0)), state_spec],
        out_shape=[jax.ShapeDtypeStruct((bsz, t, GROUP_W), BF16 if finalize else F32),
                   jax.ShapeDtypeStruct((bsz, 2, SSM_STATE, 256), F32)],
        scratch_shapes=[pltpu.VMEM((2, SSM_STATE, 256), F32)],
        compiler_params=_cparams(("parallel", "arbitrary")),
        name="ssd_rev" if reverse else "ssd_fwd",
    )(*args)
    return y, ht


def _ssd_consts(conv_w, conv_b, dt_bias, a_log, d_skip, norm_w):
    cwx, cwb = conv_w[:, :GROUP_W], conv_w[:, GROUP_W:]
    cbx, cbb = conv_b[:GROUP_W].reshape(1, -1), conv_b[GROUP_W:].reshape(1, -1)
    pad = lambda v: jnp.pad(v.reshape(1, -1), ((0, 0), (0, 128 - v.size)))
    return (cwx, cbx, cwb, cbb, pad(dt_bias), pad(a_log),
            jnp.repeat(d_skip, HEAD_DIM).reshape(1, -1), norm_w.reshape(1, -1))


def _ssd_mixer(p, pc, consts):
    bsz = p.shape[0]
    zero = jnp.zeros((bsz, 2, SSM_STATE, 256), F32)
    ycf, hf = _ssd_direction(pc, consts, zero, False)
    yc, hb = _ssd_direction(pc, consts, zero, True, yf=ycf)
    ylf, _ = _ssd_direction(p, consts, hf, False)
    yl, _ = _ssd_direction(p, consts, hb, True, yf=ylf)
    return yl, yc


HY_CB = 128
HY_UNROLL = 16


def _hy_dims(seq):
    n = 2 * seq
    n1 = {4096: 128, 1024: 64, 512: 32, 256: 32, 128: 16}[seq]
    n2 = n // n1
    h = n1 // 2
    return dict(n=n, n1=n1, n2=n2, h=h, pa=h + 8, pb=n2 + 8, pc=n1 + 8)


def _hy_matrices(seq):
    d = _hy_dims(seq)
    n, n1, n2, h = d["n"], d["n1"], d["n2"], d["h"]

    def cis(num, den, sign):
        ang = (2.0 * math.pi / den) * (num % den).astype(F32)
        return jnp.cos(ang), sign * jnp.sin(ang)

    k1 = jnp.arange(n1, dtype=jnp.int32)
    nn = n2 * jnp.arange(n1, dtype=jnp.int32)[None, None, :] + jnp.arange(n2, dtype=jnp.int32)[:, None, None]
    e1r, e1i = cis(k1[None, :, None] * nn, n, -1.0)
    m1f = jnp.concatenate([e1r, e1i], axis=1)
    m1 = jnp.concatenate([jnp.concatenate([e1r[..., :h], -e1i[..., :h]], axis=2),
                          jnp.concatenate([e1i[..., :h], e1r[..., :h]], axis=2)], axis=1)
    a2 = jnp.arange(n2, dtype=jnp.int32)
    g2r, g2i = cis(a2[:, None] * a2[None, :], n2, -1.0)
    m2 = jnp.concatenate([jnp.concatenate([g2r, -g2i], axis=1),
                          jnp.concatenate([g2i, g2r], axis=1)], axis=0)
    num3 = (a2[None, :, None] * a2[None, None, :] * n1 + k1[:, None, None] * a2[None, :, None])
    e3r, e3i = cis(num3, n, 1.0)
    m3 = jnp.concatenate([jnp.concatenate([e3r, -e3i], axis=2),
                          jnp.concatenate([e3i, e3r], axis=2)], axis=1)
    hh = jnp.arange(h, dtype=jnp.int32)
    d4r, d4i = cis(hh[:, None] * k1[None, :], n1, 1.0)
    m4 = jnp.concatenate([jnp.concatenate([d4r, -d4i], axis=1),
                          jnp.concatenate([d4i, d4r], axis=1)], axis=0) / n
    return tuple(m.astype(BF16) for m in (m1f, m1, m2, m3, m4))


def _hy_prep_kernel(p_ref, w_ref, b_ref, o_ref, *, dims):
    n2, h, pa = dims["n2"], dims["h"], dims["pa"]
    seq = h * n2
    o_ref[...] = jnp.zeros_like(o_ref)
    w0, w1, w2, bias = w_ref[0:1, :], w_ref[1:2, :], w_ref[2:3, :], b_ref[...]
    ri = lax.broadcasted_iota(jnp.int32, (n2, o_ref.shape[-1]), 0)

    def body(i, carry):
        r0 = pl.multiple_of(i * n2, n2)
        x = p_ref[pl.ds(r0, n2), :].astype(F32)
        pstart = pl.multiple_of(jnp.maximum(r0 - HALO, 0), HALO)
        nstart = pl.multiple_of(jnp.minimum(r0 + n2, seq - HALO), HALO)
        prev_row = p_ref[pl.ds(pstart, HALO), :].astype(F32)[HALO - 1:HALO, :] * jnp.where(i > 0, 1.0, 0.0).astype(F32)
        next_row = p_ref[pl.ds(nstart, HALO), :].astype(F32)[0:1, :] * jnp.where(i < h - 1, 1.0, 0.0).astype(F32)
        up = jnp.where(ri == 0, prev_row, pltpu.roll(x, 1, 0))
        dn = jnp.where(ri == n2 - 1, next_row, pltpu.roll(x, n2 - 1, 0))
        o_ref[pl.ds(i, n2, stride=pa), :] = up * w0 + x * w1 + dn * w2 + bias
        return carry

    lax.fori_loop(0, h, body, 0)


def _hy_prep(p, short_w, short_b):
    bsz, t, _ = p.shape
    dims = _hy_dims(t)
    rows = dims["n2"] * dims["pa"]
    ncb = 3 * GROUP_W // HY_CB
    return pl.pallas_call(
        functools.partial(_hy_prep_kernel, dims=dims),
        grid=(bsz, ncb),
        in_specs=[pl.BlockSpec((None, t, HY_CB), lambda b, j: (b, 0, COL_HY // HY_CB + j)),
                  pl.BlockSpec((3, HY_CB), lambda b, j: (0, j)),
                  pl.BlockSpec((1, HY_CB), lambda b, j: (0, j))],
        out_specs=pl.BlockSpec((None, rows, HY_CB), lambda b, j: (b, 0, j)),
        out_shape=jax.ShapeDtypeStruct((bsz, rows, 3 * GROUP_W), F32),
        compiler_params=_cparams(("parallel", "parallel")),
        name="hy_prep",
    )(p, short_w, short_b.reshape(1, -1))


def _hy_filter_kernel(z_ref, w1_ref, b1_ref, w2_ref, b2_ref, w3_ref, b3_ref, w4_ref, fr_ref, dl_ref, o_ref):
    hi = lax.Precision.HIGHEST
    z = z_ref[...]
    fr = fr_ref[...]
    h = jnp.sin(fr * (jnp.dot(z, w1_ref[...], preferred_element_type=F32, precision=hi) + b1_ref[...]))
    h = jnp.sin(fr * (jnp.dot(h, w2_ref[...], preferred_element_type=F32, precision=hi) + b2_ref[...]))
    h = jnp.sin(fr * (jnp.dot(h, w3_ref[...], preferred_element_type=F32, precision=hi) + b3_ref[...]))
    full = jnp.dot(h.astype(BF16), w4_ref[...].astype(BF16), preferred_element_type=F32)
    t = z[:, 0:1]
    is_bwd = z[:, HY_EMB:HY_EMB + 1] > 0.5
    live = z[:, HY_EMB + 1:HY_EMB + 2]
    decay = jnp.exp(-t * jnp.abs(dl_ref[...])) * live
    for o in range(2):
        fwd = full[:, o * 2 * GROUP_W:o * 2 * GROUP_W + GROUP_W]
        bwd = full[:, o * 2 * GROUP_W + GROUP_W:(o + 1) * 2 * GROUP_W]
        o_ref[o] = jnp.where(is_bwd, bwd, fwd) * decay


def _hy_filter_features(seq):
    d = _hy_dims(seq)
    n, n1, n2 = d["n"], d["n1"], d["n2"]
    row = np.arange(n)
    time = n2 * (row % n1) + row // n1
    is_bwd = time > seq
    pos = np.where(is_bwd, n - time, time)
    live = (time != seq).astype(np.float64)
    pos = np.where(time == seq, 0, pos)
    t = np.linspace(0.0, 1.0, seq)[pos]
    bands = (HY_EMB - 1) // 2
    f = np.linspace(1e-4, bands - 1, bands)[None]
    wpos = (2.0 * math.pi * pos / seq)[:, None]
    feat = np.zeros((n, 128), np.float64)
    feat[:, 0] = t
    feat[:, 1:1 + bands] = np.cos(f * wpos)
    feat[:, 1 + bands:HY_EMB] = -np.sin(f * wpos)
    feat[:, HY_EMB] = is_bwd
    feat[:, HY_EMB + 1] = live
    return jnp.asarray(feat, F32)


def _hy_filter(seq, w1, b1, w2, b2, w3, b3, w4, freq, tr=512):
    d = _hy_dims(seq)
    n = d["n"]
    tr = min(tr, n)
    feat = _hy_filter_features(seq)
    w1p = jnp.pad(w1, ((0, 128 - HY_EMB), (0, 0)))
    max_decay = math.log(HY_DECAY_TARGET) / HY_FAST_PCT
    min_decay = math.log(HY_DECAY_TARGET) / HY_SLOW_PCT
    deltas = jnp.linspace(min_decay, max_decay, GROUP_W, dtype=F32).reshape(1, -1)
    row = lambda v: v.reshape(1, -1)
    const = lambda a: pl.BlockSpec(a.shape, lambda i: (0,) * a.ndim)
    args = [w1p, row(b1), w2, row(b2), w3, row(b3), w4, row(freq), deltas]
    return pl.pallas_call(
        _hy_filter_kernel,
        grid=(n // tr,),
        in_specs=[pl.BlockSpec((tr, 128), lambda i: (i, 0))] + [const(a) for a in args],
        out_specs=pl.BlockSpec((2, tr, GROUP_W), lambda i: (0, i, 0)),
        out_shape=jax.ShapeDtypeStruct((2, n, GROUP_W), F32),
        compiler_params=_cparams(("parallel",)),
        name="hy_filter",
    )(feat, *args)


def _hy_spectrum_kernel(k_ref, m1f_ref, m2_ref, re_ref, im_ref, tr_ref, ti_ref, *, dims):
    n1, n2, pb = dims["n1"], dims["n2"], dims["pb"]

    def stage1(j, carry):
        r0 = pl.multiple_of(j * n1, n1)
        a = jnp.dot(m1f_ref[j], k_ref[pl.ds(r0, n1), :].astype(BF16), preferred_element_type=F32)
        tr_ref[pl.ds(j, n1, stride=pb), :] = a[:n1]
        ti_ref[pl.ds(j, n1, stride=pb), :] = a[n1:]
        return carry

    lax.fori_loop(0, n2, stage1, 0, unroll=HY_UNROLL)

    def stage2(k, carry):
        r0 = pl.multiple_of(k * pb, 8)
        rhs = jnp.concatenate([tr_ref[pl.ds(r0, n2), :], ti_ref[pl.ds(r0, n2), :]], axis=0).astype(BF16)
        x = jnp.dot(m2_ref[...], rhs, preferred_element_type=F32)
        o0 = pl.multiple_of(k * n2, n2)
        re_ref[pl.ds(o0, n2), :] = x[:n2]
        im_ref[pl.ds(o0, n2), :] = x[n2:]
        return carry

    lax.fori_loop(0, n1, stage2, 0, unroll=HY_UNROLL)


def _hy_spectrum(kern, mats, seq):
    d = _hy_dims(seq)
    n, n1, pb = d["n"], d["n1"], d["pb"]
    m1f, _, m2, _, _ = mats
    ncb = GROUP_W // HY_CB
    blk = pl.BlockSpec((None, n, HY_CB), lambda o, j: (o, 0, j))
    const = lambda a: pl.BlockSpec(a.shape, lambda o, j: (0,) * a.ndim)
    return pl.pallas_call(
        functools.partial(_hy_spectrum_kernel, dims=d),
        grid=(2, ncb),
        in_specs=[blk, const(m1f), const(m2)],
        out_specs=[blk, blk],
        out_shape=[jax.ShapeDtypeStruct((2, n, GROUP_W), F32)] * 2,
        scratch_shapes=[pltpu.VMEM((n1 * pb, HY_CB), F32)] * 2,
        compiler_params=_cparams(("parallel", "parallel")),
        name="hy_spectrum",
    )(kern, m1f, m2)


def _hy_conv_kernel(u_ref, g_ref, kr_ref, ki_ref, skip_ref, m1_ref, m2_ref, m3_ref, m4_ref, o_ref,
                    t1r, t1i, t2r, t2i, *, dims, natural_out):
    n1, n2, h, pa, pb, pc = (dims[k] for k in ("n1", "n2", "h", "pa", "pb", "pc"))

    def fwd1(j, carry):
        r0 = pl.multiple_of(j * pa, 8)
        rhs = jnp.concatenate([u_ref[0, pl.ds(r0, h), :], u_ref[1, pl.ds(r0, h), :]], axis=0).astype(BF16)
        a = jnp.dot(m1_ref[j], rhs, preferred_element_type=F32)
        t1r[pl.ds(j, n1, stride=pb), :] = a[:n1]
        t1i[pl.ds(j, n1, stride=pb), :] = a[n1:]
        return carry

    lax.fori_loop(0, n2, fwd1, 0, unroll=HY_UNROLL)

    def mid(k, carry):
        r0 = pl.multiple_of(k * pb, 8)
        rhs = jnp.concatenate([t1r[pl.ds(r0, n2), :], t1i[pl.ds(r0, n2), :]], axis=0).astype(BF16)
        x = jnp.dot(m2_ref[...], rhs, preferred_element_type=F32)
        f0 = pl.multiple_of(k * n2, n2)
        fr, fi = kr_ref[pl.ds(f0, n2), :], ki_ref[pl.ds(f0, n2), :]
        xr, xi = x[:n2], x[n2:]
        y = jnp.concatenate([xr * fr - xi * fi, xr * fi + xi * fr], axis=0).astype(BF16)
        c = jnp.dot(m3_ref[k], y, preferred_element_type=F32)
        t2r[pl.ds(k, n2, stride=pc), :] = c[:n2]
        t2i[pl.ds(k, n2, stride=pc), :] = c[n2:]
        return carry

    lax.fori_loop(0, n1, mid, 0, unroll=HY_UNROLL)

    if not natural_out:
        o_ref[...] = jnp.zeros_like(o_ref)
    skip = skip_ref[...]

    def inv2(j, carry):
        r0 = pl.multiple_of(j * pc, 8)
        rhs = jnp.concatenate([t2r[pl.ds(r0, n1), :], t2i[pl.ds(r0, n1), :]], axis=0).astype(BF16)
        y = jnp.dot(m4_ref[...], rhs, preferred_element_type=F32)
        a0 = pl.multiple_of(j * pa, 8)
        for e in range(2):
            val = g_ref[e, pl.ds(a0, h), :] * (y[e * h:(e + 1) * h] + skip * u_ref[e, pl.ds(a0, h), :])
            if natural_out:
                o_ref[e, pl.ds(j, h, stride=n2), :] = val.astype(o_ref.dtype)
            else:
                o_ref[e, pl.ds(a0, h), :] = val
        return carry

    lax.fori_loop(0, n2, inv2, 0, unroll=HY_UNROLL)


def _hy_conv(u_arr, u_col, g_arr, g_col, kf_re, kf_im, order, skip, mats, seq, natural_out):
    d = _hy_dims(seq)
    bsz = u_arr.shape[0]
    rows = d["n2"] * d["pa"]
    _, m1, m2, m3, m4 = mats
    ncb = GROUP_W // HY_CB
    single = pl.Buffered(1)
    const = lambda a: pl.BlockSpec(a.shape, lambda j, p: (0,) * a.ndim, pipeline_mode=single)
    in_specs = [
        pl.BlockSpec((2, rows, HY_CB), lambda j, p: (p, 0, u_col // HY_CB + j), pipeline_mode=single),
        pl.BlockSpec((2, rows, HY_CB), lambda j, p: (p, 0, g_col // HY_CB + j), pipeline_mode=single),
        pl.BlockSpec((None, d["n"], HY_CB), lambda j, p: (order, 0, j), pipeline_mode=single),
        pl.BlockSpec((None, d["n"], HY_CB), lambda j, p: (order, 0, j), pipeline_mode=single),
        pl.BlockSpec((None, 1, HY_CB), lambda j, p: (order, 0, j)),
        const(m1), const(m2), const(m3), const(m4),
    ]
    if natural_out:
        out_spec = pl.BlockSpec((2, seq, HY_CB), lambda j, p: (p, 0, j), pipeline_mode=single)
        out_shape = jax.ShapeDtypeStruct((bsz, seq, GROUP_W), F32)
    else:
        out_spec = pl.BlockSpec((2, rows, HY_CB), lambda j, p: (p, 0, j), pipeline_mode=single)
        out_shape = jax.ShapeDtypeStruct((bsz, rows, GROUP_W), F32)
    return pl.pallas_call(
        functools.partial(_hy_conv_kernel, dims=d, natural_out=natural_out),
        grid=(ncb, bsz // 2),
        in_specs=in_specs,
        out_specs=out_spec,
        out_shape=out_shape,
        scratch_shapes=[pltpu.VMEM((d["n1"] * d["pb"], HY_CB), F32)] * 2
        + [pltpu.VMEM((d["n2"] * d["pc"], HY_CB), F32)] * 2,
        compiler_params=_cparams(("parallel", "parallel")),
        name="hy_conv",
    )(u_arr, g_arr, kf_re, kf_im, skip.reshape(2, 1, GROUP_W), m1, m2, m3, m4)


def _hyena_mixer(p, short_w, short_b, filt_params, skip, mats):
    seq = p.shape[1]
    kern = _hy_filter(seq, *filt_params)
    kf_re, kf_im = _hy_spectrum(kern, mats, seq)
    ut = _hy_prep(p, short_w, short_b)
    zt = _hy_conv(ut, 2 * GROUP_W, ut, 0, kf_re, kf_im, 0, skip, mats, seq, natural_out=False)
    return _hy_conv(zt, 0, ut, GROUP_W, kf_re, kf_im, 1, skip, mats, seq, natural_out=True)


def _prep_weights(w_in, w_out):
    depth, d, _ = w_in.shape
    perm = np.asarray(HEAD_ORDER)
    q = w_in[..., :512].reshape(depth, d, N_HEADS, HEAD_DIM)[:, :, perm].reshape(depth, d, 512)
    o_a, o_ssm = 768, 768 + 3072 + 1536 + 16
    pieces = [q, w_in[..., 512:o_a], w_in[..., o_ssm - 16:o_ssm],
              jnp.zeros((depth, d, COL_HY - COL_DT - 16), w_in.dtype), w_in[..., o_a:o_ssm - 16]]
    w_in_p = jnp.concatenate(pieces, axis=-1).astype(BF16)
    assert w_in_p.shape[-1] == N_PROJ
    dm = w_out.shape[-1]
    wo_a = w_out[:, :512].reshape(depth, N_HEADS, HEAD_DIM, dm)[:, perm].reshape(depth, 512, dm)
    w_out_p = jnp.concatenate([wo_a, w_out[:, 512:]], axis=1).astype(BF16)
    return w_in_p, w_out_p


TILE_W_IN = (1024, 1408)
TILE_MLP_UP = (1024, 2048)
TILE_DOWN = (1024, 1024, 2048)
TILE_OUT = (1024, 1024)


def kernel(x, c, ctx, c_ctx, ada_w, ada_b, norm_mix, norm_mlp, w_in, w_out, attn_sink,
           hy_short_w, hy_short_b, hy_w1, hy_b1, hy_w2, hy_b2, hy_w3, hy_b3, hy_w4, hy_freq, hy_skip,
           na_rpb, ssm_conv_w, ssm_conv_b, ssm_dt_bias, ssm_a_log, ssm_d, ssm_norm,
           mlp_w1, mlp_w2, final_norm):
    bsz, seq, d = x.shape
    lc = ctx.shape[1]
    depth = ada_w.shape[0]
    assert bsz % 2 == 0 and bsz <= 7 and d == D_MODEL

    cs = jnp.zeros((8, d), F32).at[:bsz].set(c).at[bsz].set(c_ctx)
    mod = _ada_mod(cs, ada_w, ada_b)
    cos_t, sin_t = _rope_tables(seq)
    na_plan = _na_plan(seq)
    bias_tabs = _na_bias_tables(na_rpb, na_plan[-1])
    mats_l = _hy_matrices(seq)
    mats_c = _hy_matrices(lc)
    lat_row = lambda b: b
    ctx_row = lambda b: bsz
    tm_c = lc

    w_in_p, w_out_p = _prep_weights(w_in, w_out)
    w1_b = mlp_w1.astype(BF16)
    w2_b = mlp_w2.astype(BF16)

    xc = ctx
    for i in range(depth):
        last = i == depth - 1
        mod3 = mod[i].reshape(8, 1, 6 * d)
        p = _norm_matmul(x, norm_mix[i], mod3, lat_row, 0, 1, w_in_p, i, P_DTYPE, False, *TILE_W_IN)
        pc = _norm_matmul(xc, norm_mix[i], mod3, ctx_row, 0, 1, w_in_p, i, P_DTYPE, False, tm_c, TILE_W_IN[1])

        filt_params = (hy_w1[i], hy_b1[i], hy_w2[i], hy_b2[i], hy_w3[i], hy_b3[i], hy_w4[i], hy_freq[i])
        ssd_consts = _ssd_consts(ssm_conv_w[i], ssm_conv_b[i], ssm_dt_bias[i], ssm_a_log[i], ssm_d[i], ssm_norm[i])

        ya = _win_attn(attn_sink[i], p, pc, cos_t, sin_t, local=True)
        yb = _hyena_mixer(p, hy_short_w[i], hy_short_b[i], filt_params, hy_skip[i], mats_l)
        yn = _na_attn(p, pc, bias_tabs, i, na_plan, local=True)
        yd, ydc = _ssd_mixer(p, pc, ssd_consts)
        x = _out_proj((ya, yb, yn, yd), w_out_p, i, x, mod3, lat_row, 2, *TILE_OUT)
        hid = _norm_matmul(x, norm_mlp[i], mod3, lat_row, 3, 4, w1_b, i, BF16, True, *TILE_MLP_UP)
        x = _matmul_residual(hid, w2_b, i, x, mod3, lat_row, 5, *TILE_DOWN)

        if not last:
            yac = _win_attn(attn_sink[i], pc, pc, None, None, local=False)
            ybc = _hyena_mixer(pc, hy_short_w[i], hy_short_b[i], filt_params, hy_skip[i], mats_c)
            ync = _na_attn(pc, pc, None, i, None, local=False)
            xc = _out_proj((yac, ybc, ync, ydc), w_out_p, i, xc, mod3, ctx_row, 2, tm_c, TILE_OUT[1])
            hidc = _norm_matmul(xc, norm_mlp[i], mod3, ctx_row, 3, 4, w1_b, i, BF16, True, tm_c, TILE_MLP_UP[1])
            xc = _matmul_residual(hidc, w2_b, i, xc, mod3, ctx_row, 5, tm_c, *TILE_DOWN[1:])
    return _final_norm(x, final_norm)
```

```python
import functools
import math

import numpy as np
import jax
import jax.numpy as jnp
from jax import lax
from jax.experimental import pallas as pl
from jax.experimental.pallas import tpu as pltpu

F32 = jnp.float32
BF16 = jnp.bfloat16

D_MODEL = 2048
GRID_W = 64
EPS = 1e-6
NEG = -1e30
HEAD_DIM = 64
GROUP_W = D_MODEL // 4
N_HEADS = GROUP_W // HEAD_DIM
WINDOW = 128
BLOCK = 128
ROPE_BASE = 10000.0
HY_EMB = 33
HY_FFN = 64
HY_DECAY_TARGET = 1e-2
HY_FAST_PCT = 0.3
HY_SLOW_PCT = 1.5
NA_KR = 8
NA_KC = 16
SSM_STATE = 128
SSM_CHUNK = 128
D_FF = 4 * D_MODEL
SCALE = HEAD_DIM ** -0.5
LOG2E = math.log2(math.e)

COL_QA, COL_KA, COL_VA, COL_DT = 0, 512, 640, 768
COL_HY = 1024
COL_NQ, COL_NK, COL_NV = 2560, 3072, 3584
COL_SZ, COL_SX, COL_SBC = 4096, 4608, 5120
N_PROJ = 5632
HEAD_ORDER = (0, 4, 1, 5, 2, 6, 3, 7)

P_DTYPE = BF16
HALO = 16

V7X_VMEM_BYTES = 64 * 1024 * 1024
VMEM_LIMIT = 56 * 1024 * 1024


def _cparams(sem):
    return pltpu.CompilerParams(dimension_semantics=sem, vmem_limit_bytes=VMEM_LIMIT)


def _silu(x):
    return x * jax.nn.sigmoid(x)


def _ada_kernel(cs_ref, w_ref, b_ref, o_ref):
    a = _silu(cs_ref[...]).astype(BF16)
    o_ref[...] = jnp.dot(a, w_ref[...].astype(BF16), preferred_element_type=F32) + b_ref[...]


def _ada_mod(cs, ada_w, ada_b, tn=1024):
    depth, d, n = ada_w.shape
    return pl.pallas_call(
        _ada_kernel,
        grid=(depth, n // tn),
        in_specs=[
            pl.BlockSpec((8, d), lambda i, j: (0, 0)),
            pl.BlockSpec((None, d, tn), lambda i, j: (i, 0, j)),
            pl.BlockSpec((None, 1, tn), lambda i, j: (i, 0, j)),
        ],
        out_specs=pl.BlockSpec((None, 8, tn), lambda i, j: (i, 0, j)),
        out_shape=jax.ShapeDtypeStruct((depth, 8, n), F32),
        compiler_params=_cparams(("parallel", "parallel")),
        name="ada_mod",
    )(cs, ada_w, ada_b.reshape(depth, 1, n))


NORM_ROWS = 16


def _norm_matmul_kernel(x_ref, g_ref, sh_ref, sc_ref, w_ref, o_ref, h_ref, *, act):
    @pl.when(pl.program_id(2) == 0)
    def _():
        gain = g_ref[...] * (1.0 + sc_ref[...])
        shift = sh_ref[...]

        def rows(r, carry):
            r0 = pl.multiple_of(r * NORM_ROWS, NORM_ROWS)
            xf = x_ref[pl.ds(r0, NORM_ROWS), :]
            ms = jnp.mean(xf * xf, axis=-1, keepdims=True)
            h_ref[pl.ds(r0, NORM_ROWS), :] = (xf * lax.rsqrt(ms + EPS) * gain + shift).astype(BF16)
            return carry

        lax.fori_loop(0, x_ref.shape[0] // NORM_ROWS, rows, 0, unroll=8)

    r = jnp.dot(h_ref[...], w_ref[...], preferred_element_type=F32)
    if act:
        r = jnp.square(jnp.maximum(r, 0.0))
    o_ref[...] = r.astype(o_ref.dtype)


def _norm_matmul(x, g, mod3, row_of_b, sh_idx, sc_idx, w, layer, out_dtype, act, tm, tn):
    bsz, t, d = x.shape
    n = w.shape[2]
    tm = min(tm, t)
    return pl.pallas_call(
        functools.partial(_norm_matmul_kernel, act=act),
        grid=(bsz, t // tm, n // tn),
        in_specs=[
            pl.BlockSpec((None, tm, d), lambda b, m, j: (b, m, 0)),
            pl.BlockSpec((1, d), lambda b, m, j: (0, 0)),
            pl.BlockSpec((None, 1, d), lambda b, m, j: (row_of_b(b), 0, sh_idx)),
            pl.BlockSpec((None, 1, d), lambda b, m, j: (row_of_b(b), 0, sc_idx)),
            pl.BlockSpec((None, d, tn), lambda b, m, j: (layer, 0, j)),
        ],
        out_specs=pl.BlockSpec((None, tm, tn), lambda b, m, j: (b, m, j)),
        out_shape=jax.ShapeDtypeStruct((bsz, t, n), out_dtype),
        scratch_shapes=[pltpu.VMEM((tm, d), BF16)],
        compiler_params=_cparams(("parallel", "parallel", "arbitrary")),
        name="norm_matmul",
    )(x, g.reshape(1, d), mod3, mod3, w)


def _mm_res_kernel(a_ref, w_ref, x_ref, gate_ref, o_ref, acc_ref, *, nk):
    k = pl.program_id(3)

    @pl.when(k == 0)
    def _():
        acc_ref[...] = jnp.zeros_like(acc_ref)

    acc_ref[...] += jnp.dot(a_ref[...], w_ref[...], preferred_element_type=F32)

    @pl.when(k == nk - 1)
    def _():
        o_ref[...] = x_ref[...] + gate_ref[...] * acc_ref[...]


def _matmul_residual(a, w, layer, x, mod3, row_of_b, gate_idx, tm, tn, tk):
    bsz, t, kdim = a.shape
    n = w.shape[2]
    tm = min(tm, t)
    nk = kdim // tk
    return pl.pallas_call(
        functools.partial(_mm_res_kernel, nk=nk),
        grid=(bsz, t // tm, n // tn, nk),
        in_specs=[
            pl.BlockSpec((None, tm, tk), lambda b, m, j, k: (b, m, k)),
            pl.BlockSpec((None, tk, tn), lambda b, m, j, k: (layer, k, j)),
            pl.BlockSpec((None, tm, tn), lambda b, m, j, k: (b, m, j)),
            pl.BlockSpec((None, 1, tn), lambda b, m, j, k: (row_of_b(b), 0, gate_idx * (D_MODEL // tn) + j)),
        ],
        out_specs=pl.BlockSpec((None, tm, tn), lambda b, m, j, k: (b, m, j)),
        out_shape=jax.ShapeDtypeStruct(x.shape, F32),
        scratch_shapes=[pltpu.VMEM((tm, tn), F32)],
        compiler_params=_cparams(("parallel", "parallel", "parallel", "arbitrary")),
        name="matmul_residual",
    )(a, w, x, mod3)


def _out_proj_kernel(ya_ref, yb_ref, yn_ref, yd_ref, w_ref, x_ref, gate_ref, o_ref):
    acc = None
    for g, y_ref in enumerate((ya_ref, yb_ref, yn_ref, yd_ref)):
        part = jnp.dot(y_ref[...].astype(BF16), w_ref[GROUP_W * g:GROUP_W * (g + 1), :],
                       preferred_element_type=F32)
        acc = part if acc is None else acc + part
    o_ref[...] = x_ref[...] + gate_ref[...] * acc


def _out_proj(ys, w, layer, x, mod3, row_of_b, gate_idx, tm, tn):
    bsz, t, d = x.shape
    tm = min(tm, t)
    y_spec = pl.BlockSpec((None, tm, GROUP_W), lambda b, m, j: (b, m, 0))
    return pl.pallas_call(
        _out_proj_kernel,
        grid=(bsz, t // tm, d // tn),
        in_specs=[y_spec, y_spec, y_spec, y_spec,
                  pl.BlockSpec((None, 4 * GROUP_W, tn), lambda b, m, j: (layer, 0, j)),
                  pl.BlockSpec((None, tm, tn), lambda b, m, j: (b, m, j)),
                  pl.BlockSpec((None, 1, tn), lambda b, m, j: (row_of_b(b), 0, gate_idx * (d // tn) + j))],
        out_specs=pl.BlockSpec((None, tm, tn), lambda b, m, j: (b, m, j)),
        out_shape=jax.ShapeDtypeStruct(x.shape, F32),
        compiler_params=_cparams(("parallel", "parallel", "parallel")),
        name="out_proj",
    )(*ys, w, x, mod3)


def _final_norm_kernel(x_ref, g_ref, o_ref):
    xf = x_ref[...]
    ms = jnp.mean(xf * xf, axis=-1, keepdims=True)
    o_ref[...] = xf * lax.rsqrt(ms + EPS) * g_ref[...]


def _final_norm(x, g, tm=1024):
    bsz, t, d = x.shape
    tm = min(tm, t)
    return pl.pallas_call(
        _final_norm_kernel,
        grid=(bsz, t // tm),
        in_specs=[pl.BlockSpec((None, tm, d), lambda b, m: (b, m, 0)),
                  pl.BlockSpec((1, d), lambda b, m: (0, 0))],
        out_specs=pl.BlockSpec((None, tm, d), lambda b, m: (b, m, 0)),
        out_shape=jax.ShapeDtypeStruct(x.shape, F32),
        compiler_params=_cparams(("parallel", "parallel")),
        name="final_norm",
    )(x, g.reshape(1, d))


def _rope(x, cos, sin_signed, lane_lo):
    w = x.shape[-1]
    partner = jnp.where(lane_lo, pltpu.roll(x, w - 16, 1), pltpu.roll(x, 16, 1))
    return x * cos + partner * sin_signed


def _win_attn_kernel(sink_ref, q_ref, kc_ref, vc_ref, *rest, seq, local):
    if local:
        k_ref, v_ref, cos_ref, sin_ref, mask_ref, o_ref = rest
    else:
        (o_ref,) = rest
    n = pl.program_id(1)
    lane = lax.broadcasted_iota(jnp.int32, (BLOCK, 128), 1)
    lo = lane < HEAD_DIM
    q = q_ref[...].astype(F32) * (SCALE * LOG2E)
    if local:
        lane_lo = (lane % 32) < 16
        r0 = pl.multiple_of(n * BLOCK, BLOCK)
        cos_q = cos_ref[pl.ds(r0, BLOCK), :]
        sin_q = sin_ref[pl.ds(r0, BLOCK), :]
    rows = []
    for m in range(4):
        qm = q[:, 128 * m:128 * (m + 1)]
        if local:
            qm = _rope(qm, cos_q, sin_q, lane_lo)
        rows.append(jnp.where(lo, qm, 0.0))
        rows.append(jnp.where(lo, 0.0, qm))
    qbd = jnp.concatenate(rows, axis=0).astype(BF16)
    nt = (((1,), (1,)), ((), ()))
    kc = kc_ref[...].astype(BF16)
    vc = vc_ref[...].astype(BF16)
    s_ctx = lax.dot_general(qbd, kc, nt, preferred_element_type=F32)
    sink = jnp.concatenate(
        [jnp.full((BLOCK, 1), sink_ref[HEAD_ORDER[i]] * LOG2E, F32) for i in range(N_HEADS)], axis=0)
    mx = jnp.maximum(jnp.max(s_ctx, axis=-1, keepdims=True), sink)
    if local:
        start = pl.multiple_of(jnp.clip((n - 1) * BLOCK, 0, seq - 3 * BLOCK), BLOCK)
        lane3 = lax.broadcasted_iota(jnp.int32, (3 * BLOCK, 128), 1)
        kb = _rope(k_ref[pl.ds(start, 3 * BLOCK), :].astype(F32), cos_ref[pl.ds(start, 3 * BLOCK), :],
                   sin_ref[pl.ds(start, 3 * BLOCK), :], (lane3 % 32) < 16).astype(BF16)
        vb = v_ref[pl.ds(start, 3 * BLOCK), :].astype(BF16)
        s_loc = lax.dot_general(qbd, kb, nt, preferred_element_type=F32)
        s_loc = (s_loc.reshape(N_HEADS, BLOCK, 3 * BLOCK) + mask_ref[...][None]).reshape(s_loc.shape)
        mx = jnp.maximum(mx, jnp.max(s_loc, axis=-1, keepdims=True))
    p_ctx = jnp.exp2(s_ctx - mx)
    den = jnp.sum(p_ctx, axis=-1, keepdims=True) + jnp.exp2(sink - mx)
    acc = jnp.dot(p_ctx.astype(BF16), vc, preferred_element_type=F32)
    if local:
        p_loc = jnp.exp2(s_loc - mx)
        den = den + jnp.sum(p_loc, axis=-1, keepdims=True)
        acc = acc + jnp.dot(p_loc.astype(BF16), vb, preferred_element_type=F32)
    o = acc / den
    outs = [jnp.where(lo, o[(2 * m) * BLOCK:(2 * m + 1) * BLOCK], o[(2 * m + 1) * BLOCK:(2 * m + 2) * BLOCK])
            for m in range(4)]
    o_ref[...] = jnp.concatenate(outs, axis=1).astype(o_ref.dtype)


def _win_attn(sink, pq, pc, cos_t, sin_t, local):
    bsz, t, _ = pq.shape
    lc = pc.shape[1]
    nb = t // BLOCK
    in_specs = [
        pl.BlockSpec(memory_space=pltpu.SMEM),
        pl.BlockSpec((None, BLOCK, 512), lambda b, n: (b, n, COL_QA // 512)),
        pl.BlockSpec((None, lc, 128), lambda b, n: (b, 0, COL_KA // 128)),
        pl.BlockSpec((None, lc, 128), lambda b, n: (b, 0, COL_VA // 128)),
    ]
    args = [sink, pq, pc, pc]
    if local:
        in_specs += [
            pl.BlockSpec((None, t, 128), lambda b, n: (b, 0, COL_KA // 128)),
            pl.BlockSpec((None, t, 128), lambda b, n: (b, 0, COL_VA // 128)),
            pl.BlockSpec((t, 128), lambda b, n: (0, 0)),
            pl.BlockSpec((t, 128), lambda b, n: (0, 0)),
            pl.BlockSpec((None, BLOCK, 3 * BLOCK),
                         lambda b, n: (jnp.where(n == 0, 0, jnp.where(n == nb - 1, 2, 1)), 0, 0)),
        ]
        args += [pq, pq, cos_t, sin_t, _window_masks(t)]
    return pl.pallas_call(
        functools.partial(_win_attn_kernel, seq=t, local=local),
        grid=(bsz, t // BLOCK),
        in_specs=in_specs,
        out_specs=pl.BlockSpec((None, BLOCK, GROUP_W), lambda b, n: (b, n, 0)),
        out_shape=jax.ShapeDtypeStruct((bsz, t, GROUP_W), BF16),
        compiler_params=_cparams(("parallel", "arbitrary")),
        name="win_attn" if local else "ctx_attn_a",
    )(*args)


def _window_masks(seq):
    nb = seq // BLOCK
    qi = np.arange(BLOCK)[:, None]
    kj = np.arange(3 * BLOCK)[None, :]
    tabs = []
    for n in (0, 1, nb - 1):
        start = int(np.clip((n - 1) * BLOCK, 0, seq - 3 * BLOCK))
        rel = (start + kj) - (n * BLOCK + qi)
        tabs.append(np.where(np.abs(rel) <= WINDOW, 0.0, NEG))
    return jnp.asarray(np.stack(tabs), F32)


def _rope_tables(seq):
    t = np.arange(seq)
    row, col = t // GRID_W, t % GRID_W
    quarter = HEAD_DIM // 4
    inv = ROPE_BASE ** (-np.arange(quarter, dtype=np.float64) / quarter)
    inv = inv.astype(np.float32).astype(np.float64)
    lane = np.arange(128)
    j = lane % HEAD_DIM
    pos = np.where((j < HEAD_DIM // 2)[None, :], row[:, None], col[:, None]).astype(np.float64)
    ang = (pos * inv[j % quarter][None, :]).astype(np.float32)
    cos = np.cos(ang.astype(np.float64))
    sin = np.sin(ang.astype(np.float64))
    sign = np.where((j % 32) < 16, -1.0, 1.0)[None, :]
    return jnp.asarray(cos, F32), jnp.asarray(sin * sign, F32)


NA_ROWS_PER_STEP = 4


def _na_kernel(var_ref, ws_ref, q_ref, kc_ref, vc_ref, *rest, local, win_rows):
    del var_ref
    if local:
        k_ref, v_ref, bias_ref, o_ref = rest
    else:
        (o_ref,) = rest
    g = pl.program_id(1)
    tq = q_ref.shape[0]
    q = q_ref[...].astype(F32) * (SCALE * LOG2E)
    head = lax.broadcasted_iota(jnp.int32, (tq, 256), 1) // HEAD_DIM
    nt = (((1,), (1,)), ((), ()))
    outs = []
    for half in range(2):
        cols = slice(256 * half, 256 * (half + 1))
        q4 = q[:, cols]
        qbd = jnp.concatenate([jnp.where(head == h, q4, 0.0) for h in range(4)], axis=0).astype(BF16)
        kc4 = kc_ref[:, cols].astype(BF16)
        vc4 = vc_ref[:, cols].astype(BF16)
        s_ctx = lax.dot_general(qbd, kc4, nt, preferred_element_type=F32)
        mx = jnp.max(s_ctx, axis=-1, keepdims=True)
        if local:
            nkey = win_rows * GRID_W
            start = pl.multiple_of(ws_ref[g] * GRID_W, GRID_W)
            k4 = k_ref[pl.ds(start, nkey), cols].astype(BF16)
            v4 = v_ref[pl.ds(start, nkey), cols].astype(BF16)
            bias = bias_ref[4 * half:4 * half + 4].astype(F32).reshape(4 * tq, nkey)
            s_loc = lax.dot_general(qbd, k4, nt, preferred_element_type=F32) + bias
            mx = jnp.maximum(mx, jnp.max(s_loc, axis=-1, keepdims=True))
        p_ctx = jnp.exp2(s_ctx - mx)
        den = jnp.sum(p_ctx, axis=-1, keepdims=True)
        acc = jnp.dot(p_ctx.astype(BF16), vc4, preferred_element_type=F32)
        if local:
            p_loc = jnp.exp2(s_loc - mx)
            den = den + jnp.sum(p_loc, axis=-1, keepdims=True)
            acc = acc + jnp.dot(p_loc.astype(BF16), v4, preferred_element_type=F32)
        o = acc / den
        o4 = jnp.where(head == 0, o[0:tq], 0.0)
        for h in range(1, 4):
            o4 = o4 + jnp.where(head == h, o[h * tq:(h + 1) * tq], 0.0)
        outs.append(o4)
    o_ref[...] = jnp.concatenate(outs, axis=1).astype(o_ref.dtype)


def _na_plan(seq):
    rows = seq // GRID_W
    kr = min(NA_KR, rows)
    r_step = NA_ROWS_PER_STEP
    win_rows = min(r_step + kr, rows)
    n_groups = rows // r_step
    wstart = np.zeros(n_groups, np.int32)
    pats = []
    keys = {}
    var = np.zeros(n_groups, np.int32)
    for g in range(n_groups):
        r0 = g * r_step
        ws = int(np.clip(r0 - kr // 2, 0, rows - win_rows))
        wstart[g] = ws
        r = r0 + np.arange(r_step)
        rstart = np.clip(r - kr // 2, 0, rows - kr)
        krow = ws + np.arange(win_rows)
        valid = (krow[None, :] >= rstart[:, None]) & (krow[None, :] < rstart[:, None] + kr)
        roff = krow[None, :] - r[:, None] + NA_KR - 1
        key = (valid.tobytes(), np.where(valid, roff, 0).tobytes())
        if key not in keys:
            keys[key] = len(pats)
            pats.append((valid, np.where(valid, roff, 0)))
        var[g] = keys[key]
    return rows, win_rows, n_groups, wstart, var, pats


def _na_bias_tables(rpb, pats):
    cq = np.arange(GRID_W)
    ck = np.arange(GRID_W)
    cstart = np.clip(cq - NA_KC // 2, 0, GRID_W - NA_KC)
    col_valid = (ck[None] >= cstart[:, None]) & (ck[None] < cstart[:, None] + NA_KC)
    coff = np.clip(ck[None] - cq[:, None], -(NA_KC - 1), NA_KC - 1) + NA_KC - 1
    by_col = jnp.where(col_valid, rpb[..., coff] * LOG2E, NEG).astype(BF16)
    masked = jnp.full(by_col.shape[:2] + (GRID_W, GRID_W), NEG, BF16)
    tabs = []
    for valid, roff in pats:
        r_step, win_rows = valid.shape
        q_rows = []
        for i in range(r_step):
            blocks = [by_col[:, :, int(roff[i, a])] if valid[i, a] else masked for a in range(win_rows)]
            q_rows.append(jnp.concatenate(blocks, axis=-1))
        tabs.append(jnp.concatenate(q_rows, axis=-2))
    return jnp.stack(tabs, axis=1)


def _na_attn(pq, pc, bias_tabs, layer, plan, local):
    bsz, t, _ = pq.shape
    lc = pc.shape[1]
    if local:
        rows, win_rows, n_groups, wstart, var, _ = plan
        tq = NA_ROWS_PER_STEP * GRID_W
    else:
        win_rows, n_groups, tq = 0, 1, t
        wstart = np.zeros(1, np.int32)
        var = np.zeros(1, np.int32)
    in_specs = [
        pl.BlockSpec((None, tq, 512), lambda b, g, vr, ws: (b, g, COL_NQ // 512)),
        pl.BlockSpec((None, lc, 512), lambda b, g, vr, ws: (b, 0, COL_NK // 512)),
        pl.BlockSpec((None, lc, 512), lambda b, g, vr, ws: (b, 0, COL_NV // 512)),
    ]
    args = [pq, pc, pc]
    if local:
        in_specs += [
            pl.BlockSpec((None, t, 512), lambda b, g, vr, ws: (b, 0, COL_NK // 512)),
            pl.BlockSpec((None, t, 512), lambda b, g, vr, ws: (b, 0, COL_NV // 512)),
            pl.BlockSpec((None, None, N_HEADS, tq, win_rows * GRID_W),
                         lambda b, g, vr, ws: (layer, vr[g], 0, 0, 0)),
        ]
        args += [pq, pq, bias_tabs]
    grid_spec = pltpu.PrefetchScalarGridSpec(
        num_scalar_prefetch=2,
        grid=(bsz, n_groups),
        in_specs=in_specs,
        out_specs=pl.BlockSpec((None, tq, GROUP_W), lambda b, g, vr, ws: (b, g, 0)),
    )
    return pl.pallas_call(
        functools.partial(_na_kernel, local=local, win_rows=win_rows),
        grid_spec=grid_spec,
        out_shape=jax.ShapeDtypeStruct((bsz, t, GROUP_W), BF16),
        compiler_params=_cparams(("parallel", "arbitrary")),
        name="na_attn" if local else "ctx_attn_c",
    )(jnp.asarray(var), jnp.asarray(wstart), *args)


def _softplus(x):
    return jnp.maximum(x, 0.0) + jnp.log(1.0 + jnp.exp(-jnp.abs(x)))


def _bf16_parts(x, n):
    parts = []
    for _ in range(n):
        part = x.astype(BF16)
        parts.append(part)
        x = x - part.astype(F32)
    return parts


def _conv3_silu(cur, prev_blk, next_blk, w_ref, b_ref, has_prev, has_next):
    x = cur.astype(F32)
    rows = x.shape[0]
    prev_row = prev_blk.astype(F32)[HALO - 1:HALO, :] * has_prev
    next_row = next_blk.astype(F32)[0:1, :] * has_next
    ri = lax.broadcasted_iota(jnp.int32, x.shape, 0)
    up = jnp.where(ri == 0, prev_row, pltpu.roll(x, 1, 0))
    dn = jnp.where(ri == rows - 1, next_row, pltpu.roll(x, rows - 1, 0))
    u = up * w_ref[0:1, :] + x * w_ref[1:2, :] + dn * w_ref[2:3, :] + b_ref[...]
    return _silu(u)


def _ssd_kernel(*refs, reverse, finalize, nc):
    (xs_ref, bc_ref, xsp_ref, xsn_ref, bcp_ref, bcn_ref, dt_ref, cwx_ref, cbx_ref, cwb_ref, cbb_ref,
     dtb_ref, alog_ref, h0_ref) = refs[:14]
    if finalize:
        z_ref, yf_ref, dskip_ref, nw_ref, y_ref, ht_ref, st_ref = refs[14:]
    else:
        y_ref, ht_ref, st_ref = refs[14:]
    c = pl.program_id(1)
    cid = (nc - 1 - c) if reverse else c
    d_off = 8 if reverse else 0
    hi = lax.Precision.HIGHEST
    L = SSM_CHUNK

    @pl.when(c == 0)
    def _():
        st_ref[...] = h0_ref[...]

    has_prev = jnp.where(cid > 0, 1.0, 0.0).astype(F32)
    has_next = jnp.where(cid < nc - 1, 1.0, 0.0).astype(F32)
    xs = _conv3_silu(xs_ref[...], xsp_ref[...], xsn_ref[...], cwx_ref, cbx_ref, has_prev, has_next)
    bc = _conv3_silu(bc_ref[...], bcp_ref[...], bcn_ref[...], cwb_ref, cbb_ref, has_prev, has_next)

    dt = _softplus(dt_ref[...].astype(F32) + dtb_ref[...])
    a = dt * (-jnp.exp(alog_ref[...]))
    ri = lax.broadcasted_iota(jnp.int32, (L, L), 0)
    ci = lax.broadcasted_iota(jnp.int32, (L, L), 1)
    keep = (ci >= ri) if reverse else (ci <= ri)
    tri = keep.astype(BF16)
    nt = (((1,), (1,)), ((), ()))
    a_t = a.T
    c_col = sum(jnp.dot(tri, part, preferred_element_type=F32) for part in _bf16_parts(a, 3))
    c_row = sum(lax.dot_general(part, tri, nt, preferred_element_type=F32) for part in _bf16_parts(a_t, 3))
    ej = lax.broadcasted_iota(jnp.int32, (128, GROUP_W), 0)
    eh = lax.broadcasted_iota(jnp.int32, (128, GROUP_W), 1) // HEAD_DIM
    expand = (ej == eh + d_off).astype(BF16)
    c_exp = sum(jnp.dot(part, expand, preferred_element_type=F32) for part in _bf16_parts(c_col, 2))
    dt_exp = sum(jnp.dot(part, expand, preferred_element_type=F32) for part in _bf16_parts(dt, 2))
    end = 0 if reverse else L - 1
    cend = c_exp[end:end + 1, :]
    x_dt = xs * dt_exp
    out_decay = jnp.exp(c_exp)
    x_dec = x_dt * jnp.exp(cend - c_exp)
    chunk_decay = jnp.exp(cend)

    head4 = lax.broadcasted_iota(jnp.int32, (L, 256), 1) // HEAD_DIM
    ys = []
    for g in range(2):
        gl = slice(256 * g, 256 * (g + 1))
        b_g = bc[:, 128 * g:128 * (g + 1)]
        c_g = bc[:, 256 + 128 * g:256 + 128 * (g + 1)].astype(BF16)
        cb = lax.dot_general(c_g, b_g.astype(BF16), nt, preferred_element_type=F32)
        ms = []
        for hh in range(4):
            j = d_off + 4 * g + hh
            diff = c_col[:, j:j + 1] - c_row[j:j + 1, :]
            ms.append(cb * jnp.exp(jnp.where(keep, diff, NEG)))
        m_g = jnp.concatenate(ms, axis=0).astype(BF16)
        o = jnp.dot(m_g, x_dt[:, gl].astype(BF16), preferred_element_type=F32)
        y_diag = jnp.where(head4 == 0, o[0:L], 0.0)
        for hh in range(1, 4):
            y_diag = y_diag + jnp.where(head4 == hh, o[hh * L:(hh + 1) * L], 0.0)
        st = st_ref[g]
        y_off = jnp.dot(c_g, st.astype(BF16), preferred_element_type=F32) * out_decay[:, gl]
        ys.append(y_diag + y_off)
        st_ref[g] = chunk_decay[:, gl] * st + jnp.dot(
            b_g.T.astype(BF16), x_dec[:, gl].astype(BF16), preferred_element_type=F32)
    y = jnp.concatenate(ys, axis=1)

    if finalize:
        y = y + yf_ref[...] + xs * dskip_ref[...]
        y = y * _silu(z_ref[...].astype(F32))
        halves = []
        for g in range(2):
            yg = y[:, 256 * g:256 * (g + 1)]
            halves.append(yg * lax.rsqrt(jnp.mean(yg * yg, axis=-1, keepdims=True) + EPS))
        y = jnp.concatenate(halves, axis=1) * nw_ref[...]
    y_ref[...] = y.astype(y_ref.dtype)

    @pl.when(c == nc - 1)
    def _():
        ht_ref[...] = st_ref[...]


def _ssd_direction(p, consts, h0, reverse, yf=None):
    cwx, cbx, cwb, cbb, dtb, alog, dskip, nw = consts
    bsz, t, _ = p.shape
    nc = t // SSM_CHUNK
    finalize = yf is not None
    hb = SSM_CHUNK // HALO
    nhalo = t // HALO

    def cid(c):
        return (nc - 1 - c) if reverse else c

    def cur(col, width):
        return pl.BlockSpec((None, SSM_CHUNK, width), lambda b, c: (b, cid(c), col // width))

    def prev(col, width):
        return pl.BlockSpec((None, HALO, width), lambda b, c: (b, jnp.maximum(cid(c) * hb - 1, 0), col // width))

    def nxt(col, width):
        return pl.BlockSpec((None, HALO, width), lambda b, c: (b, jnp.minimum((cid(c) + 1) * hb, nhalo - 1), col // width))

    def const(arr):
        return pl.BlockSpec(arr.shape, lambda b, c: (0,) * arr.ndim)

    state_spec = pl.BlockSpec((None, 2, SSM_STATE, 256), lambda b, c: (b, 0, 0, 0))
    in_specs = [cur(COL_SX, 512), cur(COL_SBC, 512), prev(COL_SX, 512), nxt(COL_SX, 512),
                prev(COL_SBC, 512), nxt(COL_SBC, 512), cur(COL_DT, 128),
                const(cwx), const(cbx), const(cwb), const(cbb), const(dtb), const(alog), state_spec]
    args = [p, p, p, p, p, p, p, cwx, cbx, cwb, cbb, dtb, alog, h0]
    if finalize:
        in_specs += [cur(COL_SZ, 512),
                     pl.BlockSpec((None, SSM_CHUNK, GROUP_W), lambda b, c: (b, cid(c), 0)),
                     const(dskip), const(nw)]
        args += [p, yf, dskip, nw]
    y, ht = pl.pallas_call(
        functools.partial(_ssd_kernel, reverse=reverse, finalize=finalize, nc=nc),
        grid=(bsz, nc),
        in_specs=in_specs,
        out_specs=[pl.BlockSpec((None, SSM_CHUNK, GROUP_W), lambda b, c: (b, cid(c), 0)), state_spec],
        out_shape=[jax.ShapeDtypeStruct((bsz, t, GROUP_W), BF16 if finalize else F32),
                   jax.ShapeDtypeStruct((bsz, 2, SSM_STATE, 256), F32)],
        scratch_shapes=[pltpu.VMEM((2, SSM_STATE, 256), F32)],
        compiler_params=_cparams(("parallel", "arbitrary")),
        name="ssd_rev" if reverse else "ssd_fwd",
    )(*args)
    return y, ht


def _ssd_consts(conv_w, conv_b, dt_bias, a_log, d_skip, norm_w):
    cwx, cwb = conv_w[:, :GROUP_W], conv_w[:, GROUP_W:]
    cbx, cbb = conv_b[:GROUP_W].reshape(1, -1), conv_b[GROUP_W:].reshape(1, -1)
    pad = lambda v: jnp.pad(v.reshape(1, -1), ((0, 0), (0, 128 - v.size)))
    return (cwx, cbx, cwb, cbb, pad(dt_bias), pad(a_log),
            jnp.repeat(d_skip, HEAD_DIM).reshape(1, -1), norm_w.reshape(1, -1))


def _ssd_mixer(p, pc, consts):
    bsz = p.shape[0]
    zero = jnp.zeros((bsz, 2, SSM_STATE, 256), F32)
    ycf, hf = _ssd_direction(pc, consts, zero, False)
    yc, hb = _ssd_direction(pc, consts, zero, True, yf=ycf)
    ylf, _ = _ssd_direction(p, consts, hf, False)
    yl, _ = _ssd_direction(p, consts, hb, True, yf=ylf)
    return yl, yc


HY_CB = 128
HY_UNROLL = 16


def _hy_dims(seq):
    n = 2 * seq
    n1 = {4096: 128, 1024: 64, 512: 32, 256: 32, 128: 16}[seq]
    n2 = n // n1
    h = n1 // 2
    return dict(n=n, n1=n1, n2=n2, h=h, pa=h + 8, pb=n2 + 8, pc=n1 + 8)


def _hy_matrices(seq):
    d = _hy_dims(seq)
    n, n1, n2, h = d["n"], d["n1"], d["n2"], d["h"]

    def cis(num, den, sign):
        ang = (2.0 * math.pi / den) * (num % den).astype(F32)
        return jnp.cos(ang), sign * jnp.sin(ang)

    k1 = jnp.arange(n1, dtype=jnp.int32)
    nn = n2 * jnp.arange(n1, dtype=jnp.int32)[None, None, :] + jnp.arange(n2, dtype=jnp.int32)[:, None, None]
    e1r, e1i = cis(k1[None, :, None] * nn, n, -1.0)
    m1f = jnp.concatenate([e1r, e1i], axis=1)
    m1 = jnp.concatenate([jnp.concatenate([e1r[..., :h], -e1i[..., :h]], axis=2),
                          jnp.concatenate([e1i[..., :h], e1r[..., :h]], axis=2)], axis=1)
    a2 = jnp.arange(n2, dtype=jnp.int32)
    g2r, g2i = cis(a2[:, None] * a2[None, :], n2, -1.0)
    m2 = jnp.concatenate([jnp.concatenate([g2r, -g2i], axis=1),
                          jnp.concatenate([g2i, g2r], axis=1)], axis=0)
    num3 = (a2[None, :, None] * a2[None, None, :] * n1 + k1[:, None, None] * a2[None, :, None])
    e3r, e3i = cis(num3, n, 1.0)
    m3 = jnp.concatenate([jnp.concatenate([e3r, -e3i], axis=2),
                          jnp.concatenate([e3i, e3r], axis=2)], axis=1)
    hh = jnp.arange(h, dtype=jnp.int32)
    d4r, d4i = cis(hh[:, None] * k1[None, :], n1, 1.0)
    m4 = jnp.concatenate([jnp.concatenate([d4r, -d4i], axis=1),
                          jnp.concatenate([d4i, d4r], axis=1)], axis=0) / n
    return tuple(m.astype(BF16) for m in (m1f, m1, m2, m3, m4))


def _hy_prep_kernel(p_ref, w_ref, b_ref, o_ref, *, dims):
    n2, h, pa = dims["n2"], dims["h"], dims["pa"]
    seq = h * n2
    o_ref[...] = jnp.zeros_like(o_ref)
    w0, w1, w2, bias = w_ref[0:1, :], w_ref[1:2, :], w_ref[2:3, :], b_ref[...]
    ri = lax.broadcasted_iota(jnp.int32, (n2, o_ref.shape[-1]), 0)

    def body(i, carry):
        r0 = pl.multiple_of(i * n2, n2)
        x = p_ref[pl.ds(r0, n2), :].astype(F32)
        pstart = pl.multiple_of(jnp.maximum(r0 - HALO, 0), HALO)
        nstart = pl.multiple_of(jnp.minimum(r0 + n2, seq - HALO), HALO)
        prev_row = p_ref[pl.ds(pstart, HALO), :].astype(F32)[HALO - 1:HALO, :] * jnp.where(i > 0, 1.0, 0.0).astype(F32)
        next_row = p_ref[pl.ds(nstart, HALO), :].astype(F32)[0:1, :] * jnp.where(i < h - 1, 1.0, 0.0).astype(F32)
        up = jnp.where(ri == 0, prev_row, pltpu.roll(x, 1, 0))
        dn = jnp.where(ri == n2 - 1, next_row, pltpu.roll(x, n2 - 1, 0))
        o_ref[pl.ds(i, n2, stride=pa), :] = up * w0 + x * w1 + dn * w2 + bias
        return carry

    lax.fori_loop(0, h, body, 0)


def _hy_prep(p, short_w, short_b):
    bsz, t, _ = p.shape
    dims = _hy_dims(t)
    rows = dims["n2"] * dims["pa"]
    ncb = 3 * GROUP_W // HY_CB
    return pl.pallas_call(
        functools.partial(_hy_prep_kernel, dims=dims),
        grid=(bsz, ncb),
        in_specs=[pl.BlockSpec((None, t, HY_CB), lambda b, j: (b, 0, COL_HY // HY_CB + j)),
                  pl.BlockSpec((3, HY_CB), lambda b, j: (0, j)),
                  pl.BlockSpec((1, HY_CB), lambda b, j: (0, j))],
        out_specs=pl.BlockSpec((None, rows, HY_CB), lambda b, j: (b, 0, j)),
        out_shape=jax.ShapeDtypeStruct((bsz, rows, 3 * GROUP_W), F32),
        compiler_params=_cparams(("parallel", "parallel")),
        name="hy_prep",
    )(p, short_w, short_b.reshape(1, -1))


def _hy_filter_kernel(z_ref, w1_ref, b1_ref, w2_ref, b2_ref, w3_ref, b3_ref, w4_ref, fr_ref, dl_ref, o_ref):
    hi = lax.Precision.HIGHEST
    z = z_ref[...]
    fr = fr_ref[...]
    h = jnp.sin(fr * (jnp.dot(z, w1_ref[...], preferred_element_type=F32, precision=hi) + b1_ref[...]))
    h = jnp.sin(fr * (jnp.dot(h, w2_ref[...], preferred_element_type=F32, precision=hi) + b2_ref[...]))
    h = jnp.sin(fr * (jnp.dot(h, w3_ref[...], preferred_element_type=F32, precision=hi) + b3_ref[...]))
    full = jnp.dot(h.astype(BF16), w4_ref[...].astype(BF16), preferred_element_type=F32)
    t = z[:, 0:1]
    is_bwd = z[:, HY_EMB:HY_EMB + 1] > 0.5
    live = z[:, HY_EMB + 1:HY_EMB + 2]
    decay = jnp.exp(-t * jnp.abs(dl_ref[...])) * live
    for o in range(2):
        fwd = full[:, o * 2 * GROUP_W:o * 2 * GROUP_W + GROUP_W]
        bwd = full[:, o * 2 * GROUP_W + GROUP_W:(o + 1) * 2 * GROUP_W]
        o_ref[o] = jnp.where(is_bwd, bwd, fwd) * decay


def _hy_filter_features(seq):
    d = _hy_dims(seq)
    n, n1, n2 = d["n"], d["n1"], d["n2"]
    row = np.arange(n)
    time = n2 * (row % n1) + row // n1
    is_bwd = time > seq
    pos = np.where(is_bwd, n - time, time)
    live = (time != seq).astype(np.float64)
    pos = np.where(time == seq, 0, pos)
    t = np.linspace(0.0, 1.0, seq)[pos]
    bands = (HY_EMB - 1) // 2
    f = np.linspace(1e-4, bands - 1, bands)[None]
    wpos = (2.0 * math.pi * pos / seq)[:, None]
    feat = np.zeros((n, 128), np.float64)
    feat[:, 0] = t
    feat[:, 1:1 + bands] = np.cos(f * wpos)
    feat[:, 1 + bands:HY_EMB] = -np.sin(f * wpos)
    feat[:, HY_EMB] = is_bwd
    feat[:, HY_EMB + 1] = live
    return jnp.asarray(feat, F32)


def _hy_filter(seq, w1, b1, w2, b2, w3, b3, w4, freq, tr=512):
    d = _hy_dims(seq)
    n = d["n"]
    tr = min(tr, n)
    feat = _hy_filter_features(seq)
    w1p = jnp.pad(w1, ((0, 128 - HY_EMB), (0, 0)))
    max_decay = math.log(HY_DECAY_TARGET) / HY_FAST_PCT
    min_decay = math.log(HY_DECAY_TARGET) / HY_SLOW_PCT
    deltas = jnp.linspace(min_decay, max_decay, GROUP_W, dtype=F32).reshape(1, -1)
    row = lambda v: v.reshape(1, -1)
    const = lambda a: pl.BlockSpec(a.shape, lambda i: (0,) * a.ndim)
    args = [w1p, row(b1), w2, row(b2), w3, row(b3), w4, row(freq), deltas]
    return pl.pallas_call(
        _hy_filter_kernel,
        grid=(n // tr,),
        in_specs=[pl.BlockSpec((tr, 128), lambda i: (i, 0))] + [const(a) for a in args],
        out_specs=pl.BlockSpec((2, tr, GROUP_W), lambda i: (0, i, 0)),
        out_shape=jax.ShapeDtypeStruct((2, n, GROUP_W), F32),
        compiler_params=_cparams(("parallel",)),
        name="hy_filter",
    )(feat, *args)


def _hy_spectrum_kernel(k_ref, m1f_ref, m2_ref, re_ref, im_ref, tr_ref, ti_ref, *, dims):
    n1, n2, pb = dims["n1"], dims["n2"], dims["pb"]

    def stage1(j, carry):
        r0 = pl.multiple_of(j * n1, n1)
        a = jnp.dot(m1f_ref[j], k_ref[pl.ds(r0, n1), :].astype(BF16), preferred_element_type=F32)
        tr_ref[pl.ds(j, n1, stride=pb), :] = a[:n1]
        ti_ref[pl.ds(j, n1, stride=pb), :] = a[n1:]
        return carry

    lax.fori_loop(0, n2, stage1, 0, unroll=HY_UNROLL)

    def stage2(k, carry):
        r0 = pl.multiple_of(k * pb, 8)
        rhs = jnp.concatenate([tr_ref[pl.ds(r0, n2), :], ti_ref[pl.ds(r0, n2), :]], axis=0).astype(BF16)
        x = jnp.dot(m2_ref[...], rhs, preferred_element_type=F32)
        o0 = pl.multiple_of(k * n2, n2)
        re_ref[pl.ds(o0, n2), :] = x[:n2]
        im_ref[pl.ds(o0, n2), :] = x[n2:]
        return carry

    lax.fori_loop(0, n1, stage2, 0, unroll=HY_UNROLL)


def _hy_spectrum(kern, mats, seq):
    d = _hy_dims(seq)
    n, n1, pb = d["n"], d["n1"], d["pb"]
    m1f, _, m2, _, _ = mats
    ncb = GROUP_W // HY_CB
    blk = pl.BlockSpec((None, n, HY_CB), lambda o, j: (o, 0, j))
    const = lambda a: pl.BlockSpec(a.shape, lambda o, j: (0,) * a.ndim)
    return pl.pallas_call(
        functools.partial(_hy_spectrum_kernel, dims=d),
        grid=(2, ncb),
        in_specs=[blk, const(m1f), const(m2)],
        out_specs=[blk, blk],
        out_shape=[jax.ShapeDtypeStruct((2, n, GROUP_W), F32)] * 2,
        scratch_shapes=[pltpu.VMEM((n1 * pb, HY_CB), F32)] * 2,
        compiler_params=_cparams(("parallel", "parallel")),
        name="hy_spectrum",
    )(kern, m1f, m2)


def _pack_pair(re, im):
    half = jnp.uint32(0x8000)
    r = lax.bitcast_convert_type(re, jnp.uint32) + half
    i = lax.bitcast_convert_type(im, jnp.uint32) + half
    return (r & jnp.uint32(0xFFFF0000)) | (i >> 16)


def _unpack_pair(w):
    re = lax.bitcast_convert_type(w & jnp.uint32(0xFFFF0000), F32)
    im = lax.bitcast_convert_type(w << 16, F32)
    return re, im


def _hy_conv_kernel(u_ref, g_ref, kr_ref, ki_ref, skip_ref, m1_ref, m2_ref, m3_ref, m4_ref, o_ref,
                    t1, t2, *, dims, natural_out):
    n1, n2, h, pa, pb, pc = (dims[k] for k in ("n1", "n2", "h", "pa", "pb", "pc"))

    def fwd1(j, carry):
        r0 = pl.multiple_of(j * pa, 8)
        rhs = jnp.concatenate([u_ref[0, pl.ds(r0, h), :], u_ref[1, pl.ds(r0, h), :]], axis=0).astype(BF16)
        a = jnp.dot(m1_ref[j], rhs, preferred_element_type=F32)
        t1[pl.ds(j, n1, stride=pb), :] = _pack_pair(a[:n1], a[n1:])
        return carry

    lax.fori_loop(0, n2, fwd1, 0, unroll=HY_UNROLL)

    def mid(k, carry):
        r0 = pl.multiple_of(k * pb, 8)
        rhs = jnp.concatenate(_unpack_pair(t1[pl.ds(r0, n2), :]), axis=0).astype(BF16)
        x = jnp.dot(m2_ref[...], rhs, preferred_element_type=F32)
        f0 = pl.multiple_of(k * n2, n2)
        fr, fi = kr_ref[pl.ds(f0, n2), :], ki_ref[pl.ds(f0, n2), :]
        xr, xi = x[:n2], x[n2:]
        y = jnp.concatenate([xr * fr - xi * fi, xr * fi + xi * fr], axis=0).astype(BF16)
        c = jnp.dot(m3_ref[k], y, preferred_element_type=F32)
        t2[pl.ds(k, n2, stride=pc), :] = _pack_pair(c[:n2], c[n2:])
        return carry

    lax.fori_loop(0, n1, mid, 0, unroll=HY_UNROLL)

    if not natural_out:
        o_ref[...] = jnp.zeros_like(o_ref)
    skip = skip_ref[...]

    def inv2(j, carry):
        r0 = pl.multiple_of(j * pc, 8)
        rhs = jnp.concatenate(_unpack_pair(t2[pl.ds(r0, n1), :]), axis=0).astype(BF16)
        y = jnp.dot(m4_ref[...], rhs, preferred_element_type=F32)
        a0 = pl.multiple_of(j * pa, 8)
        for e in range(2):
            val = g_ref[e, pl.ds(a0, h), :] * (y[e * h:(e + 1) * h] + skip * u_ref[e, pl.ds(a0, h), :])
            if natural_out:
                o_ref[e, pl.ds(j, h, stride=n2), :] = val.astype(o_ref.dtype)
            else:
                o_ref[e, pl.ds(a0, h), :] = val
        return carry

    lax.fori_loop(0, n2, inv2, 0, unroll=HY_UNROLL)


def _hy_conv(u_arr, u_col, g_arr, g_col, kf_re, kf_im, order, skip, mats, seq, natural_out):
    d = _hy_dims(seq)
    bsz = u_arr.shape[0]
    rows = d["n2"] * d["pa"]
    _, m1, m2, m3, m4 = mats
    ncb = GROUP_W // HY_CB
    single = pl.Buffered(1)
    const = lambda a: pl.BlockSpec(a.shape, lambda j, p: (0,) * a.ndim, pipeline_mode=single)
    in_specs = [
        pl.BlockSpec((2, rows, HY_CB), lambda j, p: (p, 0, u_col // HY_CB + j)),
        pl.BlockSpec((2, rows, HY_CB), lambda j, p: (p, 0, g_col // HY_CB + j)),
        pl.BlockSpec((None, d["n"], HY_CB), lambda j, p: (order, 0, j), pipeline_mode=single),
        pl.BlockSpec((None, d["n"], HY_CB), lambda j, p: (order, 0, j), pipeline_mode=single),
        pl.BlockSpec((None, 1, HY_CB), lambda j, p: (order, 0, j)),
        const(m1), const(m2), const(m3), const(m4),
    ]
    if natural_out:
        out_spec = pl.BlockSpec((2, seq, HY_CB), lambda j, p: (p, 0, j))
        out_shape = jax.ShapeDtypeStruct((bsz, seq, GROUP_W), F32)
    else:
        out_spec = pl.BlockSpec((2, rows, HY_CB), lambda j, p: (p, 0, j))
        out_shape = jax.ShapeDtypeStruct((bsz, rows, GROUP_W), F32)
    return pl.pallas_call(
        functools.partial(_hy_conv_kernel, dims=d, natural_out=natural_out),
        grid=(ncb, bsz // 2),
        in_specs=in_specs,
        out_specs=out_spec,
        out_shape=out_shape,
        scratch_shapes=[pltpu.VMEM((d["n1"] * d["pb"], HY_CB), jnp.uint32),
                        pltpu.VMEM((d["n2"] * d["pc"], HY_CB), jnp.uint32)],
        compiler_params=_cparams(("parallel", "parallel")),
        name="hy_conv",
    )(u_arr, g_arr, kf_re, kf_im, skip.reshape(2, 1, GROUP_W), m1, m2, m3, m4)


def _hyena_mixer(p, short_w, short_b, filt_params, skip, mats):
    seq = p.shape[1]
    kern = _hy_filter(seq, *filt_params)
    kf_re, kf_im = _hy_spectrum(kern, mats, seq)
    ut = _hy_prep(p, short_w, short_b)
    zt = _hy_conv(ut, 2 * GROUP_W, ut, 0, kf_re, kf_im, 0, skip, mats, seq, natural_out=False)
    return _hy_conv(zt, 0, ut, GROUP_W, kf_re, kf_im, 1, skip, mats, seq, natural_out=True)


def _prep_weights(w_in, w_out):
    depth, d, _ = w_in.shape
    perm = np.asarray(HEAD_ORDER)
    q = w_in[..., :512].reshape(depth, d, N_HEADS, HEAD_DIM)[:, :, perm].reshape(depth, d, 512)
    o_a, o_ssm = 768, 768 + 3072 + 1536 + 16
    pieces = [q, w_in[..., 512:o_a], w_in[..., o_ssm - 16:o_ssm],
              jnp.zeros((depth, d, COL_HY - COL_DT - 16), w_in.dtype), w_in[..., o_a:o_ssm - 16]]
    w_in_p = jnp.concatenate(pieces, axis=-1).astype(BF16)
    assert w_in_p.shape[-1] == N_PROJ
    dm = w_out.shape[-1]
    wo_a = w_out[:, :512].reshape(depth, N_HEADS, HEAD_DIM, dm)[:, perm].reshape(depth, 512, dm)
    w_out_p = jnp.concatenate([wo_a, w_out[:, 512:]], axis=1).astype(BF16)
    return w_in_p, w_out_p


TILE_W_IN = (1024, 1408)
TILE_MLP_UP = (1024, 2048)
TILE_DOWN = (1024, 1024, 2048)
TILE_OUT = (1024, 1024)


def kernel(x, c, ctx, c_ctx, ada_w, ada_b, norm_mix, norm_mlp, w_in, w_out, attn_sink,
           hy_short_w, hy_short_b, hy_w1, hy_b1, hy_w2, hy_b2, hy_w3, hy_b3, hy_w4, hy_freq, hy_skip,
           na_rpb, ssm_conv_w, ssm_conv_b, ssm_dt_bias, ssm_a_log, ssm_d, ssm_norm,
           mlp_w1, mlp_w2, final_norm):
    bsz, seq, d = x.shape
    lc = ctx.shape[1]
    depth = ada_w.shape[0]
    assert bsz % 2 == 0 and bsz <= 7 and d == D_MODEL

    cs = jnp.zeros((8, d), F32).at[:bsz].set(c).at[bsz].set(c_ctx)
    mod = _ada_mod(cs, ada_w, ada_b)
    cos_t, sin_t = _rope_tables(seq)
    na_plan = _na_plan(seq)
    bias_tabs = _na_bias_tables(na_rpb, na_plan[-1])
    mats_l = _hy_matrices(seq)
    mats_c = _hy_matrices(lc)
    lat_row = lambda b: b
    ctx_row = lambda b: bsz
    tm_c = lc

    w_in_p, w_out_p = _prep_weights(w_in, w_out)
    w1_b = mlp_w1.astype(BF16)
    w2_b = mlp_w2.astype(BF16)

    xc = ctx
    for i in range(depth):
        last = i == depth - 1
        mod3 = mod[i].reshape(8, 1, 6 * d)
        p = _norm_matmul(x, norm_mix[i], mod3, lat_row, 0, 1, w_in_p, i, P_DTYPE, False, *TILE_W_IN)
        pc = _norm_matmul(xc, norm_mix[i], mod3, ctx_row, 0, 1, w_in_p, i, P_DTYPE, False, tm_c, TILE_W_IN[1])

        filt_params = (hy_w1[i], hy_b1[i], hy_w2[i], hy_b2[i], hy_w3[i], hy_b3[i], hy_w4[i], hy_freq[i])
        ssd_consts = _ssd_consts(ssm_conv_w[i], ssm_conv_b[i], ssm_dt_bias[i], ssm_a_log[i], ssm_d[i], ssm_norm[i])

        ya = _win_attn(attn_sink[i], p, pc, cos_t, sin_t, local=True)
        yb = _hyena_mixer(p, hy_short_w[i], hy_short_b[i], filt_params, hy_skip[i], mats_l)
        yn = _na_attn(p, pc, bias_tabs, i, na_plan, local=True)
        yd, ydc = _ssd_mixer(p, pc, ssd_consts)
        x = _out_proj((ya, yb, yn, yd), w_out_p, i, x, mod3, lat_row, 2, *TILE_OUT)
        hid = _norm_matmul(x, norm_mlp[i], mod3, lat_row, 3, 4, w1_b, i, BF16, True, *TILE_MLP_UP)
        x = _matmul_residual(hid, w2_b, i, x, mod3, lat_row, 5, *TILE_DOWN)

        if not last:
            yac = _win_attn(attn_sink[i], pc, pc, None, None, local=False)
            ybc = _hyena_mixer(pc, hy_short_w[i], hy_short_b[i], filt_params, hy_skip[i], mats_c)
            ync = _na_attn(pc, pc, None, i, None, local=False)
            xc = _out_proj((yac, ybc, ync, ydc), w_out_p, i, xc, mod3, ctx_row, 2, tm_c, TILE_OUT[1])
            hidc = _norm_matmul(xc, norm_mlp[i], mod3, ctx_row, 3, 4, w1_b, i, BF16, True, tm_c, TILE_MLP_UP[1])
            xc = _matmul_residual(hidc, w2_b, i, xc, mod3, ctx_row, 5, tm_c, *TILE_DOWN[1:])
    return _final_norm(x, final_norm)
```

```python
import functools
import math

import numpy as np
import jax
import jax.numpy as jnp
from jax import lax
from jax.experimental import pallas as pl
from jax.experimental.pallas import tpu as pltpu

F32 = jnp.float32
BF16 = jnp.bfloat16

D_MODEL = 2048
GRID_W = 64
EPS = 1e-6
NEG = -1e30
HEAD_DIM = 64
GROUP_W = D_MODEL // 4
N_HEADS = GROUP_W // HEAD_DIM
WINDOW = 128
BLOCK = 128
ROPE_BASE = 10000.0
HY_EMB = 33
HY_FFN = 64
HY_DECAY_TARGET = 1e-2
HY_FAST_PCT = 0.3
HY_SLOW_PCT = 1.5
NA_KR = 8
NA_KC = 16
SSM_STATE = 128
SSM_CHUNK = 128
D_FF = 4 * D_MODEL
SCALE = HEAD_DIM ** -0.5
LOG2E = math.log2(math.e)

COL_QA, COL_KA, COL_VA, COL_DT = 0, 512, 640, 768
COL_HY = 1024
COL_NQ, COL_NK, COL_NV = 2560, 3072, 3584
COL_SZ, COL_SX, COL_SBC = 4096, 4608, 5120
N_PROJ = 5632
HEAD_ORDER = (0, 4, 1, 5, 2, 6, 3, 7)

P_DTYPE = BF16
HALO = 16

V7X_VMEM_BYTES = 64 * 1024 * 1024
VMEM_LIMIT = 56 * 1024 * 1024


def _cparams(sem):
    return pltpu.CompilerParams(dimension_semantics=sem, vmem_limit_bytes=VMEM_LIMIT)


def _silu(x):
    return x * jax.nn.sigmoid(x)


def _ada_kernel(cs_ref, w_ref, b_ref, o_ref):
    a = _silu(cs_ref[...]).astype(BF16)
    o_ref[...] = jnp.dot(a, w_ref[...].astype(BF16), preferred_element_type=F32) + b_ref[...]


def _ada_mod(cs, ada_w, ada_b, tn=1024):
    depth, d, n = ada_w.shape
    return pl.pallas_call(
        _ada_kernel,
        grid=(depth, n // tn),
        in_specs=[
            pl.BlockSpec((8, d), lambda i, j: (0, 0)),
            pl.BlockSpec((None, d, tn), lambda i, j: (i, 0, j)),
            pl.BlockSpec((None, 1, tn), lambda i, j: (i, 0, j)),
        ],
        out_specs=pl.BlockSpec((None, 8, tn), lambda i, j: (i, 0, j)),
        out_shape=jax.ShapeDtypeStruct((depth, 8, n), F32),
        compiler_params=_cparams(("parallel", "parallel")),
        name="ada_mod",
    )(cs, ada_w, ada_b.reshape(depth, 1, n))


NORM_ROWS = 16


def _norm_matmul_kernel(x_ref, g_ref, sh_ref, sc_ref, w_ref, o_ref, h_ref, *, act):
    @pl.when(pl.program_id(2) == 0)
    def _():
        gain = g_ref[...] * (1.0 + sc_ref[...])
        shift = sh_ref[...]

        def rows(r, carry):
            r0 = pl.multiple_of(r * NORM_ROWS, NORM_ROWS)
            xf = x_ref[pl.ds(r0, NORM_ROWS), :]
            ms = jnp.mean(xf * xf, axis=-1, keepdims=True)
            h_ref[pl.ds(r0, NORM_ROWS), :] = (xf * lax.rsqrt(ms + EPS) * gain + shift).astype(BF16)
            return carry

        lax.fori_loop(0, x_ref.shape[0] // NORM_ROWS, rows, 0, unroll=8)

    r = jnp.dot(h_ref[...], w_ref[...], preferred_element_type=F32)
    if act:
        r = jnp.square(jnp.maximum(r, 0.0))
    o_ref[...] = r.astype(o_ref.dtype)


def _norm_matmul(x, g, mod3, row_of_b, sh_idx, sc_idx, w, layer, out_dtype, act, tm, tn):
    bsz, t, d = x.shape
    n = w.shape[2]
    tm = min(tm, t)
    return pl.pallas_call(
        functools.partial(_norm_matmul_kernel, act=act),
        grid=(bsz, t // tm, n // tn),
        in_specs=[
            pl.BlockSpec((None, tm, d), lambda b, m, j: (b, m, 0)),
            pl.BlockSpec((1, d), lambda b, m, j: (0, 0)),
            pl.BlockSpec((None, 1, d), lambda b, m, j: (row_of_b(b), 0, sh_idx)),
            pl.BlockSpec((None, 1, d), lambda b, m, j: (row_of_b(b), 0, sc_idx)),
            pl.BlockSpec((None, d, tn), lambda b, m, j: (layer, 0, j)),
        ],
        out_specs=pl.BlockSpec((None, tm, tn), lambda b, m, j: (b, m, j)),
        out_shape=jax.ShapeDtypeStruct((bsz, t, n), out_dtype),
        scratch_shapes=[pltpu.VMEM((tm, d), BF16)],
        compiler_params=_cparams(("parallel", "parallel", "arbitrary")),
        name="norm_matmul",
    )(x, g.reshape(1, d), mod3, mod3, w)


def _mm_res_kernel(a_ref, w_ref, x_ref, gate_ref, o_ref, acc_ref, *, nk):
    k = pl.program_id(3)

    @pl.when(k == 0)
    def _():
        acc_ref[...] = jnp.zeros_like(acc_ref)

    acc_ref[...] += jnp.dot(a_ref[...], w_ref[...], preferred_element_type=F32)

    @pl.when(k == nk - 1)
    def _():
        o_ref[...] = x_ref[...] + gate_ref[...] * acc_ref[...]


def _matmul_residual(a, w, layer, x, mod3, row_of_b, gate_idx, tm, tn, tk):
    bsz, t, kdim = a.shape
    n = w.shape[2]
    tm = min(tm, t)
    nk = kdim // tk
    return pl.pallas_call(
        functools.partial(_mm_res_kernel, nk=nk),
        grid=(bsz, t // tm, n // tn, nk),
        in_specs=[
            pl.BlockSpec((None, tm, tk), lambda b, m, j, k: (b, m, k)),
            pl.BlockSpec((None, tk, tn), lambda b, m, j, k: (layer, k, j)),
            pl.BlockSpec((None, tm, tn), lambda b, m, j, k: (b, m, j)),
            pl.BlockSpec((None, 1, tn), lambda b, m, j, k: (row_of_b(b), 0, gate_idx * (D_MODEL // tn) + j)),
        ],
        out_specs=pl.BlockSpec((None, tm, tn), lambda b, m, j, k: (b, m, j)),
        out_shape=jax.ShapeDtypeStruct(x.shape, F32),
        scratch_shapes=[pltpu.VMEM((tm, tn), F32)],
        compiler_params=_cparams(("parallel", "parallel", "parallel", "arbitrary")),
        name="matmul_residual",
    )(a, w, x, mod3)


def _out_proj_kernel(ya_ref, yb_ref, yn_ref, yd_ref, w_ref, x_ref, gate_ref, o_ref):
    acc = None
    for g, y_ref in enumerate((ya_ref, yb_ref, yn_ref, yd_ref)):
        part = jnp.dot(y_ref[...].astype(BF16), w_ref[GROUP_W * g:GROUP_W * (g + 1), :],
                       preferred_element_type=F32)
        acc = part if acc is None else acc + part
    o_ref[...] = x_ref[...] + gate_ref[...] * acc


def _out_proj(ys, w, layer, x, mod3, row_of_b, gate_idx, tm, tn):
    bsz, t, d = x.shape
    tm = min(tm, t)
    y_spec = pl.BlockSpec((None, tm, GROUP_W), lambda b, m, j: (b, m, 0))
    return pl.pallas_call(
        _out_proj_kernel,
        grid=(bsz, t // tm, d // tn),
        in_specs=[y_spec, y_spec, y_spec, y_spec,
                  pl.BlockSpec((None, 4 * GROUP_W, tn), lambda b, m, j: (layer, 0, j),
                               pipeline_mode=pl.Buffered(1) if tn == d else None),
                  pl.BlockSpec((None, tm, tn), lambda b, m, j: (b, m, j)),
                  pl.BlockSpec((None, 1, tn), lambda b, m, j: (row_of_b(b), 0, gate_idx * (d // tn) + j))],
        out_specs=pl.BlockSpec((None, tm, tn), lambda b, m, j: (b, m, j)),
        out_shape=jax.ShapeDtypeStruct(x.shape, F32),
        compiler_params=_cparams(("parallel", "parallel", "parallel")),
        name="out_proj",
    )(*ys, w, x, mod3)


def _final_norm_kernel(x_ref, g_ref, o_ref):
    xf = x_ref[...]
    ms = jnp.mean(xf * xf, axis=-1, keepdims=True)
    o_ref[...] = xf * lax.rsqrt(ms + EPS) * g_ref[...]


def _final_norm(x, g, tm=1024):
    bsz, t, d = x.shape
    tm = min(tm, t)
    return pl.pallas_call(
        _final_norm_kernel,
        grid=(bsz, t // tm),
        in_specs=[pl.BlockSpec((None, tm, d), lambda b, m: (b, m, 0)),
                  pl.BlockSpec((1, d), lambda b, m: (0, 0))],
        out_specs=pl.BlockSpec((None, tm, d), lambda b, m: (b, m, 0)),
        out_shape=jax.ShapeDtypeStruct(x.shape, F32),
        compiler_params=_cparams(("parallel", "parallel")),
        name="final_norm",
    )(x, g.reshape(1, d))


def _rope(x, cos, sin_signed, lane_lo):
    w = x.shape[-1]
    partner = jnp.where(lane_lo, pltpu.roll(x, w - 16, 1), pltpu.roll(x, 16, 1))
    return x * cos + partner * sin_signed


def _win_attn_kernel(sink_ref, q_ref, kc_ref, vc_ref, *rest, seq, local):
    if local:
        k_ref, v_ref, cos_ref, sin_ref, mask_ref, o_ref = rest
    else:
        (o_ref,) = rest
    n = pl.program_id(1)
    lane = lax.broadcasted_iota(jnp.int32, (BLOCK, 128), 1)
    lo = lane < HEAD_DIM
    q = q_ref[...].astype(F32) * (SCALE * LOG2E)
    if local:
        lane_lo = (lane % 32) < 16
        r0 = pl.multiple_of(n * BLOCK, BLOCK)
        cos_q = cos_ref[pl.ds(r0, BLOCK), :]
        sin_q = sin_ref[pl.ds(r0, BLOCK), :]
    rows = []
    for m in range(4):
        qm = q[:, 128 * m:128 * (m + 1)]
        if local:
            qm = _rope(qm, cos_q, sin_q, lane_lo)
        rows.append(jnp.where(lo, qm, 0.0))
        rows.append(jnp.where(lo, 0.0, qm))
    qbd = jnp.concatenate(rows, axis=0).astype(BF16)
    nt = (((1,), (1,)), ((), ()))
    kc = kc_ref[...].astype(BF16)
    vc = vc_ref[...].astype(BF16)
    s_ctx = lax.dot_general(qbd, kc, nt, preferred_element_type=F32)
    sink = jnp.concatenate(
        [jnp.full((BLOCK, 1), sink_ref[HEAD_ORDER[i]] * LOG2E, F32) for i in range(N_HEADS)], axis=0)
    mx = jnp.maximum(jnp.max(s_ctx, axis=-1, keepdims=True), sink)
    if local:
        start = pl.multiple_of(jnp.clip((n - 1) * BLOCK, 0, seq - 3 * BLOCK), BLOCK)
        lane3 = lax.broadcasted_iota(jnp.int32, (3 * BLOCK, 128), 1)
        kb = _rope(k_ref[pl.ds(start, 3 * BLOCK), :].astype(F32), cos_ref[pl.ds(start, 3 * BLOCK), :],
                   sin_ref[pl.ds(start, 3 * BLOCK), :], (lane3 % 32) < 16).astype(BF16)
        vb = v_ref[pl.ds(start, 3 * BLOCK), :].astype(BF16)
        s_loc = lax.dot_general(qbd, kb, nt, preferred_element_type=F32)
        s_loc = (s_loc.reshape(N_HEADS, BLOCK, 3 * BLOCK) + mask_ref[...][None]).reshape(s_loc.shape)
        mx = jnp.maximum(mx, jnp.max(s_loc, axis=-1, keepdims=True))
    p_ctx = jnp.exp2(s_ctx - mx)
    den = jnp.sum(p_ctx, axis=-1, keepdims=True) + jnp.exp2(sink - mx)
    acc = jnp.dot(p_ctx.astype(BF16), vc, preferred_element_type=F32)
    if local:
        p_loc = jnp.exp2(s_loc - mx)
        den = den + jnp.sum(p_loc, axis=-1, keepdims=True)
        acc = acc + jnp.dot(p_loc.astype(BF16), vb, preferred_element_type=F32)
    o = acc / den
    outs = [jnp.where(lo, o[(2 * m) * BLOCK:(2 * m + 1) * BLOCK], o[(2 * m + 1) * BLOCK:(2 * m + 2) * BLOCK])
            for m in range(4)]
    o_ref[...] = jnp.concatenate(outs, axis=1).astype(o_ref.dtype)


def _win_attn(sink, pq, pc, cos_t, sin_t, local):
    bsz, t, _ = pq.shape
    lc = pc.shape[1]
    nb = t // BLOCK
    in_specs = [
        pl.BlockSpec(memory_space=pltpu.SMEM),
        pl.BlockSpec((None, BLOCK, 512), lambda b, n: (b, n, COL_QA // 512)),
        pl.BlockSpec((None, lc, 128), lambda b, n: (b, 0, COL_KA // 128)),
        pl.BlockSpec((None, lc, 128), lambda b, n: (b, 0, COL_VA // 128)),
    ]
    args = [sink, pq, pc, pc]
    if local:
        in_specs += [
            pl.BlockSpec((None, t, 128), lambda b, n: (b, 0, COL_KA // 128)),
            pl.BlockSpec((None, t, 128), lambda b, n: (b, 0, COL_VA // 128)),
            pl.BlockSpec((t, 128), lambda b, n: (0, 0)),
            pl.BlockSpec((t, 128), lambda b, n: (0, 0)),
            pl.BlockSpec((None, BLOCK, 3 * BLOCK),
                         lambda b, n: (jnp.where(n == 0, 0, jnp.where(n == nb - 1, 2, 1)), 0, 0)),
        ]
        args += [pq, pq, cos_t, sin_t, _window_masks(t)]
    return pl.pallas_call(
        functools.partial(_win_attn_kernel, seq=t, local=local),
        grid=(bsz, t // BLOCK),
        in_specs=in_specs,
        out_specs=pl.BlockSpec((None, BLOCK, GROUP_W), lambda b, n: (b, n, 0)),
        out_shape=jax.ShapeDtypeStruct((bsz, t, GROUP_W), BF16),
        compiler_params=_cparams(("parallel", "arbitrary")),
        name="win_attn" if local else "ctx_attn_a",
    )(*args)


def _window_masks(seq):
    nb = seq // BLOCK
    qi = np.arange(BLOCK)[:, None]
    kj = np.arange(3 * BLOCK)[None, :]
    tabs = []
    for n in (0, 1, nb - 1):
        start = int(np.clip((n - 1) * BLOCK, 0, seq - 3 * BLOCK))
        rel = (start + kj) - (n * BLOCK + qi)
        tabs.append(np.where(np.abs(rel) <= WINDOW, 0.0, NEG))
    return jnp.asarray(np.stack(tabs), F32)


def _rope_tables(seq):
    t = np.arange(seq)
    row, col = t // GRID_W, t % GRID_W
    quarter = HEAD_DIM // 4
    inv = ROPE_BASE ** (-np.arange(quarter, dtype=np.float64) / quarter)
    inv = inv.astype(np.float32).astype(np.float64)
    lane = np.arange(128)
    j = lane % HEAD_DIM
    pos = np.where((j < HEAD_DIM // 2)[None, :], row[:, None], col[:, None]).astype(np.float64)
    ang = (pos * inv[j % quarter][None, :]).astype(np.float32)
    cos = np.cos(ang.astype(np.float64))
    sin = np.sin(ang.astype(np.float64))
    sign = np.where((j % 32) < 16, -1.0, 1.0)[None, :]
    return jnp.asarray(cos, F32), jnp.asarray(sin * sign, F32)


NA_ROWS_PER_STEP = 4


def _na_kernel(var_ref, ws_ref, q_ref, kc_ref, vc_ref, *rest, local, win_rows):
    del var_ref
    if local:
        k_ref, v_ref, bias_ref, o_ref = rest
    else:
        (o_ref,) = rest
    g = pl.program_id(1)
    tq = q_ref.shape[0]
    q = q_ref[...].astype(F32) * (SCALE * LOG2E)
    head = lax.broadcasted_iota(jnp.int32, (tq, 256), 1) // HEAD_DIM
    nt = (((1,), (1,)), ((), ()))
    outs = []
    for half in range(2):
        cols = slice(256 * half, 256 * (half + 1))
        q4 = q[:, cols]
        qbd = jnp.concatenate([jnp.where(head == h, q4, 0.0) for h in range(4)], axis=0).astype(BF16)
        kc4 = kc_ref[:, cols].astype(BF16)
        vc4 = vc_ref[:, cols].astype(BF16)
        s_ctx = lax.dot_general(qbd, kc4, nt, preferred_element_type=F32)
        mx = jnp.max(s_ctx, axis=-1, keepdims=True)
        if local:
            nkey = win_rows * GRID_W
            start = pl.multiple_of(ws_ref[g] * GRID_W, GRID_W)
            k4 = k_ref[pl.ds(start, nkey), cols].astype(BF16)
            v4 = v_ref[pl.ds(start, nkey), cols].astype(BF16)
            bias = bias_ref[4 * half:4 * half + 4].astype(F32).reshape(4 * tq, nkey)
            s_loc = lax.dot_general(qbd, k4, nt, preferred_element_type=F32) + bias
            mx = jnp.maximum(mx, jnp.max(s_loc, axis=-1, keepdims=True))
        p_ctx = jnp.exp2(s_ctx - mx)
        den = jnp.sum(p_ctx, axis=-1, keepdims=True)
        acc = jnp.dot(p_ctx.astype(BF16), vc4, preferred_element_type=F32)
        if local:
            p_loc = jnp.exp2(s_loc - mx)
            den = den + jnp.sum(p_loc, axis=-1, keepdims=True)
            acc = acc + jnp.dot(p_loc.astype(BF16), v4, preferred_element_type=F32)
        o = acc / den
        o4 = jnp.where(head == 0, o[0:tq], 0.0)
        for h in range(1, 4):
            o4 = o4 + jnp.where(head == h, o[h * tq:(h + 1) * tq], 0.0)
        outs.append(o4)
    o_ref[...] = jnp.concatenate(outs, axis=1).astype(o_ref.dtype)


def _na_plan(seq):
    rows = seq // GRID_W
    kr = min(NA_KR, rows)
    r_step = NA_ROWS_PER_STEP
    win_rows = min(r_step + kr, rows)
    n_groups = rows // r_step
    wstart = np.zeros(n_groups, np.int32)
    pats = []
    keys = {}
    var = np.zeros(n_groups, np.int32)
    for g in range(n_groups):
        r0 = g * r_step
        ws = int(np.clip(r0 - kr // 2, 0, rows - win_rows))
        wstart[g] = ws
        r = r0 + np.arange(r_step)
        rstart = np.clip(r - kr // 2, 0, rows - kr)
        krow = ws + np.arange(win_rows)
        valid = (krow[None, :] >= rstart[:, None]) & (krow[None, :] < rstart[:, None] + kr)
        roff = krow[None, :] - r[:, None] + NA_KR - 1
        key = (valid.tobytes(), np.where(valid, roff, 0).tobytes())
        if key not in keys:
            keys[key] = len(pats)
            pats.append((valid, np.where(valid, roff, 0)))
        var[g] = keys[key]
    return rows, win_rows, n_groups, wstart, var, pats


def _na_bias_tables(rpb, pats):
    cq = np.arange(GRID_W)
    ck = np.arange(GRID_W)
    cstart = np.clip(cq - NA_KC // 2, 0, GRID_W - NA_KC)
    col_valid = (ck[None] >= cstart[:, None]) & (ck[None] < cstart[:, None] + NA_KC)
    coff = np.clip(ck[None] - cq[:, None], -(NA_KC - 1), NA_KC - 1) + NA_KC - 1
    by_col = jnp.where(col_valid, rpb[..., coff] * LOG2E, NEG).astype(BF16)
    masked = jnp.full(by_col.shape[:2] + (GRID_W, GRID_W), NEG, BF16)
    tabs = []
    for valid, roff in pats:
        r_step, win_rows = valid.shape
        q_rows = []
        for i in range(r_step):
            blocks = [by_col[:, :, int(roff[i, a])] if valid[i, a] else masked for a in range(win_rows)]
            q_rows.append(jnp.concatenate(blocks, axis=-1))
        tabs.append(jnp.concatenate(q_rows, axis=-2))
    return jnp.stack(tabs, axis=1)


def _na_attn(pq, pc, bias_tabs, layer, plan, local):
    bsz, t, _ = pq.shape
    lc = pc.shape[1]
    if local:
        rows, win_rows, n_groups, wstart, var, _ = plan
        tq = NA_ROWS_PER_STEP * GRID_W
    else:
        win_rows, n_groups, tq = 0, 1, t
        wstart = np.zeros(1, np.int32)
        var = np.zeros(1, np.int32)
    in_specs = [
        pl.BlockSpec((None, tq, 512), lambda b, g, vr, ws: (b, g, COL_NQ // 512)),
        pl.BlockSpec((None, lc, 512), lambda b, g, vr, ws: (b, 0, COL_NK // 512)),
        pl.BlockSpec((None, lc, 512), lambda b, g, vr, ws: (b, 0, COL_NV // 512)),
    ]
    args = [pq, pc, pc]
    if local:
        in_specs += [
            pl.BlockSpec((None, t, 512), lambda b, g, vr, ws: (b, 0, COL_NK // 512)),
            pl.BlockSpec((None, t, 512), lambda b, g, vr, ws: (b, 0, COL_NV // 512)),
            pl.BlockSpec((None, None, N_HEADS, tq, win_rows * GRID_W),
                         lambda b, g, vr, ws: (layer, vr[g], 0, 0, 0)),
        ]
        args += [pq, pq, bias_tabs]
    grid_spec = pltpu.PrefetchScalarGridSpec(
        num_scalar_prefetch=2,
        grid=(bsz, n_groups),
        in_specs=in_specs,
        out_specs=pl.BlockSpec((None, tq, GROUP_W), lambda b, g, vr, ws: (b, g, 0)),
    )
    return pl.pallas_call(
        functools.partial(_na_kernel, local=local, win_rows=win_rows),
        grid_spec=grid_spec,
        out_shape=jax.ShapeDtypeStruct((bsz, t, GROUP_W), BF16),
        compiler_params=_cparams(("parallel", "arbitrary")),
        name="na_attn" if local else "ctx_attn_c",
    )(jnp.asarray(var), jnp.asarray(wstart), *args)


def _softplus(x):
    return jnp.maximum(x, 0.0) + jnp.log(1.0 + jnp.exp(-jnp.abs(x)))


def _bf16_parts(x, n):
    parts = []
    for _ in range(n):
        part = x.astype(BF16)
        parts.append(part)
        x = x - part.astype(F32)
    return parts


def _conv3_silu(cur, prev_blk, next_blk, w_ref, b_ref, has_prev, has_next):
    x = cur.astype(F32)
    rows = x.shape[0]
    prev_row = prev_blk.astype(F32)[HALO - 1:HALO, :] * has_prev
    next_row = next_blk.astype(F32)[0:1, :] * has_next
    ri = lax.broadcasted_iota(jnp.int32, x.shape, 0)
    up = jnp.where(ri == 0, prev_row, pltpu.roll(x, 1, 0))
    dn = jnp.where(ri == rows - 1, next_row, pltpu.roll(x, rows - 1, 0))
    u = up * w_ref[0:1, :] + x * w_ref[1:2, :] + dn * w_ref[2:3, :] + b_ref[...]
    return _silu(u)


def _ssd_kernel(*refs, reverse, nc):
    finalize = reverse
    if reverse:
        (xsc_ref, bcc_ref, dt_ref, dtb_ref, alog_ref, h0_ref,
         z_ref, yf_ref, dskip_ref, nw_ref, y_ref, ht_ref, st_ref) = refs
    else:
        (xs_ref, bc_ref, xsp_ref, xsn_ref, bcp_ref, bcn_ref, dt_ref, cwx_ref, cbx_ref, cwb_ref, cbb_ref,
         dtb_ref, alog_ref, h0_ref, y_ref, ht_ref, xsc_ref, bcc_ref, st_ref) = refs
    c = pl.program_id(1)
    cid = (nc - 1 - c) if reverse else c
    d_off = 8 if reverse else 0
    L = SSM_CHUNK

    @pl.when(c == 0)
    def _():
        st_ref[...] = h0_ref[...]

    if reverse:
        xs = xsc_ref[...]
        bc = bcc_ref[...].astype(F32)
    else:
        has_prev = jnp.where(cid > 0, 1.0, 0.0).astype(F32)
        has_next = jnp.where(cid < nc - 1, 1.0, 0.0).astype(F32)
        xs = _conv3_silu(xs_ref[...], xsp_ref[...], xsn_ref[...], cwx_ref, cbx_ref, has_prev, has_next)
        bc = _conv3_silu(bc_ref[...], bcp_ref[...], bcn_ref[...], cwb_ref, cbb_ref, has_prev, has_next)
        xsc_ref[...] = xs
        bcc_ref[...] = bc.astype(BF16)

    dt = _softplus(dt_ref[...].astype(F32) + dtb_ref[...])
    a = dt * (-jnp.exp(alog_ref[...]))
    ri = lax.broadcasted_iota(jnp.int32, (L, L), 0)
    ci = lax.broadcasted_iota(jnp.int32, (L, L), 1)
    keep = (ci >= ri) if reverse else (ci <= ri)
    tri = keep.astype(BF16)
    nt = (((1,), (1,)), ((), ()))
    a_t = a.T
    c_col = sum(jnp.dot(tri, part, preferred_element_type=F32) for part in _bf16_parts(a, 3))
    c_row = sum(lax.dot_general(part, tri, nt, preferred_element_type=F32) for part in _bf16_parts(a_t, 3))
    ej = lax.broadcasted_iota(jnp.int32, (128, GROUP_W), 0)
    eh = lax.broadcasted_iota(jnp.int32, (128, GROUP_W), 1) // HEAD_DIM
    expand = (ej == eh + d_off).astype(BF16)
    c_exp = sum(jnp.dot(part, expand, preferred_element_type=F32) for part in _bf16_parts(c_col, 2))
    dt_exp = sum(jnp.dot(part, expand, preferred_element_type=F32) for part in _bf16_parts(dt, 2))
    end = 0 if reverse else L - 1
    cend = c_exp[end:end + 1, :]
    x_dt = xs * dt_exp
    out_decay = jnp.exp(c_exp)
    x_dec = x_dt * jnp.exp(cend - c_exp)
    chunk_decay = jnp.exp(cend)

    head4 = lax.broadcasted_iota(jnp.int32, (L, 256), 1) // HEAD_DIM
    ys = []
    for g in range(2):
        gl = slice(256 * g, 256 * (g + 1))
        b_g = bc[:, 128 * g:128 * (g + 1)]
        c_g = bc[:, 256 + 128 * g:256 + 128 * (g + 1)].astype(BF16)
        cb = lax.dot_general(c_g, b_g.astype(BF16), nt, preferred_element_type=F32)
        ms = []
        for hh in range(4):
            j = d_off + 4 * g + hh
            diff = c_col[:, j:j + 1] - c_row[j:j + 1, :]
            ms.append(cb * jnp.exp(jnp.where(keep, diff, NEG)))
        m_g = jnp.concatenate(ms, axis=0).astype(BF16)
        o = jnp.dot(m_g, x_dt[:, gl].astype(BF16), preferred_element_type=F32)
        y_diag = jnp.where(head4 == 0, o[0:L], 0.0)
        for hh in range(1, 4):
            y_diag = y_diag + jnp.where(head4 == hh, o[hh * L:(hh + 1) * L], 0.0)
        st = st_ref[g]
        y_off = jnp.dot(c_g, st.astype(BF16), preferred_element_type=F32) * out_decay[:, gl]
        ys.append(y_diag + y_off)
        st_ref[g] = chunk_decay[:, gl] * st + jnp.dot(
            b_g.T.astype(BF16), x_dec[:, gl].astype(BF16), preferred_element_type=F32)
    y = jnp.concatenate(ys, axis=1)

    if finalize:
        y = y + yf_ref[...] + xs * dskip_ref[...]
        y = y * _silu(z_ref[...].astype(F32))
        halves = []
        for g in range(2):
            yg = y[:, 256 * g:256 * (g + 1)]
            halves.append(yg * lax.rsqrt(jnp.mean(yg * yg, axis=-1, keepdims=True) + EPS))
        y = jnp.concatenate(halves, axis=1) * nw_ref[...]
    y_ref[...] = y.astype(y_ref.dtype)

    @pl.when(c == nc - 1)
    def _():
        ht_ref[...] = st_ref[...]


def _ssd_direction(p, consts, h0, reverse, fwd=None):
    cwx, cbx, cwb, cbb, dtb, alog, dskip, nw = consts
    bsz, t, _ = p.shape
    nc = t // SSM_CHUNK
    hb = SSM_CHUNK // HALO
    nhalo = t // HALO

    def cid(c):
        return (nc - 1 - c) if reverse else c

    def cur(col, width):
        return pl.BlockSpec((None, SSM_CHUNK, width), lambda b, c: (b, cid(c), col // width))

    def prev(col, width):
        return pl.BlockSpec((None, HALO, width), lambda b, c: (b, jnp.maximum(cid(c) * hb - 1, 0), col // width))

    def nxt(col, width):
        return pl.BlockSpec((None, HALO, width), lambda b, c: (b, jnp.minimum((cid(c) + 1) * hb, nhalo - 1), col // width))

    def const(arr):
        return pl.BlockSpec(arr.shape, lambda b, c: (0,) * arr.ndim)

    state_spec = pl.BlockSpec((None, 2, SSM_STATE, 256), lambda b, c: (b, 0, 0, 0))
    chunk_spec = pl.BlockSpec((None, SSM_CHUNK, GROUP_W), lambda b, c: (b, cid(c), 0))
    state_shape = jax.ShapeDtypeStruct((bsz, 2, SSM_STATE, 256), F32)
    if reverse:
        yf, xs_act, bc_act = fwd
        in_specs = [chunk_spec, chunk_spec, cur(COL_DT, 128), const(dtb), const(alog), state_spec,
                    cur(COL_SZ, 512), chunk_spec, const(dskip), const(nw)]
        args = [xs_act, bc_act, p, dtb, alog, h0, p, yf, dskip, nw]
        out_specs = [chunk_spec, state_spec]
        out_shape = [jax.ShapeDtypeStruct((bsz, t, GROUP_W), BF16), state_shape]
    else:
        in_specs = [cur(COL_SX, 512), cur(COL_SBC, 512), prev(COL_SX, 512), nxt(COL_SX, 512),
                    prev(COL_SBC, 512), nxt(COL_SBC, 512), cur(COL_DT, 128),
                    const(cwx), const(cbx), const(cwb), const(cbb), const(dtb), const(alog), state_spec]
        args = [p, p, p, p, p, p, p, cwx, cbx, cwb, cbb, dtb, alog, h0]
        out_specs = [chunk_spec, state_spec, chunk_spec, chunk_spec]
        out_shape = [jax.ShapeDtypeStruct((bsz, t, GROUP_W), F32), state_shape,
                     jax.ShapeDtypeStruct((bsz, t, GROUP_W), F32), jax.ShapeDtypeStruct((bsz, t, GROUP_W), BF16)]
    return pl.pallas_call(
        functools.partial(_ssd_kernel, reverse=reverse, nc=nc),
        grid=(bsz, nc),
        in_specs=in_specs,
        out_specs=out_specs,
        out_shape=out_shape,
        scratch_shapes=[pltpu.VMEM((2, SSM_STATE, 256), F32)],
        compiler_params=_cparams(("parallel", "arbitrary")),
        name="ssd_rev" if reverse else "ssd_fwd",
    )(*args)


def _ssd_consts(conv_w, conv_b, dt_bias, a_log, d_skip, norm_w):
    cwx, cwb = conv_w[:, :GROUP_W], conv_w[:, GROUP_W:]
    cbx, cbb = conv_b[:GROUP_W].reshape(1, -1), conv_b[GROUP_W:].reshape(1, -1)
    pad = lambda v: jnp.pad(v.reshape(1, -1), ((0, 0), (0, 128 - v.size)))
    return (cwx, cbx, cwb, cbb, pad(dt_bias), pad(a_log),
            jnp.repeat(d_skip, HEAD_DIM).reshape(1, -1), norm_w.reshape(1, -1))


def _ssd_mixer(p, pc, consts):
    bsz = p.shape[0]
    zero = jnp.zeros((bsz, 2, SSM_STATE, 256), F32)
    ycf, hf, *act_c = _ssd_direction(pc, consts, zero, False)
    yc, hb = _ssd_direction(pc, consts, zero, True, fwd=(ycf, *act_c))
    ylf, _, *act_l = _ssd_direction(p, consts, hf, False)
    yl, _ = _ssd_direction(p, consts, hb, True, fwd=(ylf, *act_l))
    return yl, yc


HY_CB = 128
HY_UNROLL = 16


def _hy_dims(seq):
    n = 2 * seq
    n1 = {4096: 128, 1024: 64, 512: 32, 256: 32, 128: 16}[seq]
    n2 = n // n1
    h = n1 // 2
    return dict(n=n, n1=n1, n2=n2, h=h, pa=h + 8, pb=n2 + 8, pc=n1 + 8)


def _hy_matrices(seq):
    d = _hy_dims(seq)
    n, n1, n2, h = d["n"], d["n1"], d["n2"], d["h"]

    def cis(num, den, sign):
        ang = (2.0 * math.pi / den) * (num % den).astype(F32)
        return jnp.cos(ang), sign * jnp.sin(ang)

    k1 = jnp.arange(n1, dtype=jnp.int32)
    nn = n2 * jnp.arange(n1, dtype=jnp.int32)[None, None, :] + jnp.arange(n2, dtype=jnp.int32)[:, None, None]
    e1r, e1i = cis(k1[None, :, None] * nn, n, -1.0)
    m1f = jnp.concatenate([e1r, e1i], axis=1)
    m1 = jnp.concatenate([jnp.concatenate([e1r[..., :h], -e1i[..., :h]], axis=2),
                          jnp.concatenate([e1i[..., :h], e1r[..., :h]], axis=2)], axis=1)
    a2 = jnp.arange(n2, dtype=jnp.int32)
    g2r, g2i = cis(a2[:, None] * a2[None, :], n2, -1.0)
    m2 = jnp.concatenate([jnp.concatenate([g2r, -g2i], axis=1),
                          jnp.concatenate([g2i, g2r], axis=1)], axis=0)
    num3 = (a2[None, :, None] * a2[None, None, :] * n1 + k1[:, None, None] * a2[None, :, None])
    e3r, e3i = cis(num3, n, 1.0)
    m3 = jnp.concatenate([jnp.concatenate([e3r, -e3i], axis=2),
                          jnp.concatenate([e3i, e3r], axis=2)], axis=1)
    hh = jnp.arange(h, dtype=jnp.int32)
    d4r, d4i = cis(hh[:, None] * k1[None, :], n1, 1.0)
    m4 = jnp.concatenate([jnp.concatenate([d4r, -d4i], axis=1),
                          jnp.concatenate([d4i, d4r], axis=1)], axis=0) / n
    return tuple(m.astype(BF16) for m in (m1f, m1, m2, m3, m4))


def _hy_prep_kernel(p_ref, w_ref, b_ref, o_ref, *, dims):
    n2, h, pa = dims["n2"], dims["h"], dims["pa"]
    seq = h * n2
    for j in range(n2):
        o_ref[j * pa + h:(j + 1) * pa, :] = jnp.zeros((pa - h, o_ref.shape[-1]), o_ref.dtype)
    w0, w1, w2, bias = w_ref[0:1, :], w_ref[1:2, :], w_ref[2:3, :], b_ref[...]
    ri = lax.broadcasted_iota(jnp.int32, (n2, o_ref.shape[-1]), 0)

    def body(i, carry):
        r0 = pl.multiple_of(i * n2, n2)
        x = p_ref[pl.ds(r0, n2), :].astype(F32)
        pstart = pl.multiple_of(jnp.maximum(r0 - HALO, 0), HALO)
        nstart = pl.multiple_of(jnp.minimum(r0 + n2, seq - HALO), HALO)
        prev_row = p_ref[pl.ds(pstart, HALO), :].astype(F32)[HALO - 1:HALO, :] * jnp.where(i > 0, 1.0, 0.0).astype(F32)
        next_row = p_ref[pl.ds(nstart, HALO), :].astype(F32)[0:1, :] * jnp.where(i < h - 1, 1.0, 0.0).astype(F32)
        up = jnp.where(ri == 0, prev_row, pltpu.roll(x, 1, 0))
        dn = jnp.where(ri == n2 - 1, next_row, pltpu.roll(x, n2 - 1, 0))
        o_ref[pl.ds(i, n2, stride=pa), :] = up * w0 + x * w1 + dn * w2 + bias
        return carry

    lax.fori_loop(0, h, body, 0)


def _hy_prep(p, short_w, short_b):
    bsz, t, _ = p.shape
    dims = _hy_dims(t)
    rows = dims["n2"] * dims["pa"]
    ncb = 3 * GROUP_W // HY_CB
    return pl.pallas_call(
        functools.partial(_hy_prep_kernel, dims=dims),
        grid=(bsz, ncb),
        in_specs=[pl.BlockSpec((None, t, HY_CB), lambda b, j: (b, 0, COL_HY // HY_CB + j)),
                  pl.BlockSpec((3, HY_CB), lambda b, j: (0, j)),
                  pl.BlockSpec((1, HY_CB), lambda b, j: (0, j))],
        out_specs=pl.BlockSpec((None, rows, HY_CB), lambda b, j: (b, 0, j)),
        out_shape=jax.ShapeDtypeStruct((bsz, rows, 3 * GROUP_W), F32),
        compiler_params=_cparams(("parallel", "parallel")),
        name="hy_prep",
    )(p, short_w, short_b.reshape(1, -1))


def _hy_filter_kernel(z_ref, w1_ref, b1_ref, w2_ref, b2_ref, w3_ref, b3_ref, w4_ref, fr_ref, dl_ref, o_ref):
    hi = lax.Precision.HIGHEST
    z = z_ref[...]
    fr = fr_ref[...]
    h = jnp.sin(fr * (jnp.dot(z, w1_ref[...], preferred_element_type=F32, precision=hi) + b1_ref[...]))
    h = jnp.sin(fr * (jnp.dot(h, w2_ref[...], preferred_element_type=F32, precision=hi) + b2_ref[...]))
    h = jnp.sin(fr * (jnp.dot(h, w3_ref[...], preferred_element_type=F32, precision=hi) + b3_ref[...]))
    full = jnp.dot(h.astype(BF16), w4_ref[...].astype(BF16), preferred_element_type=F32)
    t = z[:, 0:1]
    is_bwd = z[:, HY_EMB:HY_EMB + 1] > 0.5
    live = z[:, HY_EMB + 1:HY_EMB + 2]
    decay = jnp.exp(-t * jnp.abs(dl_ref[...])) * live
    for o in range(2):
        fwd = full[:, o * 2 * GROUP_W:o * 2 * GROUP_W + GROUP_W]
        bwd = full[:, o * 2 * GROUP_W + GROUP_W:(o + 1) * 2 * GROUP_W]
        o_ref[o] = jnp.where(is_bwd, bwd, fwd) * decay


def _hy_filter_features(seq):
    d = _hy_dims(seq)
    n, n1, n2 = d["n"], d["n1"], d["n2"]
    row = np.arange(n)
    time = n2 * (row % n1) + row // n1
    is_bwd = time > seq
    pos = np.where(is_bwd, n - time, time)
    live = (time != seq).astype(np.float64)
    pos = np.where(time == seq, 0, pos)
    t = np.linspace(0.0, 1.0, seq)[pos]
    bands = (HY_EMB - 1) // 2
    f = np.linspace(1e-4, bands - 1, bands)[None]
    wpos = (2.0 * math.pi * pos / seq)[:, None]
    feat = np.zeros((n, 128), np.float64)
    feat[:, 0] = t
    feat[:, 1:1 + bands] = np.cos(f * wpos)
    feat[:, 1 + bands:HY_EMB] = -np.sin(f * wpos)
    feat[:, HY_EMB] = is_bwd
    feat[:, HY_EMB + 1] = live
    return jnp.asarray(feat, F32)


def _hy_filter(seq, w1, b1, w2, b2, w3, b3, w4, freq, tr=512):
    d = _hy_dims(seq)
    n = d["n"]
    tr = min(tr, n)
    feat = _hy_filter_features(seq)
    w1p = jnp.pad(w1, ((0, 128 - HY_EMB), (0, 0)))
    max_decay = math.log(HY_DECAY_TARGET) / HY_FAST_PCT
    min_decay = math.log(HY_DECAY_TARGET) / HY_SLOW_PCT
    deltas = jnp.linspace(min_decay, max_decay, GROUP_W, dtype=F32).reshape(1, -1)
    row = lambda v: v.reshape(1, -1)
    const = lambda a: pl.BlockSpec(a.shape, lambda i: (0,) * a.ndim)
    args = [w1p, row(b1), w2, row(b2), w3, row(b3), w4, row(freq), deltas]
    return pl.pallas_call(
        _hy_filter_kernel,
        grid=(n // tr,),
        in_specs=[pl.BlockSpec((tr, 128), lambda i: (i, 0))] + [const(a) for a in args],
        out_specs=pl.BlockSpec((2, tr, GROUP_W), lambda i: (0, i, 0)),
        out_shape=jax.ShapeDtypeStruct((2, n, GROUP_W), F32),
        compiler_params=_cparams(("parallel",)),
        name="hy_filter",
    )(feat, *args)


def _hy_spectrum_kernel(k_ref, m1f_ref, m2_ref, re_ref, im_ref, tr_ref, ti_ref, *, dims):
    n1, n2, pb = dims["n1"], dims["n2"], dims["pb"]

    def stage1(j, carry):
        r0 = pl.multiple_of(j * n1, n1)
        a = jnp.dot(m1f_ref[j], k_ref[pl.ds(r0, n1), :].astype(BF16), preferred_element_type=F32)
        tr_ref[pl.ds(j, n1, stride=pb), :] = a[:n1]
        ti_ref[pl.ds(j, n1, stride=pb), :] = a[n1:]
        return carry

    lax.fori_loop(0, n2, stage1, 0, unroll=HY_UNROLL)

    def stage2(k, carry):
        r0 = pl.multiple_of(k * pb, 8)
        rhs = jnp.concatenate([tr_ref[pl.ds(r0, n2), :], ti_ref[pl.ds(r0, n2), :]], axis=0).astype(BF16)
        x = jnp.dot(m2_ref[...], rhs, preferred_element_type=F32)
        o0 = pl.multiple_of(k * n2, n2)
        re_ref[pl.ds(o0, n2), :] = x[:n2]
        im_ref[pl.ds(o0, n2), :] = x[n2:]
        return carry

    lax.fori_loop(0, n1, stage2, 0, unroll=HY_UNROLL)


def _hy_spectrum(kern, mats, seq):
    d = _hy_dims(seq)
    n, n1, pb = d["n"], d["n1"], d["pb"]
    m1f, _, m2, _, _ = mats
    ncb = GROUP_W // HY_CB
    blk = pl.BlockSpec((None, n, HY_CB), lambda o, j: (o, 0, j))
    const = lambda a: pl.BlockSpec(a.shape, lambda o, j: (0,) * a.ndim)
    return pl.pallas_call(
        functools.partial(_hy_spectrum_kernel, dims=d),
        grid=(2, ncb),
        in_specs=[blk, const(m1f), const(m2)],
        out_specs=[blk, blk],
        out_shape=[jax.ShapeDtypeStruct((2, n, GROUP_W), F32)] * 2,
        scratch_shapes=[pltpu.VMEM((n1 * pb, HY_CB), F32)] * 2,
        compiler_params=_cparams(("parallel", "parallel")),
        name="hy_spectrum",
    )(kern, m1f, m2)


def _pack_pair(re, im):
    half = jnp.uint32(0x8000)
    r = lax.bitcast_convert_type(re, jnp.uint32) + half
    i = lax.bitcast_convert_type(im, jnp.uint32) + half
    return (r & jnp.uint32(0xFFFF0000)) | (i >> 16)


def _unpack_pair(w):
    re = lax.bitcast_convert_type(w & jnp.uint32(0xFFFF0000), F32)
    im = lax.bitcast_convert_type(w << 16, F32)
    return re, im


def _hy_conv_kernel(u_ref, g_ref, kr_ref, ki_ref, skip_ref, m1_ref, m2_ref, m3_ref, m4_ref, o_ref,
                    t1, t2, *, dims, natural_out):
    n1, n2, h, pa, pb, pc = (dims[k] for k in ("n1", "n2", "h", "pa", "pb", "pc"))

    def fwd1(j, carry):
        r0 = pl.multiple_of(j * pa, 8)
        rhs = jnp.concatenate([u_ref[0, pl.ds(r0, h), :], u_ref[1, pl.ds(r0, h), :]], axis=0).astype(BF16)
        a = jnp.dot(m1_ref[j], rhs, preferred_element_type=F32)
        t1[pl.ds(j, n1, stride=pb), :] = _pack_pair(a[:n1], a[n1:])
        return carry

    lax.fori_loop(0, n2, fwd1, 0, unroll=HY_UNROLL)

    def mid(k, carry):
        r0 = pl.multiple_of(k * pb, 8)
        rhs = jnp.concatenate(_unpack_pair(t1[pl.ds(r0, n2), :]), axis=0).astype(BF16)
        x = jnp.dot(m2_ref[...], rhs, preferred_element_type=F32)
        f0 = pl.multiple_of(k * n2, n2)
        fr, fi = kr_ref[pl.ds(f0, n2), :], ki_ref[pl.ds(f0, n2), :]
        xr, xi = x[:n2], x[n2:]
        y = jnp.concatenate([xr * fr - xi * fi, xr * fi + xi * fr], axis=0).astype(BF16)
        c = jnp.dot(m3_ref[k], y, preferred_element_type=F32)
        t2[pl.ds(k, n2, stride=pc), :] = _pack_pair(c[:n2], c[n2:])
        return carry

    lax.fori_loop(0, n1, mid, 0, unroll=HY_UNROLL)

    if not natural_out:
        o_ref[...] = jnp.zeros_like(o_ref)
    skip = skip_ref[...]

    def inv2(j, carry):
        r0 = pl.multiple_of(j * pc, 8)
        rhs = jnp.concatenate(_unpack_pair(t2[pl.ds(r0, n1), :]), axis=0).astype(BF16)
        y = jnp.dot(m4_ref[...], rhs, preferred_element_type=F32)
        a0 = pl.multiple_of(j * pa, 8)
        for e in range(2):
            val = g_ref[e, pl.ds(a0, h), :] * (y[e * h:(e + 1) * h] + skip * u_ref[e, pl.ds(a0, h), :])
            if natural_out:
                o_ref[e, pl.ds(j, h, stride=n2), :] = val.astype(o_ref.dtype)
            else:
                o_ref[e, pl.ds(a0, h), :] = val
        return carry

    lax.fori_loop(0, n2, inv2, 0, unroll=HY_UNROLL)


def _hy_conv(u_arr, u_col, g_arr, g_col, kf_re, kf_im, order, skip, mats, seq, natural_out):
    d = _hy_dims(seq)
    bsz = u_arr.shape[0]
    rows = d["n2"] * d["pa"]
    _, m1, m2, m3, m4 = mats
    ncb = GROUP_W // HY_CB
    single = pl.Buffered(1)
    const = lambda a: pl.BlockSpec(a.shape, lambda j, p: (0,) * a.ndim, pipeline_mode=single)
    in_specs = [
        pl.BlockSpec((2, rows, HY_CB), lambda j, p: (p, 0, u_col // HY_CB + j)),
        pl.BlockSpec((2, rows, HY_CB), lambda j, p: (p, 0, g_col // HY_CB + j)),
        pl.BlockSpec((None, d["n"], HY_CB), lambda j, p: (order, 0, j), pipeline_mode=single),
        pl.BlockSpec((None, d["n"], HY_CB), lambda j, p: (order, 0, j), pipeline_mode=single),
        pl.BlockSpec((None, 1, HY_CB), lambda j, p: (order, 0, j)),
        const(m1), const(m2), const(m3), const(m4),
    ]
    if natural_out:
        out_spec = pl.BlockSpec((2, seq, HY_CB), lambda j, p: (p, 0, j))
        out_shape = jax.ShapeDtypeStruct((bsz, seq, GROUP_W), F32)
    else:
        out_spec = pl.BlockSpec((2, rows, HY_CB), lambda j, p: (p, 0, j))
        out_shape = jax.ShapeDtypeStruct((bsz, rows, GROUP_W), F32)
    return pl.pallas_call(
        functools.partial(_hy_conv_kernel, dims=d, natural_out=natural_out),
        grid=(ncb, bsz // 2),
        in_specs=in_specs,
        out_specs=out_spec,
        out_shape=out_shape,
        scratch_shapes=[pltpu.VMEM((d["n1"] * d["pb"], HY_CB), jnp.uint32),
                        pltpu.VMEM((d["n2"] * d["pc"], HY_CB), jnp.uint32)],
        compiler_params=_cparams(("parallel", "parallel")),
        name="hy_conv",
    )(u_arr, g_arr, kf_re, kf_im, skip.reshape(2, 1, GROUP_W), m1, m2, m3, m4)


def _hyena_mixer(p, short_w, short_b, filt_params, skip, mats):
    seq = p.shape[1]
    kern = _hy_filter(seq, *filt_params)
    kf_re, kf_im = _hy_spectrum(kern, mats, seq)
    ut = _hy_prep(p, short_w, short_b)
    zt = _hy_conv(ut, 2 * GROUP_W, ut, 0, kf_re, kf_im, 0, skip, mats, seq, natural_out=False)
    return _hy_conv(zt, 0, ut, GROUP_W, kf_re, kf_im, 1, skip, mats, seq, natural_out=True)


def _prep_weights(w_in, w_out):
    depth, d, _ = w_in.shape
    perm = np.asarray(HEAD_ORDER)
    q = w_in[..., :512].reshape(depth, d, N_HEADS, HEAD_DIM)[:, :, perm].reshape(depth, d, 512)
    o_a, o_ssm = 768, 768 + 3072 + 1536 + 16
    pieces = [q, w_in[..., 512:o_a], w_in[..., o_ssm - 16:o_ssm],
              jnp.zeros((depth, d, COL_HY - COL_DT - 16), w_in.dtype), w_in[..., o_a:o_ssm - 16]]
    w_in_p = jnp.concatenate(pieces, axis=-1).astype(BF16)
    assert w_in_p.shape[-1] == N_PROJ
    dm = w_out.shape[-1]
    wo_a = w_out[:, :512].reshape(depth, N_HEADS, HEAD_DIM, dm)[:, perm].reshape(depth, 512, dm)
    w_out_p = jnp.concatenate([wo_a, w_out[:, 512:]], axis=1).astype(BF16)
    return w_in_p, w_out_p


TILE_W_IN = (512, 2816)
TILE_MLP_UP = (1024, 2048)
TILE_DOWN = (1024, 1024, 2048)
TILE_OUT = (512, 2048)


def kernel(x, c, ctx, c_ctx, ada_w, ada_b, norm_mix, norm_mlp, w_in, w_out, attn_sink,
           hy_short_w, hy_short_b, hy_w1, hy_b1, hy_w2, hy_b2, hy_w3, hy_b3, hy_w4, hy_freq, hy_skip,
           na_rpb, ssm_conv_w, ssm_conv_b, ssm_dt_bias, ssm_a_log, ssm_d, ssm_norm,
           mlp_w1, mlp_w2, final_norm):
    bsz, seq, d = x.shape
    lc = ctx.shape[1]
    depth = ada_w.shape[0]
    assert bsz % 2 == 0 and bsz <= 7 and d == D_MODEL

    cs = jnp.zeros((8, d), F32).at[:bsz].set(c).at[bsz].set(c_ctx)
    mod = _ada_mod(cs, ada_w, ada_b)
    cos_t, sin_t = _rope_tables(seq)
    na_plan = _na_plan(seq)
    bias_tabs = _na_bias_tables(na_rpb, na_plan[-1])
    mats_l = _hy_matrices(seq)
    mats_c = _hy_matrices(lc)
    lat_row = lambda b: b
    ctx_row = lambda b: bsz
    tm_c = lc

    w_in_p, w_out_p = _prep_weights(w_in, w_out)
    w1_b = mlp_w1.astype(BF16)
    w2_b = mlp_w2.astype(BF16)

    xc = ctx
    for i in range(depth):
        last = i == depth - 1
        mod3 = mod[i].reshape(8, 1, 6 * d)
        p = _norm_matmul(x, norm_mix[i], mod3, lat_row, 0, 1, w_in_p, i, P_DTYPE, False, *TILE_W_IN)
        pc = _norm_matmul(xc, norm_mix[i], mod3, ctx_row, 0, 1, w_in_p, i, P_DTYPE, False, tm_c, TILE_W_IN[1])

        filt_params = (hy_w1[i], hy_b1[i], hy_w2[i], hy_b2[i], hy_w3[i], hy_b3[i], hy_w4[i], hy_freq[i])
        ssd_consts = _ssd_consts(ssm_conv_w[i], ssm_conv_b[i], ssm_dt_bias[i], ssm_a_log[i], ssm_d[i], ssm_norm[i])

        ya = _win_attn(attn_sink[i], p, pc, cos_t, sin_t, local=True)
        yb = _hyena_mixer(p, hy_short_w[i], hy_short_b[i], filt_params, hy_skip[i], mats_l)
        yn = _na_attn(p, pc, bias_tabs, i, na_plan, local=True)
        yd, ydc = _ssd_mixer(p, pc, ssd_consts)
        x = _out_proj((ya, yb, yn, yd), w_out_p, i, x, mod3, lat_row, 2, *TILE_OUT)
        hid = _norm_matmul(x, norm_mlp[i], mod3, lat_row, 3, 4, w1_b, i, BF16, True, *TILE_MLP_UP)
        x = _matmul_residual(hid, w2_b, i, x, mod3, lat_row, 5, *TILE_DOWN)

        if not last:
            yac = _win_attn(attn_sink[i], pc, pc, None, None, local=False)
            ybc = _hyena_mixer(pc, hy_short_w[i], hy_short_b[i], filt_params, hy_skip[i], mats_c)
            ync = _na_attn(pc, pc, None, i, None, local=False)
            xc = _out_proj((yac, ybc, ync, ydc), w_out_p, i, xc, mod3, ctx_row, 2, tm_c, TILE_OUT[1])
            hidc = _norm_matmul(xc, norm_mlp[i], mod3, ctx_row, 3, 4, w1_b, i, BF16, True, tm_c, TILE_MLP_UP[1])
            xc = _matmul_residual(hidc, w2_b, i, xc, mod3, ctx_row, 5, tm_c, *TILE_DOWN[1:])
    return _final_norm(x, final_norm)
```

```python
import functools
import math

import numpy as np
import jax
import jax.numpy as jnp
from jax import lax
from jax.experimental import pallas as pl
from jax.experimental.pallas import tpu as pltpu

F32 = jnp.float32
BF16 = jnp.bfloat16

D_MODEL = 2048
GRID_W = 64
EPS = 1e-6
NEG = -1e30
HEAD_DIM = 64
GROUP_W = D_MODEL // 4
N_HEADS = GROUP_W // HEAD_DIM
WINDOW = 128
BLOCK = 128
ROPE_BASE = 10000.0
HY_EMB = 33
HY_FFN = 64
HY_DECAY_TARGET = 1e-2
HY_FAST_PCT = 0.3
HY_SLOW_PCT = 1.5
NA_KR = 8
NA_KC = 16
SSM_STATE = 128
SSM_CHUNK = 128
D_FF = 4 * D_MODEL
SCALE = HEAD_DIM ** -0.5
LOG2E = math.log2(math.e)

COL_QA, COL_KA, COL_VA, COL_DT = 0, 512, 640, 768
COL_HY = 1024
COL_NQ, COL_NK, COL_NV = 2560, 3072, 3584
COL_SZ, COL_SX, COL_SBC = 4096, 4608, 5120
N_PROJ = 5632
HEAD_ORDER = (0, 4, 1, 5, 2, 6, 3, 7)

P_DTYPE = BF16
HALO = 16

V7X_VMEM_BYTES = 64 * 1024 * 1024
VMEM_LIMIT = 56 * 1024 * 1024


def _cparams(sem):
    return pltpu.CompilerParams(dimension_semantics=sem, vmem_limit_bytes=VMEM_LIMIT)


def _silu(x):
    return x * jax.nn.sigmoid(x)


def _ada_kernel(cs_ref, w_ref, b_ref, o_ref):
    a = _silu(cs_ref[...]).astype(BF16)
    o_ref[...] = jnp.dot(a, w_ref[...].astype(BF16), preferred_element_type=F32) + b_ref[...]


def _ada_mod(cs, ada_w, ada_b, tn=1024):
    depth, d, n = ada_w.shape
    return pl.pallas_call(
        _ada_kernel,
        grid=(depth, n // tn),
        in_specs=[
            pl.BlockSpec((8, d), lambda i, j: (0, 0)),
            pl.BlockSpec((None, d, tn), lambda i, j: (i, 0, j)),
            pl.BlockSpec((None, 1, tn), lambda i, j: (i, 0, j)),
        ],
        out_specs=pl.BlockSpec((None, 8, tn), lambda i, j: (i, 0, j)),
        out_shape=jax.ShapeDtypeStruct((depth, 8, n), F32),
        compiler_params=_cparams(("parallel", "parallel")),
        name="ada_mod",
    )(cs, ada_w, ada_b.reshape(depth, 1, n))


NORM_ROWS = 16


def _norm_matmul_kernel(x_ref, g_ref, sh_ref, sc_ref, w_ref, o_ref, h_ref, *, act):
    @pl.when(pl.program_id(2) == 0)
    def _():
        gain = g_ref[...] * (1.0 + sc_ref[...])
        shift = sh_ref[...]

        def rows(r, carry):
            r0 = pl.multiple_of(r * NORM_ROWS, NORM_ROWS)
            xf = x_ref[pl.ds(r0, NORM_ROWS), :]
            ms = jnp.mean(xf * xf, axis=-1, keepdims=True)
            h_ref[pl.ds(r0, NORM_ROWS), :] = (xf * lax.rsqrt(ms + EPS) * gain + shift).astype(BF16)
            return carry

        lax.fori_loop(0, x_ref.shape[0] // NORM_ROWS, rows, 0, unroll=8)

    r = jnp.dot(h_ref[...], w_ref[...], preferred_element_type=F32)
    if act:
        r = jnp.square(jnp.maximum(r, 0.0))
    o_ref[...] = r.astype(o_ref.dtype)


def _norm_matmul(x, g, mod3, row_of_b, sh_idx, sc_idx, w, layer, out_dtype, act, tm, tn):
    bsz, t, d = x.shape
    n = w.shape[2]
    tm = min(tm, t)
    return pl.pallas_call(
        functools.partial(_norm_matmul_kernel, act=act),
        grid=(bsz, t // tm, n // tn),
        in_specs=[
            pl.BlockSpec((None, tm, d), lambda b, m, j: (b, m, 0)),
            pl.BlockSpec((1, d), lambda b, m, j: (0, 0)),
            pl.BlockSpec((None, 1, d), lambda b, m, j: (row_of_b(b), 0, sh_idx)),
            pl.BlockSpec((None, 1, d), lambda b, m, j: (row_of_b(b), 0, sc_idx)),
            pl.BlockSpec((None, d, tn), lambda b, m, j: (layer, 0, j)),
        ],
        out_specs=pl.BlockSpec((None, tm, tn), lambda b, m, j: (b, m, j)),
        out_shape=jax.ShapeDtypeStruct((bsz, t, n), out_dtype),
        scratch_shapes=[pltpu.VMEM((tm, d), BF16)],
        compiler_params=_cparams(("parallel", "parallel", "arbitrary")),
        name="norm_matmul",
    )(x, g.reshape(1, d), mod3, mod3, w)


def _mm_res_kernel(a_ref, w_ref, x_ref, gate_ref, o_ref, acc_ref, *, nk):
    k = pl.program_id(3)

    @pl.when(k == 0)
    def _():
        acc_ref[...] = jnp.zeros_like(acc_ref)

    acc_ref[...] += jnp.dot(a_ref[...], w_ref[...], preferred_element_type=F32)

    @pl.when(k == nk - 1)
    def _():
        o_ref[...] = x_ref[...] + gate_ref[...] * acc_ref[...]


def _matmul_residual(a, w, layer, x, mod3, row_of_b, gate_idx, tm, tn, tk):
    bsz, t, kdim = a.shape
    n = w.shape[2]
    tm = min(tm, t)
    nk = kdim // tk
    return pl.pallas_call(
        functools.partial(_mm_res_kernel, nk=nk),
        grid=(bsz, t // tm, n // tn, nk),
        in_specs=[
            pl.BlockSpec((None, tm, tk), lambda b, m, j, k: (b, m, k)),
            pl.BlockSpec((None, tk, tn), lambda b, m, j, k: (layer, k, j)),
            pl.BlockSpec((None, tm, tn), lambda b, m, j, k: (b, m, j)),
            pl.BlockSpec((None, 1, tn), lambda b, m, j, k: (row_of_b(b), 0, gate_idx * (D_MODEL // tn) + j)),
        ],
        out_specs=pl.BlockSpec((None, tm, tn), lambda b, m, j, k: (b, m, j)),
        out_shape=jax.ShapeDtypeStruct(x.shape, F32),
        scratch_shapes=[pltpu.VMEM((tm, tn), F32)],
        compiler_params=_cparams(("parallel", "parallel", "parallel", "arbitrary")),
        name="matmul_residual",
    )(a, w, x, mod3)


def _out_proj_kernel(ya_ref, yb_ref, yn_ref, yd_ref, w_ref, x_ref, gate_ref, o_ref):
    acc = None
    for g, y_ref in enumerate((ya_ref, yb_ref, yn_ref, yd_ref)):
        part = jnp.dot(y_ref[...].astype(BF16), w_ref[GROUP_W * g:GROUP_W * (g + 1), :],
                       preferred_element_type=F32)
        acc = part if acc is None else acc + part
    o_ref[...] = x_ref[...] + gate_ref[...] * acc


def _out_proj(ys, w, layer, x, mod3, row_of_b, gate_idx, tm, tn):
    bsz, t, d = x.shape
    tm = min(tm, t)
    y_spec = pl.BlockSpec((None, tm, GROUP_W), lambda b, m, j: (b, m, 0))
    return pl.pallas_call(
        _out_proj_kernel,
        grid=(bsz, t // tm, d // tn),
        in_specs=[y_spec, y_spec, y_spec, y_spec,
                  pl.BlockSpec((None, 4 * GROUP_W, tn), lambda b, m, j: (layer, 0, j),
                               pipeline_mode=pl.Buffered(1) if tn == d else None),
                  pl.BlockSpec((None, tm, tn), lambda b, m, j: (b, m, j)),
                  pl.BlockSpec((None, 1, tn), lambda b, m, j: (row_of_b(b), 0, gate_idx * (d // tn) + j))],
        out_specs=pl.BlockSpec((None, tm, tn), lambda b, m, j: (b, m, j)),
        out_shape=jax.ShapeDtypeStruct(x.shape, F32),
        compiler_params=_cparams(("parallel", "parallel", "parallel")),
        name="out_proj",
    )(*ys, w, x, mod3)


def _final_norm_kernel(x_ref, g_ref, o_ref):
    xf = x_ref[...]
    ms = jnp.mean(xf * xf, axis=-1, keepdims=True)
    o_ref[...] = xf * lax.rsqrt(ms + EPS) * g_ref[...]


def _final_norm(x, g, tm=1024):
    bsz, t, d = x.shape
    tm = min(tm, t)
    return pl.pallas_call(
        _final_norm_kernel,
        grid=(bsz, t // tm),
        in_specs=[pl.BlockSpec((None, tm, d), lambda b, m: (b, m, 0)),
                  pl.BlockSpec((1, d), lambda b, m: (0, 0))],
        out_specs=pl.BlockSpec((None, tm, d), lambda b, m: (b, m, 0)),
        out_shape=jax.ShapeDtypeStruct(x.shape, F32),
        compiler_params=_cparams(("parallel", "parallel")),
        name="final_norm",
    )(x, g.reshape(1, d))


WIN_QBLOCKS = 4


def _rope(x, cos, sin_signed, lane_lo):
    w = x.shape[-1]
    partner = jnp.where(lane_lo, pltpu.roll(x, w - 16, 1), pltpu.roll(x, 16, 1))
    return x * cos + partner * sin_signed


def _win_attn_kernel(sink_ref, q_ref, kc_ref, vc_ref, *rest, seq, local):
    if local:
        k_ref, v_ref, cos_ref, sin_ref, mask_ref, o_ref = rest
    else:
        (o_ref,) = rest
    nb = seq // BLOCK
    lane = lax.broadcasted_iota(jnp.int32, (BLOCK, 128), 1)
    lo = lane < HEAD_DIM
    lane_lo = (lane % 32) < 16
    lane3_lo = (lax.broadcasted_iota(jnp.int32, (3 * BLOCK, 128), 1) % 32) < 16
    nt = (((1,), (1,)), ((), ()))
    kc = kc_ref[...].astype(BF16)
    vc = vc_ref[...].astype(BF16)
    sink = jnp.concatenate(
        [jnp.full((BLOCK, 1), sink_ref[HEAD_ORDER[i]] * LOG2E, F32) for i in range(N_HEADS)], axis=0)
    nq = q_ref.shape[0] // BLOCK
    for sub in range(nq):
        n = pl.program_id(1) * nq + sub
        q = q_ref[sub * BLOCK:(sub + 1) * BLOCK, :].astype(F32) * (SCALE * LOG2E)
        if local:
            r0 = pl.multiple_of(n * BLOCK, BLOCK)
            cos_q = cos_ref[pl.ds(r0, BLOCK), :]
            sin_q = sin_ref[pl.ds(r0, BLOCK), :]
        rows = []
        for m in range(4):
            qm = q[:, 128 * m:128 * (m + 1)]
            if local:
                qm = _rope(qm, cos_q, sin_q, lane_lo)
            rows.append(jnp.where(lo, qm, 0.0))
            rows.append(jnp.where(lo, 0.0, qm))
        qbd = jnp.concatenate(rows, axis=0).astype(BF16)
        s_ctx = lax.dot_general(qbd, kc, nt, preferred_element_type=F32)
        mx = jnp.maximum(jnp.max(s_ctx, axis=-1, keepdims=True), sink)
        if local:
            start = pl.multiple_of(jnp.clip((n - 1) * BLOCK, 0, seq - 3 * BLOCK), BLOCK)
            kb = _rope(k_ref[pl.ds(start, 3 * BLOCK), :].astype(F32), cos_ref[pl.ds(start, 3 * BLOCK), :],
                       sin_ref[pl.ds(start, 3 * BLOCK), :], lane3_lo).astype(BF16)
            vb = v_ref[pl.ds(start, 3 * BLOCK), :].astype(BF16)
            mask = mask_ref[jnp.where(n == 0, 0, jnp.where(n == nb - 1, 2, 1))]
            s_loc = lax.dot_general(qbd, kb, nt, preferred_element_type=F32)
            s_loc = (s_loc.reshape(N_HEADS, BLOCK, 3 * BLOCK) + mask[None]).reshape(s_loc.shape)
            mx = jnp.maximum(mx, jnp.max(s_loc, axis=-1, keepdims=True))
        p_ctx = jnp.exp2(s_ctx - mx)
        den = jnp.sum(p_ctx, axis=-1, keepdims=True) + jnp.exp2(sink - mx)
        acc = jnp.dot(p_ctx.astype(BF16), vc, preferred_element_type=F32)
        if local:
            p_loc = jnp.exp2(s_loc - mx)
            den = den + jnp.sum(p_loc, axis=-1, keepdims=True)
            acc = acc + jnp.dot(p_loc.astype(BF16), vb, preferred_element_type=F32)
        o = acc / den
        outs = [jnp.where(lo, o[(2 * m) * BLOCK:(2 * m + 1) * BLOCK], o[(2 * m + 1) * BLOCK:(2 * m + 2) * BLOCK])
                for m in range(4)]
        o_ref[sub * BLOCK:(sub + 1) * BLOCK, :] = jnp.concatenate(outs, axis=1).astype(o_ref.dtype)


def _win_attn(sink, pq, pc, cos_t, sin_t, local):
    bsz, t, _ = pq.shape
    lc = pc.shape[1]
    tq = min(WIN_QBLOCKS * BLOCK, t)
    in_specs = [
        pl.BlockSpec(memory_space=pltpu.SMEM),
        pl.BlockSpec((None, tq, 512), lambda b, n: (b, n, COL_QA // 512)),
        pl.BlockSpec((None, lc, 128), lambda b, n: (b, 0, COL_KA // 128)),
        pl.BlockSpec((None, lc, 128), lambda b, n: (b, 0, COL_VA // 128)),
    ]
    args = [sink, pq, pc, pc]
    if local:
        in_specs += [
            pl.BlockSpec((None, t, 128), lambda b, n: (b, 0, COL_KA // 128)),
            pl.BlockSpec((None, t, 128), lambda b, n: (b, 0, COL_VA // 128)),
            pl.BlockSpec((t, 128), lambda b, n: (0, 0)),
            pl.BlockSpec((t, 128), lambda b, n: (0, 0)),
            pl.BlockSpec((3, BLOCK, 3 * BLOCK), lambda b, n: (0, 0, 0)),
        ]
        args += [pq, pq, cos_t, sin_t, _window_masks(t)]
    return pl.pallas_call(
        functools.partial(_win_attn_kernel, seq=t, local=local),
        grid=(bsz, t // tq),
        in_specs=in_specs,
        out_specs=pl.BlockSpec((None, tq, GROUP_W), lambda b, n: (b, n, 0)),
        out_shape=jax.ShapeDtypeStruct((bsz, t, GROUP_W), BF16),
        compiler_params=_cparams(("parallel", "arbitrary")),
        name="win_attn" if local else "ctx_attn_a",
    )(*args)


def _window_masks(seq):
    nb = seq // BLOCK
    qi = np.arange(BLOCK)[:, None]
    kj = np.arange(3 * BLOCK)[None, :]
    tabs = []
    for n in (0, 1, nb - 1):
        start = int(np.clip((n - 1) * BLOCK, 0, seq - 3 * BLOCK))
        rel = (start + kj) - (n * BLOCK + qi)
        tabs.append(np.where(np.abs(rel) <= WINDOW, 0.0, NEG))
    return jnp.asarray(np.stack(tabs), F32)


def _rope_tables(seq):
    t = np.arange(seq)
    row, col = t // GRID_W, t % GRID_W
    quarter = HEAD_DIM // 4
    inv = ROPE_BASE ** (-np.arange(quarter, dtype=np.float64) / quarter)
    inv = inv.astype(np.float32).astype(np.float64)
    lane = np.arange(128)
    j = lane % HEAD_DIM
    pos = np.where((j < HEAD_DIM // 2)[None, :], row[:, None], col[:, None]).astype(np.float64)
    ang = (pos * inv[j % quarter][None, :]).astype(np.float32)
    cos = np.cos(ang.astype(np.float64))
    sin = np.sin(ang.astype(np.float64))
    sign = np.where((j % 32) < 16, -1.0, 1.0)[None, :]
    return jnp.asarray(cos, F32), jnp.asarray(sin * sign, F32)


NA_ROWS_PER_STEP = 4


def _na_kernel(var_ref, ws_ref, q_ref, kc_ref, vc_ref, *rest, local, win_rows):
    del var_ref
    if local:
        k_ref, v_ref, bias_ref, o_ref = rest
    else:
        (o_ref,) = rest
    g = pl.program_id(1)
    tq = q_ref.shape[0]
    q = q_ref[...].astype(F32) * (SCALE * LOG2E)
    head = lax.broadcasted_iota(jnp.int32, (tq, 256), 1) // HEAD_DIM
    nt = (((1,), (1,)), ((), ()))
    outs = []
    for half in range(2):
        cols = slice(256 * half, 256 * (half + 1))
        q4 = q[:, cols]
        qbd = jnp.concatenate([jnp.where(head == h, q4, 0.0) for h in range(4)], axis=0).astype(BF16)
        kc4 = kc_ref[:, cols].astype(BF16)
        vc4 = vc_ref[:, cols].astype(BF16)
        s_ctx = lax.dot_general(qbd, kc4, nt, preferred_element_type=F32)
        mx = jnp.max(s_ctx, axis=-1, keepdims=True)
        if local:
            nkey = win_rows * GRID_W
            start = pl.multiple_of(ws_ref[g] * GRID_W, GRID_W)
            k4 = k_ref[pl.ds(start, nkey), cols].astype(BF16)
            v4 = v_ref[pl.ds(start, nkey), cols].astype(BF16)
            bias = bias_ref[4 * half:4 * half + 4].astype(F32).reshape(4 * tq, nkey)
            s_loc = lax.dot_general(qbd, k4, nt, preferred_element_type=F32) + bias
            mx = jnp.maximum(mx, jnp.max(s_loc, axis=-1, keepdims=True))
        p_ctx = jnp.exp2(s_ctx - mx)
        den = jnp.sum(p_ctx, axis=-1, keepdims=True)
        acc = jnp.dot(p_ctx.astype(BF16), vc4, preferred_element_type=F32)
        if local:
            p_loc = jnp.exp2(s_loc - mx)
            den = den + jnp.sum(p_loc, axis=-1, keepdims=True)
            acc = acc + jnp.dot(p_loc.astype(BF16), v4, preferred_element_type=F32)
        o = acc / den
        o4 = jnp.where(head == 0, o[0:tq], 0.0)
        for h in range(1, 4):
            o4 = o4 + jnp.where(head == h, o[h * tq:(h + 1) * tq], 0.0)
        outs.append(o4)
    o_ref[...] = jnp.concatenate(outs, axis=1).astype(o_ref.dtype)


def _na_plan(seq):
    rows = seq // GRID_W
    kr = min(NA_KR, rows)
    r_step = NA_ROWS_PER_STEP
    win_rows = min(r_step + kr, rows)
    n_groups = rows // r_step
    wstart = np.zeros(n_groups, np.int32)
    pats = []
    keys = {}
    var = np.zeros(n_groups, np.int32)
    for g in range(n_groups):
        r0 = g * r_step
        ws = int(np.clip(r0 - kr // 2, 0, rows - win_rows))
        wstart[g] = ws
        r = r0 + np.arange(r_step)
        rstart = np.clip(r - kr // 2, 0, rows - kr)
        krow = ws + np.arange(win_rows)
        valid = (krow[None, :] >= rstart[:, None]) & (krow[None, :] < rstart[:, None] + kr)
        roff = krow[None, :] - r[:, None] + NA_KR - 1
        key = (valid.tobytes(), np.where(valid, roff, 0).tobytes())
        if key not in keys:
            keys[key] = len(pats)
            pats.append((valid, np.where(valid, roff, 0)))
        var[g] = keys[key]
    return rows, win_rows, n_groups, wstart, var, pats


def _na_bias_tables(rpb, pats):
    cq = np.arange(GRID_W)
    ck = np.arange(GRID_W)
    cstart = np.clip(cq - NA_KC // 2, 0, GRID_W - NA_KC)
    col_valid = (ck[None] >= cstart[:, None]) & (ck[None] < cstart[:, None] + NA_KC)
    coff = np.clip(ck[None] - cq[:, None], -(NA_KC - 1), NA_KC - 1) + NA_KC - 1
    by_col = jnp.where(col_valid, rpb[..., coff] * LOG2E, NEG).astype(BF16)
    masked = jnp.full(by_col.shape[:2] + (GRID_W, GRID_W), NEG, BF16)
    tabs = []
    for valid, roff in pats:
        r_step, win_rows = valid.shape
        q_rows = []
        for i in range(r_step):
            blocks = [by_col[:, :, int(roff[i, a])] if valid[i, a] else masked for a in range(win_rows)]
            q_rows.append(jnp.concatenate(blocks, axis=-1))
        tabs.append(jnp.concatenate(q_rows, axis=-2))
    return jnp.stack(tabs, axis=1)


def _na_attn(pq, pc, bias_tabs, layer, plan, local):
    bsz, t, _ = pq.shape
    lc = pc.shape[1]
    if local:
        rows, win_rows, n_groups, wstart, var, _ = plan
        tq = NA_ROWS_PER_STEP * GRID_W
    else:
        win_rows, n_groups, tq = 0, 1, t
        wstart = np.zeros(1, np.int32)
        var = np.zeros(1, np.int32)
    in_specs = [
        pl.BlockSpec((None, tq, 512), lambda b, g, vr, ws: (b, g, COL_NQ // 512)),
        pl.BlockSpec((None, lc, 512), lambda b, g, vr, ws: (b, 0, COL_NK // 512)),
        pl.BlockSpec((None, lc, 512), lambda b, g, vr, ws: (b, 0, COL_NV // 512)),
    ]
    args = [pq, pc, pc]
    if local:
        in_specs += [
            pl.BlockSpec((None, t, 512), lambda b, g, vr, ws: (b, 0, COL_NK // 512)),
            pl.BlockSpec((None, t, 512), lambda b, g, vr, ws: (b, 0, COL_NV // 512)),
            pl.BlockSpec((None, None, N_HEADS, tq, win_rows * GRID_W),
                         lambda b, g, vr, ws: (layer, vr[g], 0, 0, 0)),
        ]
        args += [pq, pq, bias_tabs]
    grid_spec = pltpu.PrefetchScalarGridSpec(
        num_scalar_prefetch=2,
        grid=(bsz, n_groups),
        in_specs=in_specs,
        out_specs=pl.BlockSpec((None, tq, GROUP_W), lambda b, g, vr, ws: (b, g, 0)),
    )
    return pl.pallas_call(
        functools.partial(_na_kernel, local=local, win_rows=win_rows),
        grid_spec=grid_spec,
        out_shape=jax.ShapeDtypeStruct((bsz, t, GROUP_W), BF16),
        compiler_params=_cparams(("parallel", "arbitrary")),
        name="na_attn" if local else "ctx_attn_c",
    )(jnp.asarray(var), jnp.asarray(wstart), *args)


SSD_CHUNKS = 4


def _softplus(x):
    return jnp.maximum(x, 0.0) + jnp.log(1.0 + jnp.exp(-jnp.abs(x)))


def _bf16_parts(x, n):
    parts = []
    for _ in range(n):
        part = x.astype(BF16)
        parts.append(part)
        x = x - part.astype(F32)
    return parts


def _conv3_silu(cur, prev_blk, next_blk, w_ref, b_ref, has_prev, has_next):
    x = cur.astype(F32)
    rows = x.shape[0]
    prev_row = prev_blk.astype(F32)[HALO - 1:HALO, :] * has_prev
    next_row = next_blk.astype(F32)[0:1, :] * has_next
    ri = lax.broadcasted_iota(jnp.int32, x.shape, 0)
    up = jnp.where(ri == 0, prev_row, pltpu.roll(x, 1, 0))
    dn = jnp.where(ri == rows - 1, next_row, pltpu.roll(x, rows - 1, 0))
    u = up * w_ref[0:1, :] + x * w_ref[1:2, :] + dn * w_ref[2:3, :] + b_ref[...]
    return _silu(u)


def _ssd_chunk(xs, bc, dt, a, st_ref, keep, tri, expand, head4, d_off, reverse):
    L = SSM_CHUNK
    nt = (((1,), (1,)), ((), ()))
    a_t = a.T
    c_col = sum(jnp.dot(tri, part, preferred_element_type=F32) for part in _bf16_parts(a, 3))
    c_row = sum(lax.dot_general(part, tri, nt, preferred_element_type=F32) for part in _bf16_parts(a_t, 3))
    c_exp = sum(jnp.dot(part, expand, preferred_element_type=F32) for part in _bf16_parts(c_col, 2))
    dt_exp = sum(jnp.dot(part, expand, preferred_element_type=F32) for part in _bf16_parts(dt, 2))
    end = 0 if reverse else L - 1
    cend = c_exp[end:end + 1, :]
    x_dt = xs * dt_exp
    out_decay = jnp.exp(c_exp)
    x_dec = x_dt * jnp.exp(cend - c_exp)
    chunk_decay = jnp.exp(cend)

    ys = []
    for g in range(2):
        gl = slice(256 * g, 256 * (g + 1))
        b_g = bc[:, 128 * g:128 * (g + 1)]
        c_g = bc[:, 256 + 128 * g:256 + 128 * (g + 1)].astype(BF16)
        cb = lax.dot_general(c_g, b_g.astype(BF16), nt, preferred_element_type=F32)
        ms = []
        for hh in range(4):
            j = d_off + 4 * g + hh
            diff = c_col[:, j:j + 1] - c_row[j:j + 1, :]
            ms.append(cb * jnp.exp(jnp.where(keep, diff, NEG)))
        m_g = jnp.concatenate(ms, axis=0).astype(BF16)
        o = jnp.dot(m_g, x_dt[:, gl].astype(BF16), preferred_element_type=F32)
        y_diag = jnp.where(head4 == 0, o[0:L], 0.0)
        for hh in range(1, 4):
            y_diag = y_diag + jnp.where(head4 == hh, o[hh * L:(hh + 1) * L], 0.0)
        st = st_ref[g]
        y_off = jnp.dot(c_g, st.astype(BF16), preferred_element_type=F32) * out_decay[:, gl]
        ys.append(y_diag + y_off)
        st_ref[g] = chunk_decay[:, gl] * st + jnp.dot(
            b_g.T.astype(BF16), x_dec[:, gl].astype(BF16), preferred_element_type=F32)
    return jnp.concatenate(ys, axis=1)


def _ssd_kernel(*refs, reverse, nc):
    finalize = reverse
    if reverse:
        (xsc_ref, bcc_ref, dt_ref, dtb_ref, alog_ref, h0_ref,
         z_ref, yf_ref, dskip_ref, nw_ref, y_ref, ht_ref, st_ref) = refs
    else:
        (xs_ref, bc_ref, xsp_ref, xsn_ref, bcp_ref, bcn_ref, dt_ref, cwx_ref, cbx_ref, cwb_ref, cbb_ref,
         dtb_ref, alog_ref, h0_ref, y_ref, ht_ref, xsc_ref, bcc_ref, st_ref) = refs
    c = pl.program_id(1)
    cid = (nc - 1 - c) if reverse else c
    d_off = 8 if reverse else 0
    L = SSM_CHUNK
    n_sub = dt_ref.shape[0] // L

    @pl.when(c == 0)
    def _():
        st_ref[...] = h0_ref[...]

    if reverse:
        xs = xsc_ref[...]
        bc = bcc_ref[...].astype(F32)
    else:
        has_prev = jnp.where(cid > 0, 1.0, 0.0).astype(F32)
        has_next = jnp.where(cid < nc - 1, 1.0, 0.0).astype(F32)
        xs = _conv3_silu(xs_ref[...], xsp_ref[...], xsn_ref[...], cwx_ref, cbx_ref, has_prev, has_next)
        bc = _conv3_silu(bc_ref[...], bcp_ref[...], bcn_ref[...], cwb_ref, cbb_ref, has_prev, has_next)
        xsc_ref[...] = xs
        bcc_ref[...] = bc.astype(BF16)

    dt_all = _softplus(dt_ref[...].astype(F32) + dtb_ref[...])
    a_all = dt_all * (-jnp.exp(alog_ref[...]))
    ri = lax.broadcasted_iota(jnp.int32, (L, L), 0)
    ci = lax.broadcasted_iota(jnp.int32, (L, L), 1)
    keep = (ci >= ri) if reverse else (ci <= ri)
    tri = keep.astype(BF16)
    nt = (((1,), (1,)), ((), ()))
    ej = lax.broadcasted_iota(jnp.int32, (128, GROUP_W), 0)
    eh = lax.broadcasted_iota(jnp.int32, (128, GROUP_W), 1) // HEAD_DIM
    expand = (ej == eh + d_off).astype(BF16)
    head4 = lax.broadcasted_iota(jnp.int32, (L, 256), 1) // HEAD_DIM
    xs_all, bc_all = xs, bc
    y_sub = [None] * n_sub
    for sub in (range(n_sub - 1, -1, -1) if reverse else range(n_sub)):
        y_sub[sub] = _ssd_chunk(xs_all[sub * L:(sub + 1) * L], bc_all[sub * L:(sub + 1) * L],
                                dt_all[sub * L:(sub + 1) * L], a_all[sub * L:(sub + 1) * L],
                                st_ref, keep, tri, expand, head4, d_off, reverse)
    y = jnp.concatenate(y_sub, axis=0) if n_sub > 1 else y_sub[0]
    if finalize:
        y = y + yf_ref[...] + xs * dskip_ref[...]
        y = y * _silu(z_ref[...].astype(F32))
        halves = []
        for g in range(2):
            yg = y[:, 256 * g:256 * (g + 1)]
            halves.append(yg * lax.rsqrt(jnp.mean(yg * yg, axis=-1, keepdims=True) + EPS))
        y = jnp.concatenate(halves, axis=1) * nw_ref[...]
    y_ref[...] = y.astype(y_ref.dtype)

    @pl.when(c == nc - 1)
    def _():
        ht_ref[...] = st_ref[...]


def _ssd_direction(p, consts, h0, reverse, fwd=None):
    cwx, cbx, cwb, cbb, dtb, alog, dskip, nw = consts
    bsz, t, _ = p.shape
    rows = min(SSD_CHUNKS * SSM_CHUNK, t)
    nc = t // rows
    hb = rows // HALO
    nhalo = t // HALO

    def cid(c):
        return (nc - 1 - c) if reverse else c

    def cur(col, width):
        return pl.BlockSpec((None, rows, width), lambda b, c: (b, cid(c), col // width))

    def prev(col, width):
        return pl.BlockSpec((None, HALO, width), lambda b, c: (b, jnp.maximum(cid(c) * hb - 1, 0), col // width))

    def nxt(col, width):
        return pl.BlockSpec((None, HALO, width), lambda b, c: (b, jnp.minimum((cid(c) + 1) * hb, nhalo - 1), col // width))

    def const(arr):
        return pl.BlockSpec(arr.shape, lambda b, c: (0,) * arr.ndim)

    state_spec = pl.BlockSpec((None, 2, SSM_STATE, 256), lambda b, c: (b, 0, 0, 0))
    chunk_spec = pl.BlockSpec((None, rows, GROUP_W), lambda b, c: (b, cid(c), 0))
    state_shape = jax.ShapeDtypeStruct((bsz, 2, SSM_STATE, 256), F32)
    if reverse:
        yf, xs_act, bc_act = fwd
        in_specs = [chunk_spec, chunk_spec, cur(COL_DT, 128), const(dtb), const(alog), state_spec,
                    cur(COL_SZ, 512), chunk_spec, const(dskip), const(nw)]
        args = [xs_act, bc_act, p, dtb, alog, h0, p, yf, dskip, nw]
        out_specs = [chunk_spec, state_spec]
        out_shape = [jax.ShapeDtypeStruct((bsz, t, GROUP_W), BF16), state_shape]
    else:
        in_specs = [cur(COL_SX, 512), cur(COL_SBC, 512), prev(COL_SX, 512), nxt(COL_SX, 512),
                    prev(COL_SBC, 512), nxt(COL_SBC, 512), cur(COL_DT, 128),
                    const(cwx), const(cbx), const(cwb), const(cbb), const(dtb), const(alog), state_spec]
        args = [p, p, p, p, p, p, p, cwx, cbx, cwb, cbb, dtb, alog, h0]
        out_specs = [chunk_spec, state_spec, chunk_spec, chunk_spec]
        out_shape = [jax.ShapeDtypeStruct((bsz, t, GROUP_W), F32), state_shape,
                     jax.ShapeDtypeStruct((bsz, t, GROUP_W), F32), jax.ShapeDtypeStruct((bsz, t, GROUP_W), BF16)]
    return pl.pallas_call(
        functools.partial(_ssd_kernel, reverse=reverse, nc=nc),
        grid=(bsz, nc),
        in_specs=in_specs,
        out_specs=out_specs,
        out_shape=out_shape,
        scratch_shapes=[pltpu.VMEM((2, SSM_STATE, 256), F32)],
        compiler_params=_cparams(("parallel", "arbitrary")),
        name="ssd_rev" if reverse else "ssd_fwd",
    )(*args)


def _ssd_consts(conv_w, conv_b, dt_bias, a_log, d_skip, norm_w):
    cwx, cwb = conv_w[:, :GROUP_W], conv_w[:, GROUP_W:]
    cbx, cbb = conv_b[:GROUP_W].reshape(1, -1), conv_b[GROUP_W:].reshape(1, -1)
    pad = lambda v: jnp.pad(v.reshape(1, -1), ((0, 0), (0, 128 - v.size)))
    return (cwx, cbx, cwb, cbb, pad(dt_bias), pad(a_log),
            jnp.repeat(d_skip, HEAD_DIM).reshape(1, -1), norm_w.reshape(1, -1))


def _ssd_mixer(p, pc, consts):
    bsz = p.shape[0]
    zero = jnp.zeros((bsz, 2, SSM_STATE, 256), F32)
    ycf, hf, *act_c = _ssd_direction(pc, consts, zero, False)
    yc, hb = _ssd_direction(pc, consts, zero, True, fwd=(ycf, *act_c))
    ylf, _, *act_l = _ssd_direction(p, consts, hf, False)
    yl, _ = _ssd_direction(p, consts, hb, True, fwd=(ylf, *act_l))
    return yl, yc


HY_CB = 128
HY_UNROLL = 16


def _hy_dims(seq):
    n = 2 * seq
    n1 = {4096: 128, 1024: 64, 512: 32, 256: 32, 128: 16}[seq]
    n2 = n // n1
    h = n1 // 2
    return dict(n=n, n1=n1, n2=n2, h=h, pa=h + 8, pb=n2 + 8, pc=n1 + 8)


def _hy_matrices(seq):
    d = _hy_dims(seq)
    n, n1, n2, h = d["n"], d["n1"], d["n2"], d["h"]

    def cis(num, den, sign):
        ang = (2.0 * math.pi / den) * (num % den).astype(F32)
        return jnp.cos(ang), sign * jnp.sin(ang)

    k1 = jnp.arange(n1, dtype=jnp.int32)
    nn = n2 * jnp.arange(n1, dtype=jnp.int32)[None, None, :] + jnp.arange(n2, dtype=jnp.int32)[:, None, None]
    e1r, e1i = cis(k1[None, :, None] * nn, n, -1.0)
    m1f = jnp.concatenate([e1r, e1i], axis=1)
    m1 = jnp.concatenate([jnp.concatenate([e1r[..., :h], -e1i[..., :h]], axis=2),
                          jnp.concatenate([e1i[..., :h], e1r[..., :h]], axis=2)], axis=1)
    a2 = jnp.arange(n2, dtype=jnp.int32)
    g2r, g2i = cis(a2[:, None] * a2[None, :], n2, -1.0)
    m2 = jnp.concatenate([jnp.concatenate([g2r, -g2i], axis=1),
                          jnp.concatenate([g2i, g2r], axis=1)], axis=0)
    num3 = (a2[None, :, None] * a2[None, None, :] * n1 + k1[:, None, None] * a2[None, :, None])
    e3r, e3i = cis(num3, n, 1.0)
    m3 = jnp.concatenate([jnp.concatenate([e3r, -e3i], axis=2),
                          jnp.concatenate([e3i, e3r], axis=2)], axis=1)
    hh = jnp.arange(h, dtype=jnp.int32)
    d4r, d4i = cis(hh[:, None] * k1[None, :], n1, 1.0)
    m4 = jnp.concatenate([jnp.concatenate([d4r, -d4i], axis=1),
                          jnp.concatenate([d4i, d4r], axis=1)], axis=0) / n
    return tuple(m.astype(BF16) for m in (m1f, m1, m2, m3, m4))


def _hy_prep_kernel(p_ref, w_ref, b_ref, o_ref, *, dims):
    n2, h, pa = dims["n2"], dims["h"], dims["pa"]
    seq = h * n2
    for j in range(n2):
        o_ref[j * pa + h:(j + 1) * pa, :] = jnp.zeros((pa - h, o_ref.shape[-1]), o_ref.dtype)
    w0, w1, w2, bias = w_ref[0:1, :], w_ref[1:2, :], w_ref[2:3, :], b_ref[...]
    ri = lax.broadcasted_iota(jnp.int32, (n2, o_ref.shape[-1]), 0)

    def body(i, carry):
        r0 = pl.multiple_of(i * n2, n2)
        x = p_ref[pl.ds(r0, n2), :].astype(F32)
        pstart = pl.multiple_of(jnp.maximum(r0 - HALO, 0), HALO)
        nstart = pl.multiple_of(jnp.minimum(r0 + n2, seq - HALO), HALO)
        prev_row = p_ref[pl.ds(pstart, HALO), :].astype(F32)[HALO - 1:HALO, :] * jnp.where(i > 0, 1.0, 0.0).astype(F32)
        next_row = p_ref[pl.ds(nstart, HALO), :].astype(F32)[0:1, :] * jnp.where(i < h - 1, 1.0, 0.0).astype(F32)
        up = jnp.where(ri == 0, prev_row, pltpu.roll(x, 1, 0))
        dn = jnp.where(ri == n2 - 1, next_row, pltpu.roll(x, n2 - 1, 0))
        o_ref[pl.ds(i, n2, stride=pa), :] = up * w0 + x * w1 + dn * w2 + bias
        return carry

    lax.fori_loop(0, h, body, 0)


def _hy_prep(p, short_w, short_b):
    bsz, t, _ = p.shape
    dims = _hy_dims(t)
    rows = dims["n2"] * dims["pa"]
    ncb = 3 * GROUP_W // HY_CB
    return pl.pallas_call(
        functools.partial(_hy_prep_kernel, dims=dims),
        grid=(bsz, ncb),
        in_specs=[pl.BlockSpec((None, t, HY_CB), lambda b, j: (b, 0, COL_HY // HY_CB + j)),
                  pl.BlockSpec((3, HY_CB), lambda b, j: (0, j)),
                  pl.BlockSpec((1, HY_CB), lambda b, j: (0, j))],
        out_specs=pl.BlockSpec((None, rows, HY_CB), lambda b, j: (b, 0, j)),
        out_shape=jax.ShapeDtypeStruct((bsz, rows, 3 * GROUP_W), F32),
        compiler_params=_cparams(("parallel", "parallel")),
        name="hy_prep",
    )(p, short_w, short_b.reshape(1, -1))


def _hy_filter_kernel(z_ref, w1_ref, b1_ref, w2_ref, b2_ref, w3_ref, b3_ref, w4_ref, fr_ref, dl_ref, o_ref):
    hi = lax.Precision.HIGHEST
    z = z_ref[...]
    fr = fr_ref[...]
    h = jnp.sin(fr * (jnp.dot(z, w1_ref[...], preferred_element_type=F32, precision=hi) + b1_ref[...]))
    h = jnp.sin(fr * (jnp.dot(h, w2_ref[...], preferred_element_type=F32, precision=hi) + b2_ref[...]))
    h = jnp.sin(fr * (jnp.dot(h, w3_ref[...], preferred_element_type=F32, precision=hi) + b3_ref[...]))
    full = jnp.dot(h.astype(BF16), w4_ref[...].astype(BF16), preferred_element_type=F32)
    t = z[:, 0:1]
    is_bwd = z[:, HY_EMB:HY_EMB + 1] > 0.5
    live = z[:, HY_EMB + 1:HY_EMB + 2]
    decay = jnp.exp(-t * jnp.abs(dl_ref[...])) * live
    for o in range(2):
        fwd = full[:, o * 2 * GROUP_W:o * 2 * GROUP_W + GROUP_W]
        bwd = full[:, o * 2 * GROUP_W + GROUP_W:(o + 1) * 2 * GROUP_W]
        o_ref[o] = jnp.where(is_bwd, bwd, fwd) * decay


def _hy_filter_features(seq):
    d = _hy_dims(seq)
    n, n1, n2 = d["n"], d["n1"], d["n2"]
    row = np.arange(n)
    time = n2 * (row % n1) + row // n1
    is_bwd = time > seq
    pos = np.where(is_bwd, n - time, time)
    live = (time != seq).astype(np.float64)
    pos = np.where(time == seq, 0, pos)
    t = np.linspace(0.0, 1.0, seq)[pos]
    bands = (HY_EMB - 1) // 2
    f = np.linspace(1e-4, bands - 1, bands)[None]
    wpos = (2.0 * math.pi * pos / seq)[:, None]
    feat = np.zeros((n, 128), np.float64)
    feat[:, 0] = t
    feat[:, 1:1 + bands] = np.cos(f * wpos)
    feat[:, 1 + bands:HY_EMB] = -np.sin(f * wpos)
    feat[:, HY_EMB] = is_bwd
    feat[:, HY_EMB + 1] = live
    return jnp.asarray(feat, F32)


def _hy_filter(seq, w1, b1, w2, b2, w3, b3, w4, freq, tr=512):
    d = _hy_dims(seq)
    n = d["n"]
    tr = min(tr, n)
    feat = _hy_filter_features(seq)
    w1p = jnp.pad(w1, ((0, 128 - HY_EMB), (0, 0)))
    max_decay = math.log(HY_DECAY_TARGET) / HY_FAST_PCT
    min_decay = math.log(HY_DECAY_TARGET) / HY_SLOW_PCT
    deltas = jnp.linspace(min_decay, max_decay, GROUP_W, dtype=F32).reshape(1, -1)
    row = lambda v: v.reshape(1, -1)
    const = lambda a: pl.BlockSpec(a.shape, lambda i: (0,) * a.ndim)
    args = [w1p, row(b1), w2, row(b2), w3, row(b3), w4, row(freq), deltas]
    return pl.pallas_call(
        _hy_filter_kernel,
        grid=(n // tr,),
        in_specs=[pl.BlockSpec((tr, 128), lambda i: (i, 0))] + [const(a) for a in args],
        out_specs=pl.BlockSpec((2, tr, GROUP_W), lambda i: (0, i, 0)),
        out_shape=jax.ShapeDtypeStruct((2, n, GROUP_W), F32),
        compiler_params=_cparams(("parallel",)),
        name="hy_filter",
    )(feat, *args)


def _hy_spectrum_kernel(k_ref, m1f_ref, m2_ref, re_ref, im_ref, tr_ref, ti_ref, *, dims):
    n1, n2, pb = dims["n1"], dims["n2"], dims["pb"]

    def stage1(j, carry):
        r0 = pl.multiple_of(j * n1, n1)
        a = jnp.dot(m1f_ref[j], k_ref[pl.ds(r0, n1), :].astype(BF16), preferred_element_type=F32)
        tr_ref[pl.ds(j, n1, stride=pb), :] = a[:n1]
        ti_ref[pl.ds(j, n1, stride=pb), :] = a[n1:]
        return carry

    lax.fori_loop(0, n2, stage1, 0, unroll=HY_UNROLL)

    def stage2(k, carry):
        r0 = pl.multiple_of(k * pb, 8)
        rhs = jnp.concatenate([tr_ref[pl.ds(r0, n2), :], ti_ref[pl.ds(r0, n2), :]], axis=0).astype(BF16)
        x = jnp.dot(m2_ref[...], rhs, preferred_element_type=F32)
        o0 = pl.multiple_of(k * n2, n2)
        re_ref[pl.ds(o0, n2), :] = x[:n2]
        im_ref[pl.ds(o0, n2), :] = x[n2:]
        return carry

    lax.fori_loop(0, n1, stage2, 0, unroll=HY_UNROLL)


def _hy_spectrum(kern, mats, seq):
    d = _hy_dims(seq)
    n, n1, pb = d["n"], d["n1"], d["pb"]
    m1f, _, m2, _, _ = mats
    ncb = GROUP_W // HY_CB
    blk = pl.BlockSpec((None, n, HY_CB), lambda o, j: (o, 0, j))
    const = lambda a: pl.BlockSpec(a.shape, lambda o, j: (0,) * a.ndim)
    return pl.pallas_call(
        functools.partial(_hy_spectrum_kernel, dims=d),
        grid=(2, ncb),
        in_specs=[blk, const(m1f), const(m2)],
        out_specs=[blk, blk],
        out_shape=[jax.ShapeDtypeStruct((2, n, GROUP_W), F32)] * 2,
        scratch_shapes=[pltpu.VMEM((n1 * pb, HY_CB), F32)] * 2,
        compiler_params=_cparams(("parallel", "parallel")),
        name="hy_spectrum",
    )(kern, m1f, m2)


def _pack_pair(re, im):
    half = jnp.uint32(0x8000)
    r = lax.bitcast_convert_type(re, jnp.uint32) + half
    i = lax.bitcast_convert_type(im, jnp.uint32) + half
    return (r & jnp.uint32(0xFFFF0000)) | (i >> 16)


def _unpack_pair(w):
    re = lax.bitcast_convert_type(w & jnp.uint32(0xFFFF0000), F32)
    im = lax.bitcast_convert_type(w << 16, F32)
    return re, im


def _hy_conv_kernel(u_ref, g_ref, kr_ref, ki_ref, skip_ref, m1_ref, m2_ref, m3_ref, m4_ref, o_ref,
                    t1, t2, *, dims, natural_out):
    n1, n2, h, pa, pb, pc = (dims[k] for k in ("n1", "n2", "h", "pa", "pb", "pc"))

    def fwd1(j, carry):
        r0 = pl.multiple_of(j * pa, 8)
        rhs = jnp.concatenate([u_ref[0, pl.ds(r0, h), :], u_ref[1, pl.ds(r0, h), :]], axis=0).astype(BF16)
        a = jnp.dot(m1_ref[j], rhs, preferred_element_type=F32)
        t1[pl.ds(j, n1, stride=pb), :] = _pack_pair(a[:n1], a[n1:])
        return carry

    lax.fori_loop(0, n2, fwd1, 0, unroll=HY_UNROLL)

    def mid(k, carry):
        r0 = pl.multiple_of(k * pb, 8)
        rhs = jnp.concatenate(_unpack_pair(t1[pl.ds(r0, n2), :]), axis=0).astype(BF16)
        x = jnp.dot(m2_ref[...], rhs, preferred_element_type=F32)
        f0 = pl.multiple_of(k * n2, n2)
        fr, fi = kr_ref[pl.ds(f0, n2), :], ki_ref[pl.ds(f0, n2), :]
        xr, xi = x[:n2], x[n2:]
        y = jnp.concatenate([xr * fr - xi * fi, xr * fi + xi * fr], axis=0).astype(BF16)
        c = jnp.dot(m3_ref[k], y, preferred_element_type=F32)
        t2[pl.ds(k, n2, stride=pc), :] = _pack_pair(c[:n2], c[n2:])
        return carry

    lax.fori_loop(0, n1, mid, 0, unroll=HY_UNROLL)

    if not natural_out:
        o_ref[...] = jnp.zeros_like(o_ref)
    skip = skip_ref[...]

    def inv2(j, carry):
        r0 = pl.multiple_of(j * pc, 8)
        rhs = jnp.concatenate(_unpack_pair(t2[pl.ds(r0, n1), :]), axis=0).astype(BF16)
        y = jnp.dot(m4_ref[...], rhs, preferred_element_type=F32)
        a0 = pl.multiple_of(j * pa, 8)
        for e in range(2):
            val = g_ref[e, pl.ds(a0, h), :] * (y[e * h:(e + 1) * h] + skip * u_ref[e, pl.ds(a0, h), :])
            if natural_out:
                o_ref[e, pl.ds(j, h, stride=n2), :] = val.astype(o_ref.dtype)
            else:
                o_ref[e, pl.ds(a0, h), :] = val
        return carry

    lax.fori_loop(0, n2, inv2, 0, unroll=HY_UNROLL)


def _hy_conv(u_arr, u_col, g_arr, g_col, kf_re, kf_im, order, skip, mats, seq, natural_out):
    d = _hy_dims(seq)
    bsz = u_arr.shape[0]
    rows = d["n2"] * d["pa"]
    _, m1, m2, m3, m4 = mats
    ncb = GROUP_W // HY_CB
    single = pl.Buffered(1)
    const = lambda a: pl.BlockSpec(a.shape, lambda j, p: (0,) * a.ndim, pipeline_mode=single)
    in_specs = [
        pl.BlockSpec((2, rows, HY_CB), lambda j, p: (p, 0, u_col // HY_CB + j)),
        pl.BlockSpec((2, rows, HY_CB), lambda j, p: (p, 0, g_col // HY_CB + j)),
        pl.BlockSpec((None, d["n"], HY_CB), lambda j, p: (order, 0, j), pipeline_mode=single),
        pl.BlockSpec((None, d["n"], HY_CB), lambda j, p: (order, 0, j), pipeline_mode=single),
        pl.BlockSpec((None, 1, HY_CB), lambda j, p: (order, 0, j)),
        const(m1), const(m2), const(m3), const(m4),
    ]
    if natural_out:
        out_spec = pl.BlockSpec((2, seq, HY_CB), lambda j, p: (p, 0, j))
        out_shape = jax.ShapeDtypeStruct((bsz, seq, GROUP_W), F32)
    else:
        out_spec = pl.BlockSpec((2, rows, HY_CB), lambda j, p: (p, 0, j))
        out_shape = jax.ShapeDtypeStruct((bsz, rows, GROUP_W), F32)
    return pl.pallas_call(
        functools.partial(_hy_conv_kernel, dims=d, natural_out=natural_out),
        grid=(ncb, bsz // 2),
        in_specs=in_specs,
        out_specs=out_spec,
        out_shape=out_shape,
        scratch_shapes=[pltpu.VMEM((d["n1"] * d["pb"], HY_CB), jnp.uint32),
                        pltpu.VMEM((d["n2"] * d["pc"], HY_CB), jnp.uint32)],
        compiler_params=_cparams(("parallel", "parallel")),
        name="hy_conv",
    )(u_arr, g_arr, kf_re, kf_im, skip.reshape(2, 1, GROUP_W), m1, m2, m3, m4)


def _hyena_mixer(p, short_w, short_b, filt_params, skip, mats):
    seq = p.shape[1]
    kern = _hy_filter(seq, *filt_params)
    kf_re, kf_im = _hy_spectrum(kern, mats, seq)
    ut = _hy_prep(p, short_w, short_b)
    zt = _hy_conv(ut, 2 * GROUP_W, ut, 0, kf_re, kf_im, 0, skip, mats, seq, natural_out=False)
    return _hy_conv(zt, 0, ut, GROUP_W, kf_re, kf_im, 1, skip, mats, seq, natural_out=True)


def _prep_weights(w_in, w_out):
    depth, d, _ = w_in.shape
    perm = np.asarray(HEAD_ORDER)
    q = w_in[..., :512].reshape(depth, d, N_HEADS, HEAD_DIM)[:, :, perm].reshape(depth, d, 512)
    o_a, o_ssm = 768, 768 + 3072 + 1536 + 16
    pieces = [q, w_in[..., 512:o_a], w_in[..., o_ssm - 16:o_ssm],
              jnp.zeros((depth, d, COL_HY - COL_DT - 16), w_in.dtype), w_in[..., o_a:o_ssm - 16]]
    w_in_p = jnp.concatenate(pieces, axis=-1).astype(BF16)
    assert w_in_p.shape[-1] == N_PROJ
    dm = w_out.shape[-1]
    wo_a = w_out[:, :512].reshape(depth, N_HEADS, HEAD_DIM, dm)[:, perm].reshape(depth, 512, dm)
    w_out_p = jnp.concatenate([wo_a, w_out[:, 512:]], axis=1).astype(BF16)
    return w_in_p, w_out_p


TILE_W_IN = (512, 2816)
TILE_MLP_UP = (1024, 2048)
TILE_DOWN = (1024, 1024, 2048)
TILE_OUT = (512, 2048)


def kernel(x, c, ctx, c_ctx, ada_w, ada_b, norm_mix, norm_mlp, w_in, w_out, attn_sink,
           hy_short_w, hy_short_b, hy_w1, hy_b1, hy_w2, hy_b2, hy_w3, hy_b3, hy_w4, hy_freq, hy_skip,
           na_rpb, ssm_conv_w, ssm_conv_b, ssm_dt_bias, ssm_a_log, ssm_d, ssm_norm,
           mlp_w1, mlp_w2, final_norm):
    bsz, seq, d = x.shape
    lc = ctx.shape[1]
    depth = ada_w.shape[0]
    assert bsz % 2 == 0 and bsz <= 7 and d == D_MODEL

    cs = jnp.zeros((8, d), F32).at[:bsz].set(c).at[bsz].set(c_ctx)
    mod = _ada_mod(cs, ada_w, ada_b)
    cos_t, sin_t = _rope_tables(seq)
    na_plan = _na_plan(seq)
    bias_tabs = _na_bias_tables(na_rpb, na_plan[-1])
    mats_l = _hy_matrices(seq)
    mats_c = _hy_matrices(lc)
    lat_row = lambda b: b
    ctx_row = lambda b: bsz
    tm_c = lc

    w_in_p, w_out_p = _prep_weights(w_in, w_out)
    w1_b = mlp_w1.astype(BF16)
    w2_b = mlp_w2.astype(BF16)

    xc = ctx
    for i in range(depth):
        last = i == depth - 1
        mod3 = mod[i].reshape(8, 1, 6 * d)
        p = _norm_matmul(x, norm_mix[i], mod3, lat_row, 0, 1, w_in_p, i, P_DTYPE, False, *TILE_W_IN)
        pc = _norm_matmul(xc, norm_mix[i], mod3, ctx_row, 0, 1, w_in_p, i, P_DTYPE, False, tm_c, TILE_W_IN[1])

        filt_params = (hy_w1[i], hy_b1[i], hy_w2[i], hy_b2[i], hy_w3[i], hy_b3[i], hy_w4[i], hy_freq[i])
        ssd_consts = _ssd_consts(ssm_conv_w[i], ssm_conv_b[i], ssm_dt_bias[i], ssm_a_log[i], ssm_d[i], ssm_norm[i])

        ya = _win_attn(attn_sink[i], p, pc, cos_t, sin_t, local=True)
        yb = _hyena_mixer(p, hy_short_w[i], hy_short_b[i], filt_params, hy_skip[i], mats_l)
        yn = _na_attn(p, pc, bias_tabs, i, na_plan, local=True)
        yd, ydc = _ssd_mixer(p, pc, ssd_consts)
        x = _out_proj((ya, yb, yn, yd), w_out_p, i, x, mod3, lat_row, 2, *TILE_OUT)
        hid = _norm_matmul(x, norm_mlp[i], mod3, lat_row, 3, 4, w1_b, i, BF16, True, *TILE_MLP_UP)
        x = _matmul_residual(hid, w2_b, i, x, mod3, lat_row, 5, *TILE_DOWN)

        if not last:
            yac = _win_attn(attn_sink[i], pc, pc, None, None, local=False)
            ybc = _hyena_mixer(pc, hy_short_w[i], hy_short_b[i], filt_params, hy_skip[i], mats_c)
            ync = _na_attn(pc, pc, None, i, None, local=False)
            xc = _out_proj((yac, ybc, ync, ydc), w_out_p, i, xc, mod3, ctx_row, 2, tm_c, TILE_OUT[1])
            hidc = _norm_matmul(xc, norm_mlp[i], mod3, ctx_row, 3, 4, w1_b, i, BF16, True, tm_c, TILE_MLP_UP[1])
            xc = _matmul_residual(hidc, w2_b, i, xc, mod3, ctx_row, 5, tm_c, *TILE_DOWN[1:])
    return _final_norm(x, final_norm)
```

```python
import functools
import math

import numpy as np
import jax
import jax.numpy as jnp
from jax import lax
from jax.experimental import pallas as pl
from jax.experimental.pallas import tpu as pltpu

F32 = jnp.float32
BF16 = jnp.bfloat16

D_MODEL = 2048
GRID_W = 64
EPS = 1e-6
NEG = -1e30
HEAD_DIM = 64
GROUP_W = D_MODEL // 4
N_HEADS = GROUP_W // HEAD_DIM
WINDOW = 128
BLOCK = 128
ROPE_BASE = 10000.0
HY_EMB = 33
HY_FFN = 64
HY_DECAY_TARGET = 1e-2
HY_FAST_PCT = 0.3
HY_SLOW_PCT = 1.5
NA_KR = 8
NA_KC = 16
SSM_STATE = 128
SSM_CHUNK = 128
D_FF = 4 * D_MODEL
SCALE = HEAD_DIM ** -0.5
LOG2E = math.log2(math.e)

COL_QA, COL_KA, COL_VA, COL_DT = 0, 512, 640, 768
COL_HY = 1024
COL_NQ, COL_NK, COL_NV = 2560, 3072, 3584
COL_SZ, COL_SX, COL_SBC = 4096, 4608, 5120
N_PROJ = 5632
HEAD_ORDER = (0, 4, 1, 5, 2, 6, 3, 7)

P_DTYPE = BF16
HALO = 16

V7X_VMEM_BYTES = 64 * 1024 * 1024
VMEM_LIMIT = 56 * 1024 * 1024


def _cparams(sem):
    return pltpu.CompilerParams(dimension_semantics=sem, vmem_limit_bytes=VMEM_LIMIT)


def _silu(x):
    return x * jax.nn.sigmoid(x)


def _ada_kernel(cs_ref, w_ref, b_ref, o_ref):
    a = _silu(cs_ref[...]).astype(BF16)
    o_ref[...] = jnp.dot(a, w_ref[...].astype(BF16), preferred_element_type=F32) + b_ref[...]


def _ada_mod(cs, ada_w, ada_b, tn=1024):
    depth, d, n = ada_w.shape
    return pl.pallas_call(
        _ada_kernel,
        grid=(depth, n // tn),
        in_specs=[
            pl.BlockSpec((8, d), lambda i, j: (0, 0)),
            pl.BlockSpec((None, d, tn), lambda i, j: (i, 0, j)),
            pl.BlockSpec((None, 1, tn), lambda i, j: (i, 0, j)),
        ],
        out_specs=pl.BlockSpec((None, 8, tn), lambda i, j: (i, 0, j)),
        out_shape=jax.ShapeDtypeStruct((depth, 8, n), F32),
        compiler_params=_cparams(("parallel", "parallel")),
        name="ada_mod",
    )(cs, ada_w, ada_b.reshape(depth, 1, n))


NORM_ROWS = 16


def _norm_matmul_kernel(x_ref, g_ref, sh_ref, sc_ref, w_ref, o_ref, h_ref, *, act):
    @pl.when(pl.program_id(2) == 0)
    def _():
        gain = g_ref[...] * (1.0 + sc_ref[...])
        shift = sh_ref[...]

        def rows(r, carry):
            r0 = pl.multiple_of(r * NORM_ROWS, NORM_ROWS)
            xf = x_ref[pl.ds(r0, NORM_ROWS), :]
            ms = jnp.mean(xf * xf, axis=-1, keepdims=True)
            h_ref[pl.ds(r0, NORM_ROWS), :] = (xf * lax.rsqrt(ms + EPS) * gain + shift).astype(BF16)
            return carry

        lax.fori_loop(0, x_ref.shape[0] // NORM_ROWS, rows, 0, unroll=8)

    r = jnp.dot(h_ref[...], w_ref[...], preferred_element_type=F32)
    if act:
        r = jnp.square(jnp.maximum(r, 0.0))
    o_ref[...] = r.astype(o_ref.dtype)


def _norm_matmul(x, g, mod3, row_of_b, sh_idx, sc_idx, w, layer, out_dtype, act, tm, tn):
    bsz, t, d = x.shape
    n = w.shape[2]
    tm = min(tm, t)
    return pl.pallas_call(
        functools.partial(_norm_matmul_kernel, act=act),
        grid=(bsz, t // tm, n // tn),
        in_specs=[
            pl.BlockSpec((None, tm, d), lambda b, m, j: (b, m, 0)),
            pl.BlockSpec((1, d), lambda b, m, j: (0, 0)),
            pl.BlockSpec((None, 1, d), lambda b, m, j: (row_of_b(b), 0, sh_idx)),
            pl.BlockSpec((None, 1, d), lambda b, m, j: (row_of_b(b), 0, sc_idx)),
            pl.BlockSpec((None, d, tn), lambda b, m, j: (layer, 0, j)),
        ],
        out_specs=pl.BlockSpec((None, tm, tn), lambda b, m, j: (b, m, j)),
        out_shape=jax.ShapeDtypeStruct((bsz, t, n), out_dtype),
        scratch_shapes=[pltpu.VMEM((tm, d), BF16)],
        compiler_params=_cparams(("parallel", "parallel", "arbitrary")),
        name="norm_matmul",
    )(x, g.reshape(1, d), mod3, mod3, w)


def _mm_res_kernel(a_ref, w_ref, x_ref, gate_ref, o_ref, acc_ref, *, nk):
    k = pl.program_id(3)

    @pl.when(k == 0)
    def _():
        acc_ref[...] = jnp.zeros_like(acc_ref)

    acc_ref[...] += jnp.dot(a_ref[...], w_ref[...], preferred_element_type=F32)

    @pl.when(k == nk - 1)
    def _():
        o_ref[...] = x_ref[...] + gate_ref[...] * acc_ref[...]


def _matmul_residual(a, w, layer, x, mod3, row_of_b, gate_idx, tm, tn, tk):
    bsz, t, kdim = a.shape
    n = w.shape[2]
    tm = min(tm, t)
    nk = kdim // tk
    return pl.pallas_call(
        functools.partial(_mm_res_kernel, nk=nk),
        grid=(bsz, t // tm, n // tn, nk),
        in_specs=[
            pl.BlockSpec((None, tm, tk), lambda b, m, j, k: (b, m, k)),
            pl.BlockSpec((None, tk, tn), lambda b, m, j, k: (layer, k, j)),
            pl.BlockSpec((None, tm, tn), lambda b, m, j, k: (b, m, j)),
            pl.BlockSpec((None, 1, tn), lambda b, m, j, k: (row_of_b(b), 0, gate_idx * (D_MODEL // tn) + j)),
        ],
        out_specs=pl.BlockSpec((None, tm, tn), lambda b, m, j, k: (b, m, j)),
        out_shape=jax.ShapeDtypeStruct(x.shape, F32),
        scratch_shapes=[pltpu.VMEM((tm, tn), F32)],
        compiler_params=_cparams(("parallel", "parallel", "parallel", "arbitrary")),
        name="matmul_residual",
    )(a, w, x, mod3)


def _out_proj_kernel(ya_ref, yb_ref, yn_ref, yd_ref, w_ref, x_ref, gate_ref, o_ref):
    acc = None
    for g, y_ref in enumerate((ya_ref, yb_ref, yn_ref, yd_ref)):
        part = jnp.dot(y_ref[...].astype(BF16), w_ref[GROUP_W * g:GROUP_W * (g + 1), :],
                       preferred_element_type=F32)
        acc = part if acc is None else acc + part
    o_ref[...] = x_ref[...] + gate_ref[...] * acc


def _out_proj(ys, w, layer, x, mod3, row_of_b, gate_idx, tm, tn):
    bsz, t, d = x.shape
    tm = min(tm, t)
    y_spec = pl.BlockSpec((None, tm, GROUP_W), lambda b, m, j: (b, m, 0))
    return pl.pallas_call(
        _out_proj_kernel,
        grid=(bsz, t // tm, d // tn),
        in_specs=[y_spec, y_spec, y_spec, y_spec,
                  pl.BlockSpec((None, 4 * GROUP_W, tn), lambda b, m, j: (layer, 0, j),
                               pipeline_mode=pl.Buffered(1) if tn == d else None),
                  pl.BlockSpec((None, tm, tn), lambda b, m, j: (b, m, j)),
                  pl.BlockSpec((None, 1, tn), lambda b, m, j: (row_of_b(b), 0, gate_idx * (d // tn) + j))],
        out_specs=pl.BlockSpec((None, tm, tn), lambda b, m, j: (b, m, j)),
        out_shape=jax.ShapeDtypeStruct(x.shape, F32),
        compiler_params=_cparams(("parallel", "parallel", "parallel")),
        name="out_proj",
    )(*ys, w, x, mod3)


def _final_norm_kernel(x_ref, g_ref, o_ref):
    xf = x_ref[...]
    ms = jnp.mean(xf * xf, axis=-1, keepdims=True)
    o_ref[...] = xf * lax.rsqrt(ms + EPS) * g_ref[...]


def _final_norm(x, g, tm=1024):
    bsz, t, d = x.shape
    tm = min(tm, t)
    return pl.pallas_call(
        _final_norm_kernel,
        grid=(bsz, t // tm),
        in_specs=[pl.BlockSpec((None, tm, d), lambda b, m: (b, m, 0)),
                  pl.BlockSpec((1, d), lambda b, m: (0, 0))],
        out_specs=pl.BlockSpec((None, tm, d), lambda b, m: (b, m, 0)),
        out_shape=jax.ShapeDtypeStruct(x.shape, F32),
        compiler_params=_cparams(("parallel", "parallel")),
        name="final_norm",
    )(x, g.reshape(1, d))


WIN_QBLOCKS = 4


def _rope(x, cos, sin_signed, lane_lo):
    w = x.shape[-1]
    partner = jnp.where(lane_lo, pltpu.roll(x, w - 16, 1), pltpu.roll(x, 16, 1))
    return x * cos + partner * sin_signed


def _win_attn_kernel(sink_ref, q_ref, kc_ref, vc_ref, *rest, seq, local):
    if local:
        k_ref, v_ref, cos_ref, sin_ref, mask_ref, o_ref = rest
    else:
        (o_ref,) = rest
    nb = seq // BLOCK
    lane = lax.broadcasted_iota(jnp.int32, (BLOCK, 128), 1)
    lo = lane < HEAD_DIM
    lane_lo = (lane % 32) < 16
    lane3_lo = (lax.broadcasted_iota(jnp.int32, (3 * BLOCK, 128), 1) % 32) < 16
    nt = (((1,), (1,)), ((), ()))
    kc = kc_ref[...].astype(BF16)
    vc = vc_ref[...].astype(BF16)
    sink = jnp.concatenate(
        [jnp.full((BLOCK, 1), sink_ref[HEAD_ORDER[i]] * LOG2E, F32) for i in range(N_HEADS)], axis=0)
    nq = q_ref.shape[0] // BLOCK
    for sub in range(nq):
        n = pl.program_id(1) * nq + sub
        q = q_ref[sub * BLOCK:(sub + 1) * BLOCK, :].astype(F32) * (SCALE * LOG2E)
        if local:
            r0 = pl.multiple_of(n * BLOCK, BLOCK)
            cos_q = cos_ref[pl.ds(r0, BLOCK), :]
            sin_q = sin_ref[pl.ds(r0, BLOCK), :]
        rows = []
        for m in range(4):
            qm = q[:, 128 * m:128 * (m + 1)]
            if local:
                qm = _rope(qm, cos_q, sin_q, lane_lo)
            rows.append(jnp.where(lo, qm, 0.0))
            rows.append(jnp.where(lo, 0.0, qm))
        qbd = jnp.concatenate(rows, axis=0).astype(BF16)
        s_ctx = lax.dot_general(qbd, kc, nt, preferred_element_type=F32)
        mx = jnp.maximum(jnp.max(s_ctx, axis=-1, keepdims=True), sink)
        if local:
            start = pl.multiple_of(jnp.clip((n - 1) * BLOCK, 0, seq - 3 * BLOCK), BLOCK)
            kb = _rope(k_ref[pl.ds(start, 3 * BLOCK), :].astype(F32), cos_ref[pl.ds(start, 3 * BLOCK), :],
                       sin_ref[pl.ds(start, 3 * BLOCK), :], lane3_lo).astype(BF16)
            vb = v_ref[pl.ds(start, 3 * BLOCK), :].astype(BF16)
            mask = mask_ref[jnp.where(n == 0, 0, jnp.where(n == nb - 1, 2, 1))]
            s_loc = lax.dot_general(qbd, kb, nt, preferred_element_type=F32)
            s_loc = (s_loc.reshape(N_HEADS, BLOCK, 3 * BLOCK) + mask[None]).reshape(s_loc.shape)
            mx = jnp.maximum(mx, jnp.max(s_loc, axis=-1, keepdims=True))
        p_ctx = jnp.exp2(s_ctx - mx)
        den = jnp.sum(p_ctx, axis=-1, keepdims=True) + jnp.exp2(sink - mx)
        acc = jnp.dot(p_ctx.astype(BF16), vc, preferred_element_type=F32)
        if local:
            p_loc = jnp.exp2(s_loc - mx)
            den = den + jnp.sum(p_loc, axis=-1, keepdims=True)
            acc = acc + jnp.dot(p_loc.astype(BF16), vb, preferred_element_type=F32)
        o = acc / den
        outs = [jnp.where(lo, o[(2 * m) * BLOCK:(2 * m + 1) * BLOCK], o[(2 * m + 1) * BLOCK:(2 * m + 2) * BLOCK])
                for m in range(4)]
        o_ref[sub * BLOCK:(sub + 1) * BLOCK, :] = jnp.concatenate(outs, axis=1).astype(o_ref.dtype)


def _win_attn(sink, pq, pc, cos_t, sin_t, local):
    bsz, t, _ = pq.shape
    lc = pc.shape[1]
    tq = min(WIN_QBLOCKS * BLOCK, t)
    in_specs = [
        pl.BlockSpec(memory_space=pltpu.SMEM),
        pl.BlockSpec((None, tq, 512), lambda b, n: (b, n, COL_QA // 512)),
        pl.BlockSpec((None, lc, 128), lambda b, n: (b, 0, COL_KA // 128)),
        pl.BlockSpec((None, lc, 128), lambda b, n: (b, 0, COL_VA // 128)),
    ]
    args = [sink, pq, pc, pc]
    if local:
        in_specs += [
            pl.BlockSpec((None, t, 128), lambda b, n: (b, 0, COL_KA // 128)),
            pl.BlockSpec((None, t, 128), lambda b, n: (b, 0, COL_VA // 128)),
            pl.BlockSpec((t, 128), lambda b, n: (0, 0)),
            pl.BlockSpec((t, 128), lambda b, n: (0, 0)),
            pl.BlockSpec((3, BLOCK, 3 * BLOCK), lambda b, n: (0, 0, 0)),
        ]
        args += [pq, pq, cos_t, sin_t, _window_masks(t)]
    return pl.pallas_call(
        functools.partial(_win_attn_kernel, seq=t, local=local),
        grid=(bsz, t // tq),
        in_specs=in_specs,
        out_specs=pl.BlockSpec((None, tq, GROUP_W), lambda b, n: (b, n, 0)),
        out_shape=jax.ShapeDtypeStruct((bsz, t, GROUP_W), BF16),
        compiler_params=_cparams(("parallel", "arbitrary")),
        name="win_attn" if local else "ctx_attn_a",
    )(*args)


def _window_masks(seq):
    nb = seq // BLOCK
    qi = np.arange(BLOCK)[:, None]
    kj = np.arange(3 * BLOCK)[None, :]
    tabs = []
    for n in (0, 1, nb - 1):
        start = int(np.clip((n - 1) * BLOCK, 0, seq - 3 * BLOCK))
        rel = (start + kj) - (n * BLOCK + qi)
        tabs.append(np.where(np.abs(rel) <= WINDOW, 0.0, NEG))
    return jnp.asarray(np.stack(tabs), F32)


def _rope_tables(seq):
    t = np.arange(seq)
    row, col = t // GRID_W, t % GRID_W
    quarter = HEAD_DIM // 4
    inv = ROPE_BASE ** (-np.arange(quarter, dtype=np.float64) / quarter)
    inv = inv.astype(np.float32).astype(np.float64)
    lane = np.arange(128)
    j = lane % HEAD_DIM
    pos = np.where((j < HEAD_DIM // 2)[None, :], row[:, None], col[:, None]).astype(np.float64)
    ang = (pos * inv[j % quarter][None, :]).astype(np.float32)
    cos = np.cos(ang.astype(np.float64))
    sin = np.sin(ang.astype(np.float64))
    sign = np.where((j % 32) < 16, -1.0, 1.0)[None, :]
    return jnp.asarray(cos, F32), jnp.asarray(sin * sign, F32)


NA_ROWS_PER_STEP = 4


def _na_kernel(var_ref, ws_ref, q_ref, kc_ref, vc_ref, *rest, local, win_rows):
    del var_ref
    if local:
        k_ref, v_ref, bias_ref, o_ref = rest
    else:
        (o_ref,) = rest
    g = pl.program_id(1)
    tq = q_ref.shape[0]
    q = q_ref[...].astype(F32) * (SCALE * LOG2E)
    head = lax.broadcasted_iota(jnp.int32, (tq, 256), 1) // HEAD_DIM
    nt = (((1,), (1,)), ((), ()))
    outs = []
    for half in range(2):
        cols = slice(256 * half, 256 * (half + 1))
        q4 = q[:, cols]
        qbd = jnp.concatenate([jnp.where(head == h, q4, 0.0) for h in range(4)], axis=0).astype(BF16)
        kc4 = kc_ref[:, cols].astype(BF16)
        vc4 = vc_ref[:, cols].astype(BF16)
        s_ctx = lax.dot_general(qbd, kc4, nt, preferred_element_type=F32)
        mx = jnp.max(s_ctx, axis=-1, keepdims=True)
        if local:
            nkey = win_rows * GRID_W
            start = pl.multiple_of(ws_ref[g] * GRID_W, GRID_W)
            k4 = k_ref[pl.ds(start, nkey), cols].astype(BF16)
            v4 = v_ref[pl.ds(start, nkey), cols].astype(BF16)
            bias = bias_ref[4 * half:4 * half + 4].astype(F32).reshape(4 * tq, nkey)
            s_loc = lax.dot_general(qbd, k4, nt, preferred_element_type=F32) + bias
            mx = jnp.maximum(mx, jnp.max(s_loc, axis=-1, keepdims=True))
        p_ctx = jnp.exp2(s_ctx - mx)
        den = jnp.sum(p_ctx, axis=-1, keepdims=True)
        acc = jnp.dot(p_ctx.astype(BF16), vc4, preferred_element_type=F32)
        if local:
            p_loc = jnp.exp2(s_loc - mx)
            den = den + jnp.sum(p_loc, axis=-1, keepdims=True)
            acc = acc + jnp.dot(p_loc.astype(BF16), v4, preferred_element_type=F32)
        o = acc / den
        o4 = jnp.where(head == 0, o[0:tq], 0.0)
        for h in range(1, 4):
            o4 = o4 + jnp.where(head == h, o[h * tq:(h + 1) * tq], 0.0)
        outs.append(o4)
    o_ref[...] = jnp.concatenate(outs, axis=1).astype(o_ref.dtype)


def _na_plan(seq):
    rows = seq // GRID_W
    kr = min(NA_KR, rows)
    r_step = NA_ROWS_PER_STEP
    win_rows = min(r_step + kr, rows)
    n_groups = rows // r_step
    wstart = np.zeros(n_groups, np.int32)
    pats = []
    keys = {}
    var = np.zeros(n_groups, np.int32)
    for g in range(n_groups):
        r0 = g * r_step
        ws = int(np.clip(r0 - kr // 2, 0, rows - win_rows))
        wstart[g] = ws
        r = r0 + np.arange(r_step)
        rstart = np.clip(r - kr // 2, 0, rows - kr)
        krow = ws + np.arange(win_rows)
        valid = (krow[None, :] >= rstart[:, None]) & (krow[None, :] < rstart[:, None] + kr)
        roff = krow[None, :] - r[:, None] + NA_KR - 1
        key = (valid.tobytes(), np.where(valid, roff, 0).tobytes())
        if key not in keys:
            keys[key] = len(pats)
            pats.append((valid, np.where(valid, roff, 0)))
        var[g] = keys[key]
    return rows, win_rows, n_groups, wstart, var, pats


def _na_bias_tables(rpb, pats):
    cq = np.arange(GRID_W)
    ck = np.arange(GRID_W)
    cstart = np.clip(cq - NA_KC // 2, 0, GRID_W - NA_KC)
    col_valid = (ck[None] >= cstart[:, None]) & (ck[None] < cstart[:, None] + NA_KC)
    coff = np.clip(ck[None] - cq[:, None], -(NA_KC - 1), NA_KC - 1) + NA_KC - 1
    by_col = jnp.where(col_valid, rpb[..., coff] * LOG2E, NEG).astype(BF16)
    depth, heads, n_ro = by_col.shape[:3]
    flat = by_col.transpose(0, 1, 3, 2, 4).reshape(depth, heads, GRID_W, n_ro * GRID_W)
    tabs = []
    for valid, roff in pats:
        r_step, win_rows = valid.shape
        q_rows = []
        for i in range(r_step):
            a_ok = np.nonzero(valid[i])[0]
            a0, a1 = int(a_ok[0]), int(a_ok[-1]) + 1
            assert valid[i, a0:a1].all() and (np.diff(roff[i, a0:a1]) == 1).all()
            seg = flat[..., int(roff[i, a0]) * GRID_W:(int(roff[i, a0]) + a1 - a0) * GRID_W]
            q_rows.append(jnp.pad(seg, ((0, 0), (0, 0), (0, 0), (a0 * GRID_W, (win_rows - a1) * GRID_W)),
                                  constant_values=NEG))
        tabs.append(jnp.concatenate(q_rows, axis=-2))
    return jnp.stack(tabs, axis=1)


def _na_attn(pq, pc, bias_tabs, layer, plan, local):
    bsz, t, _ = pq.shape
    lc = pc.shape[1]
    if local:
        rows, win_rows, n_groups, wstart, var, _ = plan
        tq = NA_ROWS_PER_STEP * GRID_W
    else:
        win_rows, n_groups, tq = 0, 1, t
        wstart = np.zeros(1, np.int32)
        var = np.zeros(1, np.int32)
    in_specs = [
        pl.BlockSpec((None, tq, 512), lambda b, g, vr, ws: (b, g, COL_NQ // 512)),
        pl.BlockSpec((None, lc, 512), lambda b, g, vr, ws: (b, 0, COL_NK // 512)),
        pl.BlockSpec((None, lc, 512), lambda b, g, vr, ws: (b, 0, COL_NV // 512)),
    ]
    args = [pq, pc, pc]
    if local:
        in_specs += [
            pl.BlockSpec((None, t, 512), lambda b, g, vr, ws: (b, 0, COL_NK // 512)),
            pl.BlockSpec((None, t, 512), lambda b, g, vr, ws: (b, 0, COL_NV // 512)),
            pl.BlockSpec((None, None, N_HEADS, tq, win_rows * GRID_W),
                         lambda b, g, vr, ws: (layer, vr[g], 0, 0, 0)),
        ]
        args += [pq, pq, bias_tabs]
    grid_spec = pltpu.PrefetchScalarGridSpec(
        num_scalar_prefetch=2,
        grid=(bsz, n_groups),
        in_specs=in_specs,
        out_specs=pl.BlockSpec((None, tq, GROUP_W), lambda b, g, vr, ws: (b, g, 0)),
    )
    return pl.pallas_call(
        functools.partial(_na_kernel, local=local, win_rows=win_rows),
        grid_spec=grid_spec,
        out_shape=jax.ShapeDtypeStruct((bsz, t, GROUP_W), BF16),
        compiler_params=_cparams(("parallel", "arbitrary")),
        name="na_attn" if local else "ctx_attn_c",
    )(jnp.asarray(var), jnp.asarray(wstart), *args)


SSD_CHUNKS = 4


def _softplus(x):
    return jnp.maximum(x, 0.0) + jnp.log(1.0 + jnp.exp(-jnp.abs(x)))


def _bf16_parts(x, n):
    parts = []
    for _ in range(n):
        part = x.astype(BF16)
        parts.append(part)
        x = x - part.astype(F32)
    return parts


def _conv3_silu(cur, prev_blk, next_blk, w_ref, b_ref, has_prev, has_next):
    x = cur.astype(F32)
    rows = x.shape[0]
    prev_row = prev_blk.astype(F32)[HALO - 1:HALO, :] * has_prev
    next_row = next_blk.astype(F32)[0:1, :] * has_next
    ri = lax.broadcasted_iota(jnp.int32, x.shape, 0)
    up = jnp.where(ri == 0, prev_row, pltpu.roll(x, 1, 0))
    dn = jnp.where(ri == rows - 1, next_row, pltpu.roll(x, rows - 1, 0))
    u = up * w_ref[0:1, :] + x * w_ref[1:2, :] + dn * w_ref[2:3, :] + b_ref[...]
    return _silu(u)


def _ssd_chunk(xs, bc, dt, a, st_ref, keep, tri, expand, head4, d_off, reverse):
    L = SSM_CHUNK
    nt = (((1,), (1,)), ((), ()))
    a_t = a.T
    c_col = sum(jnp.dot(tri, part, preferred_element_type=F32) for part in _bf16_parts(a, 3))
    c_row = sum(lax.dot_general(part, tri, nt, preferred_element_type=F32) for part in _bf16_parts(a_t, 3))
    c_exp = sum(jnp.dot(part, expand, preferred_element_type=F32) for part in _bf16_parts(c_col, 2))
    dt_exp = sum(jnp.dot(part, expand, preferred_element_type=F32) for part in _bf16_parts(dt, 2))
    end = 0 if reverse else L - 1
    cend = c_exp[end:end + 1, :]
    x_dt = xs * dt_exp
    out_decay = jnp.exp(c_exp)
    x_dec = x_dt * jnp.exp(cend - c_exp)
    chunk_decay = jnp.exp(cend)

    ys = []
    for g in range(2):
        gl = slice(256 * g, 256 * (g + 1))
        b_g = bc[:, 128 * g:128 * (g + 1)]
        c_g = bc[:, 256 + 128 * g:256 + 128 * (g + 1)].astype(BF16)
        cb = lax.dot_general(c_g, b_g.astype(BF16), nt, preferred_element_type=F32)
        ms = []
        for hh in range(4):
            j = d_off + 4 * g + hh
            diff = c_col[:, j:j + 1] - c_row[j:j + 1, :]
            ms.append(cb * jnp.exp(jnp.where(keep, diff, NEG)))
        m_g = jnp.concatenate(ms, axis=0).astype(BF16)
        o = jnp.dot(m_g, x_dt[:, gl].astype(BF16), preferred_element_type=F32)
        y_diag = jnp.where(head4 == 0, o[0:L], 0.0)
        for hh in range(1, 4):
            y_diag = y_diag + jnp.where(head4 == hh, o[hh * L:(hh + 1) * L], 0.0)
        st = st_ref[g]
        y_off = jnp.dot(c_g, st.astype(BF16), preferred_element_type=F32) * out_decay[:, gl]
        ys.append(y_diag + y_off)
        st_ref[g] = chunk_decay[:, gl] * st + jnp.dot(
            b_g.T.astype(BF16), x_dec[:, gl].astype(BF16), preferred_element_type=F32)
    return jnp.concatenate(ys, axis=1)


def _ssd_kernel(*refs, reverse, nc):
    finalize = reverse
    if reverse:
        (xsc_ref, bcc_ref, dt_ref, dtb_ref, alog_ref, h0_ref,
         z_ref, yf_ref, dskip_ref, nw_ref, y_ref, ht_ref, st_ref) = refs
    else:
        (xs_ref, bc_ref, xsp_ref, xsn_ref, bcp_ref, bcn_ref, dt_ref, cwx_ref, cbx_ref, cwb_ref, cbb_ref,
         dtb_ref, alog_ref, h0_ref, y_ref, ht_ref, xsc_ref, bcc_ref, st_ref) = refs
    c = pl.program_id(1)
    cid = (nc - 1 - c) if reverse else c
    d_off = 8 if reverse else 0
    L = SSM_CHUNK
    n_sub = dt_ref.shape[0] // L

    @pl.when(c == 0)
    def _():
        st_ref[...] = h0_ref[...]

    if reverse:
        xs = xsc_ref[...]
        bc = bcc_ref[...].astype(F32)
    else:
        has_prev = jnp.where(cid > 0, 1.0, 0.0).astype(F32)
        has_next = jnp.where(cid < nc - 1, 1.0, 0.0).astype(F32)
        xs = _conv3_silu(xs_ref[...], xsp_ref[...], xsn_ref[...], cwx_ref, cbx_ref, has_prev, has_next)
        bc = _conv3_silu(bc_ref[...], bcp_ref[...], bcn_ref[...], cwb_ref, cbb_ref, has_prev, has_next)
        xsc_ref[...] = xs
        bcc_ref[...] = bc.astype(BF16)

    dt_all = _softplus(dt_ref[...].astype(F32) + dtb_ref[...])
    a_all = dt_all * (-jnp.exp(alog_ref[...]))
    ri = lax.broadcasted_iota(jnp.int32, (L, L), 0)
    ci = lax.broadcasted_iota(jnp.int32, (L, L), 1)
    keep = (ci >= ri) if reverse else (ci <= ri)
    tri = keep.astype(BF16)
    nt = (((1,), (1,)), ((), ()))
    ej = lax.broadcasted_iota(jnp.int32, (128, GROUP_W), 0)
    eh = lax.broadcasted_iota(jnp.int32, (128, GROUP_W), 1) // HEAD_DIM
    expand = (ej == eh + d_off).astype(BF16)
    head4 = lax.broadcasted_iota(jnp.int32, (L, 256), 1) // HEAD_DIM
    xs_all, bc_all = xs, bc
    y_sub = [None] * n_sub
    for sub in (range(n_sub - 1, -1, -1) if reverse else range(n_sub)):
        y_sub[sub] = _ssd_chunk(xs_all[sub * L:(sub + 1) * L], bc_all[sub * L:(sub + 1) * L],
                                dt_all[sub * L:(sub + 1) * L], a_all[sub * L:(sub + 1) * L],
                                st_ref, keep, tri, expand, head4, d_off, reverse)
    y = jnp.concatenate(y_sub, axis=0) if n_sub > 1 else y_sub[0]
    if finalize:
        y = y + yf_ref[...] + xs * dskip_ref[...]
        y = y * _silu(z_ref[...].astype(F32))
        halves = []
        for g in range(2):
            yg = y[:, 256 * g:256 * (g + 1)]
            halves.append(yg * lax.rsqrt(jnp.mean(yg * yg, axis=-1, keepdims=True) + EPS))
        y = jnp.concatenate(halves, axis=1) * nw_ref[...]
    y_ref[...] = y.astype(y_ref.dtype)

    @pl.when(c == nc - 1)
    def _():
        ht_ref[...] = st_ref[...]


def _ssd_direction(p, consts, h0, reverse, fwd=None):
    cwx, cbx, cwb, cbb, dtb, alog, dskip, nw = consts
    bsz, t, _ = p.shape
    rows = min(SSD_CHUNKS * SSM_CHUNK, t)
    nc = t // rows
    hb = rows // HALO
    nhalo = t // HALO

    def cid(c):
        return (nc - 1 - c) if reverse else c

    def cur(col, width):
        return pl.BlockSpec((None, rows, width), lambda b, c: (b, cid(c), col // width))

    def prev(col, width):
        return pl.BlockSpec((None, HALO, width), lambda b, c: (b, jnp.maximum(cid(c) * hb - 1, 0), col // width))

    def nxt(col, width):
        return pl.BlockSpec((None, HALO, width), lambda b, c: (b, jnp.minimum((cid(c) + 1) * hb, nhalo - 1), col // width))

    def const(arr):
        return pl.BlockSpec(arr.shape, lambda b, c: (0,) * arr.ndim)

    state_spec = pl.BlockSpec((None, 2, SSM_STATE, 256), lambda b, c: (b, 0, 0, 0))
    chunk_spec = pl.BlockSpec((None, rows, GROUP_W), lambda b, c: (b, cid(c), 0))
    state_shape = jax.ShapeDtypeStruct((bsz, 2, SSM_STATE, 256), F32)
    if reverse:
        yf, xs_act, bc_act = fwd
        in_specs = [chunk_spec, chunk_spec, cur(COL_DT, 128), const(dtb), const(alog), state_spec,
                    cur(COL_SZ, 512), chunk_spec, const(dskip), const(nw)]
        args = [xs_act, bc_act, p, dtb, alog, h0, p, yf, dskip, nw]
        out_specs = [chunk_spec, state_spec]
        out_shape = [jax.ShapeDtypeStruct((bsz, t, GROUP_W), BF16), state_shape]
    else:
        in_specs = [cur(COL_SX, 512), cur(COL_SBC, 512), prev(COL_SX, 512), nxt(COL_SX, 512),
                    prev(COL_SBC, 512), nxt(COL_SBC, 512), cur(COL_DT, 128),
                    const(cwx), const(cbx), const(cwb), const(cbb), const(dtb), const(alog), state_spec]
        args = [p, p, p, p, p, p, p, cwx, cbx, cwb, cbb, dtb, alog, h0]
        out_specs = [chunk_spec, state_spec, chunk_spec, chunk_spec]
        out_shape = [jax.ShapeDtypeStruct((bsz, t, GROUP_W), F32), state_shape,
                     jax.ShapeDtypeStruct((bsz, t, GROUP_W), F32), jax.ShapeDtypeStruct((bsz, t, GROUP_W), BF16)]
    return pl.pallas_call(
        functools.partial(_ssd_kernel, reverse=reverse, nc=nc),
        grid=(bsz, nc),
        in_specs=in_specs,
        out_specs=out_specs,
        out_shape=out_shape,
        scratch_shapes=[pltpu.VMEM((2, SSM_STATE, 256), F32)],
        compiler_params=_cparams(("parallel", "arbitrary")),
        name="ssd_rev" if reverse else "ssd_fwd",
    )(*args)


def _ssd_consts(conv_w, conv_b, dt_bias, a_log, d_skip, norm_w):
    cwx, cwb = conv_w[:, :GROUP_W], conv_w[:, GROUP_W:]
    cbx, cbb = conv_b[:GROUP_W].reshape(1, -1), conv_b[GROUP_W:].reshape(1, -1)
    pad = lambda v: jnp.pad(v.reshape(1, -1), ((0, 0), (0, 128 - v.size)))
    return (cwx, cbx, cwb, cbb, pad(dt_bias), pad(a_log),
            jnp.repeat(d_skip, HEAD_DIM).reshape(1, -1), norm_w.reshape(1, -1))


def _ssd_mixer(p, pc, consts):
    bsz = p.shape[0]
    zero = jnp.zeros((bsz, 2, SSM_STATE, 256), F32)
    ycf, hf, *act_c = _ssd_direction(pc, consts, zero, False)
    yc, hb = _ssd_direction(pc, consts, zero, True, fwd=(ycf, *act_c))
    ylf, _, *act_l = _ssd_direction(p, consts, hf, False)
    yl, _ = _ssd_direction(p, consts, hb, True, fwd=(ylf, *act_l))
    return yl, yc


HY_CB = 128


def _hy_cb(seq):
    del seq
    return HY_CB
HY_UNROLL = 16


def _hy_dims(seq):
    n = 2 * seq
    n1 = {4096: 128, 1024: 64, 512: 32, 256: 32, 128: 16}[seq]
    n2 = n // n1
    h = n1 // 2
    return dict(n=n, n1=n1, n2=n2, h=h, pa=h + 8, pb=n2 + 8, pc=n1 + 8)


def _hy_matrices(seq):
    d = _hy_dims(seq)
    n, n1, n2, h = d["n"], d["n1"], d["n2"], d["h"]

    def cis(num, den, sign):
        ang = (2.0 * math.pi / den) * (num % den).astype(F32)
        return jnp.cos(ang), sign * jnp.sin(ang)

    k1 = jnp.arange(n1, dtype=jnp.int32)
    nn = n2 * jnp.arange(n1, dtype=jnp.int32)[None, None, :] + jnp.arange(n2, dtype=jnp.int32)[:, None, None]
    e1r, e1i = cis(k1[None, :, None] * nn, n, -1.0)
    m1f = jnp.concatenate([e1r, e1i], axis=1)
    m1 = jnp.concatenate([jnp.concatenate([e1r[..., :h], -e1i[..., :h]], axis=2),
                          jnp.concatenate([e1i[..., :h], e1r[..., :h]], axis=2)], axis=1)
    a2 = jnp.arange(n2, dtype=jnp.int32)
    g2r, g2i = cis(a2[:, None] * a2[None, :], n2, -1.0)
    m2 = jnp.concatenate([jnp.concatenate([g2r, -g2i], axis=1),
                          jnp.concatenate([g2i, g2r], axis=1)], axis=0)
    num3 = (a2[None, :, None] * a2[None, None, :] * n1 + k1[:, None, None] * a2[None, :, None])
    e3r, e3i = cis(num3, n, 1.0)
    m3 = jnp.concatenate([jnp.concatenate([e3r, -e3i], axis=2),
                          jnp.concatenate([e3i, e3r], axis=2)], axis=1)
    hh = jnp.arange(h, dtype=jnp.int32)
    d4r, d4i = cis(hh[:, None] * k1[None, :], n1, 1.0)
    m4 = jnp.concatenate([jnp.concatenate([d4r, -d4i], axis=1),
                          jnp.concatenate([d4i, d4r], axis=1)], axis=0) / n
    return tuple(m.astype(BF16) for m in (m1f, m1, m2, m3, m4))


def _hy_prep_kernel(p_ref, w_ref, b_ref, o_ref, *, dims):
    n2, h, pa = dims["n2"], dims["h"], dims["pa"]
    seq = h * n2
    for j in range(n2):
        o_ref[j * pa + h:(j + 1) * pa, :] = jnp.zeros((pa - h, o_ref.shape[-1]), o_ref.dtype)
    w0, w1, w2, bias = w_ref[0:1, :], w_ref[1:2, :], w_ref[2:3, :], b_ref[...]
    ri = lax.broadcasted_iota(jnp.int32, (n2, o_ref.shape[-1]), 0)

    def body(i, carry):
        r0 = pl.multiple_of(i * n2, n2)
        x = p_ref[pl.ds(r0, n2), :].astype(F32)
        pstart = pl.multiple_of(jnp.maximum(r0 - HALO, 0), HALO)
        nstart = pl.multiple_of(jnp.minimum(r0 + n2, seq - HALO), HALO)
        prev_row = p_ref[pl.ds(pstart, HALO), :].astype(F32)[HALO - 1:HALO, :] * jnp.where(i > 0, 1.0, 0.0).astype(F32)
        next_row = p_ref[pl.ds(nstart, HALO), :].astype(F32)[0:1, :] * jnp.where(i < h - 1, 1.0, 0.0).astype(F32)
        up = jnp.where(ri == 0, prev_row, pltpu.roll(x, 1, 0))
        dn = jnp.where(ri == n2 - 1, next_row, pltpu.roll(x, n2 - 1, 0))
        o_ref[pl.ds(i, n2, stride=pa), :] = up * w0 + x * w1 + dn * w2 + bias
        return carry

    lax.fori_loop(0, h, body, 0)


def _hy_prep(p, short_w, short_b):
    bsz, t, _ = p.shape
    dims = _hy_dims(t)
    rows = dims["n2"] * dims["pa"]
    cb = _hy_cb(t)
    ncb = 3 * GROUP_W // cb
    return pl.pallas_call(
        functools.partial(_hy_prep_kernel, dims=dims),
        grid=(bsz, ncb),
        in_specs=[pl.BlockSpec((None, t, cb), lambda b, j: (b, 0, COL_HY // cb + j)),
                  pl.BlockSpec((3, cb), lambda b, j: (0, j)),
                  pl.BlockSpec((1, cb), lambda b, j: (0, j))],
        out_specs=pl.BlockSpec((None, rows, cb), lambda b, j: (b, 0, j)),
        out_shape=jax.ShapeDtypeStruct((bsz, rows, 3 * GROUP_W), F32),
        compiler_params=_cparams(("parallel", "parallel")),
        name="hy_prep",
    )(p, short_w, short_b.reshape(1, -1))


def _hy_filter_kernel(z_ref, w1_ref, b1_ref, w2_ref, b2_ref, w3_ref, b3_ref, w4_ref, fr_ref, dl_ref, o_ref):
    hi = lax.Precision.HIGHEST
    z = z_ref[...]
    fr = fr_ref[...]
    h = jnp.sin(fr * (jnp.dot(z, w1_ref[...], preferred_element_type=F32, precision=hi) + b1_ref[...]))
    h = jnp.sin(fr * (jnp.dot(h, w2_ref[...], preferred_element_type=F32, precision=hi) + b2_ref[...]))
    h = jnp.sin(fr * (jnp.dot(h, w3_ref[...], preferred_element_type=F32, precision=hi) + b3_ref[...]))
    full = jnp.dot(h.astype(BF16), w4_ref[...].astype(BF16), preferred_element_type=F32)
    t = z[:, 0:1]
    is_bwd = z[:, HY_EMB:HY_EMB + 1] > 0.5
    live = z[:, HY_EMB + 1:HY_EMB + 2]
    decay = jnp.exp(-t * jnp.abs(dl_ref[...])) * live
    for o in range(2):
        fwd = full[:, o * 2 * GROUP_W:o * 2 * GROUP_W + GROUP_W]
        bwd = full[:, o * 2 * GROUP_W + GROUP_W:(o + 1) * 2 * GROUP_W]
        o_ref[o] = jnp.where(is_bwd, bwd, fwd) * decay


def _hy_filter_features(seq):
    d = _hy_dims(seq)
    n, n1, n2 = d["n"], d["n1"], d["n2"]
    row = np.arange(n)
    time = n2 * (row % n1) + row // n1
    is_bwd = time > seq
    pos = np.where(is_bwd, n - time, time)
    live = (time != seq).astype(np.float64)
    pos = np.where(time == seq, 0, pos)
    t = np.linspace(0.0, 1.0, seq)[pos]
    bands = (HY_EMB - 1) // 2
    f = np.linspace(1e-4, bands - 1, bands)[None]
    wpos = (2.0 * math.pi * pos / seq)[:, None]
    feat = np.zeros((n, 128), np.float64)
    feat[:, 0] = t
    feat[:, 1:1 + bands] = np.cos(f * wpos)
    feat[:, 1 + bands:HY_EMB] = -np.sin(f * wpos)
    feat[:, HY_EMB] = is_bwd
    feat[:, HY_EMB + 1] = live
    return jnp.asarray(feat, F32)


def _hy_filter(seq, w1, b1, w2, b2, w3, b3, w4, freq, tr=512):
    d = _hy_dims(seq)
    n = d["n"]
    tr = min(tr, n)
    feat = _hy_filter_features(seq)
    w1p = jnp.pad(w1, ((0, 128 - HY_EMB), (0, 0)))
    max_decay = math.log(HY_DECAY_TARGET) / HY_FAST_PCT
    min_decay = math.log(HY_DECAY_TARGET) / HY_SLOW_PCT
    deltas = jnp.linspace(min_decay, max_decay, GROUP_W, dtype=F32).reshape(1, -1)
    row = lambda v: v.reshape(1, -1)
    const = lambda a: pl.BlockSpec(a.shape, lambda i: (0,) * a.ndim)
    args = [w1p, row(b1), w2, row(b2), w3, row(b3), w4, row(freq), deltas]
    return pl.pallas_call(
        _hy_filter_kernel,
        grid=(n // tr,),
        in_specs=[pl.BlockSpec((tr, 128), lambda i: (i, 0))] + [const(a) for a in args],
        out_specs=pl.BlockSpec((2, tr, GROUP_W), lambda i: (0, i, 0)),
        out_shape=jax.ShapeDtypeStruct((2, n, GROUP_W), F32),
        compiler_params=_cparams(("parallel",)),
        name="hy_filter",
    )(feat, *args)


def _hy_spectrum_kernel(k_ref, m1f_ref, m2_ref, re_ref, im_ref, tr_ref, ti_ref, *, dims):
    n1, n2, pb = dims["n1"], dims["n2"], dims["pb"]

    def stage1(j, carry):
        r0 = pl.multiple_of(j * n1, n1)
        a = jnp.dot(m1f_ref[j], k_ref[pl.ds(r0, n1), :].astype(BF16), preferred_element_type=F32)
        tr_ref[pl.ds(j, n1, stride=pb), :] = a[:n1]
        ti_ref[pl.ds(j, n1, stride=pb), :] = a[n1:]
        return carry

    lax.fori_loop(0, n2, stage1, 0, unroll=HY_UNROLL)

    def stage2(k, carry):
        r0 = pl.multiple_of(k * pb, 8)
        rhs = jnp.concatenate([tr_ref[pl.ds(r0, n2), :], ti_ref[pl.ds(r0, n2), :]], axis=0).astype(BF16)
        x = jnp.dot(m2_ref[...], rhs, preferred_element_type=F32)
        o0 = pl.multiple_of(k * n2, n2)
        re_ref[pl.ds(o0, n2), :] = x[:n2]
        im_ref[pl.ds(o0, n2), :] = x[n2:]
        return carry

    lax.fori_loop(0, n1, stage2, 0, unroll=HY_UNROLL)


def _hy_spectrum(kern, mats, seq):
    d = _hy_dims(seq)
    n, n1, pb = d["n"], d["n1"], d["pb"]
    m1f, _, m2, _, _ = mats
    cb = _hy_cb(seq)
    ncb = GROUP_W // cb
    blk = pl.BlockSpec((None, n, cb), lambda o, j: (o, 0, j))
    const = lambda a: pl.BlockSpec(a.shape, lambda o, j: (0,) * a.ndim)
    return pl.pallas_call(
        functools.partial(_hy_spectrum_kernel, dims=d),
        grid=(2, ncb),
        in_specs=[blk, const(m1f), const(m2)],
        out_specs=[blk, blk],
        out_shape=[jax.ShapeDtypeStruct((2, n, GROUP_W), F32)] * 2,
        scratch_shapes=[pltpu.VMEM((n1 * pb, cb), F32)] * 2,
        compiler_params=_cparams(("parallel", "parallel")),
        name="hy_spectrum",
    )(kern, m1f, m2)


def _pack_pair(re, im):
    half = jnp.uint32(0x8000)
    r = lax.bitcast_convert_type(re, jnp.uint32) + half
    i = lax.bitcast_convert_type(im, jnp.uint32) + half
    return (r & jnp.uint32(0xFFFF0000)) | (i >> 16)


def _unpack_pair(w):
    re = lax.bitcast_convert_type(w & jnp.uint32(0xFFFF0000), F32)
    im = lax.bitcast_convert_type(w << 16, F32)
    return re, im


def _hy_conv_kernel(u_ref, g_ref, kr_ref, ki_ref, skip_ref, m1_ref, m2_ref, m3_ref, m4_ref, o_ref,
                    t1, t2, *, dims, natural_out):
    n1, n2, h, pa, pb, pc = (dims[k] for k in ("n1", "n2", "h", "pa", "pb", "pc"))

    def fwd1(j, carry):
        r0 = pl.multiple_of(j * pa, 8)
        rhs = jnp.concatenate([u_ref[0, pl.ds(r0, h), :], u_ref[1, pl.ds(r0, h), :]], axis=0).astype(BF16)
        a = jnp.dot(m1_ref[j], rhs, preferred_element_type=F32)
        t1[pl.ds(j, n1, stride=pb), :] = _pack_pair(a[:n1], a[n1:])
        return carry

    lax.fori_loop(0, n2, fwd1, 0, unroll=HY_UNROLL)

    def mid(k, carry):
        r0 = pl.multiple_of(k * pb, 8)
        rhs = jnp.concatenate(_unpack_pair(t1[pl.ds(r0, n2), :]), axis=0).astype(BF16)
        x = jnp.dot(m2_ref[...], rhs, preferred_element_type=F32)
        f0 = pl.multiple_of(k * n2, n2)
        fr, fi = kr_ref[pl.ds(f0, n2), :], ki_ref[pl.ds(f0, n2), :]
        xr, xi = x[:n2], x[n2:]
        y = jnp.concatenate([xr * fr - xi * fi, xr * fi + xi * fr], axis=0).astype(BF16)
        c = jnp.dot(m3_ref[k], y, preferred_element_type=F32)
        t2[pl.ds(k, n2, stride=pc), :] = _pack_pair(c[:n2], c[n2:])
        return carry

    lax.fori_loop(0, n1, mid, 0, unroll=HY_UNROLL)

    if not natural_out:
        o_ref[...] = jnp.zeros_like(o_ref)
    skip = skip_ref[...]

    def inv2(j, carry):
        r0 = pl.multiple_of(j * pc, 8)
        rhs = jnp.concatenate(_unpack_pair(t2[pl.ds(r0, n1), :]), axis=0).astype(BF16)
        y = jnp.dot(m4_ref[...], rhs, preferred_element_type=F32)
        a0 = pl.multiple_of(j * pa, 8)
        for e in range(2):
            val = g_ref[e, pl.ds(a0, h), :] * (y[e * h:(e + 1) * h] + skip * u_ref[e, pl.ds(a0, h), :])
            if natural_out:
                o_ref[e, pl.ds(j, h, stride=n2), :] = val.astype(o_ref.dtype)
            else:
                o_ref[e, pl.ds(a0, h), :] = val
        return carry

    lax.fori_loop(0, n2, inv2, 0, unroll=HY_UNROLL)


def _hy_conv(u_arr, u_col, g_arr, g_col, kf_re, kf_im, order, skip, mats, seq, natural_out):
    d = _hy_dims(seq)
    bsz = u_arr.shape[0]
    rows = d["n2"] * d["pa"]
    _, m1, m2, m3, m4 = mats
    cb = _hy_cb(seq)
    ncb = GROUP_W // cb
    single = pl.Buffered(1)
    const = lambda a: pl.BlockSpec(a.shape, lambda j, p: (0,) * a.ndim, pipeline_mode=single)
    in_specs = [
        pl.BlockSpec((2, rows, cb), lambda j, p: (p, 0, u_col // cb + j)),
        pl.BlockSpec((2, rows, cb), lambda j, p: (p, 0, g_col // cb + j)),
        pl.BlockSpec((None, d["n"], cb), lambda j, p: (order, 0, j), pipeline_mode=single),
        pl.BlockSpec((None, d["n"], cb), lambda j, p: (order, 0, j), pipeline_mode=single),
        pl.BlockSpec((None, 1, cb), lambda j, p: (order, 0, j)),
        const(m1), const(m2), const(m3), const(m4),
    ]
    if natural_out:
        out_spec = pl.BlockSpec((2, seq, cb), lambda j, p: (p, 0, j))
        out_shape = jax.ShapeDtypeStruct((bsz, seq, GROUP_W), F32)
    else:
        out_spec = pl.BlockSpec((2, rows, cb), lambda j, p: (p, 0, j))
        out_shape = jax.ShapeDtypeStruct((bsz, rows, GROUP_W), F32)
    return pl.pallas_call(
        functools.partial(_hy_conv_kernel, dims=d, natural_out=natural_out),
        grid=(ncb, bsz // 2),
        in_specs=in_specs,
        out_specs=out_spec,
        out_shape=out_shape,
        scratch_shapes=[pltpu.VMEM((d["n1"] * d["pb"], cb), jnp.uint32),
                        pltpu.VMEM((d["n2"] * d["pc"], cb), jnp.uint32)],
        compiler_params=_cparams(("parallel", "parallel")),
        name="hy_conv",
    )(u_arr, g_arr, kf_re, kf_im, skip.reshape(2, 1, GROUP_W), m1, m2, m3, m4)


def _hyena_mixer(p, short_w, short_b, filt_params, skip, mats):
    seq = p.shape[1]
    kern = _hy_filter(seq, *filt_params)
    kf_re, kf_im = _hy_spectrum(kern, mats, seq)
    ut = _hy_prep(p, short_w, short_b)
    zt = _hy_conv(ut, 2 * GROUP_W, ut, 0, kf_re, kf_im, 0, skip, mats, seq, natural_out=False)
    return _hy_conv(zt, 0, ut, GROUP_W, kf_re, kf_im, 1, skip, mats, seq, natural_out=True)


def _prep_weights(w_in, w_out):
    depth, d, _ = w_in.shape
    perm = np.asarray(HEAD_ORDER)
    q = w_in[..., :512].reshape(depth, d, N_HEADS, HEAD_DIM)[:, :, perm].reshape(depth, d, 512)
    o_a, o_ssm = 768, 768 + 3072 + 1536 + 16
    head = jnp.concatenate([q, w_in[..., 512:o_a], w_in[..., o_ssm - 16:o_ssm],
                            jnp.zeros((depth, d, COL_HY - COL_DT - 16), w_in.dtype)], axis=-1).astype(BF16)
    tail = jnp.pad(w_in[..., o_a:o_ssm - 16].astype(BF16), ((0, 0), (0, 0), (COL_HY, 0)))
    w_in_p = lax.dynamic_update_slice(tail, head, (0, 0, 0))
    assert w_in_p.shape[-1] == N_PROJ and head.shape[-1] == COL_HY
    dm = w_out.shape[-1]
    wo_a = w_out[:, :512].reshape(depth, N_HEADS, HEAD_DIM, dm)[:, perm].reshape(depth, 512, dm)
    w_out_p = jnp.concatenate([wo_a, w_out[:, 512:]], axis=1).astype(BF16)
    return w_in_p, w_out_p


TILE_W_IN = (512, 2816)
TILE_MLP_UP = (1024, 2048)
TILE_DOWN = (1024, 1024, 2048)
TILE_OUT = (512, 2048)


def kernel(x, c, ctx, c_ctx, ada_w, ada_b, norm_mix, norm_mlp, w_in, w_out, attn_sink,
           hy_short_w, hy_short_b, hy_w1, hy_b1, hy_w2, hy_b2, hy_w3, hy_b3, hy_w4, hy_freq, hy_skip,
           na_rpb, ssm_conv_w, ssm_conv_b, ssm_dt_bias, ssm_a_log, ssm_d, ssm_norm,
           mlp_w1, mlp_w2, final_norm):
    bsz, seq, d = x.shape
    lc = ctx.shape[1]
    depth = ada_w.shape[0]
    assert bsz % 2 == 0 and bsz <= 7 and d == D_MODEL

    cs = jnp.zeros((8, d), F32).at[:bsz].set(c).at[bsz].set(c_ctx)
    mod = _ada_mod(cs, ada_w, ada_b)
    cos_t, sin_t = _rope_tables(seq)
    na_plan = _na_plan(seq)
    bias_tabs = _na_bias_tables(na_rpb, na_plan[-1])
    mats_l = _hy_matrices(seq)
    mats_c = _hy_matrices(lc)
    lat_row = lambda b: b
    ctx_row = lambda b: bsz
    tm_c = lc

    w_in_p, w_out_p = _prep_weights(w_in, w_out)
    w1_b = mlp_w1.astype(BF16)
    w2_b = mlp_w2.astype(BF16)

    xc = ctx
    for i in range(depth):
        last = i == depth - 1
        mod3 = mod[i].reshape(8, 1, 6 * d)
        p = _norm_matmul(x, norm_mix[i], mod3, lat_row, 0, 1, w_in_p, i, P_DTYPE, False, *TILE_W_IN)
        pc = _norm_matmul(xc, norm_mix[i], mod3, ctx_row, 0, 1, w_in_p, i, P_DTYPE, False, tm_c, TILE_W_IN[1])

        filt_params = (hy_w1[i], hy_b1[i], hy_w2[i], hy_b2[i], hy_w3[i], hy_b3[i], hy_w4[i], hy_freq[i])
        ssd_consts = _ssd_consts(ssm_conv_w[i], ssm_conv_b[i], ssm_dt_bias[i], ssm_a_log[i], ssm_d[i], ssm_norm[i])

        ya = _win_attn(attn_sink[i], p, pc, cos_t, sin_t, local=True)
        yb = _hyena_mixer(p, hy_short_w[i], hy_short_b[i], filt_params, hy_skip[i], mats_l)
        yn = _na_attn(p, pc, bias_tabs, i, na_plan, local=True)
        yd, ydc = _ssd_mixer(p, pc, ssd_consts)
        x = _out_proj((ya, yb, yn, yd), w_out_p, i, x, mod3, lat_row, 2, *TILE_OUT)
        hid = _norm_matmul(x, norm_mlp[i], mod3, lat_row, 3, 4, w1_b, i, BF16, True, *TILE_MLP_UP)
        x = _matmul_residual(hid, w2_b, i, x, mod3, lat_row, 5, *TILE_DOWN)

        if not last:
            yac = _win_attn(attn_sink[i], pc, pc, None, None, local=False)
            ybc = _hyena_mixer(pc, hy_short_w[i], hy_short_b[i], filt_params, hy_skip[i], mats_c)
            ync = _na_attn(pc, pc, None, i, None, local=False)
            xc = _out_proj((yac, ybc, ync, ydc), w_out_p, i, xc, mod3, ctx_row, 2, tm_c, TILE_OUT[1])
            hidc = _norm_matmul(xc, norm_mlp[i], mod3, ctx_row, 3, 4, w1_b, i, BF16, True, tm_c, TILE_MLP_UP[1])
            xc = _matmul_residual(hidc, w2_b, i, xc, mod3, ctx_row, 5, tm_c, *TILE_DOWN[1:])
    return _final_norm(x, final_norm)
```

```python
import functools
import math

import numpy as np
import jax
import jax.numpy as jnp
from jax import lax
from jax.experimental import pallas as pl
from jax.experimental.pallas import tpu as pltpu

F32 = jnp.float32
BF16 = jnp.bfloat16

D_MODEL = 2048
GRID_W = 64
EPS = 1e-6
NEG = -1e30
HEAD_DIM = 64
GROUP_W = D_MODEL // 4
N_HEADS = GROUP_W // HEAD_DIM
WINDOW = 128
BLOCK = 128
ROPE_BASE = 10000.0
HY_EMB = 33
HY_FFN = 64
HY_DECAY_TARGET = 1e-2
HY_FAST_PCT = 0.3
HY_SLOW_PCT = 1.5
NA_KR = 8
NA_KC = 16
SSM_STATE = 128
SSM_CHUNK = 128
D_FF = 4 * D_MODEL
SCALE = HEAD_DIM ** -0.5
LOG2E = math.log2(math.e)

COL_QA, COL_KA, COL_VA, COL_DT = 0, 512, 640, 768
COL_HY = 1024
COL_NQ, COL_NK, COL_NV = 2560, 3072, 3584
COL_SZ, COL_SX, COL_SBC = 4096, 4608, 5120
N_PROJ = 5632
HEAD_ORDER = (0, 4, 1, 5, 2, 6, 3, 7)

P_DTYPE = BF16
HALO = 16

V7X_VMEM_BYTES = 64 * 1024 * 1024
VMEM_LIMIT = 56 * 1024 * 1024


def _cparams(sem):
    return pltpu.CompilerParams(dimension_semantics=sem, vmem_limit_bytes=VMEM_LIMIT)


def _silu(x):
    return x * jax.nn.sigmoid(x)


def _ada_kernel(cs_ref, w_ref, b_ref, o_ref):
    a = _silu(cs_ref[...]).astype(BF16)
    o_ref[...] = jnp.dot(a, w_ref[...].astype(BF16), preferred_element_type=F32) + b_ref[...]


def _ada_mod(cs, ada_w, ada_b, tn=1024):
    depth, d, n = ada_w.shape
    return pl.pallas_call(
        _ada_kernel,
        grid=(depth, n // tn),
        in_specs=[
            pl.BlockSpec((8, d), lambda i, j: (0, 0)),
            pl.BlockSpec((None, d, tn), lambda i, j: (i, 0, j)),
            pl.BlockSpec((None, 1, tn), lambda i, j: (i, 0, j)),
        ],
        out_specs=pl.BlockSpec((None, 8, tn), lambda i, j: (i, 0, j)),
        out_shape=jax.ShapeDtypeStruct((depth, 8, n), F32),
        compiler_params=_cparams(("parallel", "parallel")),
        name="ada_mod",
    )(cs, ada_w, ada_b.reshape(depth, 1, n))


NORM_ROWS = 16


def _norm_matmul_kernel(x_ref, g_ref, sh_ref, sc_ref, w_ref, o_ref, h_ref, *, act):
    @pl.when(pl.program_id(2) == 0)
    def _():
        gain = g_ref[...] * (1.0 + sc_ref[...])
        shift = sh_ref[...]

        def rows(r, carry):
            r0 = pl.multiple_of(r * NORM_ROWS, NORM_ROWS)
            xf = x_ref[pl.ds(r0, NORM_ROWS), :]
            ms = jnp.mean(xf * xf, axis=-1, keepdims=True)
            h_ref[pl.ds(r0, NORM_ROWS), :] = (xf * lax.rsqrt(ms + EPS) * gain + shift).astype(BF16)
            return carry

        lax.fori_loop(0, x_ref.shape[0] // NORM_ROWS, rows, 0, unroll=8)

    r = jnp.dot(h_ref[...], w_ref[...], preferred_element_type=F32)
    if act:
        r = jnp.square(jnp.maximum(r, 0.0))
    o_ref[...] = r.astype(o_ref.dtype)


def _norm_matmul(x, g, mod3, row_of_b, sh_idx, sc_idx, w, layer, out_dtype, act, tm, tn):
    bsz, t, d = x.shape
    n = w.shape[2]
    tm = min(tm, t)
    return pl.pallas_call(
        functools.partial(_norm_matmul_kernel, act=act),
        grid=(bsz, t // tm, n // tn),
        in_specs=[
            pl.BlockSpec((None, tm, d), lambda b, m, j: (b, m, 0)),
            pl.BlockSpec((1, d), lambda b, m, j: (0, 0)),
            pl.BlockSpec((None, 1, d), lambda b, m, j: (row_of_b(b), 0, sh_idx)),
            pl.BlockSpec((None, 1, d), lambda b, m, j: (row_of_b(b), 0, sc_idx)),
            pl.BlockSpec((None, d, tn), lambda b, m, j: (layer, 0, j)),
        ],
        out_specs=pl.BlockSpec((None, tm, tn), lambda b, m, j: (b, m, j)),
        out_shape=jax.ShapeDtypeStruct((bsz, t, n), out_dtype),
        scratch_shapes=[pltpu.VMEM((tm, d), BF16)],
        compiler_params=_cparams(("parallel", "parallel", "arbitrary")),
        name="norm_matmul",
    )(x, g.reshape(1, d), mod3, mod3, w)


def _mm_res_kernel(a_ref, w_ref, x_ref, gate_ref, o_ref, acc_ref, *, nk):
    k = pl.program_id(3)

    @pl.when(k == 0)
    def _():
        acc_ref[...] = jnp.zeros_like(acc_ref)

    acc_ref[...] += jnp.dot(a_ref[...], w_ref[...], preferred_element_type=F32)

    @pl.when(k == nk - 1)
    def _():
        o_ref[...] = x_ref[...] + gate_ref[...] * acc_ref[...]


def _matmul_residual(a, w, layer, x, mod3, row_of_b, gate_idx, tm, tn, tk):
    bsz, t, kdim = a.shape
    n = w.shape[2]
    tm = min(tm, t)
    nk = kdim // tk
    return pl.pallas_call(
        functools.partial(_mm_res_kernel, nk=nk),
        grid=(bsz, t // tm, n // tn, nk),
        in_specs=[
            pl.BlockSpec((None, tm, tk), lambda b, m, j, k: (b, m, k)),
            pl.BlockSpec((None, tk, tn), lambda b, m, j, k: (layer, k, j)),
            pl.BlockSpec((None, tm, tn), lambda b, m, j, k: (b, m, j)),
            pl.BlockSpec((None, 1, tn), lambda b, m, j, k: (row_of_b(b), 0, gate_idx * (D_MODEL // tn) + j)),
        ],
        out_specs=pl.BlockSpec((None, tm, tn), lambda b, m, j, k: (b, m, j)),
        out_shape=jax.ShapeDtypeStruct(x.shape, F32),
        scratch_shapes=[pltpu.VMEM((tm, tn), F32)],
        compiler_params=_cparams(("parallel", "parallel", "parallel", "arbitrary")),
        name="matmul_residual",
    )(a, w, x, mod3)


def _out_proj_kernel(ya_ref, yb_ref, yn_ref, yd_ref, w_ref, x_ref, gate_ref, o_ref):
    acc = None
    for g, y_ref in enumerate((ya_ref, yb_ref, yn_ref, yd_ref)):
        part = jnp.dot(y_ref[...].astype(BF16), w_ref[GROUP_W * g:GROUP_W * (g + 1), :],
                       preferred_element_type=F32)
        acc = part if acc is None else acc + part
    o_ref[...] = x_ref[...] + gate_ref[...] * acc


def _out_proj(ys, w, layer, x, mod3, row_of_b, gate_idx, tm, tn):
    bsz, t, d = x.shape
    tm = min(tm, t)
    y_spec = pl.BlockSpec((None, tm, GROUP_W), lambda b, m, j: (b, m, 0))
    return pl.pallas_call(
        _out_proj_kernel,
        grid=(bsz, t // tm, d // tn),
        in_specs=[y_spec, y_spec, y_spec, y_spec,
                  pl.BlockSpec((None, 4 * GROUP_W, tn), lambda b, m, j: (layer, 0, j),
                               pipeline_mode=pl.Buffered(1) if tn == d else None),
                  pl.BlockSpec((None, tm, tn), lambda b, m, j: (b, m, j)),
                  pl.BlockSpec((None, 1, tn), lambda b, m, j: (row_of_b(b), 0, gate_idx * (d // tn) + j))],
        out_specs=pl.BlockSpec((None, tm, tn), lambda b, m, j: (b, m, j)),
        out_shape=jax.ShapeDtypeStruct(x.shape, F32),
        compiler_params=_cparams(("parallel", "parallel", "parallel")),
        name="out_proj",
    )(*ys, w, x, mod3)


def _final_norm_kernel(x_ref, g_ref, o_ref):
    xf = x_ref[...]
    ms = jnp.mean(xf * xf, axis=-1, keepdims=True)
    o_ref[...] = xf * lax.rsqrt(ms + EPS) * g_ref[...]


def _final_norm(x, g, tm=1024):
    bsz, t, d = x.shape
    tm = min(tm, t)
    return pl.pallas_call(
        _final_norm_kernel,
        grid=(bsz, t // tm),
        in_specs=[pl.BlockSpec((None, tm, d), lambda b, m: (b, m, 0)),
                  pl.BlockSpec((1, d), lambda b, m: (0, 0))],
        out_specs=pl.BlockSpec((None, tm, d), lambda b, m: (b, m, 0)),
        out_shape=jax.ShapeDtypeStruct(x.shape, F32),
        compiler_params=_cparams(("parallel", "parallel")),
        name="final_norm",
    )(x, g.reshape(1, d))


WIN_QBLOCKS = 4


def _rope(x, cos, sin_signed, lane_lo):
    w = x.shape[-1]
    partner = jnp.where(lane_lo, pltpu.roll(x, w - 16, 1), pltpu.roll(x, 16, 1))
    return x * cos + partner * sin_signed


def _win_attn_kernel(sink_ref, q_ref, kc_ref, vc_ref, *rest, seq, local):
    if local:
        k_ref, v_ref, cos_ref, sin_ref, mask_ref, o_ref = rest
    else:
        (o_ref,) = rest
    nb = seq // BLOCK
    lane = lax.broadcasted_iota(jnp.int32, (BLOCK, 128), 1)
    lo = lane < HEAD_DIM
    lane_lo = (lane % 32) < 16
    lane3_lo = (lax.broadcasted_iota(jnp.int32, (3 * BLOCK, 128), 1) % 32) < 16
    nt = (((1,), (1,)), ((), ()))
    kc = kc_ref[...].astype(BF16)
    vc = vc_ref[...].astype(BF16)
    sink = jnp.concatenate(
        [jnp.full((BLOCK, 1), sink_ref[HEAD_ORDER[i]] * LOG2E, F32) for i in range(N_HEADS)], axis=0)
    nq = q_ref.shape[0] // BLOCK
    for sub in range(nq):
        n = pl.program_id(1) * nq + sub
        q = q_ref[sub * BLOCK:(sub + 1) * BLOCK, :].astype(F32) * (SCALE * LOG2E)
        if local:
            r0 = pl.multiple_of(n * BLOCK, BLOCK)
            cos_q = cos_ref[pl.ds(r0, BLOCK), :]
            sin_q = sin_ref[pl.ds(r0, BLOCK), :]
        rows = []
        for m in range(4):
            qm = q[:, 128 * m:128 * (m + 1)]
            if local:
                qm = _rope(qm, cos_q, sin_q, lane_lo)
            rows.append(jnp.where(lo, qm, 0.0))
            rows.append(jnp.where(lo, 0.0, qm))
        qbd = jnp.concatenate(rows, axis=0).astype(BF16)
        s_ctx = lax.dot_general(qbd, kc, nt, preferred_element_type=F32)
        mx = jnp.maximum(jnp.max(s_ctx, axis=-1, keepdims=True), sink)
        if local:
            start = pl.multiple_of(jnp.clip((n - 1) * BLOCK, 0, seq - 3 * BLOCK), BLOCK)
            kb = _rope(k_ref[pl.ds(start, 3 * BLOCK), :].astype(F32), cos_ref[pl.ds(start, 3 * BLOCK), :],
                       sin_ref[pl.ds(start, 3 * BLOCK), :], lane3_lo).astype(BF16)
            vb = v_ref[pl.ds(start, 3 * BLOCK), :].astype(BF16)
            mask = mask_ref[jnp.where(n == 0, 0, jnp.where(n == nb - 1, 2, 1))]
            s_loc = lax.dot_general(qbd, kb, nt, preferred_element_type=F32)
            s_loc = (s_loc.reshape(N_HEADS, BLOCK, 3 * BLOCK) + mask[None]).reshape(s_loc.shape)
            mx = jnp.maximum(mx, jnp.max(s_loc, axis=-1, keepdims=True))
        p_ctx = jnp.exp2(s_ctx - mx)
        den = jnp.sum(p_ctx, axis=-1, keepdims=True) + jnp.exp2(sink - mx)
        acc = jnp.dot(p_ctx.astype(BF16), vc, preferred_element_type=F32)
        if local:
            p_loc = jnp.exp2(s_loc - mx)
            den = den + jnp.sum(p_loc, axis=-1, keepdims=True)
            acc = acc + jnp.dot(p_loc.astype(BF16), vb, preferred_element_type=F32)
        o = acc / den
        outs = [jnp.where(lo, o[(2 * m) * BLOCK:(2 * m + 1) * BLOCK], o[(2 * m + 1) * BLOCK:(2 * m + 2) * BLOCK])
                for m in range(4)]
        o_ref[sub * BLOCK:(sub + 1) * BLOCK, :] = jnp.concatenate(outs, axis=1).astype(o_ref.dtype)


def _win_attn(sink, pq, pc, cos_t, sin_t, local):
    bsz, t, _ = pq.shape
    lc = pc.shape[1]
    tq = min(WIN_QBLOCKS * BLOCK, t)
    in_specs = [
        pl.BlockSpec(memory_space=pltpu.SMEM),
        pl.BlockSpec((None, tq, 512), lambda b, n: (b, n, COL_QA // 512)),
        pl.BlockSpec((None, lc, 128), lambda b, n: (b, 0, COL_KA // 128)),
        pl.BlockSpec((None, lc, 128), lambda b, n: (b, 0, COL_VA // 128)),
    ]
    args = [sink, pq, pc, pc]
    if local:
        in_specs += [
            pl.BlockSpec((None, t, 128), lambda b, n: (b, 0, COL_KA // 128)),
            pl.BlockSpec((None, t, 128), lambda b, n: (b, 0, COL_VA // 128)),
            pl.BlockSpec((t, 128), lambda b, n: (0, 0)),
            pl.BlockSpec((t, 128), lambda b, n: (0, 0)),
            pl.BlockSpec((3, BLOCK, 3 * BLOCK), lambda b, n: (0, 0, 0)),
        ]
        args += [pq, pq, cos_t, sin_t, _window_masks(t)]
    return pl.pallas_call(
        functools.partial(_win_attn_kernel, seq=t, local=local),
        grid=(bsz, t // tq),
        in_specs=in_specs,
        out_specs=pl.BlockSpec((None, tq, GROUP_W), lambda b, n: (b, n, 0)),
        out_shape=jax.ShapeDtypeStruct((bsz, t, GROUP_W), BF16),
        compiler_params=_cparams(("parallel", "arbitrary")),
        name="win_attn" if local else "ctx_attn_a",
    )(*args)


def _window_masks(seq):
    nb = seq // BLOCK
    qi = np.arange(BLOCK)[:, None]
    kj = np.arange(3 * BLOCK)[None, :]
    tabs = []
    for n in (0, 1, nb - 1):
        start = int(np.clip((n - 1) * BLOCK, 0, seq - 3 * BLOCK))
        rel = (start + kj) - (n * BLOCK + qi)
        tabs.append(np.where(np.abs(rel) <= WINDOW, 0.0, NEG))
    return jnp.asarray(np.stack(tabs), F32)


def _rope_tables(seq):
    t = np.arange(seq)
    row, col = t // GRID_W, t % GRID_W
    quarter = HEAD_DIM // 4
    inv = ROPE_BASE ** (-np.arange(quarter, dtype=np.float64) / quarter)
    inv = inv.astype(np.float32).astype(np.float64)
    lane = np.arange(128)
    j = lane % HEAD_DIM
    pos = np.where((j < HEAD_DIM // 2)[None, :], row[:, None], col[:, None]).astype(np.float64)
    ang = (pos * inv[j % quarter][None, :]).astype(np.float32)
    cos = np.cos(ang.astype(np.float64))
    sin = np.sin(ang.astype(np.float64))
    sign = np.where((j % 32) < 16, -1.0, 1.0)[None, :]
    return jnp.asarray(cos, F32), jnp.asarray(sin * sign, F32)


NA_ROWS_PER_STEP = 4


def _na_kernel(var_ref, ws_ref, q_ref, kc_ref, vc_ref, *rest, local, win_rows):
    del var_ref
    if local:
        k_ref, v_ref, bias_ref, o_ref = rest
    else:
        (o_ref,) = rest
    g = pl.program_id(1)
    tq = q_ref.shape[0]
    q = q_ref[...].astype(F32) * (SCALE * LOG2E)
    head = lax.broadcasted_iota(jnp.int32, (tq, 256), 1) // HEAD_DIM
    nt = (((1,), (1,)), ((), ()))
    outs = []
    for half in range(2):
        cols = slice(256 * half, 256 * (half + 1))
        q4 = q[:, cols]
        qbd = jnp.concatenate([jnp.where(head == h, q4, 0.0) for h in range(4)], axis=0).astype(BF16)
        kc4 = kc_ref[:, cols].astype(BF16)
        vc4 = vc_ref[:, cols].astype(BF16)
        s_ctx = lax.dot_general(qbd, kc4, nt, preferred_element_type=F32)
        mx = jnp.max(s_ctx, axis=-1, keepdims=True)
        if local:
            nkey = win_rows * GRID_W
            start = pl.multiple_of(ws_ref[g] * GRID_W, GRID_W)
            k4 = k_ref[pl.ds(start, nkey), cols].astype(BF16)
            v4 = v_ref[pl.ds(start, nkey), cols].astype(BF16)
            bias = bias_ref[4 * half:4 * half + 4].astype(F32).reshape(4 * tq, nkey)
            s_loc = lax.dot_general(qbd, k4, nt, preferred_element_type=F32) + bias
            mx = jnp.maximum(mx, jnp.max(s_loc, axis=-1, keepdims=True))
        p_ctx = jnp.exp2(s_ctx - mx)
        den = jnp.sum(p_ctx, axis=-1, keepdims=True)
        acc = jnp.dot(p_ctx.astype(BF16), vc4, preferred_element_type=F32)
        if local:
            p_loc = jnp.exp2(s_loc - mx)
            den = den + jnp.sum(p_loc, axis=-1, keepdims=True)
            acc = acc + jnp.dot(p_loc.astype(BF16), v4, preferred_element_type=F32)
        o = acc / den
        o4 = jnp.where(head == 0, o[0:tq], 0.0)
        for h in range(1, 4):
            o4 = o4 + jnp.where(head == h, o[h * tq:(h + 1) * tq], 0.0)
        outs.append(o4)
    o_ref[...] = jnp.concatenate(outs, axis=1).astype(o_ref.dtype)


def _na_plan(seq):
    rows = seq // GRID_W
    kr = min(NA_KR, rows)
    r_step = NA_ROWS_PER_STEP
    win_rows = min(r_step + kr, rows)
    n_groups = rows // r_step
    wstart = np.zeros(n_groups, np.int32)
    pats = []
    keys = {}
    var = np.zeros(n_groups, np.int32)
    for g in range(n_groups):
        r0 = g * r_step
        ws = int(np.clip(r0 - kr // 2, 0, rows - win_rows))
        wstart[g] = ws
        r = r0 + np.arange(r_step)
        rstart = np.clip(r - kr // 2, 0, rows - kr)
        krow = ws + np.arange(win_rows)
        valid = (krow[None, :] >= rstart[:, None]) & (krow[None, :] < rstart[:, None] + kr)
        roff = krow[None, :] - r[:, None] + NA_KR - 1
        key = (valid.tobytes(), np.where(valid, roff, 0).tobytes())
        if key not in keys:
            keys[key] = len(pats)
            pats.append((valid, np.where(valid, roff, 0)))
        var[g] = keys[key]
    return rows, win_rows, n_groups, wstart, var, pats


def _na_bias_tables(rpb, pats):
    cq = np.arange(GRID_W)
    ck = np.arange(GRID_W)
    cstart = np.clip(cq - NA_KC // 2, 0, GRID_W - NA_KC)
    col_valid = (ck[None] >= cstart[:, None]) & (ck[None] < cstart[:, None] + NA_KC)
    coff = np.clip(ck[None] - cq[:, None], -(NA_KC - 1), NA_KC - 1) + NA_KC - 1
    by_col = jnp.where(col_valid, rpb[..., coff] * LOG2E, NEG).astype(BF16)
    depth, heads, n_ro = by_col.shape[:3]
    flat = by_col.transpose(0, 1, 3, 2, 4).reshape(depth, heads, GRID_W, n_ro * GRID_W)
    tabs = []
    for valid, roff in pats:
        r_step, win_rows = valid.shape
        q_rows = []
        for i in range(r_step):
            a_ok = np.nonzero(valid[i])[0]
            a0, a1 = int(a_ok[0]), int(a_ok[-1]) + 1
            assert valid[i, a0:a1].all() and (np.diff(roff[i, a0:a1]) == 1).all()
            seg = flat[..., int(roff[i, a0]) * GRID_W:(int(roff[i, a0]) + a1 - a0) * GRID_W]
            q_rows.append(jnp.pad(seg, ((0, 0), (0, 0), (0, 0), (a0 * GRID_W, (win_rows - a1) * GRID_W)),
                                  constant_values=NEG))
        tabs.append(jnp.concatenate(q_rows, axis=-2))
    return jnp.stack(tabs, axis=1)


def _na_attn(pq, pc, bias_tabs, layer, plan, local):
    bsz, t, _ = pq.shape
    lc = pc.shape[1]
    if local:
        rows, win_rows, n_groups, wstart, var, _ = plan
        tq = NA_ROWS_PER_STEP * GRID_W
    else:
        win_rows, n_groups, tq = 0, 1, t
        wstart = np.zeros(1, np.int32)
        var = np.zeros(1, np.int32)
    in_specs = [
        pl.BlockSpec((None, tq, 512), lambda b, g, vr, ws: (b, g, COL_NQ // 512)),
        pl.BlockSpec((None, lc, 512), lambda b, g, vr, ws: (b, 0, COL_NK // 512)),
        pl.BlockSpec((None, lc, 512), lambda b, g, vr, ws: (b, 0, COL_NV // 512)),
    ]
    args = [pq, pc, pc]
    if local:
        in_specs += [
            pl.BlockSpec((None, t, 512), lambda b, g, vr, ws: (b, 0, COL_NK // 512)),
            pl.BlockSpec((None, t, 512), lambda b, g, vr, ws: (b, 0, COL_NV // 512)),
            pl.BlockSpec((None, None, N_HEADS, tq, win_rows * GRID_W),
                         lambda b, g, vr, ws: (layer, vr[g], 0, 0, 0)),
        ]
        args += [pq, pq, bias_tabs]
    grid_spec = pltpu.PrefetchScalarGridSpec(
        num_scalar_prefetch=2,
        grid=(bsz, n_groups),
        in_specs=in_specs,
        out_specs=pl.BlockSpec((None, tq, GROUP_W), lambda b, g, vr, ws: (b, g, 0)),
    )
    return pl.pallas_call(
        functools.partial(_na_kernel, local=local, win_rows=win_rows),
        grid_spec=grid_spec,
        out_shape=jax.ShapeDtypeStruct((bsz, t, GROUP_W), BF16),
        compiler_params=_cparams(("parallel", "arbitrary")),
        name="na_attn" if local else "ctx_attn_c",
    )(jnp.asarray(var), jnp.asarray(wstart), *args)


SSD_CHUNKS = 4


def _softplus(x):
    return jnp.maximum(x, 0.0) + jnp.log(1.0 + jnp.exp(-jnp.abs(x)))


def _bf16_parts(x, n):
    parts = []
    for _ in range(n):
        part = x.astype(BF16)
        parts.append(part)
        x = x - part.astype(F32)
    return parts


def _conv3_silu(cur, prev_blk, next_blk, w_ref, b_ref, has_prev, has_next):
    x = cur.astype(F32)
    rows = x.shape[0]
    prev_row = prev_blk.astype(F32)[HALO - 1:HALO, :] * has_prev
    next_row = next_blk.astype(F32)[0:1, :] * has_next
    ri = lax.broadcasted_iota(jnp.int32, x.shape, 0)
    up = jnp.where(ri == 0, prev_row, pltpu.roll(x, 1, 0))
    dn = jnp.where(ri == rows - 1, next_row, pltpu.roll(x, rows - 1, 0))
    u = up * w_ref[0:1, :] + x * w_ref[1:2, :] + dn * w_ref[2:3, :] + b_ref[...]
    return _silu(u)


def _ssd_chunk(xs, bc, dt, a, st_ref, keep, tri, expand, head4, d_off, reverse):
    L = SSM_CHUNK
    nt = (((1,), (1,)), ((), ()))
    a_t = a.T
    c_col = sum(jnp.dot(tri, part, preferred_element_type=F32) for part in _bf16_parts(a, 3))
    c_row = sum(lax.dot_general(part, tri, nt, preferred_element_type=F32) for part in _bf16_parts(a_t, 3))
    c_exp = sum(jnp.dot(part, expand, preferred_element_type=F32) for part in _bf16_parts(c_col, 2))
    dt_exp = sum(jnp.dot(part, expand, preferred_element_type=F32) for part in _bf16_parts(dt, 2))
    end = 0 if reverse else L - 1
    cend = c_exp[end:end + 1, :]
    x_dt = xs * dt_exp
    out_decay = jnp.exp(c_exp)
    x_dec = x_dt * jnp.exp(cend - c_exp)
    chunk_decay = jnp.exp(cend)

    ys = []
    for g in range(2):
        gl = slice(256 * g, 256 * (g + 1))
        b_g = bc[:, 128 * g:128 * (g + 1)]
        c_g = bc[:, 256 + 128 * g:256 + 128 * (g + 1)].astype(BF16)
        cb = lax.dot_general(c_g, b_g.astype(BF16), nt, preferred_element_type=F32)
        ms = []
        for hh in range(4):
            j = d_off + 4 * g + hh
            diff = c_col[:, j:j + 1] - c_row[j:j + 1, :]
            ms.append(cb * jnp.exp(jnp.where(keep, diff, NEG)))
        m_g = jnp.concatenate(ms, axis=0).astype(BF16)
        o = jnp.dot(m_g, x_dt[:, gl].astype(BF16), preferred_element_type=F32)
        y_diag = jnp.where(head4 == 0, o[0:L], 0.0)
        for hh in range(1, 4):
            y_diag = y_diag + jnp.where(head4 == hh, o[hh * L:(hh + 1) * L], 0.0)
        st = st_ref[g]
        y_off = jnp.dot(c_g, st.astype(BF16), preferred_element_type=F32) * out_decay[:, gl]
        ys.append(y_diag + y_off)
        st_ref[g] = chunk_decay[:, gl] * st + jnp.dot(
            b_g.T.astype(BF16), x_dec[:, gl].astype(BF16), preferred_element_type=F32)
    return jnp.concatenate(ys, axis=1)


def _ssd_kernel(*refs, reverse, nc):
    finalize = reverse
    if reverse:
        (xsc_ref, bcc_ref, dt_ref, dtb_ref, alog_ref, h0_ref,
         z_ref, yf_ref, dskip_ref, nw_ref, y_ref, ht_ref, st_ref) = refs
    else:
        (xs_ref, bc_ref, xsp_ref, xsn_ref, bcp_ref, bcn_ref, dt_ref, cwx_ref, cbx_ref, cwb_ref, cbb_ref,
         dtb_ref, alog_ref, h0_ref, y_ref, ht_ref, xsc_ref, bcc_ref, st_ref) = refs
    c = pl.program_id(1)
    cid = (nc - 1 - c) if reverse else c
    d_off = 8 if reverse else 0
    L = SSM_CHUNK
    n_sub = dt_ref.shape[0] // L

    @pl.when(c == 0)
    def _():
        st_ref[...] = h0_ref[...]

    if reverse:
        xs = xsc_ref[...]
        bc = bcc_ref[...].astype(F32)
    else:
        has_prev = jnp.where(cid > 0, 1.0, 0.0).astype(F32)
        has_next = jnp.where(cid < nc - 1, 1.0, 0.0).astype(F32)
        xs = _conv3_silu(xs_ref[...], xsp_ref[...], xsn_ref[...], cwx_ref, cbx_ref, has_prev, has_next)
        bc = _conv3_silu(bc_ref[...], bcp_ref[...], bcn_ref[...], cwb_ref, cbb_ref, has_prev, has_next)
        xsc_ref[...] = xs
        bcc_ref[...] = bc.astype(BF16)

    dt_all = _softplus(dt_ref[...].astype(F32) + dtb_ref[...])
    a_all = dt_all * (-jnp.exp(alog_ref[...]))
    ri = lax.broadcasted_iota(jnp.int32, (L, L), 0)
    ci = lax.broadcasted_iota(jnp.int32, (L, L), 1)
    keep = (ci >= ri) if reverse else (ci <= ri)
    tri = keep.astype(BF16)
    nt = (((1,), (1,)), ((), ()))
    ej = lax.broadcasted_iota(jnp.int32, (128, GROUP_W), 0)
    eh = lax.broadcasted_iota(jnp.int32, (128, GROUP_W), 1) // HEAD_DIM
    expand = (ej == eh + d_off).astype(BF16)
    head4 = lax.broadcasted_iota(jnp.int32, (L, 256), 1) // HEAD_DIM
    xs_all, bc_all = xs, bc
    y_sub = [None] * n_sub
    for sub in (range(n_sub - 1, -1, -1) if reverse else range(n_sub)):
        y_sub[sub] = _ssd_chunk(xs_all[sub * L:(sub + 1) * L], bc_all[sub * L:(sub + 1) * L],
                                dt_all[sub * L:(sub + 1) * L], a_all[sub * L:(sub + 1) * L],
                                st_ref, keep, tri, expand, head4, d_off, reverse)
    y = jnp.concatenate(y_sub, axis=0) if n_sub > 1 else y_sub[0]
    if finalize:
        y = y + yf_ref[...] + xs * dskip_ref[...]
        y = y * _silu(z_ref[...].astype(F32))
        halves = []
        for g in range(2):
            yg = y[:, 256 * g:256 * (g + 1)]
            halves.append(yg * lax.rsqrt(jnp.mean(yg * yg, axis=-1, keepdims=True) + EPS))
        y = jnp.concatenate(halves, axis=1) * nw_ref[...]
    y_ref[...] = y.astype(y_ref.dtype)

    @pl.when(c == nc - 1)
    def _():
        ht_ref[...] = st_ref[...]


def _ssd_direction(p, consts, h0, reverse, fwd=None):
    cwx, cbx, cwb, cbb, dtb, alog, dskip, nw = consts
    bsz, t, _ = p.shape
    rows = min(SSD_CHUNKS * SSM_CHUNK, t)
    nc = t // rows
    hb = rows // HALO
    nhalo = t // HALO

    def cid(c):
        return (nc - 1 - c) if reverse else c

    def cur(col, width):
        return pl.BlockSpec((None, rows, width), lambda b, c: (b, cid(c), col // width))

    def prev(col, width):
        return pl.BlockSpec((None, HALO, width), lambda b, c: (b, jnp.maximum(cid(c) * hb - 1, 0), col // width))

    def nxt(col, width):
        return pl.BlockSpec((None, HALO, width), lambda b, c: (b, jnp.minimum((cid(c) + 1) * hb, nhalo - 1), col // width))

    def const(arr):
        return pl.BlockSpec(arr.shape, lambda b, c: (0,) * arr.ndim)

    state_spec = pl.BlockSpec((None, 2, SSM_STATE, 256), lambda b, c: (b, 0, 0, 0))
    chunk_spec = pl.BlockSpec((None, rows, GROUP_W), lambda b, c: (b, cid(c), 0))
    state_shape = jax.ShapeDtypeStruct((bsz, 2, SSM_STATE, 256), F32)
    if reverse:
        yf, xs_act, bc_act = fwd
        in_specs = [chunk_spec, chunk_spec, cur(COL_DT, 128), const(dtb), const(alog), state_spec,
                    cur(COL_SZ, 512), chunk_spec, const(dskip), const(nw)]
        args = [xs_act, bc_act, p, dtb, alog, h0, p, yf, dskip, nw]
        out_specs = [chunk_spec, state_spec]
        out_shape = [jax.ShapeDtypeStruct((bsz, t, GROUP_W), BF16), state_shape]
    else:
        in_specs = [cur(COL_SX, 512), cur(COL_SBC, 512), prev(COL_SX, 512), nxt(COL_SX, 512),
                    prev(COL_SBC, 512), nxt(COL_SBC, 512), cur(COL_DT, 128),
                    const(cwx), const(cbx), const(cwb), const(cbb), const(dtb), const(alog), state_spec]
        args = [p, p, p, p, p, p, p, cwx, cbx, cwb, cbb, dtb, alog, h0]
        out_specs = [chunk_spec, state_spec, chunk_spec, chunk_spec]
        out_shape = [jax.ShapeDtypeStruct((bsz, t, GROUP_W), F32), state_shape,
                     jax.ShapeDtypeStruct((bsz, t, GROUP_W), F32), jax.ShapeDtypeStruct((bsz, t, GROUP_W), BF16)]
    return pl.pallas_call(
        functools.partial(_ssd_kernel, reverse=reverse, nc=nc),
        grid=(bsz, nc),
        in_specs=in_specs,
        out_specs=out_specs,
        out_shape=out_shape,
        scratch_shapes=[pltpu.VMEM((2, SSM_STATE, 256), F32)],
        compiler_params=_cparams(("parallel", "arbitrary")),
        name="ssd_rev" if reverse else "ssd_fwd",
    )(*args)


def _ssd_consts(conv_w, conv_b, dt_bias, a_log, d_skip, norm_w):
    cwx, cwb = conv_w[:, :GROUP_W], conv_w[:, GROUP_W:]
    cbx, cbb = conv_b[:GROUP_W].reshape(1, -1), conv_b[GROUP_W:].reshape(1, -1)
    pad = lambda v: jnp.pad(v.reshape(1, -1), ((0, 0), (0, 128 - v.size)))
    return (cwx, cbx, cwb, cbb, pad(dt_bias), pad(a_log),
            jnp.repeat(d_skip, HEAD_DIM).reshape(1, -1), norm_w.reshape(1, -1))


def _ssd_mixer(p, pc, consts):
    bsz = p.shape[0]
    zero = jnp.zeros((bsz, 2, SSM_STATE, 256), F32)
    ycf, hf, *act_c = _ssd_direction(pc, consts, zero, False)
    yc, hb = _ssd_direction(pc, consts, zero, True, fwd=(ycf, *act_c))
    ylf, _, *act_l = _ssd_direction(p, consts, hf, False)
    yl, _ = _ssd_direction(p, consts, hb, True, fwd=(ylf, *act_l))
    return yl, yc


HY_CB = 128


def _hy_cb(seq):
    del seq
    return HY_CB
HY_UNROLL = 16


def _hy_dims(seq):
    n = 2 * seq
    n1 = {4096: 128, 1024: 64, 512: 32, 256: 32, 128: 16}[seq]
    n2 = n // n1
    h = n1 // 2
    return dict(n=n, n1=n1, n2=n2, h=h, pa=h + 8, pb=n2 + 8, pc=n1 + 8)


def _hy_matrices(seq):
    d = _hy_dims(seq)
    n, n1, n2, h = d["n"], d["n1"], d["n2"], d["h"]

    def cis(num, den, sign):
        ang = (2.0 * math.pi / den) * (num % den).astype(F32)
        return jnp.cos(ang), sign * jnp.sin(ang)

    k1 = jnp.arange(n1, dtype=jnp.int32)
    nn = n2 * jnp.arange(n1, dtype=jnp.int32)[None, None, :] + jnp.arange(n2, dtype=jnp.int32)[:, None, None]
    e1r, e1i = cis(k1[None, :, None] * nn, n, -1.0)
    m1f = jnp.concatenate([e1r, e1i], axis=1)
    m1 = jnp.concatenate([jnp.concatenate([e1r[..., :h], -e1i[..., :h]], axis=2),
                          jnp.concatenate([e1i[..., :h], e1r[..., :h]], axis=2)], axis=1)
    a2 = jnp.arange(n2, dtype=jnp.int32)
    g2r, g2i = cis(a2[:, None] * a2[None, :], n2, -1.0)
    m2 = jnp.concatenate([jnp.concatenate([g2r, -g2i], axis=1),
                          jnp.concatenate([g2i, g2r], axis=1)], axis=0)
    num3 = (a2[None, :, None] * a2[None, None, :] * n1 + k1[:, None, None] * a2[None, :, None])
    e3r, e3i = cis(num3, n, 1.0)
    m3 = jnp.concatenate([jnp.concatenate([e3r, -e3i], axis=2),
                          jnp.concatenate([e3i, e3r], axis=2)], axis=1)
    hh = jnp.arange(h, dtype=jnp.int32)
    d4r, d4i = cis(hh[:, None] * k1[None, :], n1, 1.0)
    m4 = jnp.concatenate([jnp.concatenate([d4r, -d4i], axis=1),
                          jnp.concatenate([d4i, d4r], axis=1)], axis=0) / n
    return tuple(m.astype(BF16) for m in (m1f, m1, m2, m3, m4))


def _hy_prep_kernel(p_ref, w_ref, b_ref, o_ref, *, dims):
    n2, h, pa = dims["n2"], dims["h"], dims["pa"]
    seq = h * n2
    for j in range(n2):
        o_ref[j * pa + h:(j + 1) * pa, :] = jnp.zeros((pa - h, o_ref.shape[-1]), o_ref.dtype)
    w0, w1, w2, bias = w_ref[0:1, :], w_ref[1:2, :], w_ref[2:3, :], b_ref[...]
    ri = lax.broadcasted_iota(jnp.int32, (n2, o_ref.shape[-1]), 0)

    def body(i, carry):
        r0 = pl.multiple_of(i * n2, n2)
        x = p_ref[pl.ds(r0, n2), :].astype(F32)
        pstart = pl.multiple_of(jnp.maximum(r0 - HALO, 0), HALO)
        nstart = pl.multiple_of(jnp.minimum(r0 + n2, seq - HALO), HALO)
        prev_row = p_ref[pl.ds(pstart, HALO), :].astype(F32)[HALO - 1:HALO, :] * jnp.where(i > 0, 1.0, 0.0).astype(F32)
        next_row = p_ref[pl.ds(nstart, HALO), :].astype(F32)[0:1, :] * jnp.where(i < h - 1, 1.0, 0.0).astype(F32)
        up = jnp.where(ri == 0, prev_row, pltpu.roll(x, 1, 0))
        dn = jnp.where(ri == n2 - 1, next_row, pltpu.roll(x, n2 - 1, 0))
        o_ref[pl.ds(i, n2, stride=pa), :] = up * w0 + x * w1 + dn * w2 + bias
        return carry

    lax.fori_loop(0, h, body, 0)


def _hy_prep(p, short_w, short_b):
    bsz, t, _ = p.shape
    dims = _hy_dims(t)
    rows = dims["n2"] * dims["pa"]
    cb = _hy_cb(t)
    ncb = 3 * GROUP_W // cb
    return pl.pallas_call(
        functools.partial(_hy_prep_kernel, dims=dims),
        grid=(bsz, ncb),
        in_specs=[pl.BlockSpec((None, t, cb), lambda b, j: (b, 0, COL_HY // cb + j)),
                  pl.BlockSpec((3, cb), lambda b, j: (0, j)),
                  pl.BlockSpec((1, cb), lambda b, j: (0, j))],
        out_specs=pl.BlockSpec((None, rows, cb), lambda b, j: (b, 0, j)),
        out_shape=jax.ShapeDtypeStruct((bsz, rows, 3 * GROUP_W), F32),
        compiler_params=_cparams(("parallel", "parallel")),
        name="hy_prep",
    )(p, short_w, short_b.reshape(1, -1))


def _hy_filter_kernel(z_ref, w1_ref, b1_ref, w2_ref, b2_ref, w3_ref, b3_ref, w4_ref, fr_ref, dl_ref, o_ref):
    hi = lax.Precision.HIGHEST
    z = z_ref[...]
    fr = fr_ref[...]
    h = jnp.sin(fr * (jnp.dot(z, w1_ref[...], preferred_element_type=F32, precision=hi) + b1_ref[...]))
    h = jnp.sin(fr * (jnp.dot(h, w2_ref[...], preferred_element_type=F32, precision=hi) + b2_ref[...]))
    h = jnp.sin(fr * (jnp.dot(h, w3_ref[...], preferred_element_type=F32, precision=hi) + b3_ref[...]))
    full = jnp.dot(h.astype(BF16), w4_ref[...].astype(BF16), preferred_element_type=F32)
    t = z[:, 0:1]
    is_bwd = z[:, HY_EMB:HY_EMB + 1] > 0.5
    live = z[:, HY_EMB + 1:HY_EMB + 2]
    decay = jnp.exp(-t * jnp.abs(dl_ref[...])) * live
    for o in range(2):
        fwd = full[:, o * 2 * GROUP_W:o * 2 * GROUP_W + GROUP_W]
        bwd = full[:, o * 2 * GROUP_W + GROUP_W:(o + 1) * 2 * GROUP_W]
        o_ref[o] = jnp.where(is_bwd, bwd, fwd) * decay


def _hy_filter_features(seq):
    d = _hy_dims(seq)
    n, n1, n2 = d["n"], d["n1"], d["n2"]
    row = np.arange(n)
    time = n2 * (row % n1) + row // n1
    is_bwd = time > seq
    pos = np.where(is_bwd, n - time, time)
    live = (time != seq).astype(np.float64)
    pos = np.where(time == seq, 0, pos)
    t = np.linspace(0.0, 1.0, seq)[pos]
    bands = (HY_EMB - 1) // 2
    f = np.linspace(1e-4, bands - 1, bands)[None]
    wpos = (2.0 * math.pi * pos / seq)[:, None]
    feat = np.zeros((n, 128), np.float64)
    feat[:, 0] = t
    feat[:, 1:1 + bands] = np.cos(f * wpos)
    feat[:, 1 + bands:HY_EMB] = -np.sin(f * wpos)
    feat[:, HY_EMB] = is_bwd
    feat[:, HY_EMB + 1] = live
    return jnp.asarray(feat, F32)


def _hy_filter(seq, w1, b1, w2, b2, w3, b3, w4, freq, tr=512):
    d = _hy_dims(seq)
    n = d["n"]
    tr = min(tr, n)
    feat = _hy_filter_features(seq)
    w1p = jnp.pad(w1, ((0, 128 - HY_EMB), (0, 0)))
    max_decay = math.log(HY_DECAY_TARGET) / HY_FAST_PCT
    min_decay = math.log(HY_DECAY_TARGET) / HY_SLOW_PCT
    deltas = jnp.linspace(min_decay, max_decay, GROUP_W, dtype=F32).reshape(1, -1)
    row = lambda v: v.reshape(1, -1)
    const = lambda a: pl.BlockSpec(a.shape, lambda i: (0,) * a.ndim)
    args = [w1p, row(b1), w2, row(b2), w3, row(b3), w4, row(freq), deltas]
    return pl.pallas_call(
        _hy_filter_kernel,
        grid=(n // tr,),
        in_specs=[pl.BlockSpec((tr, 128), lambda i: (i, 0))] + [const(a) for a in args],
        out_specs=pl.BlockSpec((2, tr, GROUP_W), lambda i: (0, i, 0)),
        out_shape=jax.ShapeDtypeStruct((2, n, GROUP_W), F32),
        compiler_params=_cparams(("parallel",)),
        name="hy_filter",
    )(feat, *args)


def _hy_spectrum_kernel(k_ref, m1f_ref, m2_ref, re_ref, im_ref, tr_ref, ti_ref, *, dims):
    n1, n2, pb = dims["n1"], dims["n2"], dims["pb"]

    def stage1(j, carry):
        r0 = pl.multiple_of(j * n1, n1)
        a = jnp.dot(m1f_ref[j], k_ref[pl.ds(r0, n1), :].astype(BF16), preferred_element_type=F32)
        tr_ref[pl.ds(j, n1, stride=pb), :] = a[:n1]
        ti_ref[pl.ds(j, n1, stride=pb), :] = a[n1:]
        return carry

    lax.fori_loop(0, n2, stage1, 0, unroll=HY_UNROLL)

    def stage2(k, carry):
        r0 = pl.multiple_of(k * pb, 8)
        rhs = jnp.concatenate([tr_ref[pl.ds(r0, n2), :], ti_ref[pl.ds(r0, n2), :]], axis=0).astype(BF16)
        x = jnp.dot(m2_ref[...], rhs, preferred_element_type=F32)
        o0 = pl.multiple_of(k * n2, n2)
        re_ref[pl.ds(o0, n2), :] = x[:n2]
        im_ref[pl.ds(o0, n2), :] = x[n2:]
        return carry

    lax.fori_loop(0, n1, stage2, 0, unroll=HY_UNROLL)


def _hy_spectrum(kern, mats, seq):
    d = _hy_dims(seq)
    n, n1, pb = d["n"], d["n1"], d["pb"]
    m1f, _, m2, _, _ = mats
    cb = _hy_cb(seq)
    ncb = GROUP_W // cb
    blk = pl.BlockSpec((None, n, cb), lambda o, j: (o, 0, j))
    const = lambda a: pl.BlockSpec(a.shape, lambda o, j: (0,) * a.ndim)
    return pl.pallas_call(
        functools.partial(_hy_spectrum_kernel, dims=d),
        grid=(2, ncb),
        in_specs=[blk, const(m1f), const(m2)],
        out_specs=[blk, blk],
        out_shape=[jax.ShapeDtypeStruct((2, n, GROUP_W), F32)] * 2,
        scratch_shapes=[pltpu.VMEM((n1 * pb, cb), F32)] * 2,
        compiler_params=_cparams(("parallel", "parallel")),
        name="hy_spectrum",
    )(kern, m1f, m2)


def _pack_pair(re, im):
    half = jnp.uint32(0x8000)
    r = lax.bitcast_convert_type(re, jnp.uint32) + half
    i = lax.bitcast_convert_type(im, jnp.uint32) + half
    return (r & jnp.uint32(0xFFFF0000)) | (i >> 16)


def _unpack_pair(w):
    re = lax.bitcast_convert_type(w & jnp.uint32(0xFFFF0000), F32)
    im = lax.bitcast_convert_type(w << 16, F32)
    return re, im


def _hy_conv_kernel(u_ref, g_ref, kr_ref, ki_ref, skip_ref, m1_ref, m2_ref, m3_ref, m4_ref, o_ref,
                    t1, t2, *, dims, natural_out):
    n1, n2, h, pa, pb, pc = (dims[k] for k in ("n1", "n2", "h", "pa", "pb", "pc"))

    def fwd1(j, carry):
        r0 = pl.multiple_of(j * pa, 8)
        rhs = jnp.concatenate([u_ref[0, pl.ds(r0, h), :], u_ref[1, pl.ds(r0, h), :]], axis=0).astype(BF16)
        a = jnp.dot(m1_ref[j], rhs, preferred_element_type=F32)
        t1[pl.ds(j, n1, stride=pb), :] = _pack_pair(a[:n1], a[n1:])
        return carry

    lax.fori_loop(0, n2, fwd1, 0, unroll=HY_UNROLL)

    def mid(k, carry):
        r0 = pl.multiple_of(k * pb, 8)
        rhs = jnp.concatenate(_unpack_pair(t1[pl.ds(r0, n2), :]), axis=0).astype(BF16)
        x = jnp.dot(m2_ref[...], rhs, preferred_element_type=F32)
        f0 = pl.multiple_of(k * n2, n2)
        fr, fi = kr_ref[pl.ds(f0, n2), :], ki_ref[pl.ds(f0, n2), :]
        xr, xi = x[:n2], x[n2:]
        y = jnp.concatenate([xr * fr - xi * fi, xr * fi + xi * fr], axis=0).astype(BF16)
        c = jnp.dot(m3_ref[k], y, preferred_element_type=F32)
        t2[pl.ds(k, n2, stride=pc), :] = _pack_pair(c[:n2], c[n2:])
        return carry

    lax.fori_loop(0, n1, mid, 0, unroll=HY_UNROLL)

    if not natural_out:
        o_ref[...] = jnp.zeros_like(o_ref)
    skip = skip_ref[...]

    def inv2(j, carry):
        r0 = pl.multiple_of(j * pc, 8)
        rhs = jnp.concatenate(_unpack_pair(t2[pl.ds(r0, n1), :]), axis=0).astype(BF16)
        y = jnp.dot(m4_ref[...], rhs, preferred_element_type=F32)
        a0 = pl.multiple_of(j * pa, 8)
        for e in range(2):
            val = g_ref[e, pl.ds(a0, h), :] * (y[e * h:(e + 1) * h] + skip * u_ref[e, pl.ds(a0, h), :])
            if natural_out:
                o_ref[e, pl.ds(j, h, stride=n2), :] = val.astype(o_ref.dtype)
            else:
                o_ref[e, pl.ds(a0, h), :] = val
        return carry

    lax.fori_loop(0, n2, inv2, 0, unroll=HY_UNROLL)


def _hy_conv(u_arr, u_col, g_arr, g_col, kf_re, kf_im, order, skip, mats, seq, natural_out):
    d = _hy_dims(seq)
    bsz = u_arr.shape[0]
    rows = d["n2"] * d["pa"]
    _, m1, m2, m3, m4 = mats
    cb = _hy_cb(seq)
    ncb = GROUP_W // cb
    single = pl.Buffered(1)
    const = lambda a: pl.BlockSpec(a.shape, lambda j, p: (0,) * a.ndim, pipeline_mode=single)
    in_specs = [
        pl.BlockSpec((2, rows, cb), lambda j, p: (p, 0, u_col // cb + j)),
        pl.BlockSpec((2, rows, cb), lambda j, p: (p, 0, g_col // cb + j)),
        pl.BlockSpec((None, d["n"], cb), lambda j, p: (order, 0, j), pipeline_mode=single),
        pl.BlockSpec((None, d["n"], cb), lambda j, p: (order, 0, j), pipeline_mode=single),
        pl.BlockSpec((None, 1, cb), lambda j, p: (order, 0, j)),
        const(m1), const(m2), const(m3), const(m4),
    ]
    if natural_out:
        out_spec = pl.BlockSpec((2, seq, cb), lambda j, p: (p, 0, j))
        out_shape = jax.ShapeDtypeStruct((bsz, seq, GROUP_W), F32)
    else:
        out_spec = pl.BlockSpec((2, rows, cb), lambda j, p: (p, 0, j))
        out_shape = jax.ShapeDtypeStruct((bsz, rows, GROUP_W), F32)
    return pl.pallas_call(
        functools.partial(_hy_conv_kernel, dims=d, natural_out=natural_out),
        grid=(ncb, bsz // 2),
        in_specs=in_specs,
        out_specs=out_spec,
        out_shape=out_shape,
        scratch_shapes=[pltpu.VMEM((d["n1"] * d["pb"], cb), jnp.uint32),
                        pltpu.VMEM((d["n2"] * d["pc"], cb), jnp.uint32)],
        compiler_params=_cparams(("parallel", "parallel")),
        name="hy_conv",
    )(u_arr, g_arr, kf_re, kf_im, skip.reshape(2, 1, GROUP_W), m1, m2, m3, m4)


def _hyena_mixer(p, short_w, short_b, filt_params, skip, mats):
    seq = p.shape[1]
    kern = _hy_filter(seq, *filt_params)
    kf_re, kf_im = _hy_spectrum(kern, mats, seq)
    ut = _hy_prep(p, short_w, short_b)
    zt = _hy_conv(ut, 2 * GROUP_W, ut, 0, kf_re, kf_im, 0, skip, mats, seq, natural_out=False)
    return _hy_conv(zt, 0, ut, GROUP_W, kf_re, kf_im, 1, skip, mats, seq, natural_out=True)


CAST_BLOCK_BYTES = 8 * 1024 * 1024


def _cast_kernel(w_ref, o_ref):
    o_ref[...] = w_ref[...].astype(o_ref.dtype)


def _cast_bf16(w):
    depth, k, n = w.shape
    tr = min(k, CAST_BLOCK_BYTES // (4 * n))
    return pl.pallas_call(
        _cast_kernel,
        grid=(depth, k // tr),
        in_specs=[pl.BlockSpec((None, tr, n), lambda i, r: (i, r, 0))],
        out_specs=pl.BlockSpec((None, tr, n), lambda i, r: (i, r, 0)),
        out_shape=jax.ShapeDtypeStruct(w.shape, BF16),
        compiler_params=_cparams(("parallel", "parallel")),
        name="cast_bf16",
    )(w)


def _w_in_layout_kernel(w_ref, o_ref):
    o_a, o_dt = 768, 768 + 3072 + 1536
    lane = lax.broadcasted_iota(jnp.int32, (w_ref.shape[0], 128), 1)
    lo = lane < HEAD_DIM
    for m in range(4):
        a = w_ref[:, 128 * (m // 2):128 * (m // 2 + 1)]
        b = w_ref[:, 128 * ((m + 4) // 2):128 * ((m + 4) // 2 + 1)]
        if m % 2 == 1:
            a = pltpu.roll(a, HEAD_DIM, 1)
        else:
            b = pltpu.roll(b, HEAD_DIM, 1)
        o_ref[:, 128 * m:128 * (m + 1)] = jnp.where(lo, a, b).astype(o_ref.dtype)
    o_ref[:, COL_KA:COL_DT] = w_ref[:, COL_KA:o_a].astype(o_ref.dtype)
    o_ref[:, COL_DT:COL_HY] = jnp.zeros((w_ref.shape[0], COL_HY - COL_DT), o_ref.dtype)
    o_ref[:, COL_DT:COL_DT + 16] = w_ref[:, o_dt:o_dt + 16].astype(o_ref.dtype)
    o_ref[:, COL_HY:] = w_ref[:, o_a:o_dt].astype(o_ref.dtype)


def _w_out_layout_kernel(w_ref, o_ref):
    for i, head in enumerate(HEAD_ORDER):
        o_ref[HEAD_DIM * i:HEAD_DIM * (i + 1), :] = w_ref[HEAD_DIM * head:HEAD_DIM * (head + 1), :].astype(o_ref.dtype)
    o_ref[GROUP_W:, :] = w_ref[GROUP_W:, :].astype(o_ref.dtype)


def _prep_weights(w_in, w_out):
    depth, d, n_in = w_in.shape
    tr = 256
    w_in_p = pl.pallas_call(
        _w_in_layout_kernel,
        grid=(depth, d // tr),
        in_specs=[pl.BlockSpec((None, tr, n_in), lambda i, r: (i, r, 0))],
        out_specs=pl.BlockSpec((None, tr, N_PROJ), lambda i, r: (i, r, 0)),
        out_shape=jax.ShapeDtypeStruct((depth, d, N_PROJ), BF16),
        compiler_params=_cparams(("parallel", "parallel")),
        name="w_in_layout",
    )(w_in)
    k_out, dm = w_out.shape[1:]
    tc = 512
    w_out_p = pl.pallas_call(
        _w_out_layout_kernel,
        grid=(depth, dm // tc),
        in_specs=[pl.BlockSpec((None, k_out, tc), lambda i, c: (i, 0, c))],
        out_specs=pl.BlockSpec((None, k_out, tc), lambda i, c: (i, 0, c)),
        out_shape=jax.ShapeDtypeStruct(w_out.shape, BF16),
        compiler_params=_cparams(("parallel", "parallel")),
        name="w_out_layout",
    )(w_out)
    return w_in_p, w_out_p


TILE_W_IN = (512, 2816)
TILE_MLP_UP = (1024, 2048)
TILE_DOWN = (1024, 1024, 2048)
TILE_OUT = (512, 2048)


def kernel(x, c, ctx, c_ctx, ada_w, ada_b, norm_mix, norm_mlp, w_in, w_out, attn_sink,
           hy_short_w, hy_short_b, hy_w1, hy_b1, hy_w2, hy_b2, hy_w3, hy_b3, hy_w4, hy_freq, hy_skip,
           na_rpb, ssm_conv_w, ssm_conv_b, ssm_dt_bias, ssm_a_log, ssm_d, ssm_norm,
           mlp_w1, mlp_w2, final_norm):
    bsz, seq, d = x.shape
    lc = ctx.shape[1]
    depth = ada_w.shape[0]
    assert bsz % 2 == 0 and bsz <= 7 and d == D_MODEL

    cs = jnp.zeros((8, d), F32).at[:bsz].set(c).at[bsz].set(c_ctx)
    mod = _ada_mod(cs, ada_w, ada_b)
    cos_t, sin_t = _rope_tables(seq)
    na_plan = _na_plan(seq)
    bias_tabs = _na_bias_tables(na_rpb, na_plan[-1])
    mats_l = _hy_matrices(seq)
    mats_c = _hy_matrices(lc)
    lat_row = lambda b: b
    ctx_row = lambda b: bsz
    tm_c = lc

    w_in_p, w_out_p = _prep_weights(w_in, w_out)
    w1_b = _cast_bf16(mlp_w1)
    w2_b = _cast_bf16(mlp_w2)

    xc = ctx
    for i in range(depth):
        last = i == depth - 1
        mod3 = mod[i].reshape(8, 1, 6 * d)
        p = _norm_matmul(x, norm_mix[i], mod3, lat_row, 0, 1, w_in_p, i, P_DTYPE, False, *TILE_W_IN)
        pc = _norm_matmul(xc, norm_mix[i], mod3, ctx_row, 0, 1, w_in_p, i, P_DTYPE, False, tm_c, TILE_W_IN[1])

        filt_params = (hy_w1[i], hy_b1[i], hy_w2[i], hy_b2[i], hy_w3[i], hy_b3[i], hy_w4[i], hy_freq[i])
        ssd_consts = _ssd_consts(ssm_conv_w[i], ssm_conv_b[i], ssm_dt_bias[i], ssm_a_log[i], ssm_d[i], ssm_norm[i])

        ya = _win_attn(attn_sink[i], p, pc, cos_t, sin_t, local=True)
        yb = _hyena_mixer(p, hy_short_w[i], hy_short_b[i], filt_params, hy_skip[i], mats_l)
        yn = _na_attn(p, pc, bias_tabs, i, na_plan, local=True)
        yd, ydc = _ssd_mixer(p, pc, ssd_consts)
        x = _out_proj((ya, yb, yn, yd), w_out_p, i, x, mod3, lat_row, 2, *TILE_OUT)
        hid = _norm_matmul(x, norm_mlp[i], mod3, lat_row, 3, 4, w1_b, i, BF16, True, *TILE_MLP_UP)
        x = _matmul_residual(hid, w2_b, i, x, mod3, lat_row, 5, *TILE_DOWN)

        if not last:
            yac = _win_attn(attn_sink[i], pc, pc, None, None, local=False)
            ybc = _hyena_mixer(pc, hy_short_w[i], hy_short_b[i], filt_params, hy_skip[i], mats_c)
            ync = _na_attn(pc, pc, None, i, None, local=False)
            xc = _out_proj((yac, ybc, ync, ydc), w_out_p, i, xc, mod3, ctx_row, 2, tm_c, TILE_OUT[1])
            hidc = _norm_matmul(xc, norm_mlp[i], mod3, ctx_row, 3, 4, w1_b, i, BF16, True, tm_c, TILE_MLP_UP[1])
            xc = _matmul_residual(hidc, w2_b, i, xc, mod3, ctx_row, 5, tm_c, *TILE_DOWN[1:])
    return _final_norm(x, final_norm)
```

```python
import functools
import math

import numpy as np
import jax
import jax.numpy as jnp
from jax import lax
from jax.experimental import pallas as pl
from jax.experimental.pallas import tpu as pltpu

F32 = jnp.float32
BF16 = jnp.bfloat16

D_MODEL = 2048
GRID_W = 64
EPS = 1e-6
NEG = -1e30
HEAD_DIM = 64
GROUP_W = D_MODEL // 4
N_HEADS = GROUP_W // HEAD_DIM
WINDOW = 128
BLOCK = 128
ROPE_BASE = 10000.0
HY_EMB = 33
HY_FFN = 64
HY_DECAY_TARGET = 1e-2
HY_FAST_PCT = 0.3
HY_SLOW_PCT = 1.5
NA_KR = 8
NA_KC = 16
SSM_STATE = 128
SSM_CHUNK = 128
D_FF = 4 * D_MODEL
SCALE = HEAD_DIM ** -0.5
LOG2E = math.log2(math.e)

COL_QA, COL_KA, COL_VA, COL_DT = 0, 512, 640, 768
COL_HY = 1024
COL_NQ, COL_NK, COL_NV = 2560, 3072, 3584
COL_SZ, COL_SX, COL_SBC = 4096, 4608, 5120
N_PROJ = 5632
HEAD_ORDER = (0, 4, 1, 5, 2, 6, 3, 7)

P_DTYPE = BF16
HALO = 16

V7X_VMEM_BYTES = 64 * 1024 * 1024
VMEM_LIMIT = 56 * 1024 * 1024


def _cparams(sem):
    return pltpu.CompilerParams(dimension_semantics=sem, vmem_limit_bytes=VMEM_LIMIT)


def _silu(x):
    return x * jax.nn.sigmoid(x)


def _ada_kernel(cs_ref, w_ref, b_ref, o_ref):
    a = _silu(cs_ref[...]).astype(BF16)
    o_ref[...] = jnp.dot(a, w_ref[...].astype(BF16), preferred_element_type=F32) + b_ref[...]


def _ada_mod(cs, ada_w, ada_b, tn=1024):
    depth, d, n = ada_w.shape
    return pl.pallas_call(
        _ada_kernel,
        grid=(depth, n // tn),
        in_specs=[
            pl.BlockSpec((8, d), lambda i, j: (0, 0)),
            pl.BlockSpec((None, d, tn), lambda i, j: (i, 0, j)),
            pl.BlockSpec((None, 1, tn), lambda i, j: (i, 0, j)),
        ],
        out_specs=pl.BlockSpec((None, 8, tn), lambda i, j: (i, 0, j)),
        out_shape=jax.ShapeDtypeStruct((depth, 8, n), F32),
        compiler_params=_cparams(("parallel", "parallel")),
        name="ada_mod",
    )(cs, ada_w, ada_b.reshape(depth, 1, n))


NORM_ROWS = 16


def _norm_matmul_kernel(x_ref, g_ref, sh_ref, sc_ref, w_ref, o_ref, h_ref, *, act, w_t):
    @pl.when(pl.program_id(2) == 0)
    def _():
        gain = g_ref[...] * (1.0 + sc_ref[...])
        shift = sh_ref[...]

        def rows(r, carry):
            r0 = pl.multiple_of(r * NORM_ROWS, NORM_ROWS)
            xf = x_ref[pl.ds(r0, NORM_ROWS), :]
            ms = jnp.mean(xf * xf, axis=-1, keepdims=True)
            h_ref[pl.ds(r0, NORM_ROWS), :] = (xf * lax.rsqrt(ms + EPS) * gain + shift).astype(BF16)
            return carry

        lax.fori_loop(0, x_ref.shape[0] // NORM_ROWS, rows, 0, unroll=8)

    if w_t:
        r = lax.dot_general(h_ref[...], w_ref[...], (((1,), (1,)), ((), ())), preferred_element_type=F32)
    else:
        r = jnp.dot(h_ref[...], w_ref[...], preferred_element_type=F32)
    if act:
        r = jnp.square(jnp.maximum(r, 0.0))
    o_ref[...] = r.astype(o_ref.dtype)


def _norm_matmul(x, g, mod3, row_of_b, sh_idx, sc_idx, w, layer, out_dtype, act, tm, tn, w_t=False):
    bsz, t, d = x.shape
    n = w.shape[1] if w_t else w.shape[2]
    tm = min(tm, t)
    w_spec = (pl.BlockSpec((None, tn, d), lambda b, m, j: (layer, j, 0)) if w_t
              else pl.BlockSpec((None, d, tn), lambda b, m, j: (layer, 0, j)))
    return pl.pallas_call(
        functools.partial(_norm_matmul_kernel, act=act, w_t=w_t),
        grid=(bsz, t // tm, n // tn),
        in_specs=[
            pl.BlockSpec((None, tm, d), lambda b, m, j: (b, m, 0)),
            pl.BlockSpec((1, d), lambda b, m, j: (0, 0)),
            pl.BlockSpec((None, 1, d), lambda b, m, j: (row_of_b(b), 0, sh_idx)),
            pl.BlockSpec((None, 1, d), lambda b, m, j: (row_of_b(b), 0, sc_idx)),
            w_spec,
        ],
        out_specs=pl.BlockSpec((None, tm, tn), lambda b, m, j: (b, m, j)),
        out_shape=jax.ShapeDtypeStruct((bsz, t, n), out_dtype),
        scratch_shapes=[pltpu.VMEM((tm, d), BF16)],
        compiler_params=_cparams(("parallel", "parallel", "arbitrary")),
        name="norm_matmul",
    )(x, g.reshape(1, d), mod3, mod3, w)


def _mm_res_kernel(a_ref, w_ref, x_ref, gate_ref, o_ref, acc_ref, *, nk):
    k = pl.program_id(3)

    @pl.when(k == 0)
    def _():
        acc_ref[...] = jnp.zeros_like(acc_ref)

    acc_ref[...] += jnp.dot(a_ref[...], w_ref[...], preferred_element_type=F32)

    @pl.when(k == nk - 1)
    def _():
        o_ref[...] = x_ref[...] + gate_ref[...] * acc_ref[...]


def _matmul_residual(a, w, layer, x, mod3, row_of_b, gate_idx, tm, tn, tk):
    bsz, t, kdim = a.shape
    n = w.shape[2]
    tm = min(tm, t)
    nk = kdim // tk
    return pl.pallas_call(
        functools.partial(_mm_res_kernel, nk=nk),
        grid=(bsz, t // tm, n // tn, nk),
        in_specs=[
            pl.BlockSpec((None, tm, tk), lambda b, m, j, k: (b, m, k)),
            pl.BlockSpec((None, tk, tn), lambda b, m, j, k: (layer, k, j)),
            pl.BlockSpec((None, tm, tn), lambda b, m, j, k: (b, m, j)),
            pl.BlockSpec((None, 1, tn), lambda b, m, j, k: (row_of_b(b), 0, gate_idx * (D_MODEL // tn) + j)),
        ],
        out_specs=pl.BlockSpec((None, tm, tn), lambda b, m, j, k: (b, m, j)),
        out_shape=jax.ShapeDtypeStruct(x.shape, F32),
        scratch_shapes=[pltpu.VMEM((tm, tn), F32)],
        compiler_params=_cparams(("parallel", "parallel", "parallel", "arbitrary")),
        name="matmul_residual",
    )(a, w, x, mod3)


def _out_proj_kernel(ya_ref, yb_ref, yn_ref, yd_ref, w_ref, x_ref, gate_ref, o_ref):
    acc = None
    for g, y_ref in enumerate((ya_ref, yb_ref, yn_ref, yd_ref)):
        part = jnp.dot(y_ref[...].astype(BF16), w_ref[GROUP_W * g:GROUP_W * (g + 1), :],
                       preferred_element_type=F32)
        acc = part if acc is None else acc + part
    o_ref[...] = x_ref[...] + gate_ref[...] * acc


def _out_proj(ys, w, layer, x, mod3, row_of_b, gate_idx, tm, tn):
    bsz, t, d = x.shape
    tm = min(tm, t)
    y_spec = pl.BlockSpec((None, tm, GROUP_W), lambda b, m, j: (b, m, 0))
    return pl.pallas_call(
        _out_proj_kernel,
        grid=(bsz, t // tm, d // tn),
        in_specs=[y_spec, y_spec, y_spec, y_spec,
                  pl.BlockSpec((None, 4 * GROUP_W, tn), lambda b, m, j: (layer, 0, j),
                               pipeline_mode=pl.Buffered(1) if tn == d else None),
                  pl.BlockSpec((None, tm, tn), lambda b, m, j: (b, m, j)),
                  pl.BlockSpec((None, 1, tn), lambda b, m, j: (row_of_b(b), 0, gate_idx * (d // tn) + j))],
        out_specs=pl.BlockSpec((None, tm, tn), lambda b, m, j: (b, m, j)),
        out_shape=jax.ShapeDtypeStruct(x.shape, F32),
        compiler_params=_cparams(("parallel", "parallel", "parallel")),
        name="out_proj",
    )(*ys, w, x, mod3)


def _final_norm_kernel(x_ref, g_ref, o_ref):
    xf = x_ref[...]
    ms = jnp.mean(xf * xf, axis=-1, keepdims=True)
    o_ref[...] = xf * lax.rsqrt(ms + EPS) * g_ref[...]


def _final_norm(x, g, tm=1024):
    bsz, t, d = x.shape
    tm = min(tm, t)
    return pl.pallas_call(
        _final_norm_kernel,
        grid=(bsz, t // tm),
        in_specs=[pl.BlockSpec((None, tm, d), lambda b, m: (b, m, 0)),
                  pl.BlockSpec((1, d), lambda b, m: (0, 0))],
        out_specs=pl.BlockSpec((None, tm, d), lambda b, m: (b, m, 0)),
        out_shape=jax.ShapeDtypeStruct(x.shape, F32),
        compiler_params=_cparams(("parallel", "parallel")),
        name="final_norm",
    )(x, g.reshape(1, d))


WIN_QBLOCKS = 4


def _rope(x, cos, sin_signed, lane_lo):
    w = x.shape[-1]
    partner = jnp.where(lane_lo, pltpu.roll(x, w - 16, 1), pltpu.roll(x, 16, 1))
    return x * cos + partner * sin_signed


def _win_attn_kernel(sink_ref, q_ref, kc_ref, vc_ref, *rest, seq, local):
    if local:
        k_ref, v_ref, cos_ref, sin_ref, mask_ref, o_ref = rest
    else:
        (o_ref,) = rest
    nb = seq // BLOCK
    lane = lax.broadcasted_iota(jnp.int32, (BLOCK, 128), 1)
    lo = lane < HEAD_DIM
    lane_lo = (lane % 32) < 16
    lane3_lo = (lax.broadcasted_iota(jnp.int32, (3 * BLOCK, 128), 1) % 32) < 16
    nt = (((1,), (1,)), ((), ()))
    kc = kc_ref[...].astype(BF16)
    vc = vc_ref[...].astype(BF16)
    sink = jnp.concatenate(
        [jnp.full((BLOCK, 1), sink_ref[HEAD_ORDER[i]] * LOG2E, F32) for i in range(N_HEADS)], axis=0)
    nq = q_ref.shape[0] // BLOCK
    for sub in range(nq):
        n = pl.program_id(1) * nq + sub
        q = q_ref[sub * BLOCK:(sub + 1) * BLOCK, :].astype(F32) * (SCALE * LOG2E)
        if local:
            r0 = pl.multiple_of(n * BLOCK, BLOCK)
            cos_q = cos_ref[pl.ds(r0, BLOCK), :]
            sin_q = sin_ref[pl.ds(r0, BLOCK), :]
        rows = []
        for m in range(4):
            qm = q[:, 128 * m:128 * (m + 1)]
            if local:
                qm = _rope(qm, cos_q, sin_q, lane_lo)
            rows.append(jnp.where(lo, qm, 0.0))
            rows.append(jnp.where(lo, 0.0, qm))
        qbd = jnp.concatenate(rows, axis=0).astype(BF16)
        s_ctx = lax.dot_general(qbd, kc, nt, preferred_element_type=F32)
        mx = jnp.maximum(jnp.max(s_ctx, axis=-1, keepdims=True), sink)
        if local:
            start = pl.multiple_of(jnp.clip((n - 1) * BLOCK, 0, seq - 3 * BLOCK), BLOCK)
            kb = _rope(k_ref[pl.ds(start, 3 * BLOCK), :].astype(F32), cos_ref[pl.ds(start, 3 * BLOCK), :],
                       sin_ref[pl.ds(start, 3 * BLOCK), :], lane3_lo).astype(BF16)
            vb = v_ref[pl.ds(start, 3 * BLOCK), :].astype(BF16)
            mask = mask_ref[jnp.where(n == 0, 0, jnp.where(n == nb - 1, 2, 1))]
            s_loc = lax.dot_general(qbd, kb, nt, preferred_element_type=F32)
            s_loc = (s_loc.reshape(N_HEADS, BLOCK, 3 * BLOCK) + mask[None]).reshape(s_loc.shape)
            mx = jnp.maximum(mx, jnp.max(s_loc, axis=-1, keepdims=True))
        p_ctx = jnp.exp2(s_ctx - mx)
        den = jnp.sum(p_ctx, axis=-1, keepdims=True) + jnp.exp2(sink - mx)
        acc = jnp.dot(p_ctx.astype(BF16), vc, preferred_element_type=F32)
        if local:
            p_loc = jnp.exp2(s_loc - mx)
            den = den + jnp.sum(p_loc, axis=-1, keepdims=True)
            acc = acc + jnp.dot(p_loc.astype(BF16), vb, preferred_element_type=F32)
        o = acc / den
        outs = [jnp.where(lo, o[(2 * m) * BLOCK:(2 * m + 1) * BLOCK], o[(2 * m + 1) * BLOCK:(2 * m + 2) * BLOCK])
                for m in range(4)]
        o_ref[sub * BLOCK:(sub + 1) * BLOCK, :] = jnp.concatenate(outs, axis=1).astype(o_ref.dtype)


def _win_attn(sink, pq, pc, cos_t, sin_t, local):
    bsz, t, _ = pq.shape
    lc = pc.shape[1]
    tq = min(WIN_QBLOCKS * BLOCK, t)
    in_specs = [
        pl.BlockSpec(memory_space=pltpu.SMEM),
        pl.BlockSpec((None, tq, 512), lambda b, n: (b, n, COL_QA // 512)),
        pl.BlockSpec((None, lc, 128), lambda b, n: (b, 0, COL_KA // 128)),
        pl.BlockSpec((None, lc, 128), lambda b, n: (b, 0, COL_VA // 128)),
    ]
    args = [sink, pq, pc, pc]
    if local:
        in_specs += [
            pl.BlockSpec((None, t, 128), lambda b, n: (b, 0, COL_KA // 128)),
            pl.BlockSpec((None, t, 128), lambda b, n: (b, 0, COL_VA // 128)),
            pl.BlockSpec((t, 128), lambda b, n: (0, 0)),
            pl.BlockSpec((t, 128), lambda b, n: (0, 0)),
            pl.BlockSpec((3, BLOCK, 3 * BLOCK), lambda b, n: (0, 0, 0)),
        ]
        args += [pq, pq, cos_t, sin_t, _window_masks(t)]
    return pl.pallas_call(
        functools.partial(_win_attn_kernel, seq=t, local=local),
        grid=(bsz, t // tq),
        in_specs=in_specs,
        out_specs=pl.BlockSpec((None, tq, GROUP_W), lambda b, n: (b, n, 0)),
        out_shape=jax.ShapeDtypeStruct((bsz, t, GROUP_W), BF16),
        compiler_params=_cparams(("parallel", "arbitrary")),
        name="win_attn" if local else "ctx_attn_a",
    )(*args)


def _window_masks(seq):
    nb = seq // BLOCK
    qi = np.arange(BLOCK)[:, None]
    kj = np.arange(3 * BLOCK)[None, :]
    tabs = []
    for n in (0, 1, nb - 1):
        start = int(np.clip((n - 1) * BLOCK, 0, seq - 3 * BLOCK))
        rel = (start + kj) - (n * BLOCK + qi)
        tabs.append(np.where(np.abs(rel) <= WINDOW, 0.0, NEG))
    return jnp.asarray(np.stack(tabs), F32)


def _rope_tables(seq):
    t = np.arange(seq)
    row, col = t // GRID_W, t % GRID_W
    quarter = HEAD_DIM // 4
    inv = ROPE_BASE ** (-np.arange(quarter, dtype=np.float64) / quarter)
    inv = inv.astype(np.float32).astype(np.float64)
    lane = np.arange(128)
    j = lane % HEAD_DIM
    pos = np.where((j < HEAD_DIM // 2)[None, :], row[:, None], col[:, None]).astype(np.float64)
    ang = (pos * inv[j % quarter][None, :]).astype(np.float32)
    cos = np.cos(ang.astype(np.float64))
    sin = np.sin(ang.astype(np.float64))
    sign = np.where((j % 32) < 16, -1.0, 1.0)[None, :]
    return jnp.asarray(cos, F32), jnp.asarray(sin * sign, F32)


NA_ROWS_PER_STEP = 4


def _na_kernel(var_ref, ws_ref, q_ref, kc_ref, vc_ref, *rest, local, win_rows):
    del var_ref
    if local:
        k_ref, v_ref, bias_ref, o_ref = rest
    else:
        (o_ref,) = rest
    g = pl.program_id(1)
    tq = q_ref.shape[0]
    q = q_ref[...].astype(F32) * (SCALE * LOG2E)
    head = lax.broadcasted_iota(jnp.int32, (tq, 256), 1) // HEAD_DIM
    nt = (((1,), (1,)), ((), ()))
    outs = []
    for half in range(2):
        cols = slice(256 * half, 256 * (half + 1))
        q4 = q[:, cols]
        qbd = jnp.concatenate([jnp.where(head == h, q4, 0.0) for h in range(4)], axis=0).astype(BF16)
        kc4 = kc_ref[:, cols].astype(BF16)
        vc4 = vc_ref[:, cols].astype(BF16)
        s_ctx = lax.dot_general(qbd, kc4, nt, preferred_element_type=F32)
        mx = jnp.max(s_ctx, axis=-1, keepdims=True)
        if local:
            nkey = win_rows * GRID_W
            start = pl.multiple_of(ws_ref[g] * GRID_W, GRID_W)
            k4 = k_ref[pl.ds(start, nkey), cols].astype(BF16)
            v4 = v_ref[pl.ds(start, nkey), cols].astype(BF16)
            bias = bias_ref[4 * half:4 * half + 4].astype(F32).reshape(4 * tq, nkey)
            s_loc = lax.dot_general(qbd, k4, nt, preferred_element_type=F32) + bias
            mx = jnp.maximum(mx, jnp.max(s_loc, axis=-1, keepdims=True))
        p_ctx = jnp.exp2(s_ctx - mx)
        den = jnp.sum(p_ctx, axis=-1, keepdims=True)
        acc = jnp.dot(p_ctx.astype(BF16), vc4, preferred_element_type=F32)
        if local:
            p_loc = jnp.exp2(s_loc - mx)
            den = den + jnp.sum(p_loc, axis=-1, keepdims=True)
            acc = acc + jnp.dot(p_loc.astype(BF16), v4, preferred_element_type=F32)
        o = acc / den
        o4 = jnp.where(head == 0, o[0:tq], 0.0)
        for h in range(1, 4):
            o4 = o4 + jnp.where(head == h, o[h * tq:(h + 1) * tq], 0.0)
        outs.append(o4)
    o_ref[...] = jnp.concatenate(outs, axis=1).astype(o_ref.dtype)


def _na_plan(seq):
    rows = seq // GRID_W
    kr = min(NA_KR, rows)
    r_step = NA_ROWS_PER_STEP
    win_rows = min(r_step + kr, rows)
    n_groups = rows // r_step
    wstart = np.zeros(n_groups, np.int32)
    pats = []
    keys = {}
    var = np.zeros(n_groups, np.int32)
    for g in range(n_groups):
        r0 = g * r_step
        ws = int(np.clip(r0 - kr // 2, 0, rows - win_rows))
        wstart[g] = ws
        r = r0 + np.arange(r_step)
        rstart = np.clip(r - kr // 2, 0, rows - kr)
        krow = ws + np.arange(win_rows)
        valid = (krow[None, :] >= rstart[:, None]) & (krow[None, :] < rstart[:, None] + kr)
        roff = krow[None, :] - r[:, None] + NA_KR - 1
        key = (valid.tobytes(), np.where(valid, roff, 0).tobytes())
        if key not in keys:
            keys[key] = len(pats)
            pats.append((valid, np.where(valid, roff, 0)))
        var[g] = keys[key]
    return rows, win_rows, n_groups, wstart, var, pats


def _na_bias_tables(rpb, pats):
    cq = np.arange(GRID_W)
    ck = np.arange(GRID_W)
    cstart = np.clip(cq - NA_KC // 2, 0, GRID_W - NA_KC)
    col_valid = (ck[None] >= cstart[:, None]) & (ck[None] < cstart[:, None] + NA_KC)
    coff = np.clip(ck[None] - cq[:, None], -(NA_KC - 1), NA_KC - 1) + NA_KC - 1
    by_col = jnp.where(col_valid, rpb[..., coff] * LOG2E, NEG).astype(BF16)
    depth, heads, n_ro = by_col.shape[:3]
    flat = by_col.transpose(0, 1, 3, 2, 4).reshape(depth, heads, GRID_W, n_ro * GRID_W)
    tabs = []
    for valid, roff in pats:
        r_step, win_rows = valid.shape
        q_rows = []
        for i in range(r_step):
            a_ok = np.nonzero(valid[i])[0]
            a0, a1 = int(a_ok[0]), int(a_ok[-1]) + 1
            assert valid[i, a0:a1].all() and (np.diff(roff[i, a0:a1]) == 1).all()
            seg = flat[..., int(roff[i, a0]) * GRID_W:(int(roff[i, a0]) + a1 - a0) * GRID_W]
            q_rows.append(jnp.pad(seg, ((0, 0), (0, 0), (0, 0), (a0 * GRID_W, (win_rows - a1) * GRID_W)),
                                  constant_values=NEG))
        tabs.append(jnp.concatenate(q_rows, axis=-2))
    return jnp.stack(tabs, axis=1)


def _na_attn(pq, pc, bias_tabs, layer, plan, local):
    bsz, t, _ = pq.shape
    lc = pc.shape[1]
    if local:
        rows, win_rows, n_groups, wstart, var, _ = plan
        tq = NA_ROWS_PER_STEP * GRID_W
    else:
        win_rows, n_groups, tq = 0, 1, t
        wstart = np.zeros(1, np.int32)
        var = np.zeros(1, np.int32)
    in_specs = [
        pl.BlockSpec((None, tq, 512), lambda b, g, vr, ws: (b, g, COL_NQ // 512)),
        pl.BlockSpec((None, lc, 512), lambda b, g, vr, ws: (b, 0, COL_NK // 512)),
        pl.BlockSpec((None, lc, 512), lambda b, g, vr, ws: (b, 0, COL_NV // 512)),
    ]
    args = [pq, pc, pc]
    if local:
        in_specs += [
            pl.BlockSpec((None, t, 512), lambda b, g, vr, ws: (b, 0, COL_NK // 512)),
            pl.BlockSpec((None, t, 512), lambda b, g, vr, ws: (b, 0, COL_NV // 512)),
            pl.BlockSpec((None, None, N_HEADS, tq, win_rows * GRID_W),
                         lambda b, g, vr, ws: (layer, vr[g], 0, 0, 0)),
        ]
        args += [pq, pq, bias_tabs]
    grid_spec = pltpu.PrefetchScalarGridSpec(
        num_scalar_prefetch=2,
        grid=(bsz, n_groups),
        in_specs=in_specs,
        out_specs=pl.BlockSpec((None, tq, GROUP_W), lambda b, g, vr, ws: (b, g, 0)),
    )
    return pl.pallas_call(
        functools.partial(_na_kernel, local=local, win_rows=win_rows),
        grid_spec=grid_spec,
        out_shape=jax.ShapeDtypeStruct((bsz, t, GROUP_W), BF16),
        compiler_params=_cparams(("parallel", "arbitrary")),
        name="na_attn" if local else "ctx_attn_c",
    )(jnp.asarray(var), jnp.asarray(wstart), *args)


SSD_CHUNKS = 4


def _softplus(x):
    return jnp.maximum(x, 0.0) + jnp.log(1.0 + jnp.exp(-jnp.abs(x)))


def _bf16_parts(x, n):
    parts = []
    for _ in range(n):
        part = x.astype(BF16)
        parts.append(part)
        x = x - part.astype(F32)
    return parts


def _conv3_silu(cur, prev_blk, next_blk, w_ref, b_ref, has_prev, has_next):
    x = cur.astype(F32)
    rows = x.shape[0]
    prev_row = prev_blk.astype(F32)[HALO - 1:HALO, :] * has_prev
    next_row = next_blk.astype(F32)[0:1, :] * has_next
    ri = lax.broadcasted_iota(jnp.int32, x.shape, 0)
    up = jnp.where(ri == 0, prev_row, pltpu.roll(x, 1, 0))
    dn = jnp.where(ri == rows - 1, next_row, pltpu.roll(x, rows - 1, 0))
    u = up * w_ref[0:1, :] + x * w_ref[1:2, :] + dn * w_ref[2:3, :] + b_ref[...]
    return _silu(u)


def _ssd_chunk(xs, bc, dt, a, st_ref, keep, tri, expand, head4, d_off, reverse):
    L = SSM_CHUNK
    nt = (((1,), (1,)), ((), ()))
    a_t = a.T
    c_col = sum(jnp.dot(tri, part, preferred_element_type=F32) for part in _bf16_parts(a, 3))
    c_row = sum(lax.dot_general(part, tri, nt, preferred_element_type=F32) for part in _bf16_parts(a_t, 3))
    c_exp = sum(jnp.dot(part, expand, preferred_element_type=F32) for part in _bf16_parts(c_col, 2))
    dt_exp = sum(jnp.dot(part, expand, preferred_element_type=F32) for part in _bf16_parts(dt, 2))
    end = 0 if reverse else L - 1
    cend = c_exp[end:end + 1, :]
    x_dt = xs * dt_exp
    out_decay = jnp.exp(c_exp)
    x_dec = x_dt * jnp.exp(cend - c_exp)
    chunk_decay = jnp.exp(cend)

    ys = []
    for g in range(2):
        gl = slice(256 * g, 256 * (g + 1))
        b_g = bc[:, 128 * g:128 * (g + 1)]
        c_g = bc[:, 256 + 128 * g:256 + 128 * (g + 1)].astype(BF16)
        cb = lax.dot_general(c_g, b_g.astype(BF16), nt, preferred_element_type=F32)
        ms = []
        for hh in range(4):
            j = d_off + 4 * g + hh
            diff = c_col[:, j:j + 1] - c_row[j:j + 1, :]
            ms.append(cb * jnp.exp(jnp.where(keep, diff, NEG)))
        m_g = jnp.concatenate(ms, axis=0).astype(BF16)
        o = jnp.dot(m_g, x_dt[:, gl].astype(BF16), preferred_element_type=F32)
        y_diag = jnp.where(head4 == 0, o[0:L], 0.0)
        for hh in range(1, 4):
            y_diag = y_diag + jnp.where(head4 == hh, o[hh * L:(hh + 1) * L], 0.0)
        st = st_ref[g]
        y_off = jnp.dot(c_g, st.astype(BF16), preferred_element_type=F32) * out_decay[:, gl]
        ys.append(y_diag + y_off)
        st_ref[g] = chunk_decay[:, gl] * st + jnp.dot(
            b_g.T.astype(BF16), x_dec[:, gl].astype(BF16), preferred_element_type=F32)
    return jnp.concatenate(ys, axis=1)


def _ssd_kernel(*refs, reverse, nc):
    finalize = reverse
    if reverse:
        (xsc_ref, bcc_ref, dt_ref, dtb_ref, alog_ref, h0_ref,
         z_ref, yf_ref, dskip_ref, nw_ref, y_ref, ht_ref, st_ref) = refs
    else:
        (xs_ref, bc_ref, xsp_ref, xsn_ref, bcp_ref, bcn_ref, dt_ref, cwx_ref, cbx_ref, cwb_ref, cbb_ref,
         dtb_ref, alog_ref, h0_ref, y_ref, ht_ref, xsc_ref, bcc_ref, st_ref) = refs
    c = pl.program_id(1)
    cid = (nc - 1 - c) if reverse else c
    d_off = 8 if reverse else 0
    L = SSM_CHUNK
    n_sub = dt_ref.shape[0] // L

    @pl.when(c == 0)
    def _():
        st_ref[...] = h0_ref[...]

    if reverse:
        xs = xsc_ref[...]
        bc = bcc_ref[...].astype(F32)
    else:
        has_prev = jnp.where(cid > 0, 1.0, 0.0).astype(F32)
        has_next = jnp.where(cid < nc - 1, 1.0, 0.0).astype(F32)
        xs = _conv3_silu(xs_ref[...], xsp_ref[...], xsn_ref[...], cwx_ref, cbx_ref, has_prev, has_next)
        bc = _conv3_silu(bc_ref[...], bcp_ref[...], bcn_ref[...], cwb_ref, cbb_ref, has_prev, has_next)
        xsc_ref[...] = xs
        bcc_ref[...] = bc.astype(BF16)

    dt_all = _softplus(dt_ref[...].astype(F32) + dtb_ref[...])
    a_all = dt_all * (-jnp.exp(alog_ref[...]))
    ri = lax.broadcasted_iota(jnp.int32, (L, L), 0)
    ci = lax.broadcasted_iota(jnp.int32, (L, L), 1)
    keep = (ci >= ri) if reverse else (ci <= ri)
    tri = keep.astype(BF16)
    nt = (((1,), (1,)), ((), ()))
    ej = lax.broadcasted_iota(jnp.int32, (128, GROUP_W), 0)
    eh = lax.broadcasted_iota(jnp.int32, (128, GROUP_W), 1) // HEAD_DIM
    expand = (ej == eh + d_off).astype(BF16)
    head4 = lax.broadcasted_iota(jnp.int32, (L, 256), 1) // HEAD_DIM
    xs_all, bc_all = xs, bc
    y_sub = [None] * n_sub
    for sub in (range(n_sub - 1, -1, -1) if reverse else range(n_sub)):
        y_sub[sub] = _ssd_chunk(xs_all[sub * L:(sub + 1) * L], bc_all[sub * L:(sub + 1) * L],
                                dt_all[sub * L:(sub + 1) * L], a_all[sub * L:(sub + 1) * L],
                                st_ref, keep, tri, expand, head4, d_off, reverse)
    y = jnp.concatenate(y_sub, axis=0) if n_sub > 1 else y_sub[0]
    if finalize:
        y = y + yf_ref[...] + xs * dskip_ref[...]
        y = y * _silu(z_ref[...].astype(F32))
        halves = []
        for g in range(2):
            yg = y[:, 256 * g:256 * (g + 1)]
            halves.append(yg * lax.rsqrt(jnp.mean(yg * yg, axis=-1, keepdims=True) + EPS))
        y = jnp.concatenate(halves, axis=1) * nw_ref[...]
    y_ref[...] = y.astype(y_ref.dtype)

    @pl.when(c == nc - 1)
    def _():
        ht_ref[...] = st_ref[...]


def _ssd_direction(p, consts, h0, reverse, fwd=None):
    cwx, cbx, cwb, cbb, dtb, alog, dskip, nw = consts
    bsz, t, _ = p.shape
    rows = min(SSD_CHUNKS * SSM_CHUNK, t)
    nc = t // rows
    hb = rows // HALO
    nhalo = t // HALO

    def cid(c):
        return (nc - 1 - c) if reverse else c

    def cur(col, width):
        return pl.BlockSpec((None, rows, width), lambda b, c: (b, cid(c), col // width))

    def prev(col, width):
        return pl.BlockSpec((None, HALO, width), lambda b, c: (b, jnp.maximum(cid(c) * hb - 1, 0), col // width))

    def nxt(col, width):
        return pl.BlockSpec((None, HALO, width), lambda b, c: (b, jnp.minimum((cid(c) + 1) * hb, nhalo - 1), col // width))

    def const(arr):
        return pl.BlockSpec(arr.shape, lambda b, c: (0,) * arr.ndim)

    state_spec = pl.BlockSpec((None, 2, SSM_STATE, 256), lambda b, c: (b, 0, 0, 0))
    chunk_spec = pl.BlockSpec((None, rows, GROUP_W), lambda b, c: (b, cid(c), 0))
    state_shape = jax.ShapeDtypeStruct((bsz, 2, SSM_STATE, 256), F32)
    if reverse:
        yf, xs_act, bc_act = fwd
        in_specs = [chunk_spec, chunk_spec, cur(COL_DT, 128), const(dtb), const(alog), state_spec,
                    cur(COL_SZ, 512), chunk_spec, const(dskip), const(nw)]
        args = [xs_act, bc_act, p, dtb, alog, h0, p, yf, dskip, nw]
        out_specs = [chunk_spec, state_spec]
        out_shape = [jax.ShapeDtypeStruct((bsz, t, GROUP_W), BF16), state_shape]
    else:
        in_specs = [cur(COL_SX, 512), cur(COL_SBC, 512), prev(COL_SX, 512), nxt(COL_SX, 512),
                    prev(COL_SBC, 512), nxt(COL_SBC, 512), cur(COL_DT, 128),
                    const(cwx), const(cbx), const(cwb), const(cbb), const(dtb), const(alog), state_spec]
        args = [p, p, p, p, p, p, p, cwx, cbx, cwb, cbb, dtb, alog, h0]
        out_specs = [chunk_spec, state_spec, chunk_spec, chunk_spec]
        out_shape = [jax.ShapeDtypeStruct((bsz, t, GROUP_W), F32), state_shape,
                     jax.ShapeDtypeStruct((bsz, t, GROUP_W), F32), jax.ShapeDtypeStruct((bsz, t, GROUP_W), BF16)]
    return pl.pallas_call(
        functools.partial(_ssd_kernel, reverse=reverse, nc=nc),
        grid=(bsz, nc),
        in_specs=in_specs,
        out_specs=out_specs,
        out_shape=out_shape,
        scratch_shapes=[pltpu.VMEM((2, SSM_STATE, 256), F32)],
        compiler_params=_cparams(("parallel", "arbitrary")),
        name="ssd_rev" if reverse else "ssd_fwd",
    )(*args)


def _ssd_consts(conv_w, conv_b, dt_bias, a_log, d_skip, norm_w):
    cwx, cwb = conv_w[:, :GROUP_W], conv_w[:, GROUP_W:]
    cbx, cbb = conv_b[:GROUP_W].reshape(1, -1), conv_b[GROUP_W:].reshape(1, -1)
    pad = lambda v: jnp.pad(v.reshape(1, -1), ((0, 0), (0, 128 - v.size)))
    return (cwx, cbx, cwb, cbb, pad(dt_bias), pad(a_log),
            jnp.repeat(d_skip, HEAD_DIM).reshape(1, -1), norm_w.reshape(1, -1))


def _ssd_mixer(p, pc, consts):
    bsz = p.shape[0]
    zero = jnp.zeros((bsz, 2, SSM_STATE, 256), F32)
    ycf, hf, *act_c = _ssd_direction(pc, consts, zero, False)
    yc, hb = _ssd_direction(pc, consts, zero, True, fwd=(ycf, *act_c))
    ylf, _, *act_l = _ssd_direction(p, consts, hf, False)
    yl, _ = _ssd_direction(p, consts, hb, True, fwd=(ylf, *act_l))
    return yl, yc


HY_CB = 128


def _hy_cb(seq):
    del seq
    return HY_CB
HY_UNROLL = 16


def _hy_dims(seq):
    n = 2 * seq
    n1 = {4096: 128, 1024: 64, 512: 32, 256: 32, 128: 16}[seq]
    n2 = n // n1
    h = n1 // 2
    return dict(n=n, n1=n1, n2=n2, h=h, pa=h + 8, pb=n2 + 8, pc=n1 + 8)


def _hy_matrices(seq):
    d = _hy_dims(seq)
    n, n1, n2, h = d["n"], d["n1"], d["n2"], d["h"]

    def cis(num, den, sign):
        ang = (2.0 * math.pi / den) * (num % den).astype(F32)
        return jnp.cos(ang), sign * jnp.sin(ang)

    k1 = jnp.arange(n1, dtype=jnp.int32)
    nn = n2 * jnp.arange(n1, dtype=jnp.int32)[None, None, :] + jnp.arange(n2, dtype=jnp.int32)[:, None, None]
    e1r, e1i = cis(k1[None, :, None] * nn, n, -1.0)
    m1f = jnp.concatenate([e1r, e1i], axis=1)
    m1 = jnp.concatenate([jnp.concatenate([e1r[..., :h], -e1i[..., :h]], axis=2),
                          jnp.concatenate([e1i[..., :h], e1r[..., :h]], axis=2)], axis=1)
    a2 = jnp.arange(n2, dtype=jnp.int32)
    g2r, g2i = cis(a2[:, None] * a2[None, :], n2, -1.0)
    m2 = jnp.concatenate([jnp.concatenate([g2r, -g2i], axis=1),
                          jnp.concatenate([g2i, g2r], axis=1)], axis=0)
    num3 = (a2[None, :, None] * a2[None, None, :] * n1 + k1[:, None, None] * a2[None, :, None])
    e3r, e3i = cis(num3, n, 1.0)
    m3 = jnp.concatenate([jnp.concatenate([e3r, -e3i], axis=2),
                          jnp.concatenate([e3i, e3r], axis=2)], axis=1)
    hh = jnp.arange(h, dtype=jnp.int32)
    d4r, d4i = cis(hh[:, None] * k1[None, :], n1, 1.0)
    m4 = jnp.concatenate([jnp.concatenate([d4r, -d4i], axis=1),
                          jnp.concatenate([d4i, d4r], axis=1)], axis=0) / n
    return tuple(m.astype(BF16) for m in (m1f, m1, m2, m3, m4))


def _hy_prep_kernel(p_ref, w_ref, b_ref, o_ref, *, dims):
    n2, h, pa = dims["n2"], dims["h"], dims["pa"]
    seq = h * n2
    for j in range(n2):
        o_ref[j * pa + h:(j + 1) * pa, :] = jnp.zeros((pa - h, o_ref.shape[-1]), o_ref.dtype)
    w0, w1, w2, bias = w_ref[0:1, :], w_ref[1:2, :], w_ref[2:3, :], b_ref[...]
    ri = lax.broadcasted_iota(jnp.int32, (n2, o_ref.shape[-1]), 0)

    def body(i, carry):
        r0 = pl.multiple_of(i * n2, n2)
        x = p_ref[pl.ds(r0, n2), :].astype(F32)
        pstart = pl.multiple_of(jnp.maximum(r0 - HALO, 0), HALO)
        nstart = pl.multiple_of(jnp.minimum(r0 + n2, seq - HALO), HALO)
        prev_row = p_ref[pl.ds(pstart, HALO), :].astype(F32)[HALO - 1:HALO, :] * jnp.where(i > 0, 1.0, 0.0).astype(F32)
        next_row = p_ref[pl.ds(nstart, HALO), :].astype(F32)[0:1, :] * jnp.where(i < h - 1, 1.0, 0.0).astype(F32)
        up = jnp.where(ri == 0, prev_row, pltpu.roll(x, 1, 0))
        dn = jnp.where(ri == n2 - 1, next_row, pltpu.roll(x, n2 - 1, 0))
        o_ref[pl.ds(i, n2, stride=pa), :] = up * w0 + x * w1 + dn * w2 + bias
        return carry

    lax.fori_loop(0, h, body, 0)


def _hy_prep(p, short_w, short_b):
    bsz, t, _ = p.shape
    dims = _hy_dims(t)
    rows = dims["n2"] * dims["pa"]
    cb = _hy_cb(t)
    ncb = 3 * GROUP_W // cb
    return pl.pallas_call(
        functools.partial(_hy_prep_kernel, dims=dims),
        grid=(bsz, ncb),
        in_specs=[pl.BlockSpec((None, t, cb), lambda b, j: (b, 0, COL_HY // cb + j)),
                  pl.BlockSpec((3, cb), lambda b, j: (0, j)),
                  pl.BlockSpec((1, cb), lambda b, j: (0, j))],
        out_specs=pl.BlockSpec((None, rows, cb), lambda b, j: (b, 0, j)),
        out_shape=jax.ShapeDtypeStruct((bsz, rows, 3 * GROUP_W), F32),
        compiler_params=_cparams(("parallel", "parallel")),
        name="hy_prep",
    )(p, short_w, short_b.reshape(1, -1))


def _hy_filter_kernel(z_ref, w1_ref, b1_ref, w2_ref, b2_ref, w3_ref, b3_ref, w4_ref, fr_ref, dl_ref, o_ref):
    hi = lax.Precision.HIGHEST
    z = z_ref[...]
    fr = fr_ref[...]
    h = jnp.sin(fr * (jnp.dot(z, w1_ref[...], preferred_element_type=F32, precision=hi) + b1_ref[...]))
    h = jnp.sin(fr * (jnp.dot(h, w2_ref[...], preferred_element_type=F32, precision=hi) + b2_ref[...]))
    h = jnp.sin(fr * (jnp.dot(h, w3_ref[...], preferred_element_type=F32, precision=hi) + b3_ref[...]))
    full = jnp.dot(h.astype(BF16), w4_ref[...].astype(BF16), preferred_element_type=F32)
    t = z[:, 0:1]
    is_bwd = z[:, HY_EMB:HY_EMB + 1] > 0.5
    live = z[:, HY_EMB + 1:HY_EMB + 2]
    decay = jnp.exp(-t * jnp.abs(dl_ref[...])) * live
    for o in range(2):
        fwd = full[:, o * 2 * GROUP_W:o * 2 * GROUP_W + GROUP_W]
        bwd = full[:, o * 2 * GROUP_W + GROUP_W:(o + 1) * 2 * GROUP_W]
        o_ref[o] = jnp.where(is_bwd, bwd, fwd) * decay


def _hy_filter_features(seq):
    d = _hy_dims(seq)
    n, n1, n2 = d["n"], d["n1"], d["n2"]
    row = np.arange(n)
    time = n2 * (row % n1) + row // n1
    is_bwd = time > seq
    pos = np.where(is_bwd, n - time, time)
    live = (time != seq).astype(np.float64)
    pos = np.where(time == seq, 0, pos)
    t = np.linspace(0.0, 1.0, seq)[pos]
    bands = (HY_EMB - 1) // 2
    f = np.linspace(1e-4, bands - 1, bands)[None]
    wpos = (2.0 * math.pi * pos / seq)[:, None]
    feat = np.zeros((n, 128), np.float64)
    feat[:, 0] = t
    feat[:, 1:1 + bands] = np.cos(f * wpos)
    feat[:, 1 + bands:HY_EMB] = -np.sin(f * wpos)
    feat[:, HY_EMB] = is_bwd
    feat[:, HY_EMB + 1] = live
    return jnp.asarray(feat, F32)


def _hy_filter(seq, w1, b1, w2, b2, w3, b3, w4, freq, tr=512):
    d = _hy_dims(seq)
    n = d["n"]
    tr = min(tr, n)
    feat = _hy_filter_features(seq)
    w1p = jnp.pad(w1, ((0, 128 - HY_EMB), (0, 0)))
    max_decay = math.log(HY_DECAY_TARGET) / HY_FAST_PCT
    min_decay = math.log(HY_DECAY_TARGET) / HY_SLOW_PCT
    deltas = jnp.linspace(min_decay, max_decay, GROUP_W, dtype=F32).reshape(1, -1)
    row = lambda v: v.reshape(1, -1)
    const = lambda a: pl.BlockSpec(a.shape, lambda i: (0,) * a.ndim)
    args = [w1p, row(b1), w2, row(b2), w3, row(b3), w4, row(freq), deltas]
    return pl.pallas_call(
        _hy_filter_kernel,
        grid=(n // tr,),
        in_specs=[pl.BlockSpec((tr, 128), lambda i: (i, 0))] + [const(a) for a in args],
        out_specs=pl.BlockSpec((2, tr, GROUP_W), lambda i: (0, i, 0)),
        out_shape=jax.ShapeDtypeStruct((2, n, GROUP_W), F32),
        compiler_params=_cparams(("parallel",)),
        name="hy_filter",
    )(feat, *args)


def _hy_spectrum_kernel(k_ref, m1f_ref, m2_ref, re_ref, im_ref, tr_ref, ti_ref, *, dims):
    n1, n2, pb = dims["n1"], dims["n2"], dims["pb"]

    def stage1(j, carry):
        r0 = pl.multiple_of(j * n1, n1)
        a = jnp.dot(m1f_ref[j], k_ref[pl.ds(r0, n1), :].astype(BF16), preferred_element_type=F32)
        tr_ref[pl.ds(j, n1, stride=pb), :] = a[:n1]
        ti_ref[pl.ds(j, n1, stride=pb), :] = a[n1:]
        return carry

    lax.fori_loop(0, n2, stage1, 0, unroll=HY_UNROLL)

    def stage2(k, carry):
        r0 = pl.multiple_of(k * pb, 8)
        rhs = jnp.concatenate([tr_ref[pl.ds(r0, n2), :], ti_ref[pl.ds(r0, n2), :]], axis=0).astype(BF16)
        x = jnp.dot(m2_ref[...], rhs, preferred_element_type=F32)
        o0 = pl.multiple_of(k * n2, n2)
        re_ref[pl.ds(o0, n2), :] = x[:n2].astype(re_ref.dtype)
        im_ref[pl.ds(o0, n2), :] = x[n2:].astype(im_ref.dtype)
        return carry

    lax.fori_loop(0, n1, stage2, 0, unroll=HY_UNROLL)


def _hy_spectrum(kern, mats, seq):
    d = _hy_dims(seq)
    n, n1, pb = d["n"], d["n1"], d["pb"]
    m1f, _, m2, _, _ = mats
    cb = _hy_cb(seq)
    ncb = GROUP_W // cb
    blk = pl.BlockSpec((None, n, cb), lambda o, j: (o, 0, j))
    const = lambda a: pl.BlockSpec(a.shape, lambda o, j: (0,) * a.ndim)
    return pl.pallas_call(
        functools.partial(_hy_spectrum_kernel, dims=d),
        grid=(2, ncb),
        in_specs=[blk, const(m1f), const(m2)],
        out_specs=[blk, blk],
        out_shape=[jax.ShapeDtypeStruct((2, n, GROUP_W), BF16)] * 2,
        scratch_shapes=[pltpu.VMEM((n1 * pb, cb), F32)] * 2,
        compiler_params=_cparams(("parallel", "parallel")),
        name="hy_spectrum",
    )(kern, m1f, m2)


def _hy_conv_kernel(u_ref, g_ref, kr_ref, ki_ref, skip_ref, m1_ref, m2_ref, m3_ref, m4_ref, o_ref,
                    t1r, t1i, t2r, t2i, *, dims, natural_out):
    n1, n2, h, pa, pb, pc = (dims[k] for k in ("n1", "n2", "h", "pa", "pb", "pc"))

    def fwd1(j, carry):
        r0 = pl.multiple_of(j * pa, 8)
        rhs = jnp.concatenate([u_ref[0, pl.ds(r0, h), :], u_ref[1, pl.ds(r0, h), :]], axis=0).astype(BF16)
        a = jnp.dot(m1_ref[j], rhs, preferred_element_type=F32)
        t1r[pl.ds(j, n1, stride=pb), :] = a[:n1]
        t1i[pl.ds(j, n1, stride=pb), :] = a[n1:]
        return carry

    lax.fori_loop(0, n2, fwd1, 0, unroll=HY_UNROLL)

    def mid(k, carry):
        r0 = pl.multiple_of(k * pb, 8)
        rhs = jnp.concatenate([t1r[pl.ds(r0, n2), :], t1i[pl.ds(r0, n2), :]], axis=0).astype(BF16)
        x = jnp.dot(m2_ref[...], rhs, preferred_element_type=F32)
        f0 = pl.multiple_of(k * n2, n2)
        fr, fi = kr_ref[pl.ds(f0, n2), :].astype(F32), ki_ref[pl.ds(f0, n2), :].astype(F32)
        xr, xi = x[:n2], x[n2:]
        y = jnp.concatenate([xr * fr - xi * fi, xr * fi + xi * fr], axis=0).astype(BF16)
        c = jnp.dot(m3_ref[k], y, preferred_element_type=F32)
        t2r[pl.ds(k, n2, stride=pc), :] = c[:n2]
        t2i[pl.ds(k, n2, stride=pc), :] = c[n2:]
        return carry

    lax.fori_loop(0, n1, mid, 0, unroll=HY_UNROLL)

    if not natural_out:
        o_ref[...] = jnp.zeros_like(o_ref)
    skip = skip_ref[...]

    def inv2(j, carry):
        r0 = pl.multiple_of(j * pc, 8)
        rhs = jnp.concatenate([t2r[pl.ds(r0, n1), :], t2i[pl.ds(r0, n1), :]], axis=0).astype(BF16)
        y = jnp.dot(m4_ref[...], rhs, preferred_element_type=F32)
        a0 = pl.multiple_of(j * pa, 8)
        for e in range(2):
            val = g_ref[e, pl.ds(a0, h), :] * (y[e * h:(e + 1) * h] + skip * u_ref[e, pl.ds(a0, h), :])
            if natural_out:
                o_ref[e, pl.ds(j, h, stride=n2), :] = val.astype(o_ref.dtype)
            else:
                o_ref[e, pl.ds(a0, h), :] = val
        return carry

    lax.fori_loop(0, n2, inv2, 0, unroll=HY_UNROLL)


def _hy_conv(u_arr, u_col, g_arr, g_col, kf_re, kf_im, order, skip, mats, seq, natural_out):
    d = _hy_dims(seq)
    bsz = u_arr.shape[0]
    rows = d["n2"] * d["pa"]
    _, m1, m2, m3, m4 = mats
    cb = _hy_cb(seq)
    ncb = GROUP_W // cb
    single = pl.Buffered(1)
    const = lambda a: pl.BlockSpec(a.shape, lambda j, p: (0,) * a.ndim, pipeline_mode=single)
    in_specs = [
        pl.BlockSpec((2, rows, cb), lambda j, p: (p, 0, u_col // cb + j)),
        pl.BlockSpec((2, rows, cb), lambda j, p: (p, 0, g_col // cb + j), pipeline_mode=single),
        pl.BlockSpec((None, d["n"], cb), lambda j, p: (order, 0, j), pipeline_mode=single),
        pl.BlockSpec((None, d["n"], cb), lambda j, p: (order, 0, j), pipeline_mode=single),
        pl.BlockSpec((None, 1, cb), lambda j, p: (order, 0, j)),
        const(m1), const(m2), const(m3), const(m4),
    ]
    if natural_out:
        out_spec = pl.BlockSpec((2, seq, cb), lambda j, p: (p, 0, j))
        out_shape = jax.ShapeDtypeStruct((bsz, seq, GROUP_W), F32)
    else:
        out_spec = pl.BlockSpec((2, rows, cb), lambda j, p: (p, 0, j))
        out_shape = jax.ShapeDtypeStruct((bsz, rows, GROUP_W), F32)
    return pl.pallas_call(
        functools.partial(_hy_conv_kernel, dims=d, natural_out=natural_out),
        grid=(ncb, bsz // 2),
        in_specs=in_specs,
        out_specs=out_spec,
        out_shape=out_shape,
        scratch_shapes=[pltpu.VMEM((d["n1"] * d["pb"], cb), F32)] * 2
        + [pltpu.VMEM((d["n2"] * d["pc"], cb), F32)] * 2,
        compiler_params=_cparams(("parallel", "parallel")),
        name="hy_conv",
    )(u_arr, g_arr, kf_re, kf_im, skip.reshape(2, 1, GROUP_W), m1, m2, m3, m4)


def _hyena_mixer(p, short_w, short_b, filt_params, skip, mats):
    seq = p.shape[1]
    kern = _hy_filter(seq, *filt_params)
    kf_re, kf_im = _hy_spectrum(kern, mats, seq)
    ut = _hy_prep(p, short_w, short_b)
    zt = _hy_conv(ut, 2 * GROUP_W, ut, 0, kf_re, kf_im, 0, skip, mats, seq, natural_out=False)
    return _hy_conv(zt, 0, ut, GROUP_W, kf_re, kf_im, 1, skip, mats, seq, natural_out=True)


CAST_BLOCK_BYTES = 8 * 1024 * 1024


def _cast_kernel(w_ref, o_ref):
    o_ref[...] = w_ref[...].astype(o_ref.dtype)


def _cast_bf16(w):
    depth, k, n = w.shape
    tr = min(k, CAST_BLOCK_BYTES // (4 * n))
    return pl.pallas_call(
        _cast_kernel,
        grid=(depth, k // tr),
        in_specs=[pl.BlockSpec((None, tr, n), lambda i, r: (i, r, 0))],
        out_specs=pl.BlockSpec((None, tr, n), lambda i, r: (i, r, 0)),
        out_shape=jax.ShapeDtypeStruct(w.shape, BF16),
        compiler_params=_cparams(("parallel", "parallel")),
        name="cast_bf16",
    )(w)


def _w_in_layout_kernel(w_ref, o_ref):
    o_a, o_dt = 768, 768 + 3072 + 1536
    for i, head in enumerate(HEAD_ORDER):
        o_ref[HEAD_DIM * i:HEAD_DIM * (i + 1), :] = w_ref[HEAD_DIM * head:HEAD_DIM * (head + 1), :].astype(o_ref.dtype)
    o_ref[COL_KA:COL_DT, :] = w_ref[COL_KA:o_a, :].astype(o_ref.dtype)
    o_ref[COL_DT:COL_DT + 16, :] = w_ref[o_dt:o_dt + 16, :].astype(o_ref.dtype)
    o_ref[COL_DT + 16:COL_HY, :] = jnp.zeros((COL_HY - COL_DT - 16, o_ref.shape[1]), o_ref.dtype)
    o_ref[COL_HY:, :] = w_ref[o_a:o_dt, :].astype(o_ref.dtype)


def _w_out_layout_kernel(w_ref, o_ref):
    for i, head in enumerate(HEAD_ORDER):
        o_ref[HEAD_DIM * i:HEAD_DIM * (i + 1), :] = w_ref[HEAD_DIM * head:HEAD_DIM * (head + 1), :].astype(o_ref.dtype)
    o_ref[GROUP_W:, :] = w_ref[GROUP_W:, :].astype(o_ref.dtype)


def _prep_weights(w_in, w_out):
    depth, d, n_in = w_in.shape
    tk = 256
    w_in_p = pl.pallas_call(
        _w_in_layout_kernel,
        grid=(depth, d // tk),
        in_specs=[pl.BlockSpec((None, n_in, tk), lambda i, c: (i, 0, c))],
        out_specs=pl.BlockSpec((None, N_PROJ, tk), lambda i, c: (i, 0, c)),
        out_shape=jax.ShapeDtypeStruct((depth, N_PROJ, d), BF16),
        compiler_params=_cparams(("parallel", "parallel")),
        name="w_in_layout",
    )(jnp.swapaxes(w_in, 1, 2))
    k_out, dm = w_out.shape[1:]
    tc = 512
    w_out_p = pl.pallas_call(
        _w_out_layout_kernel,
        grid=(depth, dm // tc),
        in_specs=[pl.BlockSpec((None, k_out, tc), lambda i, c: (i, 0, c))],
        out_specs=pl.BlockSpec((None, k_out, tc), lambda i, c: (i, 0, c)),
        out_shape=jax.ShapeDtypeStruct(w_out.shape, BF16),
        compiler_params=_cparams(("parallel", "parallel")),
        name="w_out_layout",
    )(w_out)
    return w_in_p, w_out_p


TILE_W_IN = (512, 2816)
TILE_MLP_UP = (1024, 2048)
TILE_DOWN = (1024, 1024, 2048)
TILE_OUT = (512, 2048)


def kernel(x, c, ctx, c_ctx, ada_w, ada_b, norm_mix, norm_mlp, w_in, w_out, attn_sink,
           hy_short_w, hy_short_b, hy_w1, hy_b1, hy_w2, hy_b2, hy_w3, hy_b3, hy_w4, hy_freq, hy_skip,
           na_rpb, ssm_conv_w, ssm_conv_b, ssm_dt_bias, ssm_a_log, ssm_d, ssm_norm,
           mlp_w1, mlp_w2, final_norm):
    bsz, seq, d = x.shape
    lc = ctx.shape[1]
    depth = ada_w.shape[0]
    assert bsz % 2 == 0 and bsz <= 7 and d == D_MODEL

    cs = jnp.zeros((8, d), F32).at[:bsz].set(c).at[bsz].set(c_ctx)
    mod = _ada_mod(cs, ada_w, ada_b)
    cos_t, sin_t = _rope_tables(seq)
    na_plan = _na_plan(seq)
    bias_tabs = _na_bias_tables(na_rpb, na_plan[-1])
    mats_l = _hy_matrices(seq)
    mats_c = _hy_matrices(lc)
    lat_row = lambda b: b
    ctx_row = lambda b: bsz
    tm_c = lc

    w_in_p, w_out_p = _prep_weights(w_in, w_out)
    w1_b = _cast_bf16(mlp_w1)
    w2_b = _cast_bf16(mlp_w2)

    xc = ctx
    for i in range(depth):
        last = i == depth - 1
        mod3 = mod[i].reshape(8, 1, 6 * d)
        p = _norm_matmul(x, norm_mix[i], mod3, lat_row, 0, 1, w_in_p, i, P_DTYPE, False, *TILE_W_IN, w_t=True)
        pc = _norm_matmul(xc, norm_mix[i], mod3, ctx_row, 0, 1, w_in_p, i, P_DTYPE, False, tm_c, TILE_W_IN[1], w_t=True)

        filt_params = (hy_w1[i], hy_b1[i], hy_w2[i], hy_b2[i], hy_w3[i], hy_b3[i], hy_w4[i], hy_freq[i])
        ssd_consts = _ssd_consts(ssm_conv_w[i], ssm_conv_b[i], ssm_dt_bias[i], ssm_a_log[i], ssm_d[i], ssm_norm[i])

        ya = _win_attn(attn_sink[i], p, pc, cos_t, sin_t, local=True)
        yb = _hyena_mixer(p, hy_short_w[i], hy_short_b[i], filt_params, hy_skip[i], mats_l)
        yn = _na_attn(p, pc, bias_tabs, i, na_plan, local=True)
        yd, ydc = _ssd_mixer(p, pc, ssd_consts)
        x = _out_proj((ya, yb, yn, yd), w_out_p, i, x, mod3, lat_row, 2, *TILE_OUT)
        hid = _norm_matmul(x, norm_mlp[i], mod3, lat_row, 3, 4, w1_b, i, BF16, True, *TILE_MLP_UP)
        x = _matmul_residual(hid, w2_b, i, x, mod3, lat_row, 5, *TILE_DOWN)

        if not last:
            yac = _win_attn(attn_sink[i], pc, pc, None, None, local=False)
            ybc = _hyena_mixer(pc, hy_short_w[i], hy_short_b[i], filt_params, hy_skip[i], mats_c)
            ync = _na_attn(pc, pc, None, i, None, local=False)
            xc = _out_proj((yac, ybc, ync, ydc), w_out_p, i, xc, mod3, ctx_row, 2, tm_c, TILE_OUT[1])
            hidc = _norm_matmul(xc, norm_mlp[i], mod3, ctx_row, 3, 4, w1_b, i, BF16, True, tm_c, TILE_MLP_UP[1])
            xc = _matmul_residual(hidc, w2_b, i, xc, mod3, ctx_row, 5, tm_c, *TILE_DOWN[1:])
    return _final_norm(x, final_norm)
```

```python
import functools
import math

import numpy as np
import jax
import jax.numpy as jnp
from jax import lax
from jax.experimental import pallas as pl
from jax.experimental.pallas import tpu as pltpu

F32 = jnp.float32
BF16 = jnp.bfloat16

D_MODEL = 2048
GRID_W = 64
EPS = 1e-6
NEG = -1e30
HEAD_DIM = 64
GROUP_W = D_MODEL // 4
N_HEADS = GROUP_W // HEAD_DIM
WINDOW = 128
BLOCK = 128
ROPE_BASE = 10000.0
HY_EMB = 33
HY_FFN = 64
HY_DECAY_TARGET = 1e-2
HY_FAST_PCT = 0.3
HY_SLOW_PCT = 1.5
NA_KR = 8
NA_KC = 16
SSM_STATE = 128
SSM_CHUNK = 128
D_FF = 4 * D_MODEL
SCALE = HEAD_DIM ** -0.5
LOG2E = math.log2(math.e)

COL_QA, COL_KA, COL_VA, COL_DT = 0, 512, 640, 768
COL_HY = 1024
COL_NQ, COL_NK, COL_NV = 2560, 3072, 3584
COL_SZ, COL_SX, COL_SBC = 4096, 4608, 5120
N_PROJ = 5632
HEAD_ORDER = (0, 4, 1, 5, 2, 6, 3, 7)

P_DTYPE = BF16
HALO = 16

V7X_VMEM_BYTES = 64 * 1024 * 1024
VMEM_LIMIT = 56 * 1024 * 1024


def _cparams(sem):
    return pltpu.CompilerParams(dimension_semantics=sem, vmem_limit_bytes=VMEM_LIMIT)


def _silu(x):
    return x * jax.nn.sigmoid(x)


def _ada_kernel(cs_ref, w_ref, b_ref, o_ref):
    a = _silu(cs_ref[...]).astype(BF16)
    o_ref[...] = jnp.dot(a, w_ref[...].astype(BF16), preferred_element_type=F32) + b_ref[...]


def _ada_mod(cs, ada_w, ada_b, tn=1024):
    depth, d, n = ada_w.shape
    return pl.pallas_call(
        _ada_kernel,
        grid=(depth, n // tn),
        in_specs=[
            pl.BlockSpec((8, d), lambda i, j: (0, 0)),
            pl.BlockSpec((None, d, tn), lambda i, j: (i, 0, j)),
            pl.BlockSpec((None, 1, tn), lambda i, j: (i, 0, j)),
        ],
        out_specs=pl.BlockSpec((None, 8, tn), lambda i, j: (i, 0, j)),
        out_shape=jax.ShapeDtypeStruct((depth, 8, n), F32),
        compiler_params=_cparams(("parallel", "parallel")),
        name="ada_mod",
    )(cs, ada_w, ada_b.reshape(depth, 1, n))


NORM_ROWS = 16


def _norm_matmul_kernel(x_ref, g_ref, sh_ref, sc_ref, w_ref, o_ref, h_ref, *, act, w_t):
    @pl.when(pl.program_id(2) == 0)
    def _():
        gain = g_ref[...] * (1.0 + sc_ref[...])
        shift = sh_ref[...]

        def rows(r, carry):
            r0 = pl.multiple_of(r * NORM_ROWS, NORM_ROWS)
            xf = x_ref[pl.ds(r0, NORM_ROWS), :]
            ms = jnp.mean(xf * xf, axis=-1, keepdims=True)
            h_ref[pl.ds(r0, NORM_ROWS), :] = (xf * lax.rsqrt(ms + EPS) * gain + shift).astype(BF16)
            return carry

        lax.fori_loop(0, x_ref.shape[0] // NORM_ROWS, rows, 0, unroll=8)

    if w_t:
        r = lax.dot_general(h_ref[...], w_ref[...], (((1,), (1,)), ((), ())), preferred_element_type=F32)
    else:
        r = jnp.dot(h_ref[...], w_ref[...], preferred_element_type=F32)
    if act:
        r = jnp.square(jnp.maximum(r, 0.0))
    o_ref[...] = r.astype(o_ref.dtype)


def _norm_matmul(x, g, mod3, row_of_b, sh_idx, sc_idx, w, layer, out_dtype, act, tm, tn, w_t=False):
    bsz, t, d = x.shape
    n = w.shape[1] if w_t else w.shape[2]
    tm = min(tm, t)
    w_spec = (pl.BlockSpec((None, tn, d), lambda b, m, j: (layer, j, 0)) if w_t
              else pl.BlockSpec((None, d, tn), lambda b, m, j: (layer, 0, j)))
    return pl.pallas_call(
        functools.partial(_norm_matmul_kernel, act=act, w_t=w_t),
        grid=(bsz, t // tm, n // tn),
        in_specs=[
            pl.BlockSpec((None, tm, d), lambda b, m, j: (b, m, 0)),
            pl.BlockSpec((1, d), lambda b, m, j: (0, 0)),
            pl.BlockSpec((None, 1, d), lambda b, m, j: (row_of_b(b), 0, sh_idx)),
            pl.BlockSpec((None, 1, d), lambda b, m, j: (row_of_b(b), 0, sc_idx)),
            w_spec,
        ],
        out_specs=pl.BlockSpec((None, tm, tn), lambda b, m, j: (b, m, j)),
        out_shape=jax.ShapeDtypeStruct((bsz, t, n), out_dtype),
        scratch_shapes=[pltpu.VMEM((tm, d), BF16)],
        compiler_params=_cparams(("parallel", "parallel", "arbitrary")),
        name="norm_matmul",
    )(x, g.reshape(1, d), mod3, mod3, w)


def _mm_res_kernel(a_ref, w_ref, x_ref, gate_ref, o_ref, acc_ref, *, nk):
    k = pl.program_id(3)

    @pl.when(k == 0)
    def _():
        acc_ref[...] = jnp.zeros_like(acc_ref)

    acc_ref[...] += jnp.dot(a_ref[...], w_ref[...], preferred_element_type=F32)

    @pl.when(k == nk - 1)
    def _():
        o_ref[...] = x_ref[...] + gate_ref[...] * acc_ref[...]


def _matmul_residual(a, w, layer, x, mod3, row_of_b, gate_idx, tm, tn, tk):
    bsz, t, kdim = a.shape
    n = w.shape[2]
    tm = min(tm, t)
    nk = kdim // tk
    return pl.pallas_call(
        functools.partial(_mm_res_kernel, nk=nk),
        grid=(bsz, t // tm, n // tn, nk),
        in_specs=[
            pl.BlockSpec((None, tm, tk), lambda b, m, j, k: (b, m, k)),
            pl.BlockSpec((None, tk, tn), lambda b, m, j, k: (layer, k, j)),
            pl.BlockSpec((None, tm, tn), lambda b, m, j, k: (b, m, j)),
            pl.BlockSpec((None, 1, tn), lambda b, m, j, k: (row_of_b(b), 0, gate_idx * (D_MODEL // tn) + j)),
        ],
        out_specs=pl.BlockSpec((None, tm, tn), lambda b, m, j, k: (b, m, j)),
        out_shape=jax.ShapeDtypeStruct(x.shape, F32),
        scratch_shapes=[pltpu.VMEM((tm, tn), F32)],
        compiler_params=_cparams(("parallel", "parallel", "parallel", "arbitrary")),
        name="matmul_residual",
    )(a, w, x, mod3)


def _out_proj_kernel(ya_ref, yb_ref, yn_ref, yd_ref, w_ref, x_ref, gate_ref, o_ref):
    acc = None
    for g, y_ref in enumerate((ya_ref, yb_ref, yn_ref, yd_ref)):
        part = jnp.dot(y_ref[...].astype(BF16), w_ref[GROUP_W * g:GROUP_W * (g + 1), :],
                       preferred_element_type=F32)
        acc = part if acc is None else acc + part
    o_ref[...] = x_ref[...] + gate_ref[...] * acc


def _out_proj(ys, w, layer, x, mod3, row_of_b, gate_idx, tm, tn):
    bsz, t, d = x.shape
    tm = min(tm, t)
    y_spec = pl.BlockSpec((None, tm, GROUP_W), lambda b, m, j: (b, m, 0))
    return pl.pallas_call(
        _out_proj_kernel,
        grid=(bsz, t // tm, d // tn),
        in_specs=[y_spec, y_spec, y_spec, y_spec,
                  pl.BlockSpec((None, 4 * GROUP_W, tn), lambda b, m, j: (layer, 0, j),
                               pipeline_mode=pl.Buffered(1) if tn == d else None),
                  pl.BlockSpec((None, tm, tn), lambda b, m, j: (b, m, j)),
                  pl.BlockSpec((None, 1, tn), lambda b, m, j: (row_of_b(b), 0, gate_idx * (d // tn) + j))],
        out_specs=pl.BlockSpec((None, tm, tn), lambda b, m, j: (b, m, j)),
        out_shape=jax.ShapeDtypeStruct(x.shape, F32),
        compiler_params=_cparams(("parallel", "parallel", "parallel")),
        name="out_proj",
    )(*ys, w, x, mod3)


def _final_norm_kernel(x_ref, g_ref, o_ref):
    xf = x_ref[...]
    ms = jnp.mean(xf * xf, axis=-1, keepdims=True)
    o_ref[...] = xf * lax.rsqrt(ms + EPS) * g_ref[...]


def _final_norm(x, g, tm=1024):
    bsz, t, d = x.shape
    tm = min(tm, t)
    return pl.pallas_call(
        _final_norm_kernel,
        grid=(bsz, t // tm),
        in_specs=[pl.BlockSpec((None, tm, d), lambda b, m: (b, m, 0)),
                  pl.BlockSpec((1, d), lambda b, m: (0, 0))],
        out_specs=pl.BlockSpec((None, tm, d), lambda b, m: (b, m, 0)),
        out_shape=jax.ShapeDtypeStruct(x.shape, F32),
        compiler_params=_cparams(("parallel", "parallel")),
        name="final_norm",
    )(x, g.reshape(1, d))


WIN_QBLOCKS = 4


def _rope(x, cos, sin_signed, lane_lo):
    w = x.shape[-1]
    partner = jnp.where(lane_lo, pltpu.roll(x, w - 16, 1), pltpu.roll(x, 16, 1))
    return x * cos + partner * sin_signed


def _win_attn_kernel(sink_ref, q_ref, kc_ref, vc_ref, *rest, seq, local):
    if local:
        k_ref, v_ref, cos_ref, sin_ref, mask_ref, o_ref = rest
    else:
        (o_ref,) = rest
    nb = seq // BLOCK
    lane = lax.broadcasted_iota(jnp.int32, (BLOCK, 128), 1)
    lo = lane < HEAD_DIM
    lane_lo = (lane % 32) < 16
    lane3_lo = (lax.broadcasted_iota(jnp.int32, (3 * BLOCK, 128), 1) % 32) < 16
    nt = (((1,), (1,)), ((), ()))
    kc = kc_ref[...].astype(BF16)
    vc = vc_ref[...].astype(BF16)
    sink = jnp.concatenate(
        [jnp.full((BLOCK, 1), sink_ref[HEAD_ORDER[i]] * LOG2E, F32) for i in range(N_HEADS)], axis=0)
    nq = q_ref.shape[0] // BLOCK
    for sub in range(nq):
        n = pl.program_id(1) * nq + sub
        q = q_ref[sub * BLOCK:(sub + 1) * BLOCK, :].astype(F32) * (SCALE * LOG2E)
        if local:
            r0 = pl.multiple_of(n * BLOCK, BLOCK)
            cos_q = cos_ref[pl.ds(r0, BLOCK), :]
            sin_q = sin_ref[pl.ds(r0, BLOCK), :]
        rows = []
        for m in range(4):
            qm = q[:, 128 * m:128 * (m + 1)]
            if local:
                qm = _rope(qm, cos_q, sin_q, lane_lo)
            rows.append(jnp.where(lo, qm, 0.0))
            rows.append(jnp.where(lo, 0.0, qm))
        qbd = jnp.concatenate(rows, axis=0).astype(BF16)
        s_ctx = lax.dot_general(qbd, kc, nt, preferred_element_type=F32)
        mx = jnp.maximum(jnp.max(s_ctx, axis=-1, keepdims=True), sink)
        if local:
            start = pl.multiple_of(jnp.clip((n - 1) * BLOCK, 0, seq - 3 * BLOCK), BLOCK)
            kb = _rope(k_ref[pl.ds(start, 3 * BLOCK), :].astype(F32), cos_ref[pl.ds(start, 3 * BLOCK), :],
                       sin_ref[pl.ds(start, 3 * BLOCK), :], lane3_lo).astype(BF16)
            vb = v_ref[pl.ds(start, 3 * BLOCK), :].astype(BF16)
            mask = mask_ref[jnp.where(n == 0, 0, jnp.where(n == nb - 1, 2, 1))]
            s_loc = lax.dot_general(qbd, kb, nt, preferred_element_type=F32)
            s_loc = (s_loc.reshape(N_HEADS, BLOCK, 3 * BLOCK) + mask[None]).reshape(s_loc.shape)
            mx = jnp.maximum(mx, jnp.max(s_loc, axis=-1, keepdims=True))
        p_ctx = jnp.exp2(s_ctx - mx)
        den = jnp.sum(p_ctx, axis=-1, keepdims=True) + jnp.exp2(sink - mx)
        acc = jnp.dot(p_ctx.astype(BF16), vc, preferred_element_type=F32)
        if local:
            p_loc = jnp.exp2(s_loc - mx)
            den = den + jnp.sum(p_loc, axis=-1, keepdims=True)
            acc = acc + jnp.dot(p_loc.astype(BF16), vb, preferred_element_type=F32)
        o = acc / den
        outs = [jnp.where(lo, o[(2 * m) * BLOCK:(2 * m + 1) * BLOCK], o[(2 * m + 1) * BLOCK:(2 * m + 2) * BLOCK])
                for m in range(4)]
        o_ref[sub * BLOCK:(sub + 1) * BLOCK, :] = jnp.concatenate(outs, axis=1).astype(o_ref.dtype)


def _win_attn(sink, pq, pc, cos_t, sin_t, local):
    bsz, t, _ = pq.shape
    lc = pc.shape[1]
    tq = min(WIN_QBLOCKS * BLOCK, t)
    in_specs = [
        pl.BlockSpec(memory_space=pltpu.SMEM),
        pl.BlockSpec((None, tq, 512), lambda b, n: (b, n, COL_QA // 512)),
        pl.BlockSpec((None, lc, 128), lambda b, n: (b, 0, COL_KA // 128)),
        pl.BlockSpec((None, lc, 128), lambda b, n: (b, 0, COL_VA // 128)),
    ]
    args = [sink, pq, pc, pc]
    if local:
        in_specs += [
            pl.BlockSpec((None, t, 128), lambda b, n: (b, 0, COL_KA // 128)),
            pl.BlockSpec((None, t, 128), lambda b, n: (b, 0, COL_VA // 128)),
            pl.BlockSpec((t, 128), lambda b, n: (0, 0)),
            pl.BlockSpec((t, 128), lambda b, n: (0, 0)),
            pl.BlockSpec((3, BLOCK, 3 * BLOCK), lambda b, n: (0, 0, 0)),
        ]
        args += [pq, pq, cos_t, sin_t, _window_masks(t)]
    return pl.pallas_call(
        functools.partial(_win_attn_kernel, seq=t, local=local),
        grid=(bsz, t // tq),
        in_specs=in_specs,
        out_specs=pl.BlockSpec((None, tq, GROUP_W), lambda b, n: (b, n, 0)),
        out_shape=jax.ShapeDtypeStruct((bsz, t, GROUP_W), BF16),
        compiler_params=_cparams(("parallel", "arbitrary")),
        name="win_attn" if local else "ctx_attn_a",
    )(*args)


def _window_masks(seq):
    nb = seq // BLOCK
    qi = np.arange(BLOCK)[:, None]
    kj = np.arange(3 * BLOCK)[None, :]
    tabs = []
    for n in (0, 1, nb - 1):
        start = int(np.clip((n - 1) * BLOCK, 0, seq - 3 * BLOCK))
        rel = (start + kj) - (n * BLOCK + qi)
        tabs.append(np.where(np.abs(rel) <= WINDOW, 0.0, NEG))
    return jnp.asarray(np.stack(tabs), F32)


def _rope_tables(seq):
    t = np.arange(seq)
    row, col = t // GRID_W, t % GRID_W
    quarter = HEAD_DIM // 4
    inv = ROPE_BASE ** (-np.arange(quarter, dtype=np.float64) / quarter)
    inv = inv.astype(np.float32).astype(np.float64)
    lane = np.arange(128)
    j = lane % HEAD_DIM
    pos = np.where((j < HEAD_DIM // 2)[None, :], row[:, None], col[:, None]).astype(np.float64)
    ang = (pos * inv[j % quarter][None, :]).astype(np.float32)
    cos = np.cos(ang.astype(np.float64))
    sin = np.sin(ang.astype(np.float64))
    sign = np.where((j % 32) < 16, -1.0, 1.0)[None, :]
    return jnp.asarray(cos, F32), jnp.asarray(sin * sign, F32)


NA_ROWS_PER_STEP = 4


def _na_kernel(var_ref, ws_ref, q_ref, kc_ref, vc_ref, *rest, local, win_rows):
    del var_ref
    if local:
        k_ref, v_ref, bias_ref, o_ref = rest
    else:
        (o_ref,) = rest
    g = pl.program_id(1)
    tq = q_ref.shape[0]
    q = q_ref[...].astype(F32) * (SCALE * LOG2E)
    head = lax.broadcasted_iota(jnp.int32, (tq, 256), 1) // HEAD_DIM
    nt = (((1,), (1,)), ((), ()))
    outs = []
    for half in range(2):
        cols = slice(256 * half, 256 * (half + 1))
        q4 = q[:, cols]
        qbd = jnp.concatenate([jnp.where(head == h, q4, 0.0) for h in range(4)], axis=0).astype(BF16)
        kc4 = kc_ref[:, cols].astype(BF16)
        vc4 = vc_ref[:, cols].astype(BF16)
        s_ctx = lax.dot_general(qbd, kc4, nt, preferred_element_type=F32)
        mx = jnp.max(s_ctx, axis=-1, keepdims=True)
        if local:
            nkey = win_rows * GRID_W
            start = pl.multiple_of(ws_ref[g] * GRID_W, GRID_W)
            k4 = k_ref[pl.ds(start, nkey), cols].astype(BF16)
            v4 = v_ref[pl.ds(start, nkey), cols].astype(BF16)
            bias = bias_ref[4 * half:4 * half + 4].astype(F32).reshape(4 * tq, nkey)
            s_loc = lax.dot_general(qbd, k4, nt, preferred_element_type=F32) + bias
            mx = jnp.maximum(mx, jnp.max(s_loc, axis=-1, keepdims=True))
        p_ctx = jnp.exp2(s_ctx - mx)
        den = jnp.sum(p_ctx, axis=-1, keepdims=True)
        acc = jnp.dot(p_ctx.astype(BF16), vc4, preferred_element_type=F32)
        if local:
            p_loc = jnp.exp2(s_loc - mx)
            den = den + jnp.sum(p_loc, axis=-1, keepdims=True)
            acc = acc + jnp.dot(p_loc.astype(BF16), v4, preferred_element_type=F32)
        o = acc / den
        o4 = jnp.where(head == 0, o[0:tq], 0.0)
        for h in range(1, 4):
            o4 = o4 + jnp.where(head == h, o[h * tq:(h + 1) * tq], 0.0)
        outs.append(o4)
    o_ref[...] = jnp.concatenate(outs, axis=1).astype(o_ref.dtype)


def _na_plan(seq):
    rows = seq // GRID_W
    kr = min(NA_KR, rows)
    r_step = NA_ROWS_PER_STEP
    win_rows = min(r_step + kr, rows)
    n_groups = rows // r_step
    wstart = np.zeros(n_groups, np.int32)
    pats = []
    keys = {}
    var = np.zeros(n_groups, np.int32)
    for g in range(n_groups):
        r0 = g * r_step
        ws = int(np.clip(r0 - kr // 2, 0, rows - win_rows))
        wstart[g] = ws
        r = r0 + np.arange(r_step)
        rstart = np.clip(r - kr // 2, 0, rows - kr)
        krow = ws + np.arange(win_rows)
        valid = (krow[None, :] >= rstart[:, None]) & (krow[None, :] < rstart[:, None] + kr)
        roff = krow[None, :] - r[:, None] + NA_KR - 1
        key = (valid.tobytes(), np.where(valid, roff, 0).tobytes())
        if key not in keys:
            keys[key] = len(pats)
            pats.append((valid, np.where(valid, roff, 0)))
        var[g] = keys[key]
    return rows, win_rows, n_groups, wstart, var, pats


def _na_bias_tables(rpb, pats):
    cq = np.arange(GRID_W)
    ck = np.arange(GRID_W)
    cstart = np.clip(cq - NA_KC // 2, 0, GRID_W - NA_KC)
    col_valid = (ck[None] >= cstart[:, None]) & (ck[None] < cstart[:, None] + NA_KC)
    coff = np.clip(ck[None] - cq[:, None], -(NA_KC - 1), NA_KC - 1) + NA_KC - 1
    by_col = jnp.where(col_valid, rpb[..., coff] * LOG2E, NEG).astype(BF16)
    depth, heads, n_ro = by_col.shape[:3]
    flat = by_col.transpose(0, 1, 3, 2, 4).reshape(depth, heads, GRID_W, n_ro * GRID_W)
    tabs = []
    for valid, roff in pats:
        r_step, win_rows = valid.shape
        q_rows = []
        for i in range(r_step):
            a_ok = np.nonzero(valid[i])[0]
            a0, a1 = int(a_ok[0]), int(a_ok[-1]) + 1
            assert valid[i, a0:a1].all() and (np.diff(roff[i, a0:a1]) == 1).all()
            seg = flat[..., int(roff[i, a0]) * GRID_W:(int(roff[i, a0]) + a1 - a0) * GRID_W]
            q_rows.append(jnp.pad(seg, ((0, 0), (0, 0), (0, 0), (a0 * GRID_W, (win_rows - a1) * GRID_W)),
                                  constant_values=NEG))
        tabs.append(jnp.concatenate(q_rows, axis=-2))
    return jnp.stack(tabs, axis=1)


def _na_attn(pq, pc, bias_tabs, layer, plan, local):
    bsz, t, _ = pq.shape
    lc = pc.shape[1]
    if local:
        rows, win_rows, n_groups, wstart, var, _ = plan
        tq = NA_ROWS_PER_STEP * GRID_W
    else:
        win_rows, n_groups, tq = 0, 1, t
        wstart = np.zeros(1, np.int32)
        var = np.zeros(1, np.int32)
    in_specs = [
        pl.BlockSpec((None, tq, 512), lambda b, g, vr, ws: (b, g, COL_NQ // 512)),
        pl.BlockSpec((None, lc, 512), lambda b, g, vr, ws: (b, 0, COL_NK // 512)),
        pl.BlockSpec((None, lc, 512), lambda b, g, vr, ws: (b, 0, COL_NV // 512)),
    ]
    args = [pq, pc, pc]
    if local:
        in_specs += [
            pl.BlockSpec((None, t, 512), lambda b, g, vr, ws: (b, 0, COL_NK // 512)),
            pl.BlockSpec((None, t, 512), lambda b, g, vr, ws: (b, 0, COL_NV // 512)),
            pl.BlockSpec((None, None, N_HEADS, tq, win_rows * GRID_W),
                         lambda b, g, vr, ws: (layer, vr[g], 0, 0, 0)),
        ]
        args += [pq, pq, bias_tabs]
    grid_spec = pltpu.PrefetchScalarGridSpec(
        num_scalar_prefetch=2,
        grid=(bsz, n_groups),
        in_specs=in_specs,
        out_specs=pl.BlockSpec((None, tq, GROUP_W), lambda b, g, vr, ws: (b, g, 0)),
    )
    return pl.pallas_call(
        functools.partial(_na_kernel, local=local, win_rows=win_rows),
        grid_spec=grid_spec,
        out_shape=jax.ShapeDtypeStruct((bsz, t, GROUP_W), BF16),
        compiler_params=_cparams(("parallel", "arbitrary")),
        name="na_attn" if local else "ctx_attn_c",
    )(jnp.asarray(var), jnp.asarray(wstart), *args)


SSD_CHUNKS = 4


def _softplus(x):
    return jnp.maximum(x, 0.0) + jnp.log(1.0 + jnp.exp(-jnp.abs(x)))


def _bf16_parts(x, n):
    parts = []
    for _ in range(n):
        part = x.astype(BF16)
        parts.append(part)
        x = x - part.astype(F32)
    return parts


def _conv3_silu(cur, prev_blk, next_blk, w_ref, b_ref, has_prev, has_next):
    x = cur.astype(F32)
    rows = x.shape[0]
    prev_row = prev_blk.astype(F32)[HALO - 1:HALO, :] * has_prev
    next_row = next_blk.astype(F32)[0:1, :] * has_next
    ri = lax.broadcasted_iota(jnp.int32, x.shape, 0)
    up = jnp.where(ri == 0, prev_row, pltpu.roll(x, 1, 0))
    dn = jnp.where(ri == rows - 1, next_row, pltpu.roll(x, rows - 1, 0))
    u = up * w_ref[0:1, :] + x * w_ref[1:2, :] + dn * w_ref[2:3, :] + b_ref[...]
    return _silu(u)


def _ssd_chunk(xs, bc, dt, a, st_ref, keep, tri, expand, head4, d_off, reverse):
    L = SSM_CHUNK
    nt = (((1,), (1,)), ((), ()))
    a_t = a.T
    c_col = sum(jnp.dot(tri, part, preferred_element_type=F32) for part in _bf16_parts(a, 3))
    c_row = sum(lax.dot_general(part, tri, nt, preferred_element_type=F32) for part in _bf16_parts(a_t, 3))
    c_exp = sum(jnp.dot(part, expand, preferred_element_type=F32) for part in _bf16_parts(c_col, 2))
    dt_exp = sum(jnp.dot(part, expand, preferred_element_type=F32) for part in _bf16_parts(dt, 2))
    end = 0 if reverse else L - 1
    cend = c_exp[end:end + 1, :]
    x_dt = xs * dt_exp
    out_decay = jnp.exp(c_exp)
    x_dec = x_dt * jnp.exp(cend - c_exp)
    chunk_decay = jnp.exp(cend)

    ys = []
    for g in range(2):
        gl = slice(256 * g, 256 * (g + 1))
        b_g = bc[:, 128 * g:128 * (g + 1)]
        c_g = bc[:, 256 + 128 * g:256 + 128 * (g + 1)].astype(BF16)
        cb = lax.dot_general(c_g, b_g.astype(BF16), nt, preferred_element_type=F32)
        ms = []
        for hh in range(4):
            j = d_off + 4 * g + hh
            diff = c_col[:, j:j + 1] - c_row[j:j + 1, :]
            ms.append(cb * jnp.exp(jnp.where(keep, diff, NEG)))
        m_g = jnp.concatenate(ms, axis=0).astype(BF16)
        o = jnp.dot(m_g, x_dt[:, gl].astype(BF16), preferred_element_type=F32)
        y_diag = jnp.where(head4 == 0, o[0:L], 0.0)
        for hh in range(1, 4):
            y_diag = y_diag + jnp.where(head4 == hh, o[hh * L:(hh + 1) * L], 0.0)
        st = st_ref[g]
        y_off = jnp.dot(c_g, st.astype(BF16), preferred_element_type=F32) * out_decay[:, gl]
        ys.append(y_diag + y_off)
        st_ref[g] = chunk_decay[:, gl] * st + jnp.dot(
            b_g.T.astype(BF16), x_dec[:, gl].astype(BF16), preferred_element_type=F32)
    return jnp.concatenate(ys, axis=1)


def _ssd_kernel(*refs, reverse, nc):
    finalize = reverse
    if reverse:
        (xsc_ref, bcc_ref, dt_ref, dtb_ref, alog_ref, h0_ref,
         z_ref, yf_ref, dskip_ref, nw_ref, y_ref, ht_ref, st_ref) = refs
    else:
        (xs_ref, bc_ref, xsp_ref, xsn_ref, bcp_ref, bcn_ref, dt_ref, cwx_ref, cbx_ref, cwb_ref, cbb_ref,
         dtb_ref, alog_ref, h0_ref, y_ref, ht_ref, xsc_ref, bcc_ref, st_ref) = refs
    c = pl.program_id(1)
    cid = (nc - 1 - c) if reverse else c
    d_off = 8 if reverse else 0
    L = SSM_CHUNK
    n_sub = dt_ref.shape[0] // L

    @pl.when(c == 0)
    def _():
        st_ref[...] = h0_ref[...]

    if reverse:
        xs = xsc_ref[...]
        bc = bcc_ref[...].astype(F32)
    else:
        has_prev = jnp.where(cid > 0, 1.0, 0.0).astype(F32)
        has_next = jnp.where(cid < nc - 1, 1.0, 0.0).astype(F32)
        xs = _conv3_silu(xs_ref[...], xsp_ref[...], xsn_ref[...], cwx_ref, cbx_ref, has_prev, has_next)
        bc = _conv3_silu(bc_ref[...], bcp_ref[...], bcn_ref[...], cwb_ref, cbb_ref, has_prev, has_next)
        xsc_ref[...] = xs
        bcc_ref[...] = bc.astype(BF16)

    dt_all = _softplus(dt_ref[...].astype(F32) + dtb_ref[...])
    a_all = dt_all * (-jnp.exp(alog_ref[...]))
    ri = lax.broadcasted_iota(jnp.int32, (L, L), 0)
    ci = lax.broadcasted_iota(jnp.int32, (L, L), 1)
    keep = (ci >= ri) if reverse else (ci <= ri)
    tri = keep.astype(BF16)
    nt = (((1,), (1,)), ((), ()))
    ej = lax.broadcasted_iota(jnp.int32, (128, GROUP_W), 0)
    eh = lax.broadcasted_iota(jnp.int32, (128, GROUP_W), 1) // HEAD_DIM
    expand = (ej == eh + d_off).astype(BF16)
    head4 = lax.broadcasted_iota(jnp.int32, (L, 256), 1) // HEAD_DIM
    xs_all, bc_all = xs, bc
    y_sub = [None] * n_sub
    for sub in (range(n_sub - 1, -1, -1) if reverse else range(n_sub)):
        y_sub[sub] = _ssd_chunk(xs_all[sub * L:(sub + 1) * L], bc_all[sub * L:(sub + 1) * L],
                                dt_all[sub * L:(sub + 1) * L], a_all[sub * L:(sub + 1) * L],
                                st_ref, keep, tri, expand, head4, d_off, reverse)
    y = jnp.concatenate(y_sub, axis=0) if n_sub > 1 else y_sub[0]
    if finalize:
        y = y + yf_ref[...] + xs * dskip_ref[...]
        y = y * _silu(z_ref[...].astype(F32))
        halves = []
        for g in range(2):
            yg = y[:, 256 * g:256 * (g + 1)]
            halves.append(yg * lax.rsqrt(jnp.mean(yg * yg, axis=-1, keepdims=True) + EPS))
        y = jnp.concatenate(halves, axis=1) * nw_ref[...]
    y_ref[...] = y.astype(y_ref.dtype)

    @pl.when(c == nc - 1)
    def _():
        ht_ref[...] = st_ref[...]


def _ssd_direction(p, consts, h0, reverse, fwd=None):
    cwx, cbx, cwb, cbb, dtb, alog, dskip, nw = consts
    bsz, t, _ = p.shape
    rows = min(SSD_CHUNKS * SSM_CHUNK, t)
    nc = t // rows
    hb = rows // HALO
    nhalo = t // HALO

    def cid(c):
        return (nc - 1 - c) if reverse else c

    def cur(col, width):
        return pl.BlockSpec((None, rows, width), lambda b, c: (b, cid(c), col // width))

    def prev(col, width):
        return pl.BlockSpec((None, HALO, width), lambda b, c: (b, jnp.maximum(cid(c) * hb - 1, 0), col // width))

    def nxt(col, width):
        return pl.BlockSpec((None, HALO, width), lambda b, c: (b, jnp.minimum((cid(c) + 1) * hb, nhalo - 1), col // width))

    def const(arr):
        return pl.BlockSpec(arr.shape, lambda b, c: (0,) * arr.ndim)

    state_spec = pl.BlockSpec((None, 2, SSM_STATE, 256), lambda b, c: (b, 0, 0, 0))
    chunk_spec = pl.BlockSpec((None, rows, GROUP_W), lambda b, c: (b, cid(c), 0))
    state_shape = jax.ShapeDtypeStruct((bsz, 2, SSM_STATE, 256), F32)
    if reverse:
        yf, xs_act, bc_act = fwd
        in_specs = [chunk_spec, chunk_spec, cur(COL_DT, 128), const(dtb), const(alog), state_spec,
                    cur(COL_SZ, 512), chunk_spec, const(dskip), const(nw)]
        args = [xs_act, bc_act, p, dtb, alog, h0, p, yf, dskip, nw]
        out_specs = [chunk_spec, state_spec]
        out_shape = [jax.ShapeDtypeStruct((bsz, t, GROUP_W), BF16), state_shape]
    else:
        in_specs = [cur(COL_SX, 512), cur(COL_SBC, 512), prev(COL_SX, 512), nxt(COL_SX, 512),
                    prev(COL_SBC, 512), nxt(COL_SBC, 512), cur(COL_DT, 128),
                    const(cwx), const(cbx), const(cwb), const(cbb), const(dtb), const(alog), state_spec]
        args = [p, p, p, p, p, p, p, cwx, cbx, cwb, cbb, dtb, alog, h0]
        out_specs = [chunk_spec, state_spec, chunk_spec, chunk_spec]
        out_shape = [jax.ShapeDtypeStruct((bsz, t, GROUP_W), F32), state_shape,
                     jax.ShapeDtypeStruct((bsz, t, GROUP_W), F32), jax.ShapeDtypeStruct((bsz, t, GROUP_W), BF16)]
    return pl.pallas_call(
        functools.partial(_ssd_kernel, reverse=reverse, nc=nc),
        grid=(bsz, nc),
        in_specs=in_specs,
        out_specs=out_specs,
        out_shape=out_shape,
        scratch_shapes=[pltpu.VMEM((2, SSM_STATE, 256), F32)],
        compiler_params=_cparams(("parallel", "arbitrary")),
        name="ssd_rev" if reverse else "ssd_fwd",
    )(*args)


def _ssd_consts(conv_w, conv_b, dt_bias, a_log, d_skip, norm_w):
    cwx, cwb = conv_w[:, :GROUP_W], conv_w[:, GROUP_W:]
    cbx, cbb = conv_b[:GROUP_W].reshape(1, -1), conv_b[GROUP_W:].reshape(1, -1)
    pad = lambda v: jnp.pad(v.reshape(1, -1), ((0, 0), (0, 128 - v.size)))
    return (cwx, cbx, cwb, cbb, pad(dt_bias), pad(a_log),
            jnp.repeat(d_skip, HEAD_DIM).reshape(1, -1), norm_w.reshape(1, -1))


def _ssd_mixer(p, pc, consts):
    bsz = p.shape[0]
    zero = jnp.zeros((bsz, 2, SSM_STATE, 256), F32)
    ycf, hf, *act_c = _ssd_direction(pc, consts, zero, False)
    yc, hb = _ssd_direction(pc, consts, zero, True, fwd=(ycf, *act_c))
    ylf, _, *act_l = _ssd_direction(p, consts, hf, False)
    yl, _ = _ssd_direction(p, consts, hb, True, fwd=(ylf, *act_l))
    return yl, yc


HY_CB = 128


def _hy_cb(seq):
    del seq
    return HY_CB
HY_UNROLL = 16


def _hy_dims(seq):
    n = 2 * seq
    n1 = {4096: 128, 1024: 64, 512: 32, 256: 32, 128: 16}[seq]
    n2 = n // n1
    h = n1 // 2
    return dict(n=n, n1=n1, n2=n2, h=h, pa=h + 8, pz=h + 16, pb=n2 + 8, pc=n1 + 8)


def _hy_matrices(seq):
    d = _hy_dims(seq)
    n, n1, n2, h = d["n"], d["n1"], d["n2"], d["h"]

    def cis(num, den, sign):
        ang = (2.0 * math.pi / den) * (num % den).astype(F32)
        return jnp.cos(ang), sign * jnp.sin(ang)

    k1 = jnp.arange(n1, dtype=jnp.int32)
    nn = n2 * jnp.arange(n1, dtype=jnp.int32)[None, None, :] + jnp.arange(n2, dtype=jnp.int32)[:, None, None]
    e1r, e1i = cis(k1[None, :, None] * nn, n, -1.0)
    m1f = jnp.concatenate([e1r, e1i], axis=1)
    m1 = jnp.concatenate([jnp.concatenate([e1r[..., :h], -e1i[..., :h]], axis=2),
                          jnp.concatenate([e1i[..., :h], e1r[..., :h]], axis=2)], axis=1)
    a2 = jnp.arange(n2, dtype=jnp.int32)
    g2r, g2i = cis(a2[:, None] * a2[None, :], n2, -1.0)
    m2 = jnp.concatenate([jnp.concatenate([g2r, -g2i], axis=1),
                          jnp.concatenate([g2i, g2r], axis=1)], axis=0)
    num3 = (a2[None, :, None] * a2[None, None, :] * n1 + k1[:, None, None] * a2[None, :, None])
    e3r, e3i = cis(num3, n, 1.0)
    m3 = jnp.concatenate([jnp.concatenate([e3r, -e3i], axis=2),
                          jnp.concatenate([e3i, e3r], axis=2)], axis=1)
    hh = jnp.arange(h, dtype=jnp.int32)
    d4r, d4i = cis(hh[:, None] * k1[None, :], n1, 1.0)
    m4 = jnp.concatenate([jnp.concatenate([d4r, -d4i], axis=1),
                          jnp.concatenate([d4i, d4r], axis=1)], axis=0) / n
    return tuple(m.astype(BF16) for m in (m1f, m1, m2, m3, m4))


def _hy_prep_kernel(p_ref, w_ref, b_ref, o_ref, *, dims):
    n2, h, pa = dims["n2"], dims["h"], dims["pa"]
    seq = h * n2
    for j in range(n2):
        o_ref[j * pa + h:(j + 1) * pa, :] = jnp.zeros((pa - h, o_ref.shape[-1]), o_ref.dtype)
    w0, w1, w2, bias = w_ref[0:1, :], w_ref[1:2, :], w_ref[2:3, :], b_ref[...]
    ri = lax.broadcasted_iota(jnp.int32, (n2, o_ref.shape[-1]), 0)

    def body(i, carry):
        r0 = pl.multiple_of(i * n2, n2)
        x = p_ref[pl.ds(r0, n2), :].astype(F32)
        pstart = pl.multiple_of(jnp.maximum(r0 - HALO, 0), HALO)
        nstart = pl.multiple_of(jnp.minimum(r0 + n2, seq - HALO), HALO)
        prev_row = p_ref[pl.ds(pstart, HALO), :].astype(F32)[HALO - 1:HALO, :] * jnp.where(i > 0, 1.0, 0.0).astype(F32)
        next_row = p_ref[pl.ds(nstart, HALO), :].astype(F32)[0:1, :] * jnp.where(i < h - 1, 1.0, 0.0).astype(F32)
        up = jnp.where(ri == 0, prev_row, pltpu.roll(x, 1, 0))
        dn = jnp.where(ri == n2 - 1, next_row, pltpu.roll(x, n2 - 1, 0))
        o_ref[pl.ds(i, n2, stride=pa), :] = up * w0 + x * w1 + dn * w2 + bias
        return carry

    lax.fori_loop(0, h, body, 0)


def _hy_prep(p, short_w, short_b):
    bsz, t, _ = p.shape
    dims = _hy_dims(t)
    rows = dims["n2"] * dims["pa"]
    cb = _hy_cb(t)
    ncb = 3 * GROUP_W // cb
    return pl.pallas_call(
        functools.partial(_hy_prep_kernel, dims=dims),
        grid=(bsz, ncb),
        in_specs=[pl.BlockSpec((None, t, cb), lambda b, j: (b, 0, COL_HY // cb + j)),
                  pl.BlockSpec((3, cb), lambda b, j: (0, j)),
                  pl.BlockSpec((1, cb), lambda b, j: (0, j))],
        out_specs=pl.BlockSpec((None, rows, cb), lambda b, j: (b, 0, j)),
        out_shape=jax.ShapeDtypeStruct((bsz, rows, 3 * GROUP_W), F32),
        compiler_params=_cparams(("parallel", "parallel")),
        name="hy_prep",
    )(p, short_w, short_b.reshape(1, -1))


def _hy_filter_kernel(z_ref, w1_ref, b1_ref, w2_ref, b2_ref, w3_ref, b3_ref, w4_ref, fr_ref, dl_ref, o_ref):
    hi = lax.Precision.HIGHEST
    z = z_ref[...]
    fr = fr_ref[...]
    h = jnp.sin(fr * (jnp.dot(z, w1_ref[...], preferred_element_type=F32, precision=hi) + b1_ref[...]))
    h = jnp.sin(fr * (jnp.dot(h, w2_ref[...], preferred_element_type=F32, precision=hi) + b2_ref[...]))
    h = jnp.sin(fr * (jnp.dot(h, w3_ref[...], preferred_element_type=F32, precision=hi) + b3_ref[...]))
    full = jnp.dot(h.astype(BF16), w4_ref[...].astype(BF16), preferred_element_type=F32)
    t = z[:, 0:1]
    is_bwd = z[:, HY_EMB:HY_EMB + 1] > 0.5
    live = z[:, HY_EMB + 1:HY_EMB + 2]
    decay = jnp.exp(-t * jnp.abs(dl_ref[...])) * live
    for o in range(2):
        fwd = full[:, o * 2 * GROUP_W:o * 2 * GROUP_W + GROUP_W]
        bwd = full[:, o * 2 * GROUP_W + GROUP_W:(o + 1) * 2 * GROUP_W]
        o_ref[o] = jnp.where(is_bwd, bwd, fwd) * decay


def _hy_filter_features(seq):
    d = _hy_dims(seq)
    n, n1, n2 = d["n"], d["n1"], d["n2"]
    row = np.arange(n)
    time = n2 * (row % n1) + row // n1
    is_bwd = time > seq
    pos = np.where(is_bwd, n - time, time)
    live = (time != seq).astype(np.float64)
    pos = np.where(time == seq, 0, pos)
    t = np.linspace(0.0, 1.0, seq)[pos]
    bands = (HY_EMB - 1) // 2
    f = np.linspace(1e-4, bands - 1, bands)[None]
    wpos = (2.0 * math.pi * pos / seq)[:, None]
    feat = np.zeros((n, 128), np.float64)
    feat[:, 0] = t
    feat[:, 1:1 + bands] = np.cos(f * wpos)
    feat[:, 1 + bands:HY_EMB] = -np.sin(f * wpos)
    feat[:, HY_EMB] = is_bwd
    feat[:, HY_EMB + 1] = live
    return jnp.asarray(feat, F32)


def _hy_filter(seq, w1, b1, w2, b2, w3, b3, w4, freq, tr=512):
    d = _hy_dims(seq)
    n = d["n"]
    tr = min(tr, n)
    feat = _hy_filter_features(seq)
    w1p = jnp.pad(w1, ((0, 128 - HY_EMB), (0, 0)))
    max_decay = math.log(HY_DECAY_TARGET) / HY_FAST_PCT
    min_decay = math.log(HY_DECAY_TARGET) / HY_SLOW_PCT
    deltas = jnp.linspace(min_decay, max_decay, GROUP_W, dtype=F32).reshape(1, -1)
    row = lambda v: v.reshape(1, -1)
    const = lambda a: pl.BlockSpec(a.shape, lambda i: (0,) * a.ndim)
    args = [w1p, row(b1), w2, row(b2), w3, row(b3), w4, row(freq), deltas]
    return pl.pallas_call(
        _hy_filter_kernel,
        grid=(n // tr,),
        in_specs=[pl.BlockSpec((tr, 128), lambda i: (i, 0))] + [const(a) for a in args],
        out_specs=pl.BlockSpec((2, tr, GROUP_W), lambda i: (0, i, 0)),
        out_shape=jax.ShapeDtypeStruct((2, n, GROUP_W), F32),
        compiler_params=_cparams(("parallel",)),
        name="hy_filter",
    )(feat, *args)


def _hy_spectrum_kernel(k_ref, m1f_ref, m2_ref, re_ref, im_ref, tr_ref, ti_ref, *, dims):
    n1, n2, pb = dims["n1"], dims["n2"], dims["pb"]

    def stage1(j, carry):
        r0 = pl.multiple_of(j * n1, n1)
        a = jnp.dot(m1f_ref[j], k_ref[pl.ds(r0, n1), :].astype(BF16), preferred_element_type=F32)
        tr_ref[pl.ds(j, n1, stride=pb), :] = a[:n1]
        ti_ref[pl.ds(j, n1, stride=pb), :] = a[n1:]
        return carry

    lax.fori_loop(0, n2, stage1, 0, unroll=HY_UNROLL)

    def stage2(k, carry):
        r0 = pl.multiple_of(k * pb, 8)
        rhs = jnp.concatenate([tr_ref[pl.ds(r0, n2), :], ti_ref[pl.ds(r0, n2), :]], axis=0).astype(BF16)
        x = jnp.dot(m2_ref[...], rhs, preferred_element_type=F32)
        o0 = pl.multiple_of(k * n2, n2)
        re_ref[pl.ds(o0, n2), :] = x[:n2].astype(re_ref.dtype)
        im_ref[pl.ds(o0, n2), :] = x[n2:].astype(im_ref.dtype)
        return carry

    lax.fori_loop(0, n1, stage2, 0, unroll=HY_UNROLL)


def _hy_spectrum(kern, mats, seq):
    d = _hy_dims(seq)
    n, n1, pb = d["n"], d["n1"], d["pb"]
    m1f, _, m2, _, _ = mats
    cb = _hy_cb(seq)
    ncb = GROUP_W // cb
    blk = pl.BlockSpec((None, n, cb), lambda o, j: (o, 0, j))
    const = lambda a: pl.BlockSpec(a.shape, lambda o, j: (0,) * a.ndim)
    return pl.pallas_call(
        functools.partial(_hy_spectrum_kernel, dims=d),
        grid=(2, ncb),
        in_specs=[blk, const(m1f), const(m2)],
        out_specs=[blk, blk],
        out_shape=[jax.ShapeDtypeStruct((2, n, GROUP_W), BF16)] * 2,
        scratch_shapes=[pltpu.VMEM((n1 * pb, cb), F32)] * 2,
        compiler_params=_cparams(("parallel", "parallel")),
        name="hy_spectrum",
    )(kern, m1f, m2)


def _hy_conv_kernel(u_ref, g_ref, kr_ref, ki_ref, skip_ref, m1_ref, m2_ref, m3_ref, m4_ref, o_ref,
                    t1r, t1i, t2r, t2i, *, dims, natural_out):
    n1, n2, h, pb, pc = (dims[k] for k in ("n1", "n2", "h", "pb", "pc"))
    pu, pg, po = u_ref.shape[1] // n2, g_ref.shape[1] // n2, o_ref.shape[1] // n2

    def fwd1(j, carry):
        r0 = pl.multiple_of(j * pu, 8)
        rhs = jnp.concatenate([u_ref[0, pl.ds(r0, h), :], u_ref[1, pl.ds(r0, h), :]], axis=0).astype(BF16)
        a = jnp.dot(m1_ref[j], rhs, preferred_element_type=F32)
        t1r[pl.ds(j, n1, stride=pb), :] = a[:n1]
        t1i[pl.ds(j, n1, stride=pb), :] = a[n1:]
        return carry

    lax.fori_loop(0, n2, fwd1, 0, unroll=HY_UNROLL)

    def mid(k, carry):
        r0 = pl.multiple_of(k * pb, 8)
        rhs = jnp.concatenate([t1r[pl.ds(r0, n2), :], t1i[pl.ds(r0, n2), :]], axis=0).astype(BF16)
        x = jnp.dot(m2_ref[...], rhs, preferred_element_type=F32)
        f0 = pl.multiple_of(k * n2, n2)
        fr, fi = kr_ref[pl.ds(f0, n2), :].astype(F32), ki_ref[pl.ds(f0, n2), :].astype(F32)
        xr, xi = x[:n2], x[n2:]
        y = jnp.concatenate([xr * fr - xi * fi, xr * fi + xi * fr], axis=0).astype(BF16)
        c = jnp.dot(m3_ref[k], y, preferred_element_type=F32)
        t2r[pl.ds(k, n2, stride=pc), :] = c[:n2]
        t2i[pl.ds(k, n2, stride=pc), :] = c[n2:]
        return carry

    lax.fori_loop(0, n1, mid, 0, unroll=HY_UNROLL)

    if not natural_out:
        o_ref[...] = jnp.zeros_like(o_ref)
    skip = skip_ref[...]

    def inv2(j, carry):
        r0 = pl.multiple_of(j * pc, 8)
        rhs = jnp.concatenate([t2r[pl.ds(r0, n1), :], t2i[pl.ds(r0, n1), :]], axis=0).astype(BF16)
        y = jnp.dot(m4_ref[...], rhs, preferred_element_type=F32)
        u0, g0 = pl.multiple_of(j * pu, 8), pl.multiple_of(j * pg, 8)
        for e in range(2):
            val = g_ref[e, pl.ds(g0, h), :] * (y[e * h:(e + 1) * h]
                                               + skip * u_ref[e, pl.ds(u0, h), :].astype(F32))
            if natural_out:
                o_ref[e, pl.ds(j, h, stride=n2), :] = val.astype(o_ref.dtype)
            else:
                o_ref[e, pl.ds(pl.multiple_of(j * po, 16), h), :] = val.astype(o_ref.dtype)
        return carry

    lax.fori_loop(0, n2, inv2, 0, unroll=HY_UNROLL)


def _hy_conv(u_arr, u_col, g_arr, g_col, kf_re, kf_im, order, skip, mats, seq, natural_out):
    d = _hy_dims(seq)
    bsz = u_arr.shape[0]
    u_rows, g_rows = u_arr.shape[1], g_arr.shape[1]
    z_rows = d["n2"] * d["pz"]
    _, m1, m2, m3, m4 = mats
    cb = _hy_cb(seq)
    ncb = GROUP_W // cb
    single = pl.Buffered(1)
    const = lambda a: pl.BlockSpec(a.shape, lambda j, p: (0,) * a.ndim, pipeline_mode=single)
    in_specs = [
        pl.BlockSpec((2, u_rows, cb), lambda j, p: (p, 0, u_col // cb + j)),
        pl.BlockSpec((2, g_rows, cb), lambda j, p: (p, 0, g_col // cb + j)),
        pl.BlockSpec((None, d["n"], cb), lambda j, p: (order, 0, j), pipeline_mode=single),
        pl.BlockSpec((None, d["n"], cb), lambda j, p: (order, 0, j), pipeline_mode=single),
        pl.BlockSpec((None, 1, cb), lambda j, p: (order, 0, j)),
        const(m1), const(m2), const(m3), const(m4),
    ]
    if natural_out:
        out_spec = pl.BlockSpec((2, seq, cb), lambda j, p: (p, 0, j))
        out_shape = jax.ShapeDtypeStruct((bsz, seq, GROUP_W), F32)
    else:
        out_spec = pl.BlockSpec((2, z_rows, cb), lambda j, p: (p, 0, j))
        out_shape = jax.ShapeDtypeStruct((bsz, z_rows, GROUP_W), BF16)
    return pl.pallas_call(
        functools.partial(_hy_conv_kernel, dims=d, natural_out=natural_out),
        grid=(ncb, bsz // 2),
        in_specs=in_specs,
        out_specs=out_spec,
        out_shape=out_shape,
        scratch_shapes=[pltpu.VMEM((d["n1"] * d["pb"], cb), F32)] * 2
        + [pltpu.VMEM((d["n2"] * d["pc"], cb), F32)] * 2,
        compiler_params=_cparams(("parallel", "parallel")),
        name="hy_conv",
    )(u_arr, g_arr, kf_re, kf_im, skip.reshape(2, 1, GROUP_W), m1, m2, m3, m4)


def _hyena_mixer(p, short_w, short_b, filt_params, skip, mats):
    seq = p.shape[1]
    kern = _hy_filter(seq, *filt_params)
    kf_re, kf_im = _hy_spectrum(kern, mats, seq)
    ut = _hy_prep(p, short_w, short_b)
    zt = _hy_conv(ut, 2 * GROUP_W, ut, 0, kf_re, kf_im, 0, skip, mats, seq, natural_out=False)
    return _hy_conv(zt, 0, ut, GROUP_W, kf_re, kf_im, 1, skip, mats, seq, natural_out=True)


CAST_BLOCK_BYTES = 8 * 1024 * 1024


def _cast_kernel(w_ref, o_ref):
    o_ref[...] = w_ref[...].astype(o_ref.dtype)


def _cast_bf16(w):
    depth, k, n = w.shape
    tr = min(k, CAST_BLOCK_BYTES // (4 * n))
    return pl.pallas_call(
        _cast_kernel,
        grid=(depth, k // tr),
        in_specs=[pl.BlockSpec((None, tr, n), lambda i, r: (i, r, 0))],
        out_specs=pl.BlockSpec((None, tr, n), lambda i, r: (i, r, 0)),
        out_shape=jax.ShapeDtypeStruct(w.shape, BF16),
        compiler_params=_cparams(("parallel", "parallel")),
        name="cast_bf16",
    )(w)


def _w_in_layout_kernel(w_ref, o_ref):
    o_a, o_dt = 768, 768 + 3072 + 1536
    for i, head in enumerate(HEAD_ORDER):
        o_ref[HEAD_DIM * i:HEAD_DIM * (i + 1), :] = w_ref[HEAD_DIM * head:HEAD_DIM * (head + 1), :].astype(o_ref.dtype)
    o_ref[COL_KA:COL_DT, :] = w_ref[COL_KA:o_a, :].astype(o_ref.dtype)
    o_ref[COL_DT:COL_DT + 16, :] = w_ref[o_dt:o_dt + 16, :].astype(o_ref.dtype)
    o_ref[COL_DT + 16:COL_HY, :] = jnp.zeros((COL_HY - COL_DT - 16, o_ref.shape[1]), o_ref.dtype)
    o_ref[COL_HY:, :] = w_ref[o_a:o_dt, :].astype(o_ref.dtype)


def _w_out_layout_kernel(w_ref, o_ref):
    for i, head in enumerate(HEAD_ORDER):
        o_ref[HEAD_DIM * i:HEAD_DIM * (i + 1), :] = w_ref[HEAD_DIM * head:HEAD_DIM * (head + 1), :].astype(o_ref.dtype)
    o_ref[GROUP_W:, :] = w_ref[GROUP_W:, :].astype(o_ref.dtype)


def _prep_weights(w_in, w_out):
    depth, d, n_in = w_in.shape
    tk = 256
    w_in_p = pl.pallas_call(
        _w_in_layout_kernel,
        grid=(depth, d // tk),
        in_specs=[pl.BlockSpec((None, n_in, tk), lambda i, c: (i, 0, c))],
        out_specs=pl.BlockSpec((None, N_PROJ, tk), lambda i, c: (i, 0, c)),
        out_shape=jax.ShapeDtypeStruct((depth, N_PROJ, d), BF16),
        compiler_params=_cparams(("parallel", "parallel")),
        name="w_in_layout",
    )(jnp.swapaxes(w_in, 1, 2))
    k_out, dm = w_out.shape[1:]
    tc = 512
    w_out_p = pl.pallas_call(
        _w_out_layout_kernel,
        grid=(depth, dm // tc),
        in_specs=[pl.BlockSpec((None, k_out, tc), lambda i, c: (i, 0, c))],
        out_specs=pl.BlockSpec((None, k_out, tc), lambda i, c: (i, 0, c)),
        out_shape=jax.ShapeDtypeStruct(w_out.shape, BF16),
        compiler_params=_cparams(("parallel", "parallel")),
        name="w_out_layout",
    )(w_out)
    return w_in_p, w_out_p


TILE_W_IN = (512, 2816)
TILE_MLP_UP = (1024, 2048)
TILE_DOWN = (1024, 1024, 2048)
TILE_OUT = (512, 2048)


def kernel(x, c, ctx, c_ctx, ada_w, ada_b, norm_mix, norm_mlp, w_in, w_out, attn_sink,
           hy_short_w, hy_short_b, hy_w1, hy_b1, hy_w2, hy_b2, hy_w3, hy_b3, hy_w4, hy_freq, hy_skip,
           na_rpb, ssm_conv_w, ssm_conv_b, ssm_dt_bias, ssm_a_log, ssm_d, ssm_norm,
           mlp_w1, mlp_w2, final_norm):
    bsz, seq, d = x.shape
    lc = ctx.shape[1]
    depth = ada_w.shape[0]
    assert bsz % 2 == 0 and bsz <= 7 and d == D_MODEL

    cs = jnp.zeros((8, d), F32).at[:bsz].set(c).at[bsz].set(c_ctx)
    mod = _ada_mod(cs, ada_w, ada_b)
    cos_t, sin_t = _rope_tables(seq)
    na_plan = _na_plan(seq)
    bias_tabs = _na_bias_tables(na_rpb, na_plan[-1])
    mats_l = _hy_matrices(seq)
    mats_c = _hy_matrices(lc)
    lat_row = lambda b: b
    ctx_row = lambda b: bsz
    tm_c = lc

    w_in_p, w_out_p = _prep_weights(w_in, w_out)
    w1_b = _cast_bf16(mlp_w1)
    w2_b = _cast_bf16(mlp_w2)

    xc = ctx
    for i in range(depth):
        last = i == depth - 1
        mod3 = mod[i].reshape(8, 1, 6 * d)
        p = _norm_matmul(x, norm_mix[i], mod3, lat_row, 0, 1, w_in_p, i, P_DTYPE, False, *TILE_W_IN, w_t=True)
        pc = _norm_matmul(xc, norm_mix[i], mod3, ctx_row, 0, 1, w_in_p, i, P_DTYPE, False, tm_c, TILE_W_IN[1], w_t=True)

        filt_params = (hy_w1[i], hy_b1[i], hy_w2[i], hy_b2[i], hy_w3[i], hy_b3[i], hy_w4[i], hy_freq[i])
        ssd_consts = _ssd_consts(ssm_conv_w[i], ssm_conv_b[i], ssm_dt_bias[i], ssm_a_log[i], ssm_d[i], ssm_norm[i])

        ya = _win_attn(attn_sink[i], p, pc, cos_t, sin_t, local=True)
        yb = _hyena_mixer(p, hy_short_w[i], hy_short_b[i], filt_params, hy_skip[i], mats_l)
        yn = _na_attn(p, pc, bias_tabs, i, na_plan, local=True)
        yd, ydc = _ssd_mixer(p, pc, ssd_consts)
        x = _out_proj((ya, yb, yn, yd), w_out_p, i, x, mod3, lat_row, 2, *TILE_OUT)
        hid = _norm_matmul(x, norm_mlp[i], mod3, lat_row, 3, 4, w1_b, i, BF16, True, *TILE_MLP_UP)
        x = _matmul_residual(hid, w2_b, i, x, mod3, lat_row, 5, *TILE_DOWN)

        if not last:
            yac = _win_attn(attn_sink[i], pc, pc, None, None, local=False)
            ybc = _hyena_mixer(pc, hy_short_w[i], hy_short_b[i], filt_params, hy_skip[i], mats_c)
            ync = _na_attn(pc, pc, None, i, None, local=False)
            xc = _out_proj((yac, ybc, ync, ydc), w_out_p, i, xc, mod3, ctx_row, 2, tm_c, TILE_OUT[1])
            hidc = _norm_matmul(xc, norm_mlp[i], mod3, ctx_row, 3, 4, w1_b, i, BF16, True, tm_c, TILE_MLP_UP[1])
            xc = _matmul_residual(hidc, w2_b, i, xc, mod3, ctx_row, 5, tm_c, *TILE_DOWN[1:])
    return _final_norm(x, final_norm)
```

```python
import functools
import math

import numpy as np
import jax
import jax.numpy as jnp
from jax import lax
from jax.experimental import pallas as pl
from jax.experimental.pallas import tpu as pltpu

F32 = jnp.float32
BF16 = jnp.bfloat16

D_MODEL = 2048
GRID_W = 64
EPS = 1e-6
NEG = -1e30
HEAD_DIM = 64
GROUP_W = D_MODEL // 4
N_HEADS = GROUP_W // HEAD_DIM
WINDOW = 128
BLOCK = 128
ROPE_BASE = 10000.0
HY_EMB = 33
HY_FFN = 64
HY_DECAY_TARGET = 1e-2
HY_FAST_PCT = 0.3
HY_SLOW_PCT = 1.5
NA_KR = 8
NA_KC = 16
SSM_STATE = 128
SSM_CHUNK = 128
D_FF = 4 * D_MODEL
SCALE = HEAD_DIM ** -0.5
LOG2E = math.log2(math.e)

COL_QA, COL_KA, COL_VA, COL_DT = 0, 512, 640, 768
COL_HY = 1024
COL_NQ, COL_NK, COL_NV = 2560, 3072, 3584
COL_SZ, COL_SX, COL_SBC = 4096, 4608, 5120
N_PROJ = 5632
HEAD_ORDER = (0, 4, 1, 5, 2, 6, 3, 7)

P_DTYPE = BF16
HALO = 16

V7X_VMEM_BYTES = 64 * 1024 * 1024
VMEM_LIMIT = 56 * 1024 * 1024


def _cparams(sem):
    return pltpu.CompilerParams(dimension_semantics=sem, vmem_limit_bytes=VMEM_LIMIT)


def _silu(x):
    return x * jax.nn.sigmoid(x)


def _ada_kernel(cs_ref, w_ref, b_ref, o_ref):
    a = _silu(cs_ref[...]).astype(BF16)
    o_ref[...] = jnp.dot(a, w_ref[...].astype(BF16), preferred_element_type=F32) + b_ref[...]


def _ada_mod(cs, ada_w, ada_b, tn=1024):
    depth, d, n = ada_w.shape
    return pl.pallas_call(
        _ada_kernel,
        grid=(depth, n // tn),
        in_specs=[
            pl.BlockSpec((8, d), lambda i, j: (0, 0)),
            pl.BlockSpec((None, d, tn), lambda i, j: (i, 0, j)),
            pl.BlockSpec((None, 1, tn), lambda i, j: (i, 0, j)),
        ],
        out_specs=pl.BlockSpec((None, 8, tn), lambda i, j: (i, 0, j)),
        out_shape=jax.ShapeDtypeStruct((depth, 8, n), F32),
        compiler_params=_cparams(("parallel", "parallel")),
        name="ada_mod",
    )(cs, ada_w, ada_b.reshape(depth, 1, n))


NORM_ROWS = 16


def _norm_matmul_kernel(x_ref, g_ref, sh_ref, sc_ref, w_ref, o_ref, h_ref, *, act, w_t):
    @pl.when(pl.program_id(2) == 0)
    def _():
        gain = g_ref[...] * (1.0 + sc_ref[...])
        shift = sh_ref[...]

        def rows(r, carry):
            r0 = pl.multiple_of(r * NORM_ROWS, NORM_ROWS)
            xf = x_ref[pl.ds(r0, NORM_ROWS), :]
            ms = jnp.mean(xf * xf, axis=-1, keepdims=True)
            h_ref[pl.ds(r0, NORM_ROWS), :] = (xf * lax.rsqrt(ms + EPS) * gain + shift).astype(BF16)
            return carry

        lax.fori_loop(0, x_ref.shape[0] // NORM_ROWS, rows, 0, unroll=8)

    if w_t:
        r = lax.dot_general(h_ref[...], w_ref[...], (((1,), (1,)), ((), ())), preferred_element_type=F32)
    else:
        r = jnp.dot(h_ref[...], w_ref[...], preferred_element_type=F32)
    if act:
        r = jnp.square(jnp.maximum(r, 0.0))
    o_ref[...] = r.astype(o_ref.dtype)


def _norm_matmul(x, g, mod3, row_of_b, sh_idx, sc_idx, w, layer, out_dtype, act, tm, tn, w_t=False):
    bsz, t, d = x.shape
    n = w.shape[1] if w_t else w.shape[2]
    tm = min(tm, t)
    w_spec = (pl.BlockSpec((None, tn, d), lambda b, m, j: (layer, j, 0)) if w_t
              else pl.BlockSpec((None, d, tn), lambda b, m, j: (layer, 0, j)))
    return pl.pallas_call(
        functools.partial(_norm_matmul_kernel, act=act, w_t=w_t),
        grid=(bsz, t // tm, n // tn),
        in_specs=[
            pl.BlockSpec((None, tm, d), lambda b, m, j: (b, m, 0)),
            pl.BlockSpec((1, d), lambda b, m, j: (0, 0)),
            pl.BlockSpec((None, 1, d), lambda b, m, j: (row_of_b(b), 0, sh_idx)),
            pl.BlockSpec((None, 1, d), lambda b, m, j: (row_of_b(b), 0, sc_idx)),
            w_spec,
        ],
        out_specs=pl.BlockSpec((None, tm, tn), lambda b, m, j: (b, m, j)),
        out_shape=jax.ShapeDtypeStruct((bsz, t, n), out_dtype),
        scratch_shapes=[pltpu.VMEM((tm, d), BF16)],
        compiler_params=_cparams(("parallel", "parallel", "arbitrary")),
        name="norm_matmul",
    )(x, g.reshape(1, d), mod3, mod3, w)


def _mm_res_kernel(a_ref, w_ref, x_ref, gate_ref, *rest, nk, final):
    if final:
        fg_ref, o_ref, acc_ref = rest
    else:
        o_ref, acc_ref = rest
    k = pl.program_id(3)

    @pl.when(k == 0)
    def _():
        acc_ref[...] = jnp.zeros_like(acc_ref)

    acc_ref[...] += jnp.dot(a_ref[...], w_ref[...], preferred_element_type=F32)

    @pl.when(k == nk - 1)
    def _():
        y = x_ref[...] + gate_ref[...] * acc_ref[...]
        if final:
            y = y * lax.rsqrt(jnp.mean(y * y, axis=-1, keepdims=True) + EPS) * fg_ref[...]
        o_ref[...] = y


def _matmul_residual(a, w, layer, x, mod3, row_of_b, gate_idx, tm, tn, tk, final_g=None):
    bsz, t, kdim = a.shape
    n = w.shape[2]
    tm = min(tm, t)
    nk = kdim // tk
    final = final_g is not None
    assert not final or tn == n
    in_specs = [
        pl.BlockSpec((None, tm, tk), lambda b, m, j, k: (b, m, k)),
        pl.BlockSpec((None, tk, tn), lambda b, m, j, k: (layer, k, j)),
        pl.BlockSpec((None, tm, tn), lambda b, m, j, k: (b, m, j)),
        pl.BlockSpec((None, 1, tn), lambda b, m, j, k: (row_of_b(b), 0, gate_idx * (D_MODEL // tn) + j)),
    ]
    args = [a, w, x, mod3]
    if final:
        in_specs.append(pl.BlockSpec((1, n), lambda b, m, j, k: (0, 0)))
        args.append(final_g.reshape(1, n))
    return pl.pallas_call(
        functools.partial(_mm_res_kernel, nk=nk, final=final),
        grid=(bsz, t // tm, n // tn, nk),
        in_specs=in_specs,
        out_specs=pl.BlockSpec((None, tm, tn), lambda b, m, j, k: (b, m, j)),
        out_shape=jax.ShapeDtypeStruct(x.shape, F32),
        scratch_shapes=[pltpu.VMEM((tm, tn), F32)],
        compiler_params=_cparams(("parallel", "parallel", "parallel", "arbitrary")),
        name="matmul_residual",
    )(*args)


def _out_proj_kernel(ya_ref, yb_ref, yn_ref, yd_ref, w_ref, x_ref, gate_ref, o_ref):
    acc = None
    for g, y_ref in enumerate((ya_ref, yb_ref, yn_ref, yd_ref)):
        part = jnp.dot(y_ref[...].astype(BF16), w_ref[GROUP_W * g:GROUP_W * (g + 1), :],
                       preferred_element_type=F32)
        acc = part if acc is None else acc + part
    o_ref[...] = x_ref[...] + gate_ref[...] * acc


def _out_proj(ys, w, layer, x, mod3, row_of_b, gate_idx, tm, tn):
    bsz, t, d = x.shape
    tm = min(tm, t)
    y_spec = pl.BlockSpec((None, tm, GROUP_W), lambda b, m, j: (b, m, 0))
    return pl.pallas_call(
        _out_proj_kernel,
        grid=(bsz, t // tm, d // tn),
        in_specs=[y_spec, y_spec, y_spec, y_spec,
                  pl.BlockSpec((None, 4 * GROUP_W, tn), lambda b, m, j: (layer, 0, j),
                               pipeline_mode=pl.Buffered(1) if tn == d else None),
                  pl.BlockSpec((None, tm, tn), lambda b, m, j: (b, m, j)),
                  pl.BlockSpec((None, 1, tn), lambda b, m, j: (row_of_b(b), 0, gate_idx * (d // tn) + j))],
        out_specs=pl.BlockSpec((None, tm, tn), lambda b, m, j: (b, m, j)),
        out_shape=jax.ShapeDtypeStruct(x.shape, F32),
        compiler_params=_cparams(("parallel", "parallel", "parallel")),
        name="out_proj",
    )(*ys, w, x, mod3)


def _final_norm_kernel(x_ref, g_ref, o_ref):
    xf = x_ref[...]
    ms = jnp.mean(xf * xf, axis=-1, keepdims=True)
    o_ref[...] = xf * lax.rsqrt(ms + EPS) * g_ref[...]


def _final_norm(x, g, tm=1024):
    bsz, t, d = x.shape
    tm = min(tm, t)
    return pl.pallas_call(
        _final_norm_kernel,
        grid=(bsz, t // tm),
        in_specs=[pl.BlockSpec((None, tm, d), lambda b, m: (b, m, 0)),
                  pl.BlockSpec((1, d), lambda b, m: (0, 0))],
        out_specs=pl.BlockSpec((None, tm, d), lambda b, m: (b, m, 0)),
        out_shape=jax.ShapeDtypeStruct(x.shape, F32),
        compiler_params=_cparams(("parallel", "parallel")),
        name="final_norm",
    )(x, g.reshape(1, d))


WIN_QBLOCKS = 4


def _rope(x, cos, sin_signed, lane_lo):
    w = x.shape[-1]
    partner = jnp.where(lane_lo, pltpu.roll(x, w - 16, 1), pltpu.roll(x, 16, 1))
    return x * cos + partner * sin_signed


def _win_attn_kernel(sink_ref, q_ref, kc_ref, vc_ref, *rest, seq, local):
    if local:
        k_ref, v_ref, cos_ref, sin_ref, mask_ref, o_ref = rest
    else:
        (o_ref,) = rest
    nb = seq // BLOCK
    lane = lax.broadcasted_iota(jnp.int32, (BLOCK, 128), 1)
    lo = lane < HEAD_DIM
    lane_lo = (lane % 32) < 16
    lane3_lo = (lax.broadcasted_iota(jnp.int32, (3 * BLOCK, 128), 1) % 32) < 16
    nt = (((1,), (1,)), ((), ()))
    kc = kc_ref[...].astype(BF16)
    vc = vc_ref[...].astype(BF16)
    sink = jnp.concatenate(
        [jnp.full((BLOCK, 1), sink_ref[HEAD_ORDER[i]] * LOG2E, F32) for i in range(N_HEADS)], axis=0)
    nq = q_ref.shape[0] // BLOCK
    for sub in range(nq):
        n = pl.program_id(1) * nq + sub
        q = q_ref[sub * BLOCK:(sub + 1) * BLOCK, :].astype(F32) * (SCALE * LOG2E)
        if local:
            r0 = pl.multiple_of(n * BLOCK, BLOCK)
            cos_q = cos_ref[pl.ds(r0, BLOCK), :]
            sin_q = sin_ref[pl.ds(r0, BLOCK), :]
        rows = []
        for m in range(4):
            qm = q[:, 128 * m:128 * (m + 1)]
            if local:
                qm = _rope(qm, cos_q, sin_q, lane_lo)
            rows.append(jnp.where(lo, qm, 0.0))
            rows.append(jnp.where(lo, 0.0, qm))
        qbd = jnp.concatenate(rows, axis=0).astype(BF16)
        s_ctx = lax.dot_general(qbd, kc, nt, preferred_element_type=F32)
        mx = jnp.maximum(jnp.max(s_ctx, axis=-1, keepdims=True), sink)
        if local:
            start = pl.multiple_of(jnp.clip((n - 1) * BLOCK, 0, seq - 3 * BLOCK), BLOCK)
            kb = _rope(k_ref[pl.ds(start, 3 * BLOCK), :].astype(F32), cos_ref[pl.ds(start, 3 * BLOCK), :],
                       sin_ref[pl.ds(start, 3 * BLOCK), :], lane3_lo).astype(BF16)
            vb = v_ref[pl.ds(start, 3 * BLOCK), :].astype(BF16)
            mask = mask_ref[jnp.where(n == 0, 0, jnp.where(n == nb - 1, 2, 1))]
            s_loc = lax.dot_general(qbd, kb, nt, preferred_element_type=F32)
            s_loc = (s_loc.reshape(N_HEADS, BLOCK, 3 * BLOCK) + mask[None]).reshape(s_loc.shape)
            mx = jnp.maximum(mx, jnp.max(s_loc, axis=-1, keepdims=True))
        p_ctx = jnp.exp2(s_ctx - mx)
        den = jnp.sum(p_ctx, axis=-1, keepdims=True) + jnp.exp2(sink - mx)
        acc = jnp.dot(p_ctx.astype(BF16), vc, preferred_element_type=F32)
        if local:
            p_loc = jnp.exp2(s_loc - mx)
            den = den + jnp.sum(p_loc, axis=-1, keepdims=True)
            acc = acc + jnp.dot(p_loc.astype(BF16), vb, preferred_element_type=F32)
        o = acc / den
        outs = [jnp.where(lo, o[(2 * m) * BLOCK:(2 * m + 1) * BLOCK], o[(2 * m + 1) * BLOCK:(2 * m + 2) * BLOCK])
                for m in range(4)]
        o_ref[sub * BLOCK:(sub + 1) * BLOCK, :] = jnp.concatenate(outs, axis=1).astype(o_ref.dtype)


def _win_attn(sink, pq, pc, cos_t, sin_t, local):
    bsz, t, _ = pq.shape
    lc = pc.shape[1]
    tq = min(WIN_QBLOCKS * BLOCK, t)
    in_specs = [
        pl.BlockSpec(memory_space=pltpu.SMEM),
        pl.BlockSpec((None, tq, 512), lambda b, n: (b, n, COL_QA // 512)),
        pl.BlockSpec((None, lc, 128), lambda b, n: (b, 0, COL_KA // 128)),
        pl.BlockSpec((None, lc, 128), lambda b, n: (b, 0, COL_VA // 128)),
    ]
    args = [sink, pq, pc, pc]
    if local:
        in_specs += [
            pl.BlockSpec((None, t, 128), lambda b, n: (b, 0, COL_KA // 128)),
            pl.BlockSpec((None, t, 128), lambda b, n: (b, 0, COL_VA // 128)),
            pl.BlockSpec((t, 128), lambda b, n: (0, 0)),
            pl.BlockSpec((t, 128), lambda b, n: (0, 0)),
            pl.BlockSpec((3, BLOCK, 3 * BLOCK), lambda b, n: (0, 0, 0)),
        ]
        args += [pq, pq, cos_t, sin_t, _window_masks(t)]
    return pl.pallas_call(
        functools.partial(_win_attn_kernel, seq=t, local=local),
        grid=(bsz, t // tq),
        in_specs=in_specs,
        out_specs=pl.BlockSpec((None, tq, GROUP_W), lambda b, n: (b, n, 0)),
        out_shape=jax.ShapeDtypeStruct((bsz, t, GROUP_W), BF16),
        compiler_params=_cparams(("parallel", "arbitrary")),
        name="win_attn" if local else "ctx_attn_a",
    )(*args)


def _window_masks(seq):
    nb = seq // BLOCK
    qi = np.arange(BLOCK)[:, None]
    kj = np.arange(3 * BLOCK)[None, :]
    tabs = []
    for n in (0, 1, nb - 1):
        start = int(np.clip((n - 1) * BLOCK, 0, seq - 3 * BLOCK))
        rel = (start + kj) - (n * BLOCK + qi)
        tabs.append(np.where(np.abs(rel) <= WINDOW, 0.0, NEG))
    return jnp.asarray(np.stack(tabs), F32)


def _rope_tables(seq):
    t = np.arange(seq)
    row, col = t // GRID_W, t % GRID_W
    quarter = HEAD_DIM // 4
    inv = ROPE_BASE ** (-np.arange(quarter, dtype=np.float64) / quarter)
    inv = inv.astype(np.float32).astype(np.float64)
    lane = np.arange(128)
    j = lane % HEAD_DIM
    pos = np.where((j < HEAD_DIM // 2)[None, :], row[:, None], col[:, None]).astype(np.float64)
    ang = (pos * inv[j % quarter][None, :]).astype(np.float32)
    cos = np.cos(ang.astype(np.float64))
    sin = np.sin(ang.astype(np.float64))
    sign = np.where((j % 32) < 16, -1.0, 1.0)[None, :]
    return jnp.asarray(cos, F32), jnp.asarray(sin * sign, F32)


NA_ROWS_PER_STEP = 4


def _na_kernel(var_ref, ws_ref, q_ref, kc_ref, vc_ref, *rest, local, win_rows):
    del var_ref
    if local:
        k_ref, v_ref, bias_ref, o_ref = rest
    else:
        (o_ref,) = rest
    g = pl.program_id(1)
    tq = q_ref.shape[0]
    q = q_ref[...].astype(F32) * (SCALE * LOG2E)
    head = lax.broadcasted_iota(jnp.int32, (tq, 256), 1) // HEAD_DIM
    nt = (((1,), (1,)), ((), ()))
    outs = []
    for half in range(2):
        cols = slice(256 * half, 256 * (half + 1))
        q4 = q[:, cols]
        qbd = jnp.concatenate([jnp.where(head == h, q4, 0.0) for h in range(4)], axis=0).astype(BF16)
        kc4 = kc_ref[:, cols].astype(BF16)
        vc4 = vc_ref[:, cols].astype(BF16)
        s_ctx = lax.dot_general(qbd, kc4, nt, preferred_element_type=F32)
        mx = jnp.max(s_ctx, axis=-1, keepdims=True)
        if local:
            nkey = win_rows * GRID_W
            start = pl.multiple_of(ws_ref[g] * GRID_W, GRID_W)
            k4 = k_ref[pl.ds(start, nkey), cols].astype(BF16)
            v4 = v_ref[pl.ds(start, nkey), cols].astype(BF16)
            bias = bias_ref[4 * half:4 * half + 4].astype(F32).reshape(4 * tq, nkey)
            s_loc = lax.dot_general(qbd, k4, nt, preferred_element_type=F32) + bias
            mx = jnp.maximum(mx, jnp.max(s_loc, axis=-1, keepdims=True))
        p_ctx = jnp.exp2(s_ctx - mx)
        den = jnp.sum(p_ctx, axis=-1, keepdims=True)
        acc = jnp.dot(p_ctx.astype(BF16), vc4, preferred_element_type=F32)
        if local:
            p_loc = jnp.exp2(s_loc - mx)
            den = den + jnp.sum(p_loc, axis=-1, keepdims=True)
            acc = acc + jnp.dot(p_loc.astype(BF16), v4, preferred_element_type=F32)
        o = acc / den
        o4 = jnp.where(head == 0, o[0:tq], 0.0)
        for h in range(1, 4):
            o4 = o4 + jnp.where(head == h, o[h * tq:(h + 1) * tq], 0.0)
        outs.append(o4)
    o_ref[...] = jnp.concatenate(outs, axis=1).astype(o_ref.dtype)


def _na_plan(seq):
    rows = seq // GRID_W
    kr = min(NA_KR, rows)
    r_step = NA_ROWS_PER_STEP
    win_rows = min(r_step + kr, rows)
    n_groups = rows // r_step
    wstart = np.zeros(n_groups, np.int32)
    pats = []
    keys = {}
    var = np.zeros(n_groups, np.int32)
    for g in range(n_groups):
        r0 = g * r_step
        ws = int(np.clip(r0 - kr // 2, 0, rows - win_rows))
        wstart[g] = ws
        r = r0 + np.arange(r_step)
        rstart = np.clip(r - kr // 2, 0, rows - kr)
        krow = ws + np.arange(win_rows)
        valid = (krow[None, :] >= rstart[:, None]) & (krow[None, :] < rstart[:, None] + kr)
        roff = krow[None, :] - r[:, None] + NA_KR - 1
        key = (valid.tobytes(), np.where(valid, roff, 0).tobytes())
        if key not in keys:
            keys[key] = len(pats)
            pats.append((valid, np.where(valid, roff, 0)))
        var[g] = keys[key]
    return rows, win_rows, n_groups, wstart, var, pats


def _na_bias_tables(rpb, pats):
    cq = np.arange(GRID_W)
    ck = np.arange(GRID_W)
    cstart = np.clip(cq - NA_KC // 2, 0, GRID_W - NA_KC)
    col_valid = (ck[None] >= cstart[:, None]) & (ck[None] < cstart[:, None] + NA_KC)
    coff = np.clip(ck[None] - cq[:, None], -(NA_KC - 1), NA_KC - 1) + NA_KC - 1
    by_col = jnp.where(col_valid, rpb[..., coff] * LOG2E, NEG).astype(BF16)
    depth, heads, n_ro = by_col.shape[:3]
    flat = by_col.transpose(0, 1, 3, 2, 4).reshape(depth, heads, GRID_W, n_ro * GRID_W)
    tabs = []
    for valid, roff in pats:
        r_step, win_rows = valid.shape
        q_rows = []
        for i in range(r_step):
            a_ok = np.nonzero(valid[i])[0]
            a0, a1 = int(a_ok[0]), int(a_ok[-1]) + 1
            assert valid[i, a0:a1].all() and (np.diff(roff[i, a0:a1]) == 1).all()
            seg = flat[..., int(roff[i, a0]) * GRID_W:(int(roff[i, a0]) + a1 - a0) * GRID_W]
            q_rows.append(jnp.pad(seg, ((0, 0), (0, 0), (0, 0), (a0 * GRID_W, (win_rows - a1) * GRID_W)),
                                  constant_values=NEG))
        tabs.append(jnp.concatenate(q_rows, axis=-2))
    return jnp.stack(tabs, axis=1)


def _na_attn(pq, pc, bias_tabs, layer, plan, local):
    bsz, t, _ = pq.shape
    lc = pc.shape[1]
    if local:
        rows, win_rows, n_groups, wstart, var, _ = plan
        tq = NA_ROWS_PER_STEP * GRID_W
    else:
        win_rows, n_groups, tq = 0, 1, t
        wstart = np.zeros(1, np.int32)
        var = np.zeros(1, np.int32)
    in_specs = [
        pl.BlockSpec((None, tq, 512), lambda b, g, vr, ws: (b, g, COL_NQ // 512)),
        pl.BlockSpec((None, lc, 512), lambda b, g, vr, ws: (b, 0, COL_NK // 512)),
        pl.BlockSpec((None, lc, 512), lambda b, g, vr, ws: (b, 0, COL_NV // 512)),
    ]
    args = [pq, pc, pc]
    if local:
        in_specs += [
            pl.BlockSpec((None, t, 512), lambda b, g, vr, ws: (b, 0, COL_NK // 512)),
            pl.BlockSpec((None, t, 512), lambda b, g, vr, ws: (b, 0, COL_NV // 512)),
            pl.BlockSpec((None, None, N_HEADS, tq, win_rows * GRID_W),
                         lambda b, g, vr, ws: (layer, vr[g], 0, 0, 0)),
        ]
        args += [pq, pq, bias_tabs]
    grid_spec = pltpu.PrefetchScalarGridSpec(
        num_scalar_prefetch=2,
        grid=(bsz, n_groups),
        in_specs=in_specs,
        out_specs=pl.BlockSpec((None, tq, GROUP_W), lambda b, g, vr, ws: (b, g, 0)),
    )
    return pl.pallas_call(
        functools.partial(_na_kernel, local=local, win_rows=win_rows),
        grid_spec=grid_spec,
        out_shape=jax.ShapeDtypeStruct((bsz, t, GROUP_W), BF16),
        compiler_params=_cparams(("parallel", "arbitrary")),
        name="na_attn" if local else "ctx_attn_c",
    )(jnp.asarray(var), jnp.asarray(wstart), *args)


SSD_CHUNKS = 4


def _softplus(x):
    return jnp.maximum(x, 0.0) + jnp.log(1.0 + jnp.exp(-jnp.abs(x)))


def _bf16_parts(x, n):
    parts = []
    for _ in range(n):
        part = x.astype(BF16)
        parts.append(part)
        x = x - part.astype(F32)
    return parts


def _conv3_silu(cur, prev_blk, next_blk, w_ref, b_ref, has_prev, has_next):
    x = cur.astype(F32)
    rows = x.shape[0]
    prev_row = prev_blk.astype(F32)[HALO - 1:HALO, :] * has_prev
    next_row = next_blk.astype(F32)[0:1, :] * has_next
    ri = lax.broadcasted_iota(jnp.int32, x.shape, 0)
    up = jnp.where(ri == 0, prev_row, pltpu.roll(x, 1, 0))
    dn = jnp.where(ri == rows - 1, next_row, pltpu.roll(x, rows - 1, 0))
    u = up * w_ref[0:1, :] + x * w_ref[1:2, :] + dn * w_ref[2:3, :] + b_ref[...]
    return _silu(u)


def _ssd_chunk(xs, bc, dt, a, st_ref, keep, tri, expand, head4, d_off, reverse):
    L = SSM_CHUNK
    nt = (((1,), (1,)), ((), ()))
    a_t = a.T
    c_col = sum(jnp.dot(tri, part, preferred_element_type=F32) for part in _bf16_parts(a, 3))
    c_row = sum(lax.dot_general(part, tri, nt, preferred_element_type=F32) for part in _bf16_parts(a_t, 3))
    c_exp = sum(jnp.dot(part, expand, preferred_element_type=F32) for part in _bf16_parts(c_col, 2))
    dt_exp = sum(jnp.dot(part, expand, preferred_element_type=F32) for part in _bf16_parts(dt, 2))
    end = 0 if reverse else L - 1
    cend = c_exp[end:end + 1, :]
    x_dt = xs * dt_exp
    out_decay = jnp.exp(c_exp)
    x_dec = x_dt * jnp.exp(cend - c_exp)
    chunk_decay = jnp.exp(cend)

    ys = []
    for g in range(2):
        gl = slice(256 * g, 256 * (g + 1))
        b_g = bc[:, 128 * g:128 * (g + 1)]
        c_g = bc[:, 256 + 128 * g:256 + 128 * (g + 1)].astype(BF16)
        cb = lax.dot_general(c_g, b_g.astype(BF16), nt, preferred_element_type=F32)
        ms = []
        for hh in range(4):
            j = d_off + 4 * g + hh
            diff = c_col[:, j:j + 1] - c_row[j:j + 1, :]
            ms.append(cb * jnp.exp(jnp.where(keep, diff, NEG)))
        m_g = jnp.concatenate(ms, axis=0).astype(BF16)
        o = jnp.dot(m_g, x_dt[:, gl].astype(BF16), preferred_element_type=F32)
        y_diag = jnp.where(head4 == 0, o[0:L], 0.0)
        for hh in range(1, 4):
            y_diag = y_diag + jnp.where(head4 == hh, o[hh * L:(hh + 1) * L], 0.0)
        st = st_ref[g]
        y_off = jnp.dot(c_g, st.astype(BF16), preferred_element_type=F32) * out_decay[:, gl]
        ys.append(y_diag + y_off)
        st_ref[g] = chunk_decay[:, gl] * st + jnp.dot(
            b_g.T.astype(BF16), x_dec[:, gl].astype(BF16), preferred_element_type=F32)
    return jnp.concatenate(ys, axis=1)


def _ssd_kernel(*refs, reverse, nc):
    finalize = reverse
    if reverse:
        (xsc_ref, bcc_ref, dt_ref, dtb_ref, alog_ref, h0_ref,
         z_ref, yf_ref, dskip_ref, nw_ref, y_ref, ht_ref, st_ref) = refs
    else:
        (xs_ref, bc_ref, xsp_ref, xsn_ref, bcp_ref, bcn_ref, dt_ref, cwx_ref, cbx_ref, cwb_ref, cbb_ref,
         dtb_ref, alog_ref, h0_ref, y_ref, ht_ref, xsc_ref, bcc_ref, st_ref) = refs
    c = pl.program_id(1)
    cid = (nc - 1 - c) if reverse else c
    d_off = 8 if reverse else 0
    L = SSM_CHUNK
    n_sub = dt_ref.shape[0] // L

    @pl.when(c == 0)
    def _():
        st_ref[...] = h0_ref[...]

    if reverse:
        xs = xsc_ref[...]
        bc = bcc_ref[...].astype(F32)
    else:
        has_prev = jnp.where(cid > 0, 1.0, 0.0).astype(F32)
        has_next = jnp.where(cid < nc - 1, 1.0, 0.0).astype(F32)
        xs = _conv3_silu(xs_ref[...], xsp_ref[...], xsn_ref[...], cwx_ref, cbx_ref, has_prev, has_next)
        bc = _conv3_silu(bc_ref[...], bcp_ref[...], bcn_ref[...], cwb_ref, cbb_ref, has_prev, has_next)
        xsc_ref[...] = xs
        bcc_ref[...] = bc.astype(BF16)

    dt_all = _softplus(dt_ref[...].astype(F32) + dtb_ref[...])
    a_all = dt_all * (-jnp.exp(alog_ref[...]))
    ri = lax.broadcasted_iota(jnp.int32, (L, L), 0)
    ci = lax.broadcasted_iota(jnp.int32, (L, L), 1)
    keep = (ci >= ri) if reverse else (ci <= ri)
    tri = keep.astype(BF16)
    nt = (((1,), (1,)), ((), ()))
    ej = lax.broadcasted_iota(jnp.int32, (128, GROUP_W), 0)
    eh = lax.broadcasted_iota(jnp.int32, (128, GROUP_W), 1) // HEAD_DIM
    expand = (ej == eh + d_off).astype(BF16)
    head4 = lax.broadcasted_iota(jnp.int32, (L, 256), 1) // HEAD_DIM
    xs_all, bc_all = xs, bc
    y_sub = [None] * n_sub
    for sub in (range(n_sub - 1, -1, -1) if reverse else range(n_sub)):
        y_sub[sub] = _ssd_chunk(xs_all[sub * L:(sub + 1) * L], bc_all[sub * L:(sub + 1) * L],
                                dt_all[sub * L:(sub + 1) * L], a_all[sub * L:(sub + 1) * L],
                                st_ref, keep, tri, expand, head4, d_off, reverse)
    y = jnp.concatenate(y_sub, axis=0) if n_sub > 1 else y_sub[0]
    if finalize:
        y = y + yf_ref[...] + xs * dskip_ref[...]
        y = y * _silu(z_ref[...].astype(F32))
        halves = []
        for g in range(2):
            yg = y[:, 256 * g:256 * (g + 1)]
            halves.append(yg * lax.rsqrt(jnp.mean(yg * yg, axis=-1, keepdims=True) + EPS))
        y = jnp.concatenate(halves, axis=1) * nw_ref[...]
    y_ref[...] = y.astype(y_ref.dtype)

    @pl.when(c == nc - 1)
    def _():
        ht_ref[...] = st_ref[...]


def _ssd_direction(p, consts, h0, reverse, fwd=None):
    cwx, cbx, cwb, cbb, dtb, alog, dskip, nw = consts
    bsz, t, _ = p.shape
    rows = min(SSD_CHUNKS * SSM_CHUNK, t)
    nc = t // rows
    hb = rows // HALO
    nhalo = t // HALO

    def cid(c):
        return (nc - 1 - c) if reverse else c

    def cur(col, width):
        return pl.BlockSpec((None, rows, width), lambda b, c: (b, cid(c), col // width))

    def prev(col, width):
        return pl.BlockSpec((None, HALO, width), lambda b, c: (b, jnp.maximum(cid(c) * hb - 1, 0), col // width))

    def nxt(col, width):
        return pl.BlockSpec((None, HALO, width), lambda b, c: (b, jnp.minimum((cid(c) + 1) * hb, nhalo - 1), col // width))

    def const(arr):
        return pl.BlockSpec(arr.shape, lambda b, c: (0,) * arr.ndim)

    state_spec = pl.BlockSpec((None, 2, SSM_STATE, 256), lambda b, c: (b, 0, 0, 0))
    chunk_spec = pl.BlockSpec((None, rows, GROUP_W), lambda b, c: (b, cid(c), 0))
    state_shape = jax.ShapeDtypeStruct((bsz, 2, SSM_STATE, 256), F32)
    if reverse:
        yf, xs_act, bc_act = fwd
        in_specs = [chunk_spec, chunk_spec, cur(COL_DT, 128), const(dtb), const(alog), state_spec,
                    cur(COL_SZ, 512), chunk_spec, const(dskip), const(nw)]
        args = [xs_act, bc_act, p, dtb, alog, h0, p, yf, dskip, nw]
        out_specs = [chunk_spec, state_spec]
        out_shape = [jax.ShapeDtypeStruct((bsz, t, GROUP_W), BF16), state_shape]
    else:
        in_specs = [cur(COL_SX, 512), cur(COL_SBC, 512), prev(COL_SX, 512), nxt(COL_SX, 512),
                    prev(COL_SBC, 512), nxt(COL_SBC, 512), cur(COL_DT, 128),
                    const(cwx), const(cbx), const(cwb), const(cbb), const(dtb), const(alog), state_spec]
        args = [p, p, p, p, p, p, p, cwx, cbx, cwb, cbb, dtb, alog, h0]
        out_specs = [chunk_spec, state_spec, chunk_spec, chunk_spec]
        out_shape = [jax.ShapeDtypeStruct((bsz, t, GROUP_W), F32), state_shape,
                     jax.ShapeDtypeStruct((bsz, t, GROUP_W), F32), jax.ShapeDtypeStruct((bsz, t, GROUP_W), BF16)]
    return pl.pallas_call(
        functools.partial(_ssd_kernel, reverse=reverse, nc=nc),
        grid=(bsz, nc),
        in_specs=in_specs,
        out_specs=out_specs,
        out_shape=out_shape,
        scratch_shapes=[pltpu.VMEM((2, SSM_STATE, 256), F32)],
        compiler_params=_cparams(("parallel", "arbitrary")),
        name="ssd_rev" if reverse else "ssd_fwd",
    )(*args)


def _ssd_consts(conv_w, conv_b, dt_bias, a_log, d_skip, norm_w):
    cwx, cwb = conv_w[:, :GROUP_W], conv_w[:, GROUP_W:]
    cbx, cbb = conv_b[:GROUP_W].reshape(1, -1), conv_b[GROUP_W:].reshape(1, -1)
    pad = lambda v: jnp.pad(v.reshape(1, -1), ((0, 0), (0, 128 - v.size)))
    return (cwx, cbx, cwb, cbb, pad(dt_bias), pad(a_log),
            jnp.repeat(d_skip, HEAD_DIM).reshape(1, -1), norm_w.reshape(1, -1))


def _ssd_mixer(p, pc, consts):
    bsz = p.shape[0]
    zero = jnp.zeros((bsz, 2, SSM_STATE, 256), F32)
    ycf, hf, *act_c = _ssd_direction(pc, consts, zero, False)
    yc, hb = _ssd_direction(pc, consts, zero, True, fwd=(ycf, *act_c))
    ylf, _, *act_l = _ssd_direction(p, consts, hf, False)
    yl, _ = _ssd_direction(p, consts, hb, True, fwd=(ylf, *act_l))
    return yl, yc


HY_CB = 128


def _hy_cb(seq):
    del seq
    return HY_CB
HY_UNROLL = 16


def _hy_dims(seq):
    n = 2 * seq
    n1 = {4096: 128, 1024: 64, 512: 32, 256: 32, 128: 16}[seq]
    n2 = n // n1
    h = n1 // 2
    return dict(n=n, n1=n1, n2=n2, h=h, pa=h + 8, pz=h + 16, pb=n2 + 8, pc=n1 + 8)


def _hy_matrices(seq):
    d = _hy_dims(seq)
    n, n1, n2, h = d["n"], d["n1"], d["n2"], d["h"]

    def cis(num, den, sign):
        ang = (2.0 * math.pi / den) * (num % den).astype(F32)
        return jnp.cos(ang), sign * jnp.sin(ang)

    k1 = jnp.arange(n1, dtype=jnp.int32)
    nn = n2 * jnp.arange(n1, dtype=jnp.int32)[None, None, :] + jnp.arange(n2, dtype=jnp.int32)[:, None, None]
    e1r, e1i = cis(k1[None, :, None] * nn, n, -1.0)
    m1f = jnp.concatenate([e1r, e1i], axis=1)
    m1 = jnp.concatenate([jnp.concatenate([e1r[..., :h], -e1i[..., :h]], axis=2),
                          jnp.concatenate([e1i[..., :h], e1r[..., :h]], axis=2)], axis=1)
    a2 = jnp.arange(n2, dtype=jnp.int32)
    g2r, g2i = cis(a2[:, None] * a2[None, :], n2, -1.0)
    m2 = jnp.concatenate([jnp.concatenate([g2r, -g2i], axis=1),
                          jnp.concatenate([g2i, g2r], axis=1)], axis=0)
    num3 = (a2[None, :, None] * a2[None, None, :] * n1 + k1[:, None, None] * a2[None, :, None])
    e3r, e3i = cis(num3, n, 1.0)
    m3 = jnp.concatenate([jnp.concatenate([e3r, -e3i], axis=2),
                          jnp.concatenate([e3i, e3r], axis=2)], axis=1)
    hh = jnp.arange(h, dtype=jnp.int32)
    d4r, d4i = cis(hh[:, None] * k1[None, :], n1, 1.0)
    m4 = jnp.concatenate([jnp.concatenate([d4r, -d4i], axis=1),
                          jnp.concatenate([d4i, d4r], axis=1)], axis=0) / n
    return tuple(m.astype(BF16) for m in (m1f, m1, m2, m3, m4))


def _hy_prep_kernel(p_ref, w_ref, b_ref, o_ref, *, dims):
    n2, h, pa = dims["n2"], dims["h"], dims["pa"]
    seq = h * n2
    for j in range(n2):
        o_ref[j * pa + h:(j + 1) * pa, :] = jnp.zeros((pa - h, o_ref.shape[-1]), o_ref.dtype)
    w0, w1, w2, bias = w_ref[0:1, :], w_ref[1:2, :], w_ref[2:3, :], b_ref[...]
    ri = lax.broadcasted_iota(jnp.int32, (n2, o_ref.shape[-1]), 0)

    def body(i, carry):
        r0 = pl.multiple_of(i * n2, n2)
        x = p_ref[pl.ds(r0, n2), :].astype(F32)
        pstart = pl.multiple_of(jnp.maximum(r0 - HALO, 0), HALO)
        nstart = pl.multiple_of(jnp.minimum(r0 + n2, seq - HALO), HALO)
        prev_row = p_ref[pl.ds(pstart, HALO), :].astype(F32)[HALO - 1:HALO, :] * jnp.where(i > 0, 1.0, 0.0).astype(F32)
        next_row = p_ref[pl.ds(nstart, HALO), :].astype(F32)[0:1, :] * jnp.where(i < h - 1, 1.0, 0.0).astype(F32)
        up = jnp.where(ri == 0, prev_row, pltpu.roll(x, 1, 0))
        dn = jnp.where(ri == n2 - 1, next_row, pltpu.roll(x, n2 - 1, 0))
        o_ref[pl.ds(i, n2, stride=pa), :] = up * w0 + x * w1 + dn * w2 + bias
        return carry

    lax.fori_loop(0, h, body, 0)


def _hy_prep(p, short_w, short_b):
    bsz, t, _ = p.shape
    dims = _hy_dims(t)
    rows = dims["n2"] * dims["pa"]
    cb = _hy_cb(t)
    ncb = 3 * GROUP_W // cb
    return pl.pallas_call(
        functools.partial(_hy_prep_kernel, dims=dims),
        grid=(bsz, ncb),
        in_specs=[pl.BlockSpec((None, t, cb), lambda b, j: (b, 0, COL_HY // cb + j)),
                  pl.BlockSpec((3, cb), lambda b, j: (0, j)),
                  pl.BlockSpec((1, cb), lambda b, j: (0, j))],
        out_specs=pl.BlockSpec((None, rows, cb), lambda b, j: (b, 0, j)),
        out_shape=jax.ShapeDtypeStruct((bsz, rows, 3 * GROUP_W), F32),
        compiler_params=_cparams(("parallel", "parallel")),
        name="hy_prep",
    )(p, short_w, short_b.reshape(1, -1))


def _hy_filter_kernel(z_ref, w1_ref, b1_ref, w2_ref, b2_ref, w3_ref, b3_ref, w4_ref, fr_ref, dl_ref, o_ref):
    hi = lax.Precision.HIGHEST
    z = z_ref[...]
    fr = fr_ref[...]
    h = jnp.sin(fr * (jnp.dot(z, w1_ref[...], preferred_element_type=F32, precision=hi) + b1_ref[...]))
    h = jnp.sin(fr * (jnp.dot(h, w2_ref[...], preferred_element_type=F32, precision=hi) + b2_ref[...]))
    h = jnp.sin(fr * (jnp.dot(h, w3_ref[...], preferred_element_type=F32, precision=hi) + b3_ref[...]))
    full = jnp.dot(h.astype(BF16), w4_ref[...].astype(BF16), preferred_element_type=F32)
    t = z[:, 0:1]
    is_bwd = z[:, HY_EMB:HY_EMB + 1] > 0.5
    live = z[:, HY_EMB + 1:HY_EMB + 2]
    decay = jnp.exp(-t * jnp.abs(dl_ref[...])) * live
    for o in range(2):
        fwd = full[:, o * 2 * GROUP_W:o * 2 * GROUP_W + GROUP_W]
        bwd = full[:, o * 2 * GROUP_W + GROUP_W:(o + 1) * 2 * GROUP_W]
        o_ref[o] = jnp.where(is_bwd, bwd, fwd) * decay


def _hy_filter_features(seq):
    d = _hy_dims(seq)
    n, n1, n2 = d["n"], d["n1"], d["n2"]
    row = np.arange(n)
    time = n2 * (row % n1) + row // n1
    is_bwd = time > seq
    pos = np.where(is_bwd, n - time, time)
    live = (time != seq).astype(np.float64)
    pos = np.where(time == seq, 0, pos)
    t = np.linspace(0.0, 1.0, seq)[pos]
    bands = (HY_EMB - 1) // 2
    f = np.linspace(1e-4, bands - 1, bands)[None]
    wpos = (2.0 * math.pi * pos / seq)[:, None]
    feat = np.zeros((n, 128), np.float64)
    feat[:, 0] = t
    feat[:, 1:1 + bands] = np.cos(f * wpos)
    feat[:, 1 + bands:HY_EMB] = -np.sin(f * wpos)
    feat[:, HY_EMB] = is_bwd
    feat[:, HY_EMB + 1] = live
    return jnp.asarray(feat, F32)


def _hy_filter(seq, w1, b1, w2, b2, w3, b3, w4, freq, tr=512):
    d = _hy_dims(seq)
    n = d["n"]
    tr = min(tr, n)
    feat = _hy_filter_features(seq)
    w1p = jnp.pad(w1, ((0, 128 - HY_EMB), (0, 0)))
    max_decay = math.log(HY_DECAY_TARGET) / HY_FAST_PCT
    min_decay = math.log(HY_DECAY_TARGET) / HY_SLOW_PCT
    deltas = jnp.linspace(min_decay, max_decay, GROUP_W, dtype=F32).reshape(1, -1)
    row = lambda v: v.reshape(1, -1)
    const = lambda a: pl.BlockSpec(a.shape, lambda i: (0,) * a.ndim)
    args = [w1p, row(b1), w2, row(b2), w3, row(b3), w4, row(freq), deltas]
    return pl.pallas_call(
        _hy_filter_kernel,
        grid=(n // tr,),
        in_specs=[pl.BlockSpec((tr, 128), lambda i: (i, 0))] + [const(a) for a in args],
        out_specs=pl.BlockSpec((2, tr, GROUP_W), lambda i: (0, i, 0)),
        out_shape=jax.ShapeDtypeStruct((2, n, GROUP_W), F32),
        compiler_params=_cparams(("parallel",)),
        name="hy_filter",
    )(feat, *args)


def _hy_spectrum_kernel(k_ref, m1f_ref, m2_ref, re_ref, im_ref, tr_ref, ti_ref, *, dims):
    n1, n2, pb = dims["n1"], dims["n2"], dims["pb"]

    def stage1(j, carry):
        r0 = pl.multiple_of(j * n1, n1)
        a = jnp.dot(m1f_ref[j], k_ref[pl.ds(r0, n1), :].astype(BF16), preferred_element_type=F32)
        tr_ref[pl.ds(j, n1, stride=pb), :] = a[:n1]
        ti_ref[pl.ds(j, n1, stride=pb), :] = a[n1:]
        return carry

    lax.fori_loop(0, n2, stage1, 0, unroll=HY_UNROLL)

    def stage2(k, carry):
        r0 = pl.multiple_of(k * pb, 8)
        rhs = jnp.concatenate([tr_ref[pl.ds(r0, n2), :], ti_ref[pl.ds(r0, n2), :]], axis=0).astype(BF16)
        x = jnp.dot(m2_ref[...], rhs, preferred_element_type=F32)
        o0 = pl.multiple_of(k * n2, n2)
        re_ref[pl.ds(o0, n2), :] = x[:n2].astype(re_ref.dtype)
        im_ref[pl.ds(o0, n2), :] = x[n2:].astype(im_ref.dtype)
        return carry

    lax.fori_loop(0, n1, stage2, 0, unroll=HY_UNROLL)


def _hy_spectrum(kern, mats, seq):
    d = _hy_dims(seq)
    n, n1, pb = d["n"], d["n1"], d["pb"]
    m1f, _, m2, _, _ = mats
    cb = _hy_cb(seq)
    ncb = GROUP_W // cb
    blk = pl.BlockSpec((None, n, cb), lambda o, j: (o, 0, j))
    const = lambda a: pl.BlockSpec(a.shape, lambda o, j: (0,) * a.ndim)
    return pl.pallas_call(
        functools.partial(_hy_spectrum_kernel, dims=d),
        grid=(2, ncb),
        in_specs=[blk, const(m1f), const(m2)],
        out_specs=[blk, blk],
        out_shape=[jax.ShapeDtypeStruct((2, n, GROUP_W), BF16)] * 2,
        scratch_shapes=[pltpu.VMEM((n1 * pb, cb), F32)] * 2,
        compiler_params=_cparams(("parallel", "parallel")),
        name="hy_spectrum",
    )(kern, m1f, m2)


def _hy_conv_kernel(u_ref, g_ref, kr_ref, ki_ref, skip_ref, m1_ref, m2_ref, m3_ref, m4_ref, o_ref,
                    t1r, t1i, t2r, t2i, *, dims, natural_out):
    n1, n2, h, pb, pc = (dims[k] for k in ("n1", "n2", "h", "pb", "pc"))
    pu, pg, po = u_ref.shape[1] // n2, g_ref.shape[1] // n2, o_ref.shape[1] // n2

    def fwd1(j, carry):
        r0 = pl.multiple_of(j * pu, 8)
        rhs = jnp.concatenate([u_ref[0, pl.ds(r0, h), :], u_ref[1, pl.ds(r0, h), :]], axis=0).astype(BF16)
        a = jnp.dot(m1_ref[j], rhs, preferred_element_type=F32)
        t1r[pl.ds(j, n1, stride=pb), :] = a[:n1]
        t1i[pl.ds(j, n1, stride=pb), :] = a[n1:]
        return carry

    lax.fori_loop(0, n2, fwd1, 0, unroll=HY_UNROLL)

    def mid(k, carry):
        r0 = pl.multiple_of(k * pb, 8)
        rhs = jnp.concatenate([t1r[pl.ds(r0, n2), :], t1i[pl.ds(r0, n2), :]], axis=0).astype(BF16)
        x = jnp.dot(m2_ref[...], rhs, preferred_element_type=F32)
        f0 = pl.multiple_of(k * n2, n2)
        fr, fi = kr_ref[pl.ds(f0, n2), :].astype(F32), ki_ref[pl.ds(f0, n2), :].astype(F32)
        xr, xi = x[:n2], x[n2:]
        y = jnp.concatenate([xr * fr - xi * fi, xr * fi + xi * fr], axis=0).astype(BF16)
        c = jnp.dot(m3_ref[k], y, preferred_element_type=F32)
        t2r[pl.ds(k, n2, stride=pc), :] = c[:n2]
        t2i[pl.ds(k, n2, stride=pc), :] = c[n2:]
        return carry

    lax.fori_loop(0, n1, mid, 0, unroll=HY_UNROLL)

    if not natural_out:
        o_ref[...] = jnp.zeros_like(o_ref)
    skip = skip_ref[...]

    def inv2(j, carry):
        r0 = pl.multiple_of(j * pc, 8)
        rhs = jnp.concatenate([t2r[pl.ds(r0, n1), :], t2i[pl.ds(r0, n1), :]], axis=0).astype(BF16)
        y = jnp.dot(m4_ref[...], rhs, preferred_element_type=F32)
        u0, g0 = pl.multiple_of(j * pu, 8), pl.multiple_of(j * pg, 8)
        for e in range(2):
            val = g_ref[e, pl.ds(g0, h), :] * (y[e * h:(e + 1) * h]
                                               + skip * u_ref[e, pl.ds(u0, h), :].astype(F32))
            if natural_out:
                o_ref[e, pl.ds(j, h, stride=n2), :] = val.astype(o_ref.dtype)
            else:
                o_ref[e, pl.ds(pl.multiple_of(j * po, 16), h), :] = val.astype(o_ref.dtype)
        return carry

    lax.fori_loop(0, n2, inv2, 0, unroll=HY_UNROLL)


def _hy_conv(u_arr, u_col, g_arr, g_col, kf_re, kf_im, order, skip, mats, seq, natural_out):
    d = _hy_dims(seq)
    bsz = u_arr.shape[0]
    u_rows, g_rows = u_arr.shape[1], g_arr.shape[1]
    z_rows = d["n2"] * d["pz"]
    _, m1, m2, m3, m4 = mats
    cb = _hy_cb(seq)
    ncb = GROUP_W // cb
    single = pl.Buffered(1)
    const = lambda a: pl.BlockSpec(a.shape, lambda j, p: (0,) * a.ndim, pipeline_mode=single)
    in_specs = [
        pl.BlockSpec((2, u_rows, cb), lambda j, p: (p, 0, u_col // cb + j)),
        pl.BlockSpec((2, g_rows, cb), lambda j, p: (p, 0, g_col // cb + j)),
        pl.BlockSpec((None, d["n"], cb), lambda j, p: (order, 0, j), pipeline_mode=single),
        pl.BlockSpec((None, d["n"], cb), lambda j, p: (order, 0, j), pipeline_mode=single),
        pl.BlockSpec((None, 1, cb), lambda j, p: (order, 0, j)),
        const(m1), const(m2), const(m3), const(m4),
    ]
    if natural_out:
        out_spec = pl.BlockSpec((2, seq, cb), lambda j, p: (p, 0, j))
        out_shape = jax.ShapeDtypeStruct((bsz, seq, GROUP_W), F32)
    else:
        out_spec = pl.BlockSpec((2, z_rows, cb), lambda j, p: (p, 0, j))
        out_shape = jax.ShapeDtypeStruct((bsz, z_rows, GROUP_W), BF16)
    return pl.pallas_call(
        functools.partial(_hy_conv_kernel, dims=d, natural_out=natural_out),
        grid=(ncb, bsz // 2),
        in_specs=in_specs,
        out_specs=out_spec,
        out_shape=out_shape,
        scratch_shapes=[pltpu.VMEM((d["n1"] * d["pb"], cb), F32)] * 2
        + [pltpu.VMEM((d["n2"] * d["pc"], cb), F32)] * 2,
        compiler_params=_cparams(("parallel", "parallel")),
        name="hy_conv",
    )(u_arr, g_arr, kf_re, kf_im, skip.reshape(2, 1, GROUP_W), m1, m2, m3, m4)


def _hyena_mixer(p, short_w, short_b, filt_params, skip, mats):
    seq = p.shape[1]
    kern = _hy_filter(seq, *filt_params)
    kf_re, kf_im = _hy_spectrum(kern, mats, seq)
    ut = _hy_prep(p, short_w, short_b)
    zt = _hy_conv(ut, 2 * GROUP_W, ut, 0, kf_re, kf_im, 0, skip, mats, seq, natural_out=False)
    return _hy_conv(zt, 0, ut, GROUP_W, kf_re, kf_im, 1, skip, mats, seq, natural_out=True)


CAST_BLOCK_BYTES = 8 * 1024 * 1024


def _cast_kernel(w_ref, o_ref):
    o_ref[...] = w_ref[...].astype(o_ref.dtype)


def _cast_bf16(w):
    depth, k, n = w.shape
    tr = min(k, CAST_BLOCK_BYTES // (4 * n))
    return pl.pallas_call(
        _cast_kernel,
        grid=(depth, k // tr),
        in_specs=[pl.BlockSpec((None, tr, n), lambda i, r: (i, r, 0))],
        out_specs=pl.BlockSpec((None, tr, n), lambda i, r: (i, r, 0)),
        out_shape=jax.ShapeDtypeStruct(w.shape, BF16),
        compiler_params=_cparams(("parallel", "parallel")),
        name="cast_bf16",
    )(w)


def _w_in_layout_kernel(w_ref, o_ref):
    o_a, o_dt = 768, 768 + 3072 + 1536
    for i, head in enumerate(HEAD_ORDER):
        o_ref[HEAD_DIM * i:HEAD_DIM * (i + 1), :] = w_ref[HEAD_DIM * head:HEAD_DIM * (head + 1), :].astype(o_ref.dtype)
    o_ref[COL_KA:COL_DT, :] = w_ref[COL_KA:o_a, :].astype(o_ref.dtype)
    o_ref[COL_DT:COL_DT + 16, :] = w_ref[o_dt:o_dt + 16, :].astype(o_ref.dtype)
    o_ref[COL_DT + 16:COL_HY, :] = jnp.zeros((COL_HY - COL_DT - 16, o_ref.shape[1]), o_ref.dtype)
    o_ref[COL_HY:, :] = w_ref[o_a:o_dt, :].astype(o_ref.dtype)


def _w_out_layout_kernel(w_ref, o_ref):
    for i, head in enumerate(HEAD_ORDER):
        o_ref[HEAD_DIM * i:HEAD_DIM * (i + 1), :] = w_ref[HEAD_DIM * head:HEAD_DIM * (head + 1), :].astype(o_ref.dtype)
    o_ref[GROUP_W:, :] = w_ref[GROUP_W:, :].astype(o_ref.dtype)


def _prep_weights(w_in, w_out):
    depth, d, n_in = w_in.shape
    tk = 256
    w_in_p = pl.pallas_call(
        _w_in_layout_kernel,
        grid=(depth, d // tk),
        in_specs=[pl.BlockSpec((None, n_in, tk), lambda i, c: (i, 0, c))],
        out_specs=pl.BlockSpec((None, N_PROJ, tk), lambda i, c: (i, 0, c)),
        out_shape=jax.ShapeDtypeStruct((depth, N_PROJ, d), BF16),
        compiler_params=_cparams(("parallel", "parallel")),
        name="w_in_layout",
    )(jnp.swapaxes(w_in, 1, 2))
    k_out, dm = w_out.shape[1:]
    tc = 512
    w_out_p = pl.pallas_call(
        _w_out_layout_kernel,
        grid=(depth, dm // tc),
        in_specs=[pl.BlockSpec((None, k_out, tc), lambda i, c: (i, 0, c))],
        out_specs=pl.BlockSpec((None, k_out, tc), lambda i, c: (i, 0, c)),
        out_shape=jax.ShapeDtypeStruct(w_out.shape, BF16),
        compiler_params=_cparams(("parallel", "parallel")),
        name="w_out_layout",
    )(w_out)
    return w_in_p, w_out_p


TILE_W_IN = (512, 2816)
TILE_MLP_UP = (1024, 2048)
TILE_DOWN = (1024, 1024, 2048)
TILE_DOWN_LAST = (512, 2048, 2048)
TILE_OUT = (512, 2048)


def kernel(x, c, ctx, c_ctx, ada_w, ada_b, norm_mix, norm_mlp, w_in, w_out, attn_sink,
           hy_short_w, hy_short_b, hy_w1, hy_b1, hy_w2, hy_b2, hy_w3, hy_b3, hy_w4, hy_freq, hy_skip,
           na_rpb, ssm_conv_w, ssm_conv_b, ssm_dt_bias, ssm_a_log, ssm_d, ssm_norm,
           mlp_w1, mlp_w2, final_norm):
    bsz, seq, d = x.shape
    lc = ctx.shape[1]
    depth = ada_w.shape[0]
    assert bsz % 2 == 0 and bsz <= 7 and d == D_MODEL

    cs = jnp.zeros((8, d), F32).at[:bsz].set(c).at[bsz].set(c_ctx)
    mod = _ada_mod(cs, ada_w, ada_b)
    cos_t, sin_t = _rope_tables(seq)
    na_plan = _na_plan(seq)
    bias_tabs = _na_bias_tables(na_rpb, na_plan[-1])
    mats_l = _hy_matrices(seq)
    mats_c = _hy_matrices(lc)
    lat_row = lambda b: b
    ctx_row = lambda b: bsz
    tm_c = lc

    w_in_p, w_out_p = _prep_weights(w_in, w_out)
    w1_b = _cast_bf16(mlp_w1)
    w2_b = _cast_bf16(mlp_w2)

    xc = ctx
    for i in range(depth):
        last = i == depth - 1
        mod3 = mod[i].reshape(8, 1, 6 * d)
        p = _norm_matmul(x, norm_mix[i], mod3, lat_row, 0, 1, w_in_p, i, P_DTYPE, False, *TILE_W_IN, w_t=True)
        pc = _norm_matmul(xc, norm_mix[i], mod3, ctx_row, 0, 1, w_in_p, i, P_DTYPE, False, tm_c, TILE_W_IN[1], w_t=True)

        filt_params = (hy_w1[i], hy_b1[i], hy_w2[i], hy_b2[i], hy_w3[i], hy_b3[i], hy_w4[i], hy_freq[i])
        ssd_consts = _ssd_consts(ssm_conv_w[i], ssm_conv_b[i], ssm_dt_bias[i], ssm_a_log[i], ssm_d[i], ssm_norm[i])

        ya = _win_attn(attn_sink[i], p, pc, cos_t, sin_t, local=True)
        yb = _hyena_mixer(p, hy_short_w[i], hy_short_b[i], filt_params, hy_skip[i], mats_l)
        yn = _na_attn(p, pc, bias_tabs, i, na_plan, local=True)
        yd, ydc = _ssd_mixer(p, pc, ssd_consts)
        x = _out_proj((ya, yb, yn, yd), w_out_p, i, x, mod3, lat_row, 2, *TILE_OUT)
        hid = _norm_matmul(x, norm_mlp[i], mod3, lat_row, 3, 4, w1_b, i, BF16, True, *TILE_MLP_UP)
        if last:
            return _matmul_residual(hid, w2_b, i, x, mod3, lat_row, 5, *TILE_DOWN_LAST, final_g=final_norm)
        x = _matmul_residual(hid, w2_b, i, x, mod3, lat_row, 5, *TILE_DOWN)

        if not last:
            yac = _win_attn(attn_sink[i], pc, pc, None, None, local=False)
            ybc = _hyena_mixer(pc, hy_short_w[i], hy_short_b[i], filt_params, hy_skip[i], mats_c)
            ync = _na_attn(pc, pc, None, i, None, local=False)
            xc = _out_proj((yac, ybc, ync, ydc), w_out_p, i, xc, mod3, ctx_row, 2, tm_c, TILE_OUT[1])
            hidc = _norm_matmul(xc, norm_mlp[i], mod3, ctx_row, 3, 4, w1_b, i, BF16, True, tm_c, TILE_MLP_UP[1])
            xc = _matmul_residual(hidc, w2_b, i, xc, mod3, ctx_row, 5, tm_c, *TILE_DOWN[1:])
    return _final_norm(x, final_norm)
```
